```python
import math
import jax, jax.numpy as jnp
from jax import lax
import numpy as np

D_MODEL = 1024
BATCH = 8
SEQ = 4096
DEPTH = 2

CHUNK = 64
QBLK = 128
MIX_WIDTH = D_MODEL
A_HEADS = 8
A_HEAD_DIM = 64
A_WIDTH = A_HEADS * A_HEAD_DIM
KV_LORA = 128
IDX_HEADS = 8
IDX_DIM = 64
INDEX_TOPK_MAX = 256
B_HEADS = 4
B_QK_DIM = 64
B_V_DIM = 2 * B_QK_DIM
B_WIDTH = B_HEADS * B_V_DIM
MEM_TOKENS = 256
MEM_HEADS = 4
MEM_HEAD_DIM = D_MODEL // MEM_HEADS
N_EXPERTS = 64
TOP_K = 8
EXPERT_DIM = 256
SHARED_DIM = 256
ROUTED_SCALE = 2.5
MOE_BLOCK = 128
ALPHA = (2 * DEPTH) ** 0.25
BETA = (8 * DEPTH) ** -0.25
LN_EPS = 1e-5
RMS_EPS = 1e-6
IN_SIZES = (A_WIDTH, KV_LORA, IDX_HEADS * IDX_DIM, IDX_DIM, IDX_HEADS,
            2 * B_HEADS * B_QK_DIM, 2 * B_HEADS * B_QK_DIM, B_WIDTH)

kernel_name = "hybrid_dsa_diffattn_mem_moe_deepnorm"


def layer_norm(x, g, b):
    xf = x.astype(jnp.float32)
    mu = jnp.mean(xf, axis=-1, keepdims=True)
    var = jnp.mean(jnp.square(xf - mu), axis=-1, keepdims=True)
    return ((xf - mu) * lax.rsqrt(var + LN_EPS) * g + b).astype(x.dtype)


def rms_norm(x, g):
    xf = x.astype(jnp.float32)
    return (xf * lax.rsqrt(jnp.mean(xf * xf, axis=-1, keepdims=True) + RMS_EPS) * g).astype(x.dtype)


def alibi_slopes(n):
    return jnp.exp2(-8.0 * jnp.arange(1, n + 1, dtype=jnp.float32) / n)


def to_qblocks(a):
    b, s = a.shape[:2]
    return jnp.moveaxis(a.reshape(b, s // QBLK, QBLK, *a.shape[2:]), 1, 0)


def from_qblocks(a):
    a = jnp.moveaxis(a, 0, 1)
    return a.reshape(a.shape[0], -1, *a.shape[3:])


def dsa_attention(q, q_idx, w_idx, c_kv, k_idx, w_uk, w_uv):
    b, s = q.shape[:2]
    topk = min(INDEX_TOPK_MAX, s // 4)
    slopes = alibi_slopes(A_HEADS)
    key_chunk = jnp.arange(s) // CHUNK
    bidx = jnp.arange(b)[:, None, None]
    scale = A_HEAD_DIM ** -0.5

    def one_block(args):
        qb, qib, wib, blk = args
        tq = blk * QBLK + jnp.arange(QBLK)
        rel = jax.nn.relu(jnp.einsum('bqhd,bsd->bqhs', qib, k_idx) * IDX_DIM ** -0.5)
        iscore = jnp.einsum('bqh,bqhs->bqs', wib, rel).astype(jnp.float32)
        admissible = key_chunk[None, :] <= (tq // CHUNK)[:, None]
        iscore = jnp.where(admissible[None], iscore, -jnp.inf)
        _, sel = lax.top_k(iscore, topk)
        valid = (sel // CHUNK) <= (tq // CHUNK)[None, :, None]
        c_sel = c_kv[bidx, sel]
        q_lat = jnp.einsum('bqhd,hcd->bqhc', qb, w_uk)
        sc = jnp.einsum('bqhc,bqkc->bqhk', q_lat, c_sel).astype(jnp.float32) * scale
        dist = jnp.abs(tq[None, :, None] - sel).astype(jnp.float32)
        sc = sc - slopes[None, None, :, None] * dist[:, :, None, :]
        sc = jnp.where(valid[:, :, None, :], sc, -jnp.inf)
        p = jax.nn.softmax(sc, axis=-1).astype(c_kv.dtype)
        o_lat = jnp.einsum('bqhk,bqkc->bqhc', p, c_sel)
        return jnp.einsum('bqhc,hcd->bqhd', o_lat, w_uv)

    nb = s // QBLK
    out = lax.map(one_block, (to_qblocks(q), to_qblocks(q_idx), to_qblocks(w_idx), jnp.arange(nb)))
    return from_qblocks(out).reshape(b, s, A_WIDTH)


def diff_attention(q1, q2, k1, k2, v, lam):
    s = q1.shape[1]
    slopes = alibi_slopes(B_HEADS)
    ts = jnp.arange(s)
    scale = B_QK_DIM ** -0.5

    def one_block(args):
        q1b, q2b, blk = args
        tq = blk * QBLK + jnp.arange(QBLK)
        mask = (ts // CHUNK)[None, :] <= (tq // CHUNK)[:, None]
        dist = jnp.abs(tq[:, None] - ts[None, :]).astype(jnp.float32)
        bias = jnp.where(mask[None], -slopes[:, None, None] * dist[None], -jnp.inf)
        s1 = jnp.einsum('bqhd,bshd->bhqs', q1b, k1).astype(jnp.float32) * scale + bias
        s2 = jnp.einsum('bqhd,bshd->bhqs', q2b, k2).astype(jnp.float32) * scale + bias
        attn = (jax.nn.softmax(s1, axis=-1) - lam * jax.nn.softmax(s2, axis=-1)).astype(v.dtype)
        return jnp.einsum('bhqs,bshd->bqhd', attn, v)

    out = lax.map(one_block, (to_qblocks(q1), to_qblocks(q2), jnp.arange(s // QBLK)))
    return from_qblocks(out)


def hybrid_mixer(x, w_in, a_kv_norm, a_w_uk, a_w_uv, b_lq1, b_lk1, b_lq2, b_lk2, b_subln, w_o, lambda_init):
    b, s, _ = x.shape
    proj = x @ w_in
    splits = np.cumsum(IN_SIZES)[:-1].tolist()
    q_a, c_kv, q_idx, k_idx, w_idx, q_b, k_b, v_b = jnp.split(proj, splits, axis=-1)
    q_a = q_a.reshape(b, s, A_HEADS, A_HEAD_DIM)
    c_kv = rms_norm(c_kv, a_kv_norm)
    q_idx = q_idx.reshape(b, s, IDX_HEADS, IDX_DIM)
    w_idx = w_idx * IDX_HEADS ** -0.5
    out_a = dsa_attention(q_a, q_idx, w_idx, c_kv, k_idx, a_w_uk, a_w_uv)
    q_b = q_b.reshape(b, s, B_HEADS, 2, B_QK_DIM)
    k_b = k_b.reshape(b, s, B_HEADS, 2, B_QK_DIM)
    v_b = v_b.reshape(b, s, B_HEADS, B_V_DIM)
    lam = (jnp.exp(jnp.sum(b_lq1.astype(jnp.float32) * b_lk1.astype(jnp.float32)))
           - jnp.exp(jnp.sum(b_lq2.astype(jnp.float32) * b_lk2.astype(jnp.float32))) + lambda_init)
    o_b = diff_attention(q_b[..., 0, :], q_b[..., 1, :], k_b[..., 0, :], k_b[..., 1, :], v_b, lam)
    out_b = (rms_norm(o_b, b_subln) * (1.0 - lambda_init)).reshape(b, s, B_WIDTH)
    return jnp.concatenate([out_a, out_b], axis=-1) @ w_o


def memory_attention(x, mem, wq, wkv, wo):
    b, s, d = x.shape
    m = mem.shape[1]
    q = (x @ wq).reshape(b, s, MEM_HEADS, MEM_HEAD_DIM)
    kv = (mem @ wkv).reshape(b, m, 2, MEM_HEADS, MEM_HEAD_DIM)
    k, v = kv[:, :, 0], kv[:, :, 1]
    sc = jnp.einsum('bqhd,bmhd->bhqm', q, k).astype(jnp.float32) * MEM_HEAD_DIM ** -0.5
    p = jax.nn.softmax(sc, axis=-1).astype(v.dtype)
    o = jnp.einsum('bhqm,bmhd->bqhd', p, v).reshape(b, s, d)
    return o @ wo


def swiglu(x, wg, wu, wd):
    return (jax.nn.silu(x @ wg) * (x @ wu)) @ wd


def moe(x, router_w, router_bias, e_w_gate, e_w_up, e_w_down, s_w_gate, s_w_up, s_w_down):
    b, s, d = x.shape
    n = b * s
    xf = x.reshape(n, d)
    scores = jax.nn.sigmoid((xf @ router_w).astype(jnp.float32))
    _, top_e = lax.top_k(scores + router_bias.astype(jnp.float32), TOP_K)
    top_s = jnp.take_along_axis(scores, top_e, axis=-1)
    gates = top_s / jnp.sum(top_s, axis=-1, keepdims=True) * ROUTED_SCALE
    m = n * TOP_K
    flat_e = top_e.reshape(m)
    flat_tok = jnp.repeat(jnp.arange(n, dtype=jnp.int32), TOP_K)
    flat_g = gates.reshape(m)
    order = jnp.argsort(flat_e)
    se = flat_e[order]
    counts = jnp.bincount(flat_e, length=N_EXPERTS)
    padded = (counts + MOE_BLOCK - 1) // MOE_BLOCK * MOE_BLOCK
    start = jnp.cumsum(counts) - counts
    pend = jnp.cumsum(padded)
    pstart = pend - padded
    dest = pstart[se] + (jnp.arange(m) - start[se])
    rows_total = m + N_EXPERTS * MOE_BLOCK
    nblk = rows_total // MOE_BLOCK
    row_tok = jnp.full((rows_total,), n, dtype=jnp.int32).at[dest].set(flat_tok[order])
    row_gate = jnp.zeros((rows_total,), jnp.float32).at[dest].set(flat_g[order])
    blk_e = jnp.minimum(jnp.searchsorted(pend, jnp.arange(nblk) * MOE_BLOCK, side='right'), N_EXPERTS - 1)
    x_pad = jnp.concatenate([xf, jnp.zeros((1, d), xf.dtype)], axis=0)

    def body(acc, blk):
        rows, g, e = blk
        xb = x_pad[rows]
        yb = swiglu(xb, e_w_gate[e], e_w_up[e], e_w_down[e]) * g[:, None]
        return acc.at[rows].add(yb), None

    acc, _ = lax.scan(body, jnp.zeros((n + 1, d), x.dtype),
                      (row_tok.reshape(nblk, MOE_BLOCK),
                       row_gate.reshape(nblk, MOE_BLOCK).astype(x.dtype), blk_e))
    out = acc[:n] + swiglu(xf, s_w_gate, s_w_up, s_w_down)
    return out.reshape(b, s, d)


def setup_inputs(seed: int = 0) -> dict:
    key = jax.random.key(seed)
    keys = list(jax.random.split(key, 48))

    def nrm(shape, scale):
        return scale * jax.random.normal(keys.pop(), shape, jnp.float32)

    D = D_MODEL
    x = nrm((BATCH, SEQ, D), 1.0)
    mem = nrm((BATCH, MEM_TOKENS, D), 1.0)
    ln_in_g = 1.0 + nrm((D,), 0.02)
    ln_in_b = nrm((D,), 0.02)
    w_in = jnp.concatenate(
        [nrm((DEPTH, D, sz), D ** -0.5 * (BETA if i == len(IN_SIZES) - 1 else 1.0))
         for i, sz in enumerate(IN_SIZES)], axis=-1)
    a_kv_norm = 1.0 + nrm((DEPTH, KV_LORA), 0.02)
    a_w_uk = nrm((DEPTH, A_HEADS, KV_LORA, A_HEAD_DIM), KV_LORA ** -0.5)
    a_w_uv = nrm((DEPTH, A_HEADS, KV_LORA, A_HEAD_DIM), KV_LORA ** -0.5 * BETA)
    b_lq1 = nrm((DEPTH, B_QK_DIM), 0.1)
    b_lk1 = nrm((DEPTH, B_QK_DIM), 0.1)
    b_lq2 = nrm((DEPTH, B_QK_DIM), 0.1)
    b_lk2 = nrm((DEPTH, B_QK_DIM), 0.1)
    b_subln = 1.0 + nrm((DEPTH, B_V_DIM), 0.02)
    w_o = nrm((DEPTH, MIX_WIDTH, D), MIX_WIDTH ** -0.5 * BETA)
    ln1_g = 1.0 + nrm((DEPTH, D), 0.02)
    ln1_b = nrm((DEPTH, D), 0.02)
    m_wq = nrm((DEPTH, D, D), D ** -0.5)
    m_wkv = jnp.concatenate([nrm((DEPTH, D, D), D ** -0.5), nrm((DEPTH, D, D), D ** -0.5 * BETA)], axis=-1)
    m_wo = nrm((DEPTH, D, D), D ** -0.5 * BETA)
    ln2_g = 1.0 + nrm((DEPTH, D), 0.02)
    ln2_b = nrm((DEPTH, D), 0.02)
    router_w = nrm((DEPTH, D, N_EXPERTS), D ** -0.5)
    router_bias = nrm((DEPTH, N_EXPERTS), 0.01)
    e_w_gate = nrm((DEPTH, N_EXPERTS, D, EXPERT_DIM), D ** -0.5)
    e_w_up = nrm((DEPTH, N_EXPERTS, D, EXPERT_DIM), D ** -0.5 * BETA)
    e_w_down = nrm((DEPTH, N_EXPERTS, EXPERT_DIM, D), EXPERT_DIM ** -0.5 * BETA)
    s_w_gate = nrm((DEPTH, D, SHARED_DIM), D ** -0.5)
    s_w_up = nrm((DEPTH, D, SHARED_DIM), D ** -0.5 * BETA)
    s_w_down = nrm((DEPTH, SHARED_DIM, D), SHARED_DIM ** -0.5 * BETA)
    ln3_g = 1.0 + nrm((DEPTH, D), 0.02)
    ln3_b = nrm((DEPTH, D), 0.02)
    return {"x": x, "mem": mem, "ln_in_g": ln_in_g, "ln_in_b": ln_in_b, "w_in": w_in,
            "a_kv_norm": a_kv_norm, "a_w_uk": a_w_uk, "a_w_uv": a_w_uv,
            "b_lq1": b_lq1, "b_lk1": b_lk1, "b_lq2": b_lq2, "b_lk2": b_lk2, "b_subln": b_subln,
            "w_o": w_o, "ln1_g": ln1_g, "ln1_b": ln1_b,
            "m_wq": m_wq, "m_wkv": m_wkv, "m_wo": m_wo, "ln2_g": ln2_g, "ln2_b": ln2_b,
            "router_w": router_w, "router_bias": router_bias,
            "e_w_gate": e_w_gate, "e_w_up": e_w_up, "e_w_down": e_w_down,
            "s_w_gate": s_w_gate, "s_w_up": s_w_up, "s_w_down": s_w_down,
            "ln3_g": ln3_g, "ln3_b": ln3_b}


def reference(x, mem, ln_in_g, ln_in_b, w_in, a_kv_norm, a_w_uk, a_w_uv,
              b_lq1, b_lk1, b_lq2, b_lk2, b_subln, w_o, ln1_g, ln1_b,
              m_wq, m_wkv, m_wo, ln2_g, ln2_b,
              router_w, router_bias, e_w_gate, e_w_up, e_w_down,
              s_w_gate, s_w_up, s_w_down, ln3_g, ln3_b):
    h = layer_norm(x, ln_in_g, ln_in_b)
    for l in range(DEPTH):
        lambda_init = 0.8 - 0.6 * math.exp(-0.3 * l)
        y = hybrid_mixer(h, w_in[l], a_kv_norm[l], a_w_uk[l], a_w_uv[l],
                         b_lq1[l], b_lk1[l], b_lq2[l], b_lk2[l], b_subln[l], w_o[l], lambda_init)
        h = layer_norm(ALPHA * h + y, ln1_g[l], ln1_b[l])
        y = memory_attention(h, mem, m_wq[l], m_wkv[l], m_wo[l])
        h = layer_norm(ALPHA * h + y, ln2_g[l], ln2_b[l])
        y = moe(h, router_w[l], router_bias[l], e_w_gate[l], e_w_up[l], e_w_down[l],
                s_w_gate[l], s_w_up[l], s_w_down[l])
        h = layer_norm(ALPHA * h + y, ln3_g[l], ln3_b[l])
    return h
```

```python
import functools
import math

import jax
import jax.numpy as jnp
import numpy as np
from jax import lax
from jax.experimental import pallas as pl
from jax.experimental.pallas import tpu as pltpu

F32 = jnp.float32
BF16 = jnp.bfloat16
I32 = jnp.int32

D_MODEL = 1024
DEPTH = 2
CHUNK = 64
CHUNK_SHIFT = 6
A_HEADS = 8
A_HEAD_DIM = 64
A_WIDTH = A_HEADS * A_HEAD_DIM
KV_LORA = 128
IDX_HEADS = 8
IDX_DIM = 64
INDEX_TOPK_MAX = 256
B_HEADS = 4
B_QK_DIM = 64
B_V_DIM = 128
B_WIDTH = B_HEADS * B_V_DIM
MEM_HEADS = 4
MEM_HEAD_DIM = D_MODEL // MEM_HEADS
N_EXPERTS = 64
TOP_K = 8
EXPERT_DIM = 256
SHARED_DIM = 256
ROUTED_SCALE = 2.5
ALPHA = (2 * DEPTH) ** 0.25
LN_EPS = 1e-5
RMS_EPS = 1e-6

LANES = 128
SUBLANES = 8
VMEM_LIMIT = 56 * 1024 * 1024
INT_MIN = -2 ** 31
NEG_BIG = -3.0e38
M_INIT = -1.0e30

NT_DIMS = (((1,), (1,)), ((), ()))


def _cparams(n_axes):
    return pltpu.CompilerParams(dimension_semantics=("arbitrary",) * n_axes,
                                vmem_limit_bytes=VMEM_LIMIT)


def _layer_norm(x, g, b):
    mu = jnp.mean(x, axis=-1, keepdims=True)
    xc = x - mu
    var = jnp.mean(xc * xc, axis=-1, keepdims=True)
    return xc * lax.rsqrt(var + LN_EPS) * g + b


def _silu(x):
    return x * (1.0 / (1.0 + jnp.exp(-x)))


def _ln_kernel(x_ref, g_ref, b_ref, o_ref):
    o_ref[...] = _layer_norm(x_ref[...], g_ref[...], b_ref[...])


def _ln_call(x, g, b, tm):
    n, d = x.shape
    return pl.pallas_call(
        _ln_kernel,
        grid=(n // tm,),
        in_specs=[pl.BlockSpec((tm, d), lambda i: (i, 0)),
                  pl.BlockSpec((1, d), lambda i: (0, 0)),
                  pl.BlockSpec((1, d), lambda i: (0, 0))],
        out_specs=pl.BlockSpec((tm, d), lambda i: (i, 0)),
        out_shape=jax.ShapeDtypeStruct((n, d), F32),
        compiler_params=_cparams(1),
        name="ln_in",
    )(x, g.reshape(1, d), b.reshape(1, d))


_SEG_QA = (0, 512)
_SEG_CKV = (512, 640)
_SEG_QIDX = (640, 1152)
_SEG_KK = (1152, 1280)
_SEG_WIDX = (1280, 1408)
_SEG_QB = (1408, 1920)
_SEG_KB = (1920, 2432)
_SEG_VB = (2432, 2944)
_PROJ_COLS = 2944


def _proj_kernel(h_ref, w_ref, kvn_ref, qa_ref, ckv_ref, qidx_ref, kk_ref, widx_ref,
                 qb_ref, kb_ref, vb_ref):
    x = h_ref[...].astype(BF16)

    def seg(s):
        return jnp.dot(x, w_ref[:, s[0]:s[1]], preferred_element_type=F32)

    qa_ref[...] = (seg(_SEG_QA) * (A_HEAD_DIM ** -0.5)).astype(BF16)
    c = seg(_SEG_CKV)
    c = c * lax.rsqrt(jnp.mean(c * c, axis=-1, keepdims=True) + RMS_EPS) * kvn_ref[...]
    ckv_ref[...] = c.astype(BF16)
    qidx_ref[...] = (seg(_SEG_QIDX) * (IDX_DIM ** -0.5)).astype(BF16)
    kk_ref[...] = seg(_SEG_KK).astype(BF16)
    widx_ref[...] = seg(_SEG_WIDX) * (IDX_HEADS ** -0.5)
    qb_ref[...] = (seg(_SEG_QB) * (B_QK_DIM ** -0.5)).astype(BF16)
    kb_ref[...] = seg(_SEG_KB).astype(BF16)
    vb_ref[...] = seg(_SEG_VB).astype(BF16)


def _proj_call(h, w_pad, kvn, tm):
    n, d = h.shape
    widths = [(512, BF16), (128, BF16), (512, BF16), (128, BF16), (128, F32),
              (512, BF16), (512, BF16), (512, BF16)]
    return pl.pallas_call(
        _proj_kernel,
        grid=(n // tm,),
        in_specs=[pl.BlockSpec((tm, d), lambda i: (i, 0)),
                  pl.BlockSpec((d, _PROJ_COLS), lambda i: (0, 0)),
                  pl.BlockSpec((1, KV_LORA), lambda i: (0, 0))],
        out_specs=[pl.BlockSpec((tm, w), lambda i: (i, 0)) for w, _ in widths],
        out_shape=[jax.ShapeDtypeStruct((n, w), dt) for w, dt in widths],
        compiler_params=_cparams(1),
        name="proj_in",
    )(h, w_pad, kvn.reshape(1, KV_LORA))


def _pad_w_in(w_in):
    sizes = (A_WIDTH, KV_LORA, IDX_HEADS * IDX_DIM, IDX_DIM, IDX_HEADS,
             2 * B_HEADS * B_QK_DIM, 2 * B_HEADS * B_QK_DIM, B_WIDTH)
    splits = np.cumsum(sizes)[:-1].tolist()
    q_a, c_kv, q_idx, k_idx, w_idx, q_b, k_b, v_b = jnp.split(w_in, splits, axis=-1)
    w_idx = jnp.pad(w_idx, ((0, 0), (0, LANES - IDX_HEADS)))
    return jnp.concatenate([q_a, c_kv, q_idx, k_idx, k_idx, w_idx, q_b, k_b, v_b],
                           axis=-1).astype(BF16)


def _flash_update(idx, sc, vt, m_ref, l_ref, acc_ref):
    m_old = m_ref[idx]
    m_new = jnp.maximum(m_old, jnp.max(sc, axis=1, keepdims=True))
    alpha = jnp.exp(m_old - m_new)
    p = jnp.exp(sc - m_new[:, :1])
    l_ref[idx] = alpha * l_ref[idx] + jnp.sum(p, axis=1, keepdims=True)
    acc_ref[idx] = alpha * acc_ref[idx] + jnp.dot(p.astype(BF16), vt,
                                                  preferred_element_type=F32)
    m_ref[idx] = m_new


def _split_halves(q_ref, n_pairs, out_ref, rows):
    lane = lax.broadcasted_iota(I32, (rows, LANES), 1)
    for j in range(n_pairs):
        qp = q_ref[:, j * LANES:(j + 1) * LANES].astype(F32)
        out_ref[2 * j] = jnp.where(lane < 64, qp, 0.0).astype(BF16)
        out_ref[2 * j + 1] = jnp.where(lane >= 64, qp, 0.0).astype(BF16)


def _dsa_kernel(qa_ref, qidx_ref, widx_ref, kk_ref, ckv_ref, wk_ref, wv_ref, out_ref,
                keys_ref, qim_ref, qlat_ref, tj_ref, m_ref, l_ref, acc_ref, *, tq, seq, topk):
    i = pl.program_id(1)
    n_kt = i + 1
    idx_bits = int(math.log2(seq))

    _split_halves(qidx_ref, IDX_HEADS // 2, qim_ref, tq)
    qlat = jnp.dot(qa_ref[...], wk_ref[...], preferred_element_type=F32)
    for h in range(A_HEADS):
        qlat_ref[h] = qlat[:, h * KV_LORA:(h + 1) * KV_LORA].astype(BF16)

    row = lax.broadcasted_iota(I32, (tq, tq), 0)
    col = lax.broadcasted_iota(I32, (tq, tq), 1)
    tqi = i * tq + row
    qchunk = lax.shift_right_logical(tqi, CHUNK_SHIFT)
    w_all = widx_ref[...]

    def idx_body(kt, carry):
        off = pl.multiple_of(kt * tq, tq)
        kkt = kk_ref[pl.ds(off, tq), :]
        acc = jnp.zeros((tq, tq), F32)
        for h in range(IDX_HEADS):
            s = lax.dot_general(qim_ref[h], kkt, NT_DIMS, preferred_element_type=F32)
            acc = acc + w_all[:, h:h + 1] * jnp.maximum(s, 0.0)
        bits = lax.bitcast_convert_type(acc + 0.0, I32)
        key = jnp.where(bits < 0, bits ^ jnp.int32(0x7FFFFFFF), bits)
        adm = lax.shift_right_logical(off + col, CHUNK_SHIFT) <= qchunk
        keys_ref[:, pl.ds(off, tq)] = jnp.where(adm, key, jnp.int32(INT_MIN))
        return carry

    lax.fori_loop(0, n_kt, idx_body, 0)

    def count(pred):
        def body(kt, c):
            off = pl.multiple_of(kt * tq, tq)
            ind = jnp.where(pred(keys_ref[:, pl.ds(off, tq)], off + col), 1.0, 0.0)
            for j in range(tq // LANES):
                c = c + ind[:, j * LANES:(j + 1) * LANES]
            return c
        c = lax.fori_loop(0, n_kt, body, jnp.zeros((tq, LANES), F32))
        return jnp.sum(c, axis=1, keepdims=True)

    def search_body(it, t):
        cand = t + lax.shift_left(jnp.int32(1), 31 - it)
        cnt = count(lambda tile, _: tile >= cand)
        return jnp.where(cnt >= topk, cand, t)

    thr = lax.fori_loop(0, 32, search_body, jnp.full((tq, 1), INT_MIN, I32))

    live = thr > INT_MIN
    n_ge = count(lambda tile, _: tile >= thr)
    excess = jnp.logical_and(n_ge > topk, live)
    tj_ref[0] = jnp.broadcast_to(thr, (tq, LANES))
    tj_ref[1] = jnp.broadcast_to(jnp.where(live, seq, -1).astype(I32), (tq, LANES))

    @pl.when(jnp.max(jnp.where(excess, 1.0, 0.0)) > 0.0)
    def _():
        need = topk - count(lambda tile, _: tile > thr)

        def tie_body(it, hi):
            cand = hi + lax.shift_left(jnp.int32(1), idx_bits - 1 - it)
            cnt = count(lambda tile, pos: jnp.logical_and(tile == thr, pos < cand))
            return jnp.where(cnt < need, cand, hi)

        hi = lax.fori_loop(0, idx_bits, tie_body, jnp.zeros((tq, 1), I32))
        keep = jnp.where(excess, hi, jnp.where(live, seq, -1).astype(I32))
        tj_ref[1] = jnp.broadcast_to(keep, (tq, LANES))

    thr_b = tj_ref[0][:, :1]
    tie_hi = tj_ref[1][:, :1]

    m_ref[...] = jnp.full(m_ref.shape, M_INIT, F32)
    l_ref[...] = jnp.zeros(l_ref.shape, F32)
    acc_ref[...] = jnp.zeros(acc_ref.shape, F32)

    def att_body(kt, carry):
        off = pl.multiple_of(kt * tq, tq)
        tile = keys_ref[:, pl.ds(off, tq)]
        pos = off + col
        sel = jnp.logical_or(tile > thr_b,
                             jnp.logical_and(tile == thr_b, pos <= tie_hi))
        dist = jnp.abs(tqi - pos).astype(F32)
        ckt = ckv_ref[pl.ds(off, tq), :]
        for h in range(A_HEADS):
            sc = lax.dot_general(qlat_ref[h], ckt, NT_DIMS, preferred_element_type=F32)
            sc = jnp.where(sel, sc - (2.0 ** -(h + 1)) * dist, NEG_BIG)
            _flash_update(h, sc, ckt, m_ref, l_ref, acc_ref)
        return carry

    lax.fori_loop(0, n_kt, att_body, 0)

    olat = jnp.concatenate([(acc_ref[h] / l_ref[h]).astype(BF16) for h in range(A_HEADS)], axis=1)
    out_ref[...] = jnp.dot(olat, wv_ref[...], preferred_element_type=F32).astype(BF16)


def _dsa_call(qa, qidx, widx, kk, ckv, wk_bd, wv_bd, b, s, tq):
    topk = min(INDEX_TOPK_MAX, s // 4)
    assert s % tq == 0 and tq % LANES == 0 and s & (s - 1) == 0
    blk = lambda w: pl.BlockSpec((None, tq, w), lambda bi, i: (bi, i, 0))
    full = lambda w: pl.BlockSpec((None, s, w), lambda bi, i: (bi, 0, 0))
    r3 = lambda a: a.reshape(b, s, a.shape[-1])
    out = pl.pallas_call(
        functools.partial(_dsa_kernel, tq=tq, seq=s, topk=topk),
        grid=(b, s // tq),
        in_specs=[blk(A_WIDTH), blk(IDX_HEADS * IDX_DIM), blk(LANES), full(LANES), full(KV_LORA),
                  pl.BlockSpec(wk_bd.shape, lambda bi, i: (0, 0)),
                  pl.BlockSpec(wv_bd.shape, lambda bi, i: (0, 0))],
        out_specs=blk(A_WIDTH),
        out_shape=jax.ShapeDtypeStruct((b, s, A_WIDTH), BF16),
        scratch_shapes=[pltpu.VMEM((tq, s), I32),
                        pltpu.VMEM((IDX_HEADS, tq, LANES), BF16),
                        pltpu.VMEM((A_HEADS, tq, KV_LORA), BF16),
                        pltpu.VMEM((2, tq, LANES), I32),
                        pltpu.VMEM((A_HEADS, tq, LANES), F32),
                        pltpu.VMEM((A_HEADS, tq, LANES), F32),
                        pltpu.VMEM((A_HEADS, tq, KV_LORA), F32)],
        compiler_params=_cparams(2),
        name="dsa",
    )(r3(qa), r3(qidx), r3(widx), r3(kk), r3(ckv), wk_bd, wv_bd)
    return out.reshape(b * s, A_WIDTH)


def _block_diag_uk(w_uk):
    h, c, d = w_uk.shape
    eye = jnp.eye(h, dtype=w_uk.dtype)
    return jnp.einsum('hcd,hg->hdgc', w_uk, eye).reshape(h * d, h * c).astype(BF16)


def _block_diag_uv(w_uv):
    h, c, d = w_uv.shape
    eye = jnp.eye(h, dtype=w_uv.dtype)
    return jnp.einsum('hcd,hg->hcgd', w_uv, eye).reshape(h * c, h * d).astype(BF16)


def _diff_kernel(qb_ref, kb_ref, vb_ref, lam_ref, subln_ref, out_ref,
                 qm_ref, m_ref, l_ref, acc_ref, *, tq, lambda_init):
    i = pl.program_id(1)
    _split_halves(qb_ref, B_HEADS, qm_ref, tq)

    row = lax.broadcasted_iota(I32, (tq, tq), 0)
    col = lax.broadcasted_iota(I32, (tq, tq), 1)
    tqi = i * tq + row
    qchunk = lax.shift_right_logical(tqi, CHUNK_SHIFT)

    m_ref[...] = jnp.full(m_ref.shape, M_INIT, F32)
    l_ref[...] = jnp.zeros(l_ref.shape, F32)
    acc_ref[...] = jnp.zeros(acc_ref.shape, F32)

    def body(kt, carry):
        off = pl.multiple_of(kt * tq, tq)
        pos = off + col
        adm = lax.shift_right_logical(pos, CHUNK_SHIFT) <= qchunk
        dist = jnp.abs(tqi - pos).astype(F32)
        for h in range(B_HEADS):
            kt_h = kb_ref[pl.ds(off, tq), h * LANES:(h + 1) * LANES]
            vt_h = vb_ref[pl.ds(off, tq), h * LANES:(h + 1) * LANES]
            for j in range(2):
                s = lax.dot_general(qm_ref[2 * h + j], kt_h, NT_DIMS, preferred_element_type=F32)
                sc = jnp.where(adm, s - (2.0 ** (-2 * (h + 1))) * dist, NEG_BIG)
                _flash_update(2 * h + j, sc, vt_h, m_ref, l_ref, acc_ref)
        return carry

    lax.fori_loop(0, i + 1, body, 0)

    lp = lam_ref[...]
    lam = (jnp.exp(jnp.sum(lp[0:1] * lp[1:2], axis=1, keepdims=True))
           - jnp.exp(jnp.sum(lp[2:3] * lp[3:4], axis=1, keepdims=True)) + lambda_init)
    for h in range(B_HEADS):
        o = acc_ref[2 * h] / l_ref[2 * h] - lam * (acc_ref[2 * h + 1] / l_ref[2 * h + 1])
        o = o * lax.rsqrt(jnp.mean(o * o, axis=-1, keepdims=True) + RMS_EPS) * subln_ref[...]
        out_ref[:, h * B_V_DIM:(h + 1) * B_V_DIM] = (o * (1.0 - lambda_init)).astype(BF16)


def _diff_call(qb, kb, vb, lam_rows, subln, b, s, tq, lambda_init):
    blk = pl.BlockSpec((None, tq, B_WIDTH), lambda bi, i: (bi, i, 0))
    full = pl.BlockSpec((None, s, B_WIDTH), lambda bi, i: (bi, 0, 0))
    r3 = lambda a: a.reshape(b, s, a.shape[-1])
    out = pl.pallas_call(
        functools.partial(_diff_kernel, tq=tq, lambda_init=lambda_init),
        grid=(b, s // tq),
        in_specs=[blk, full, full,
                  pl.BlockSpec((SUBLANES, LANES), lambda bi, i: (0, 0)),
                  pl.BlockSpec((1, B_V_DIM), lambda bi, i: (0, 0))],
        out_specs=blk,
        out_shape=jax.ShapeDtypeStruct((b, s, B_WIDTH), BF16),
        scratch_shapes=[pltpu.VMEM((2 * B_HEADS, tq, LANES), BF16),
                        pltpu.VMEM((2 * B_HEADS, tq, LANES), F32),
                        pltpu.VMEM((2 * B_HEADS, tq, LANES), F32),
                        pltpu.VMEM((2 * B_HEADS, tq, B_V_DIM), F32)],
        compiler_params=_cparams(2),
        name="diff",
    )(r3(qb), r3(kb), r3(vb), lam_rows, subln.reshape(1, B_V_DIM))
    return out.reshape(b * s, B_WIDTH)


def _mix_out_kernel(h_ref, a_ref, b_ref, wa_ref, wb_ref, g_ref, beta_ref, o_ref):
    y = (jnp.dot(a_ref[...], wa_ref[...], preferred_element_type=F32)
         + jnp.dot(b_ref[...], wb_ref[...], preferred_element_type=F32))
    o_ref[...] = _layer_norm(ALPHA * h_ref[...] + y, g_ref[...], beta_ref[...])


def _mix_out_call(h, out_a, out_b, w_o, g, beta, tm):
    n, d = h.shape
    w = w_o.astype(BF16)
    row = lambda wd: pl.BlockSpec((tm, wd), lambda i: (i, 0))
    const = lambda shp: pl.BlockSpec(shp, lambda i: (0, 0))
    return pl.pallas_call(
        _mix_out_kernel,
        grid=(n // tm,),
        in_specs=[row(d), row(A_WIDTH), row(B_WIDTH), const((A_WIDTH, d)), const((B_WIDTH, d)),
                  const((1, d)), const((1, d))],
        out_specs=row(d),
        out_shape=jax.ShapeDtypeStruct((n, d), F32),
        compiler_params=_cparams(1),
        name="mix_out",
    )(h, out_a, out_b, w[:A_WIDTH], w[A_WIDTH:], g.reshape(1, d), beta.reshape(1, d))


def _mem_kv_kernel(m_ref, w_ref, k_ref, v_ref):
    x = m_ref[...].astype(BF16)
    d = k_ref.shape[-1]
    k_ref[...] = jnp.dot(x, w_ref[:, :d], preferred_element_type=F32).astype(BF16)
    v_ref[...] = jnp.dot(x, w_ref[:, d:], preferred_element_type=F32).astype(BF16)


def _mem_kv_call(mem2d, wkv, tm):
    n, d = mem2d.shape
    return pl.pallas_call(
        _mem_kv_kernel,
        grid=(n // tm,),
        in_specs=[pl.BlockSpec((tm, d), lambda i: (i, 0)),
                  pl.BlockSpec((d, 2 * d), lambda i: (0, 0))],
        out_specs=[pl.BlockSpec((tm, d), lambda i: (i, 0))] * 2,
        out_shape=[jax.ShapeDtypeStruct((n, d), BF16)] * 2,
        compiler_params=_cparams(1),
        name="mem_kv",
    )(mem2d, wkv.astype(BF16))


def _mem_attn_kernel(h_ref, wq_ref, k_ref, v_ref, wo_ref, g_ref, beta_ref, o_ref):
    h = h_ref[...]
    q = jnp.dot(h.astype(BF16), wq_ref[...], preferred_element_type=F32)
    q = (q * (MEM_HEAD_DIM ** -0.5)).astype(BF16)
    outs = []
    for hd in range(MEM_HEADS):
        sl = slice(hd * MEM_HEAD_DIM, (hd + 1) * MEM_HEAD_DIM)
        sc = lax.dot_general(q[:, sl], k_ref[:, sl], NT_DIMS, preferred_element_type=F32)
        p = jnp.exp(sc - jnp.max(sc, axis=1, keepdims=True))
        den = jnp.sum(p, axis=1, keepdims=True)
        o = jnp.dot(p.astype(BF16), v_ref[:, sl], preferred_element_type=F32) / den
        outs.append(o.astype(BF16))
    y = jnp.dot(jnp.concatenate(outs, axis=1), wo_ref[...], preferred_element_type=F32)
    o_ref[...] = _layer_norm(ALPHA * h + y, g_ref[...], beta_ref[...])


def _mem_attn_call(h, wq, k, v, wo, g, beta, b, s, tm):
    n, d = h.shape
    m = k.shape[0] // b
    const = lambda shp: pl.BlockSpec(shp, lambda bi, i: (0,) * len(shp))
    row = pl.BlockSpec((None, tm, d), lambda bi, i: (bi, i, 0))
    kv = pl.BlockSpec((None, m, d), lambda bi, i: (bi, 0, 0))
    out = pl.pallas_call(
        _mem_attn_kernel,
        grid=(b, s // tm),
        in_specs=[row, const((d, d)), kv, kv, const((d, d)), const((1, d)), const((1, d))],
        out_specs=row,
        out_shape=jax.ShapeDtypeStruct((b, s, d), F32),
        compiler_params=_cparams(2),
        name="mem_attn",
    )(h.reshape(b, s, d), wq.astype(BF16), k.reshape(b, m, d), v.reshape(b, m, d),
      wo.astype(BF16), g.reshape(1, d), beta.reshape(1, d))
    return out.reshape(n, d)


def _router_kernel(h_ref, rw_ref, bias_ref, e_ref, g_ref, sel_ref):
    tm = h_ref.shape[0]
    logits = jnp.dot(h_ref[...].astype(BF16), rw_ref[...], preferred_element_type=F32)
    scores = 1.0 / (1.0 + jnp.exp(-logits))
    lane = lax.broadcasted_iota(I32, (tm, LANES), 1)
    lanef = lane.astype(F32)
    ninf = jnp.float32(-jnp.inf)
    cur = jnp.where(lane < N_EXPERTS, scores + bias_ref[...], ninf)
    top_e = jnp.zeros((tm, LANES), F32)
    top_s = jnp.zeros((tm, LANES), F32)
    picked = jnp.zeros((tm, LANES), F32)
    for k in range(TOP_K):
        mx = jnp.max(cur, axis=1, keepdims=True)
        first = jnp.min(jnp.where(cur == mx, lanef, float(LANES)), axis=1, keepdims=True)
        hit = lanef == first
        s_k = jnp.sum(jnp.where(hit, scores, 0.0), axis=1, keepdims=True)
        top_e = jnp.where(lane == k, first, top_e)
        top_s = jnp.where(lane == k, s_k, top_s)
        picked = jnp.where(hit, 1.0, picked)
        cur = jnp.where(hit, ninf, cur)
    e_ref[...] = top_e.astype(I32)
    g_ref[...] = top_s / jnp.sum(top_s, axis=1, keepdims=True) * ROUTED_SCALE
    sel_ref[...] = picked.astype(I32)


def _router_call(h, router_w, router_bias, tm):
    n, d = h.shape
    rw = jnp.pad(router_w, ((0, 0), (0, LANES - N_EXPERTS))).astype(BF16)
    bias = jnp.pad(router_bias, (0, LANES - N_EXPERTS)).reshape(1, LANES)
    row = pl.BlockSpec((tm, LANES), lambda i: (i, 0))
    return pl.pallas_call(
        _router_kernel,
        grid=(n // tm,),
        in_specs=[pl.BlockSpec((tm, d), lambda i: (i, 0)),
                  pl.BlockSpec((d, LANES), lambda i: (0, 0)),
                  pl.BlockSpec((1, LANES), lambda i: (0, 0))],
        out_specs=[row, row, row],
        out_shape=[jax.ShapeDtypeStruct((n, LANES), I32),
                   jax.ShapeDtypeStruct((n, LANES), F32),
                   jax.ShapeDtypeStruct((n, LANES), I32)],
        compiler_params=_cparams(1),
        name="router",
    )(h, rw, bias)


def _dispatch_kernel(dest_ref, x_ref, xs_ref, sem, *, td):
    def row_copy(r, k):
        return pltpu.make_async_copy(x_ref.at[r], xs_ref.at[dest_ref[r * TOP_K + k]], sem)

    def issue(r, carry):
        for k in range(TOP_K):
            row_copy(r, k).start()
        return carry

    lax.fori_loop(0, td, issue, 0)

    def drain(r, carry):
        for k in range(TOP_K):
            row_copy(r, k).wait()
        return carry

    lax.fori_loop(0, td, drain, 0)


def _dispatch_call(h3, dest, td):
    n = h3.shape[0]
    m = n * TOP_K
    return pl.pallas_call(
        functools.partial(_dispatch_kernel, td=td),
        grid=(n // td,),
        in_specs=[pl.BlockSpec((td * TOP_K,), lambda i: (i,), memory_space=pltpu.SMEM),
                  pl.BlockSpec((td, SUBLANES, LANES), lambda i: (i, 0, 0))],
        out_specs=pl.BlockSpec(memory_space=pl.ANY),
        out_shape=jax.ShapeDtypeStruct((m, SUBLANES, LANES), F32),
        scratch_shapes=[pltpu.SemaphoreType.DMA(())],
        compiler_params=_cparams(1),
        name="dispatch",
    )(dest, h3)


def _moe_mm_kernel(ptile_ref, pexp_ref, plo_ref, phi_ref, xs_ref, wg_ref, wu_ref, wd_ref, y_ref,
                   *, tmm):
    p = pl.program_id(0)
    tile = ptile_ref[p]
    lo = plo_ref[p]
    hi = phi_ref[p]
    first = jnp.logical_or(p == 0, ptile_ref[jnp.maximum(p - 1, 0)] != tile)
    n_sl = D_MODEL // LANES

    @pl.when(first)
    def _():
        y_ref[...] = jnp.zeros(y_ref.shape, F32)

    @pl.when(lo < hi)
    def _():
        x = jnp.concatenate([xs_ref[:, j, :] for j in range(n_sl)], axis=1).astype(BF16)
        g = jnp.dot(x, wg_ref[...].astype(BF16), preferred_element_type=F32)
        u = jnp.dot(x, wu_ref[...].astype(BF16), preferred_element_type=F32)
        hid = (_silu(g) * u).astype(BF16)
        y = jnp.dot(hid, wd_ref[...].astype(BF16), preferred_element_type=F32)
        rowg = tile * tmm + lax.broadcasted_iota(I32, (tmm, 1), 0)
        y = jnp.where(jnp.logical_and(rowg >= lo, rowg < hi), y, 0.0)
        for j in range(n_sl):
            y_ref[:, j, :] += y[:, j * LANES:(j + 1) * LANES]


def _moe_mm_call(xs, pairs, wg, wu, wd, tmm):
    m = xs.shape[0]
    n_pairs = pairs[0].shape[0]
    rows = pl.BlockSpec((tmm, SUBLANES, LANES), lambda p, pt, pe, plo, phi: (pt[p], 0, 0))
    wspec = lambda shp: pl.BlockSpec((None,) + shp, lambda p, pt, pe, plo, phi: (pe[p], 0, 0))
    return pl.pallas_call(
        functools.partial(_moe_mm_kernel, tmm=tmm),
        grid_spec=pltpu.PrefetchScalarGridSpec(
            num_scalar_prefetch=4,
            grid=(n_pairs,),
            in_specs=[rows, wspec((D_MODEL, EXPERT_DIM)), wspec((D_MODEL, EXPERT_DIM)),
                      wspec((EXPERT_DIM, D_MODEL))],
            out_specs=rows),
        out_shape=jax.ShapeDtypeStruct((m, SUBLANES, LANES), F32),
        compiler_params=_cparams(1),
        name="moe_mm",
    )(*pairs, xs, wg, wu, wd)


def _group_pairs(counts, m, tmm):
    n_tiles = m // tmm
    n_pairs = n_tiles + N_EXPERTS
    gend = jnp.cumsum(counts)
    gstart = gend - counts
    t0 = jnp.arange(n_tiles, dtype=I32) * tmm
    e_first = jnp.searchsorted(gend, t0, side='right').astype(I32)
    e_last = jnp.searchsorted(gend, t0 + (tmm - 1), side='right').astype(I32)
    per_tile = e_last - e_first + 1
    pend = jnp.cumsum(per_tile)
    pstart = pend - per_tile
    p = jnp.arange(n_pairs, dtype=I32)
    valid = p < pend[-1]
    tile = jnp.minimum(jnp.searchsorted(pend, p, side='right'), n_tiles - 1).astype(I32)
    e = jnp.clip(e_first[tile] + (p - pstart[tile]), 0, N_EXPERTS - 1).astype(I32)
    lo = jnp.where(valid, gstart[e], 0).astype(I32)
    hi = jnp.where(valid, gend[e], 0).astype(I32)
    return tile, e, lo, hi


def _combine_kernel(dest_ref, gate_ref, h_ref, y_ref, sg_ref, su_ref, sd_ref, g_ref, beta_ref, o_ref,
                    buf_ref, osc_ref, sem, *, tc):
    def row_copy(r, k):
        return pltpu.make_async_copy(y_ref.at[dest_ref[r * TOP_K + k]], buf_ref.at[k, r], sem)

    def issue(r, carry):
        for k in range(TOP_K):
            row_copy(r, k).start()
        return carry

    lax.fori_loop(0, tc, issue, 0)

    h = h_ref[...]
    x = h.astype(BF16)
    hid = (_silu(jnp.dot(x, sg_ref[...], preferred_element_type=F32))
           * jnp.dot(x, su_ref[...], preferred_element_type=F32)).astype(BF16)
    shared = jnp.dot(hid, sd_ref[...], preferred_element_type=F32)

    def drain(r, carry):
        for k in range(TOP_K):
            row_copy(r, k).wait()
        return carry

    lax.fori_loop(0, tc, drain, 0)

    def mix(r, carry):
        acc = gate_ref[r * TOP_K] * buf_ref[0, r]
        for k in range(1, TOP_K):
            acc = acc + gate_ref[r * TOP_K + k] * buf_ref[k, r]
        osc_ref[r] = acc
        return carry

    lax.fori_loop(0, tc, mix, 0)

    routed = jnp.concatenate([osc_ref[:, j, :] for j in range(D_MODEL // LANES)], axis=1)
    o_ref[...] = _layer_norm(ALPHA * h + (routed + shared), g_ref[...], beta_ref[...])


def _combine_call(h, y, dest, gates, sg, su, sd, g, beta, tc):
    n, d = h.shape
    smem = pl.BlockSpec((tc * TOP_K,), lambda i: (i,), memory_space=pltpu.SMEM)
    const = lambda shp: pl.BlockSpec(shp, lambda i: (0, 0))
    return pl.pallas_call(
        functools.partial(_combine_kernel, tc=tc),
        grid=(n // tc,),
        in_specs=[smem, smem, pl.BlockSpec((tc, d), lambda i: (i, 0)),
                  pl.BlockSpec(memory_space=pl.ANY),
                  const((d, SHARED_DIM)), const((d, SHARED_DIM)), const((SHARED_DIM, d)),
                  const((1, d)), const((1, d))],
        out_specs=pl.BlockSpec((tc, d), lambda i: (i, 0)),
        out_shape=jax.ShapeDtypeStruct((n, d), F32),
        scratch_shapes=[pltpu.VMEM((TOP_K, tc, SUBLANES, LANES), F32),
                        pltpu.VMEM((tc, SUBLANES, LANES), F32),
                        pltpu.SemaphoreType.DMA(())],
        compiler_params=_cparams(1),
        name="combine",
    )(dest, gates, h, y, sg.astype(BF16), su.astype(BF16), sd.astype(BF16),
      g.reshape(1, d), beta.reshape(1, d))


def _tiles(b, s):
    n = b * s
    pick = lambda pref, total: next(t for t in (pref, 512, 256, 128, 64, 32, 16, 8) if t <= pref and total % t == 0)
    return dict(
        rows=pick(512, n),
        attn=pick(256, s),
        mem_rows=pick(512, s),
        mem_kv=pick(512, b * 256),
        router=pick(1024, n),
        dispatch=pick(256, n),
        moe=pick(256, n * TOP_K),
        combine=pick(128, n),
    )


def kernel(x, mem, ln_in_g, ln_in_b, w_in, a_kv_norm, a_w_uk, a_w_uv, b_lq1, b_lk1, b_lq2, b_lk2,
           b_subln, w_o, ln1_g, ln1_b, m_wq, m_wkv, m_wo, ln2_g, ln2_b, router_w, router_bias,
           e_w_gate, e_w_up, e_w_down, s_w_gate, s_w_up, s_w_down, ln3_g, ln3_b):
    b, s, d = x.shape
    n = b * s
    m = n * TOP_K
    t = _tiles(b, s)
    mem2d = mem.reshape(-1, d)

    h = _ln_call(x.reshape(n, d), ln_in_g, ln_in_b, t["rows"])
    for l in range(DEPTH):
        lambda_init = 0.8 - 0.6 * math.exp(-0.3 * l)

        qa, ckv, qidx, kk, widx, qb, kb, vb = _proj_call(h, _pad_w_in(w_in[l]), a_kv_norm[l], t["rows"])
        out_a = _dsa_call(qa, qidx, widx, kk, ckv, _block_diag_uk(a_w_uk[l]), _block_diag_uv(a_w_uv[l]),
                          b, s, t["attn"])
        lam_rows = jnp.pad(jnp.stack([b_lq1[l], b_lk1[l], b_lq2[l], b_lk2[l]]),
                           ((0, SUBLANES - 4), (0, LANES - B_QK_DIM)))
        out_b = _diff_call(qb, kb, vb, lam_rows, b_subln[l], b, s, t["attn"], lambda_init)
        h = _mix_out_call(h, out_a, out_b, w_o[l], ln1_g[l], ln1_b[l], t["rows"])

        mk, mv = _mem_kv_call(mem2d, m_wkv[l], t["mem_kv"])
        h = _mem_attn_call(h, m_wq[l], mk, mv, m_wo[l], ln2_g[l], ln2_b[l], b, s, t["mem_rows"])

        top_e, gates, picked = _router_call(h, router_w[l], router_bias[l], t["router"])
        picked = picked[:, :N_EXPERTS]
        incl = jnp.cumsum(picked, axis=0)
        counts = incl[-1]
        gstart = jnp.cumsum(counts) - counts
        slot = (incl - picked) + gstart[None, :]
        dest = jnp.take_along_axis(slot, top_e[:, :TOP_K], axis=1).reshape(m).astype(I32)
        xs = _dispatch_call(h.reshape(n, SUBLANES, LANES), dest, t["dispatch"])
        pairs = _group_pairs(counts, m, t["moe"])
        y = _moe_mm_call(xs, pairs, e_w_gate[l], e_w_up[l], e_w_down[l], t["moe"])
        h = _combine_call(h, y, dest, gates[:, :TOP_K].reshape(m), s_w_gate[l], s_w_up[l], s_w_down[l],
                          ln3_g[l], ln3_b[l], t["combine"])
    return h.reshape(b, s, d)
```

```python
import functools
import math

import jax
import jax.numpy as jnp
import numpy as np
from jax import lax
from jax.experimental import pallas as pl
from jax.experimental.pallas import tpu as pltpu

F32 = jnp.float32
BF16 = jnp.bfloat16
I32 = jnp.int32

D_MODEL = 1024
DEPTH = 2
CHUNK = 64
CHUNK_SHIFT = 6
A_HEADS = 8
A_HEAD_DIM = 64
A_WIDTH = A_HEADS * A_HEAD_DIM
KV_LORA = 128
IDX_HEADS = 8
IDX_DIM = 64
INDEX_TOPK_MAX = 256
B_HEADS = 4
B_QK_DIM = 64
B_V_DIM = 128
B_WIDTH = B_HEADS * B_V_DIM
MEM_HEADS = 4
MEM_HEAD_DIM = D_MODEL // MEM_HEADS
N_EXPERTS = 64
TOP_K = 8
EXPERT_DIM = 256
SHARED_DIM = 256
ROUTED_SCALE = 2.5
ALPHA = (2 * DEPTH) ** 0.25
LN_EPS = 1e-5
RMS_EPS = 1e-6

LANES = 128
SUBLANES = 8
VMEM_LIMIT = 56 * 1024 * 1024
INT_MIN = -2 ** 31
NEG_BIG = -3.0e38
M_INIT = -1.0e30

NT_DIMS = (((1,), (1,)), ((), ()))


def _cparams(n_axes):
    return pltpu.CompilerParams(dimension_semantics=("arbitrary",) * n_axes,
                                vmem_limit_bytes=VMEM_LIMIT)


def _layer_norm(x, g, b):
    mu = jnp.mean(x, axis=-1, keepdims=True)
    xc = x - mu
    var = jnp.mean(xc * xc, axis=-1, keepdims=True)
    return xc * lax.rsqrt(var + LN_EPS) * g + b


def _silu(x):
    return x * (1.0 / (1.0 + jnp.exp(-x)))


def _ln_kernel(x_ref, g_ref, b_ref, o_ref):
    o_ref[...] = _layer_norm(x_ref[...], g_ref[...], b_ref[...])


def _ln_call(x, g, b, tm):
    n, d = x.shape
    return pl.pallas_call(
        _ln_kernel,
        grid=(n // tm,),
        in_specs=[pl.BlockSpec((tm, d), lambda i: (i, 0)),
                  pl.BlockSpec((1, d), lambda i: (0, 0)),
                  pl.BlockSpec((1, d), lambda i: (0, 0))],
        out_specs=pl.BlockSpec((tm, d), lambda i: (i, 0)),
        out_shape=jax.ShapeDtypeStruct((n, d), F32),
        compiler_params=_cparams(1),
        name="ln_in",
    )(x, g.reshape(1, d), b.reshape(1, d))


_SEG_QA = (0, 512)
_SEG_CKV = (512, 640)
_SEG_QIDX = (640, 1152)
_SEG_KK = (1152, 1280)
_SEG_WIDX = (1280, 1408)
_SEG_QB = (1408, 1920)
_SEG_KB = (1920, 2432)
_SEG_VB = (2432, 2944)
_PROJ_COLS = 2944


def _proj_kernel(h_ref, w_ref, kvn_ref, qa_ref, ckv_ref, ckvt_ref, qidx_ref, kk_ref, widx_ref,
                 qb_ref, kb_ref, vbt_ref):
    x = h_ref[...].astype(BF16)

    def seg(s):
        return jnp.dot(x, w_ref[:, s[0]:s[1]], preferred_element_type=F32)

    qa_ref[...] = (seg(_SEG_QA) * (A_HEAD_DIM ** -0.5)).astype(BF16)
    c = seg(_SEG_CKV)
    c = c * lax.rsqrt(jnp.mean(c * c, axis=-1, keepdims=True) + RMS_EPS) * kvn_ref[...]
    ckv_ref[...] = c.astype(BF16)
    ckvt_ref[...] = c.T.astype(BF16)
    qidx_ref[...] = (seg(_SEG_QIDX) * (IDX_DIM ** -0.5)).astype(BF16)
    kk_ref[...] = seg(_SEG_KK).astype(BF16)
    widx_ref[...] = seg(_SEG_WIDX) * (IDX_HEADS ** -0.5)
    qb_ref[...] = (seg(_SEG_QB) * (B_QK_DIM ** -0.5)).astype(BF16)
    kb_ref[...] = seg(_SEG_KB).astype(BF16)
    vbt_ref[...] = seg(_SEG_VB).T.astype(BF16)


def _proj_call(h, w_pad, kvn, tm):
    n, d = h.shape
    outs = [(512, BF16, False), (128, BF16, False), (128, BF16, True), (512, BF16, False),
            (128, BF16, False), (128, F32, False), (512, BF16, False), (512, BF16, False),
            (512, BF16, True)]
    spec = lambda w, t: (pl.BlockSpec((w, tm), lambda i: (0, i)) if t
                         else pl.BlockSpec((tm, w), lambda i: (i, 0)))
    shape = lambda w, dt, t: jax.ShapeDtypeStruct((w, n) if t else (n, w), dt)
    return pl.pallas_call(
        _proj_kernel,
        grid=(n // tm,),
        in_specs=[pl.BlockSpec((tm, d), lambda i: (i, 0)),
                  pl.BlockSpec((d, _PROJ_COLS), lambda i: (0, 0)),
                  pl.BlockSpec((1, KV_LORA), lambda i: (0, 0))],
        out_specs=[spec(w, t) for w, _, t in outs],
        out_shape=[shape(w, dt, t) for w, dt, t in outs],
        compiler_params=_cparams(1),
        name="proj_in",
    )(h, w_pad, kvn.reshape(1, KV_LORA))


def _pad_w_in(w_in):
    sizes = (A_WIDTH, KV_LORA, IDX_HEADS * IDX_DIM, IDX_DIM, IDX_HEADS,
             2 * B_HEADS * B_QK_DIM, 2 * B_HEADS * B_QK_DIM, B_WIDTH)
    splits = np.cumsum(sizes)[:-1].tolist()
    q_a, c_kv, q_idx, k_idx, w_idx, q_b, k_b, v_b = jnp.split(w_in, splits, axis=-1)
    w_idx = jnp.pad(w_idx, ((0, 0), (0, LANES - IDX_HEADS)))
    return jnp.concatenate([q_a, c_kv, q_idx, k_idx, k_idx, w_idx, q_b, k_b, v_b],
                           axis=-1).astype(BF16)


def _flash_probs(sl, sc, m_ref, l_ref):
    m_old = m_ref[:, sl]
    m_new = jnp.maximum(m_old, jnp.max(sc, axis=0, keepdims=True))
    alpha = jnp.exp(m_old - m_new)
    p = jnp.exp(sc - m_new)
    l_ref[:, sl] = alpha * l_ref[:, sl] + jnp.sum(p, axis=0, keepdims=True)
    m_ref[:, sl] = m_new
    return alpha, p.astype(BF16)


def _flash_init(m_ref, l_ref, acc_ref):
    m_ref[...] = jnp.full(m_ref.shape, M_INIT, F32)
    l_ref[...] = jnp.zeros(l_ref.shape, F32)
    acc_ref[...] = jnp.zeros(acc_ref.shape, F32)


def _split_halves(q_ref, n_pairs, out_ref, rows):
    lane = lax.broadcasted_iota(I32, (rows, LANES), 1)
    for j in range(n_pairs):
        qp = q_ref[:, j * LANES:(j + 1) * LANES].astype(F32)
        out_ref[2 * j * rows:(2 * j + 1) * rows, :] = jnp.where(lane < 64, qp, 0.0).astype(BF16)
        out_ref[(2 * j + 1) * rows:(2 * j + 2) * rows, :] = jnp.where(lane >= 64, qp, 0.0).astype(BF16)


def _dsa_kernel(qa_ref, qidx_ref, widx_ref, kk_ref, ckv_ref, ckvt_ref, wk_ref, wvt_ref, out_ref,
                keys_ref, qim_ref, qlat_ref, tj_ref, m_ref, l_ref, acc_ref, *, tq, tk, seq, topk):
    i = pl.program_id(1)
    n_kt = (i + 1) * (tq // tk)
    idx_bits = int(math.log2(seq))

    _split_halves(qidx_ref, IDX_HEADS // 2, qim_ref, tq)
    qlat = jnp.dot(qa_ref[...], wk_ref[...], preferred_element_type=F32)
    for h in range(A_HEADS):
        qlat_ref[h * tq:(h + 1) * tq, :] = qlat[:, h * KV_LORA:(h + 1) * KV_LORA].astype(BF16)
    w_t = widx_ref[...].T

    row = lax.broadcasted_iota(I32, (tk, tq), 0)
    tqi = i * tq + lax.broadcasted_iota(I32, (tk, tq), 1)
    qchunk = lax.shift_right_logical(tqi, CHUNK_SHIFT)

    def idx_body(kt, carry):
        off = pl.multiple_of(kt * tk, tk)
        kkt = kk_ref[pl.ds(off, tk), :]
        s_all = lax.dot_general(kkt, qim_ref[...], NT_DIMS, preferred_element_type=F32)
        acc = jnp.zeros((tk, tq), F32)
        for h in range(IDX_HEADS):
            acc = acc + w_t[h:h + 1, :] * jnp.maximum(s_all[:, h * tq:(h + 1) * tq], 0.0)
        bits = lax.bitcast_convert_type(acc + 0.0, I32)
        key = jnp.where(bits < 0, bits ^ jnp.int32(0x7FFFFFFF), bits)
        adm = lax.shift_right_logical(off + row, CHUNK_SHIFT) <= qchunk
        keys_ref[pl.ds(off, tk), :] = jnp.where(adm, key, jnp.int32(INT_MIN))
        return carry

    lax.fori_loop(0, n_kt, idx_body, 0)

    def count(pred):
        def body(kt, c):
            off = pl.multiple_of(kt * tk, tk)
            ind = jnp.where(pred(keys_ref[pl.ds(off, tk), :], off + row), 1.0, 0.0)
            return c + jnp.sum(ind, axis=0, keepdims=True)
        return lax.fori_loop(0, n_kt, body, jnp.zeros((1, tq), F32))

    def search_body(it, t):
        cand = t + lax.shift_left(jnp.int32(1), 31 - it)
        cnt = count(lambda tile, _: tile >= cand)
        return jnp.where(cnt >= topk, cand, t)

    thr = lax.fori_loop(0, 32, search_body, jnp.full((1, tq), INT_MIN, I32))

    live = thr > INT_MIN
    n_ge = count(lambda tile, _: tile >= thr)
    excess = jnp.logical_and(n_ge > topk, live)
    all_ties = jnp.where(live, seq, -1).astype(I32)
    tj_ref[0] = jnp.broadcast_to(thr, (SUBLANES, tq))
    tj_ref[1] = jnp.broadcast_to(all_ties, (SUBLANES, tq))

    @pl.when(jnp.max(jnp.where(excess, 1.0, 0.0)) > 0.0)
    def _():
        need = topk - count(lambda tile, _: tile > thr)

        def tie_body(it, hi):
            cand = hi + lax.shift_left(jnp.int32(1), idx_bits - 1 - it)
            cnt = count(lambda tile, pos: jnp.logical_and(tile == thr, pos < cand))
            return jnp.where(cnt < need, cand, hi)

        hi = lax.fori_loop(0, idx_bits, tie_body, jnp.zeros((1, tq), I32))
        tj_ref[1] = jnp.broadcast_to(jnp.where(excess, hi, all_ties), (SUBLANES, tq))

    thr_b = tj_ref[0][:1]
    tie_hi = tj_ref[1][:1]

    _flash_init(m_ref, l_ref, acc_ref)

    def att_body(kt, carry):
        off = pl.multiple_of(kt * tk, tk)
        tile = keys_ref[pl.ds(off, tk), :]
        pos = off + row
        sel = jnp.logical_or(tile > thr_b,
                             jnp.logical_and(tile == thr_b, pos <= tie_hi))
        dist = jnp.abs(tqi - pos).astype(F32)
        ckt = ckv_ref[pl.ds(off, tk), :]
        s_all = lax.dot_general(ckt, qlat_ref[...], NT_DIMS, preferred_element_type=F32)
        alphas, probs = [], []
        for h in range(A_HEADS):
            sl = slice(h * tq, (h + 1) * tq)
            sc = jnp.where(sel, s_all[:, sl] - (2.0 ** -(h + 1)) * dist, NEG_BIG)
            alpha, p = _flash_probs(sl, sc, m_ref, l_ref)
            alphas.append(alpha)
            probs.append(p)
        pv = jnp.dot(ckvt_ref[:, pl.ds(off, tk)], jnp.concatenate(probs, axis=1),
                     preferred_element_type=F32)
        acc_ref[...] = jnp.concatenate(alphas, axis=1) * acc_ref[...] + pv
        return carry

    lax.fori_loop(0, n_kt, att_body, 0)

    inv_l = 1.0 / l_ref[...]
    olat = jnp.concatenate([(acc_ref[:, h * tq:(h + 1) * tq] * inv_l[:, h * tq:(h + 1) * tq]).astype(BF16)
                            for h in range(A_HEADS)], axis=0)
    out_t = jnp.dot(wvt_ref[...], olat, preferred_element_type=F32)
    out_ref[...] = out_t.T.astype(BF16)


def _dsa_call(qa, qidx, widx, kk, ckv, ckvt, wk_bd, wvt_bd, b, s, tq, tk):
    topk = min(INDEX_TOPK_MAX, s // 4)
    assert s % tq == 0 and tq % tk == 0 and tk % LANES == 0 and s & (s - 1) == 0
    blk = lambda w: pl.BlockSpec((None, tq, w), lambda bi, i: (bi, i, 0))
    full = lambda w: pl.BlockSpec((None, s, w), lambda bi, i: (bi, 0, 0))
    const = lambda a: pl.BlockSpec(a.shape, lambda bi, i: (0, 0))
    r3 = lambda a: a.reshape(b, s, a.shape[-1])
    out = pl.pallas_call(
        functools.partial(_dsa_kernel, tq=tq, tk=tk, seq=s, topk=topk),
        grid=(b, s // tq),
        in_specs=[blk(A_WIDTH), blk(IDX_HEADS * IDX_DIM), blk(LANES), full(LANES), full(KV_LORA),
                  pl.BlockSpec((KV_LORA, s), lambda bi, i: (0, bi)),
                  const(wk_bd), const(wvt_bd)],
        out_specs=blk(A_WIDTH),
        out_shape=jax.ShapeDtypeStruct((b, s, A_WIDTH), BF16),
        scratch_shapes=[pltpu.VMEM((s, tq), I32),
                        pltpu.VMEM((IDX_HEADS * tq, LANES), BF16),
                        pltpu.VMEM((A_HEADS * tq, KV_LORA), BF16),
                        pltpu.VMEM((2, SUBLANES, tq), I32),
                        pltpu.VMEM((1, A_HEADS * tq), F32),
                        pltpu.VMEM((1, A_HEADS * tq), F32),
                        pltpu.VMEM((KV_LORA, A_HEADS * tq), F32)],
        compiler_params=_cparams(2),
        name="dsa",
    )(r3(qa), r3(qidx), r3(widx), r3(kk), r3(ckv), ckvt, wk_bd, wvt_bd)
    return out.reshape(b * s, A_WIDTH)


def _block_diag_uk(w_uk):
    h, c, d = w_uk.shape
    eye = jnp.eye(h, dtype=w_uk.dtype)
    return jnp.einsum('hcd,hg->hdgc', w_uk, eye).reshape(h * d, h * c).astype(BF16)


def _block_diag_uv_t(w_uv):
    h, c, d = w_uv.shape
    eye = jnp.eye(h, dtype=w_uv.dtype)
    return jnp.einsum('hcd,hg->hdgc', w_uv, eye).reshape(h * d, h * c).astype(BF16)


def _diff_kernel(qb_ref, kb_ref, vbt_ref, lam_ref, subln_ref, out_ref,
                 qm_ref, m_ref, l_ref, acc_ref, *, tq, tk, lambda_init):
    i = pl.program_id(1)
    _split_halves(qb_ref, B_HEADS, qm_ref, tq)

    row = lax.broadcasted_iota(I32, (tk, tq), 0)
    tqi = i * tq + lax.broadcasted_iota(I32, (tk, tq), 1)
    qchunk = lax.shift_right_logical(tqi, CHUNK_SHIFT)
    _flash_init(m_ref, l_ref, acc_ref)

    def body(kt, carry):
        off = pl.multiple_of(kt * tk, tk)
        pos = off + row
        adm = lax.shift_right_logical(pos, CHUNK_SHIFT) <= qchunk
        dist = jnp.abs(tqi - pos).astype(F32)
        s_pairs = [lax.dot_general(kb_ref[pl.ds(off, tk), h * LANES:(h + 1) * LANES],
                                   qm_ref[2 * h * tq:(2 * h + 2) * tq, :], NT_DIMS,
                                   preferred_element_type=F32) for h in range(B_HEADS)]
        alphas, probs = [], []
        for h in range(B_HEADS):
            for j in range(2):
                sl = slice((2 * h + j) * tq, (2 * h + j + 1) * tq)
                sc = jnp.where(adm, s_pairs[h][:, j * tq:(j + 1) * tq]
                               - (2.0 ** (-2 * (h + 1))) * dist, NEG_BIG)
                alpha, p = _flash_probs(sl, sc, m_ref, l_ref)
                alphas.append(alpha)
                probs.append(p)
        pv = jnp.concatenate(
            [jnp.dot(vbt_ref[h * B_V_DIM:(h + 1) * B_V_DIM, pl.ds(off, tk)],
                     jnp.concatenate(probs[2 * h:2 * h + 2], axis=1), preferred_element_type=F32)
             for h in range(B_HEADS)], axis=1)
        acc_ref[...] = jnp.concatenate(alphas, axis=1) * acc_ref[...] + pv
        return carry

    lax.fori_loop(0, (i + 1) * (tq // tk), body, 0)

    lp = lam_ref[...]
    lam = (jnp.exp(jnp.sum(lp[0:1] * lp[1:2], axis=1, keepdims=True))
           - jnp.exp(jnp.sum(lp[2:3] * lp[3:4], axis=1, keepdims=True)) + lambda_init)
    o_all = acc_ref[...] * (1.0 / l_ref[...])
    outs = []
    for h in range(B_HEADS):
        o = (o_all[:, 2 * h * tq:(2 * h + 1) * tq]
             - lam * o_all[:, (2 * h + 1) * tq:(2 * h + 2) * tq])
        o = o * lax.rsqrt(jnp.mean(o * o, axis=0, keepdims=True) + RMS_EPS) * subln_ref[...]
        outs.append(o * (1.0 - lambda_init))
    out_ref[...] = jnp.concatenate(outs, axis=0).T.astype(BF16)


def _diff_call(qb, kb, vbt, lam_rows, subln, b, s, tq, tk, lambda_init):
    blk = pl.BlockSpec((None, tq, B_WIDTH), lambda bi, i: (bi, i, 0))
    full = pl.BlockSpec((None, s, B_WIDTH), lambda bi, i: (bi, 0, 0))
    r3 = lambda a: a.reshape(b, s, a.shape[-1])
    out = pl.pallas_call(
        functools.partial(_diff_kernel, tq=tq, tk=tk, lambda_init=lambda_init),
        grid=(b, s // tq),
        in_specs=[blk, full, pl.BlockSpec((B_WIDTH, s), lambda bi, i: (0, bi)),
                  pl.BlockSpec((SUBLANES, LANES), lambda bi, i: (0, 0)),
                  pl.BlockSpec((B_V_DIM, 1), lambda bi, i: (0, 0))],
        out_specs=blk,
        out_shape=jax.ShapeDtypeStruct((b, s, B_WIDTH), BF16),
        scratch_shapes=[pltpu.VMEM((2 * B_HEADS * tq, LANES), BF16),
                        pltpu.VMEM((1, 2 * B_HEADS * tq), F32),
                        pltpu.VMEM((1, 2 * B_HEADS * tq), F32),
                        pltpu.VMEM((B_V_DIM, 2 * B_HEADS * tq), F32)],
        compiler_params=_cparams(2),
        name="diff",
    )(r3(qb), r3(kb), vbt, lam_rows, subln.reshape(B_V_DIM, 1))
    return out.reshape(b * s, B_WIDTH)


def _mix_out_kernel(h_ref, a_ref, b_ref, wa_ref, wb_ref, g_ref, beta_ref, o_ref):
    y = (jnp.dot(a_ref[...], wa_ref[...], preferred_element_type=F32)
         + jnp.dot(b_ref[...], wb_ref[...], preferred_element_type=F32))
    o_ref[...] = _layer_norm(ALPHA * h_ref[...] + y, g_ref[...], beta_ref[...])


def _mix_out_call(h, out_a, out_b, w_o, g, beta, tm):
    n, d = h.shape
    w = w_o.astype(BF16)
    row = lambda wd: pl.BlockSpec((tm, wd), lambda i: (i, 0))
    const = lambda shp: pl.BlockSpec(shp, lambda i: (0, 0))
    return pl.pallas_call(
        _mix_out_kernel,
        grid=(n // tm,),
        in_specs=[row(d), row(A_WIDTH), row(B_WIDTH), const((A_WIDTH, d)), const((B_WIDTH, d)),
                  const((1, d)), const((1, d))],
        out_specs=row(d),
        out_shape=jax.ShapeDtypeStruct((n, d), F32),
        compiler_params=_cparams(1),
        name="mix_out",
    )(h, out_a, out_b, w[:A_WIDTH], w[A_WIDTH:], g.reshape(1, d), beta.reshape(1, d))


def _mem_kv_kernel(m_ref, w_ref, k_ref, v_ref):
    x = m_ref[...].astype(BF16)
    d = k_ref.shape[-1]
    k_ref[...] = jnp.dot(x, w_ref[:, :d], preferred_element_type=F32).astype(BF16)
    v_ref[...] = jnp.dot(x, w_ref[:, d:], preferred_element_type=F32).astype(BF16)


def _mem_kv_call(mem2d, wkv, tm):
    n, d = mem2d.shape
    return pl.pallas_call(
        _mem_kv_kernel,
        grid=(n // tm,),
        in_specs=[pl.BlockSpec((tm, d), lambda i: (i, 0)),
                  pl.BlockSpec((d, 2 * d), lambda i: (0, 0))],
        out_specs=[pl.BlockSpec((tm, d), lambda i: (i, 0))] * 2,
        out_shape=[jax.ShapeDtypeStruct((n, d), BF16)] * 2,
        compiler_params=_cparams(1),
        name="mem_kv",
    )(mem2d, wkv.astype(BF16))


def _mem_attn_kernel(h_ref, wq_ref, k_ref, v_ref, wo_ref, g_ref, beta_ref, o_ref):
    h = h_ref[...]
    q = jnp.dot(h.astype(BF16), wq_ref[...], preferred_element_type=F32)
    q = (q * (MEM_HEAD_DIM ** -0.5)).astype(BF16)
    outs = []
    for hd in range(MEM_HEADS):
        sl = slice(hd * MEM_HEAD_DIM, (hd + 1) * MEM_HEAD_DIM)
        sc = lax.dot_general(q[:, sl], k_ref[:, sl], NT_DIMS, preferred_element_type=F32)
        p = jnp.exp(sc - jnp.max(sc, axis=1, keepdims=True))
        den = jnp.sum(p, axis=1, keepdims=True)
        o = jnp.dot(p.astype(BF16), v_ref[:, sl], preferred_element_type=F32) / den
        outs.append(o.astype(BF16))
    y = jnp.dot(jnp.concatenate(outs, axis=1), wo_ref[...], preferred_element_type=F32)
    o_ref[...] = _layer_norm(ALPHA * h + y, g_ref[...], beta_ref[...])


def _mem_attn_call(h, wq, k, v, wo, g, beta, b, s, tm):
    n, d = h.shape
    m = k.shape[0] // b
    const = lambda shp: pl.BlockSpec(shp, lambda bi, i: (0,) * len(shp))
    row = pl.BlockSpec((None, tm, d), lambda bi, i: (bi, i, 0))
    kv = pl.BlockSpec((None, m, d), lambda bi, i: (bi, 0, 0))
    out = pl.pallas_call(
        _mem_attn_kernel,
        grid=(b, s // tm),
        in_specs=[row, const((d, d)), kv, kv, const((d, d)), const((1, d)), const((1, d))],
        out_specs=row,
        out_shape=jax.ShapeDtypeStruct((b, s, d), F32),
        compiler_params=_cparams(2),
        name="mem_attn",
    )(h.reshape(b, s, d), wq.astype(BF16), k.reshape(b, m, d), v.reshape(b, m, d),
      wo.astype(BF16), g.reshape(1, d), beta.reshape(1, d))
    return out.reshape(n, d)


def _router_kernel(h_ref, rw_ref, bias_ref, tri_ref, e_ref, g_ref, rank_ref, cnt_ref, run_ref):
    tm = h_ref.shape[0]

    @pl.when(pl.program_id(0) == 0)
    def _():
        run_ref[...] = jnp.zeros(run_ref.shape, F32)

    logits = jnp.dot(h_ref[...].astype(BF16), rw_ref[...], preferred_element_type=F32)
    scores = 1.0 / (1.0 + jnp.exp(-logits))
    lane = lax.broadcasted_iota(I32, (tm, LANES), 1)
    lanef = lane.astype(F32)
    ninf = jnp.float32(-jnp.inf)
    cur = jnp.where(lane < N_EXPERTS, scores + bias_ref[...], ninf)
    top_e = jnp.zeros((tm, LANES), F32)
    top_s = jnp.zeros((tm, LANES), F32)
    picked = jnp.zeros((tm, LANES), F32)
    firsts = []
    for k in range(TOP_K):
        mx = jnp.max(cur, axis=1, keepdims=True)
        first = jnp.min(jnp.where(cur == mx, lanef, float(LANES)), axis=1, keepdims=True)
        hit = lanef == first
        s_k = jnp.sum(jnp.where(hit, scores, 0.0), axis=1, keepdims=True)
        top_e = jnp.where(lane == k, first, top_e)
        top_s = jnp.where(lane == k, s_k, top_s)
        picked = jnp.where(hit, 1.0, picked)
        cur = jnp.where(hit, ninf, cur)
        firsts.append(first)
    e_ref[...] = top_e.astype(I32)
    g_ref[...] = top_s / jnp.sum(top_s, axis=1, keepdims=True) * ROUTED_SCALE

    slot = jnp.dot(tri_ref[...], picked.astype(BF16), preferred_element_type=F32) + run_ref[...]
    rank = jnp.zeros((tm, LANES), F32)
    for k in range(TOP_K):
        r_k = jnp.sum(jnp.where(lanef == firsts[k], slot, 0.0), axis=1, keepdims=True)
        rank = jnp.where(lane == k, r_k, rank)
    rank_ref[...] = rank.astype(I32)
    run = run_ref[...] + jnp.sum(picked, axis=0, keepdims=True)
    run_ref[...] = run
    cnt_ref[...] = run.astype(I32)


def _router_call(h, router_w, router_bias, tm):
    n, d = h.shape
    rw = jnp.pad(router_w, ((0, 0), (0, LANES - N_EXPERTS))).astype(BF16)
    bias = jnp.pad(router_bias, (0, LANES - N_EXPERTS)).reshape(1, LANES)
    tri = jnp.tri(tm, k=-1, dtype=BF16)
    row = pl.BlockSpec((tm, LANES), lambda i: (i, 0))
    const = lambda shp: pl.BlockSpec(shp, lambda i: (0, 0))
    return pl.pallas_call(
        _router_kernel,
        grid=(n // tm,),
        in_specs=[pl.BlockSpec((tm, d), lambda i: (i, 0)), const((d, LANES)), const((1, LANES)),
                  const((tm, tm))],
        out_specs=[row, row, row, const((1, LANES))],
        out_shape=[jax.ShapeDtypeStruct((n, LANES), I32),
                   jax.ShapeDtypeStruct((n, LANES), F32),
                   jax.ShapeDtypeStruct((n, LANES), I32),
                   jax.ShapeDtypeStruct((1, LANES), I32)],
        scratch_shapes=[pltpu.VMEM((1, LANES), F32)],
        compiler_params=_cparams(1),
        name="router",
    )(h, rw, bias, tri)


def _dispatch_kernel(dest_ref, x_ref, xs_ref, sem, *, td):
    def row_copy(r, k):
        return pltpu.make_async_copy(x_ref.at[r], xs_ref.at[dest_ref[r * TOP_K + k]], sem)

    def issue(r, carry):
        for k in range(TOP_K):
            row_copy(r, k).start()
        return carry

    lax.fori_loop(0, td, issue, 0)

    def drain(r, carry):
        for k in range(TOP_K):
            row_copy(r, k).wait()
        return carry

    lax.fori_loop(0, td, drain, 0)


def _dispatch_call(h3, dest, td):
    n = h3.shape[0]
    m = n * TOP_K
    return pl.pallas_call(
        functools.partial(_dispatch_kernel, td=td),
        grid=(n // td,),
        in_specs=[pl.BlockSpec((td * TOP_K,), lambda i: (i,), memory_space=pltpu.SMEM),
                  pl.BlockSpec((td, SUBLANES, LANES), lambda i: (i, 0, 0))],
        out_specs=pl.BlockSpec(memory_space=pl.ANY),
        out_shape=jax.ShapeDtypeStruct((m, SUBLANES, LANES), F32),
        scratch_shapes=[pltpu.SemaphoreType.DMA(())],
        compiler_params=_cparams(1),
        name="dispatch",
    )(dest, h3)


def _moe_mm_kernel(ptile_ref, pexp_ref, plo_ref, phi_ref, xs_ref, wg_ref, wu_ref, wd_ref, y_ref,
                   *, tmm):
    p = pl.program_id(0)
    tile = ptile_ref[p]
    lo = plo_ref[p]
    hi = phi_ref[p]
    first = jnp.logical_or(p == 0, ptile_ref[jnp.maximum(p - 1, 0)] != tile)
    n_sl = D_MODEL // LANES

    @pl.when(first)
    def _():
        y_ref[...] = jnp.zeros(y_ref.shape, F32)

    @pl.when(lo < hi)
    def _():
        x = jnp.concatenate([xs_ref[:, j, :] for j in range(n_sl)], axis=1).astype(BF16)
        g = jnp.dot(x, wg_ref[...].astype(BF16), preferred_element_type=F32)
        u = jnp.dot(x, wu_ref[...].astype(BF16), preferred_element_type=F32)
        hid = (_silu(g) * u).astype(BF16)
        y = jnp.dot(hid, wd_ref[...].astype(BF16), preferred_element_type=F32)
        rowg = tile * tmm + lax.broadcasted_iota(I32, (tmm, 1), 0)
        y = jnp.where(jnp.logical_and(rowg >= lo, rowg < hi), y, 0.0)
        for j in range(n_sl):
            y_ref[:, j, :] += y[:, j * LANES:(j + 1) * LANES]


def _moe_mm_call(xs, pairs, wg, wu, wd, tmm):
    m = xs.shape[0]
    n_pairs = pairs[0].shape[0]
    rows = pl.BlockSpec((tmm, SUBLANES, LANES), lambda p, pt, pe, plo, phi: (pt[p], 0, 0))
    wspec = lambda shp: pl.BlockSpec((None,) + shp, lambda p, pt, pe, plo, phi: (pe[p], 0, 0))
    return pl.pallas_call(
        functools.partial(_moe_mm_kernel, tmm=tmm),
        grid_spec=pltpu.PrefetchScalarGridSpec(
            num_scalar_prefetch=4,
            grid=(n_pairs,),
            in_specs=[rows, wspec((D_MODEL, EXPERT_DIM)), wspec((D_MODEL, EXPERT_DIM)),
                      wspec((EXPERT_DIM, D_MODEL))],
            out_specs=rows),
        out_shape=jax.ShapeDtypeStruct((m, SUBLANES, LANES), F32),
        compiler_params=_cparams(1),
        name="moe_mm",
    )(*pairs, xs, wg, wu, wd)


def _group_pairs(counts, m, tmm):
    n_tiles = m // tmm
    n_pairs = n_tiles + N_EXPERTS
    gend = jnp.cumsum(counts)
    gstart = gend - counts
    t0 = jnp.arange(n_tiles, dtype=I32) * tmm
    n_le = lambda edges, v: jnp.sum(edges[None, :] <= v[:, None], axis=1).astype(I32)
    e_first = n_le(gend, t0)
    e_last = n_le(gend, t0 + (tmm - 1))
    per_tile = e_last - e_first + 1
    pend = jnp.cumsum(per_tile)
    pstart = pend - per_tile
    p = jnp.arange(n_pairs, dtype=I32)
    valid = p < pend[-1]
    tile = jnp.minimum(n_le(pend, p), n_tiles - 1)
    e = jnp.clip(e_first[tile] + (p - pstart[tile]), 0, N_EXPERTS - 1).astype(I32)
    lo = jnp.where(valid, gstart[e], 0).astype(I32)
    hi = jnp.where(valid, gend[e], 0).astype(I32)
    return tile, e, lo, hi


def _combine_kernel(dest_ref, gate_ref, h_ref, y_ref, sg_ref, su_ref, sd_ref, g_ref, beta_ref, o_ref,
                    buf_ref, osc_ref, sem, *, tc):
    def row_copy(r, k):
        return pltpu.make_async_copy(y_ref.at[dest_ref[r * TOP_K + k]], buf_ref.at[k, r], sem)

    def issue(r, carry):
        for k in range(TOP_K):
            row_copy(r, k).start()
        return carry

    lax.fori_loop(0, tc, issue, 0)

    h = h_ref[...]
    x = h.astype(BF16)
    hid = (_silu(jnp.dot(x, sg_ref[...], preferred_element_type=F32))
           * jnp.dot(x, su_ref[...], preferred_element_type=F32)).astype(BF16)
    shared = jnp.dot(hid, sd_ref[...], preferred_element_type=F32)

    def drain(r, carry):
        for k in range(TOP_K):
            row_copy(r, k).wait()
        return carry

    lax.fori_loop(0, tc, drain, 0)

    def mix(r, carry):
        acc = gate_ref[r * TOP_K] * buf_ref[0, r]
        for k in range(1, TOP_K):
            acc = acc + gate_ref[r * TOP_K + k] * buf_ref[k, r]
        osc_ref[r] = acc
        return carry

    lax.fori_loop(0, tc, mix, 0)

    routed = jnp.concatenate([osc_ref[:, j, :] for j in range(D_MODEL // LANES)], axis=1)
    o_ref[...] = _layer_norm(ALPHA * h + (routed + shared), g_ref[...], beta_ref[...])


def _combine_call(h, y, dest, gates, sg, su, sd, g, beta, tc):
    n, d = h.shape
    smem = pl.BlockSpec((tc * TOP_K,), lambda i: (i,), memory_space=pltpu.SMEM)
    const = lambda shp: pl.BlockSpec(shp, lambda i: (0, 0))
    return pl.pallas_call(
        functools.partial(_combine_kernel, tc=tc),
        grid=(n // tc,),
        in_specs=[smem, smem, pl.BlockSpec((tc, d), lambda i: (i, 0)),
                  pl.BlockSpec(memory_space=pl.ANY),
                  const((d, SHARED_DIM)), const((d, SHARED_DIM)), const((SHARED_DIM, d)),
                  const((1, d)), const((1, d))],
        out_specs=pl.BlockSpec((tc, d), lambda i: (i, 0)),
        out_shape=jax.ShapeDtypeStruct((n, d), F32),
        scratch_shapes=[pltpu.VMEM((TOP_K, tc, SUBLANES, LANES), F32),
                        pltpu.VMEM((tc, SUBLANES, LANES), F32),
                        pltpu.SemaphoreType.DMA(())],
        compiler_params=_cparams(1),
        name="combine",
    )(dest, gates, h, y, sg.astype(BF16), su.astype(BF16), sd.astype(BF16),
      g.reshape(1, d), beta.reshape(1, d))


def _tiles(b, s):
    n = b * s
    pick = lambda pref, total: next(t for t in (pref, 512, 256, 128, 64, 32, 16, 8) if t <= pref and total % t == 0)
    return dict(
        rows=pick(512, n),
        attn_q=pick(256, s),
        attn_k=pick(256, s),
        mem_rows=pick(512, s),
        mem_kv=pick(512, b * 256),
        router=pick(1024, n),
        dispatch=pick(256, n),
        moe=pick(256, n * TOP_K),
        combine=pick(128, n),
    )


def kernel(x, mem, ln_in_g, ln_in_b, w_in, a_kv_norm, a_w_uk, a_w_uv, b_lq1, b_lk1, b_lq2, b_lk2,
           b_subln, w_o, ln1_g, ln1_b, m_wq, m_wkv, m_wo, ln2_g, ln2_b, router_w, router_bias,
           e_w_gate, e_w_up, e_w_down, s_w_gate, s_w_up, s_w_down, ln3_g, ln3_b):
    b, s, d = x.shape
    n = b * s
    m = n * TOP_K
    t = _tiles(b, s)
    mem2d = mem.reshape(-1, d)

    h = _ln_call(x.reshape(n, d), ln_in_g, ln_in_b, t["rows"])
    for l in range(DEPTH):
        lambda_init = 0.8 - 0.6 * math.exp(-0.3 * l)

        qa, ckv, ckvt, qidx, kk, widx, qb, kb, vbt = _proj_call(h, _pad_w_in(w_in[l]), a_kv_norm[l],
                                                                  t["rows"])
        out_a = _dsa_call(qa, qidx, widx, kk, ckv, ckvt, _block_diag_uk(a_w_uk[l]),
                          _block_diag_uv_t(a_w_uv[l]), b, s, t["attn_q"], t["attn_k"])
        lam_rows = jnp.pad(jnp.stack([b_lq1[l], b_lk1[l], b_lq2[l], b_lk2[l]]),
                           ((0, SUBLANES - 4), (0, LANES - B_QK_DIM)))
        out_b = _diff_call(qb, kb, vbt, lam_rows, b_subln[l], b, s, t["attn_q"], t["attn_k"], lambda_init)
        h = _mix_out_call(h, out_a, out_b, w_o[l], ln1_g[l], ln1_b[l], t["rows"])

        mk, mv = _mem_kv_call(mem2d, m_wkv[l], t["mem_kv"])
        h = _mem_attn_call(h, m_wq[l], mk, mv, m_wo[l], ln2_g[l], ln2_b[l], b, s, t["mem_rows"])

        top_e, gates, rank, counts = _router_call(h, router_w[l], router_bias[l], t["router"])
        counts = counts[0, :N_EXPERTS]
        gstart = jnp.cumsum(counts) - counts
        top_e = top_e[:, :TOP_K]
        dest = (rank[:, :TOP_K] + gstart[top_e]).reshape(m)
        xs = _dispatch_call(h.reshape(n, SUBLANES, LANES), dest, t["dispatch"])
        pairs = _group_pairs(counts, m, t["moe"])
        y = _moe_mm_call(xs, pairs, e_w_gate[l], e_w_up[l], e_w_down[l], t["moe"])
        h = _combine_call(h, y, dest, gates[:, :TOP_K].reshape(m), s_w_gate[l], s_w_up[l], s_w_down[l],
                          ln3_g[l], ln3_b[l], t["combine"])
    return h.reshape(b, s, d)
```

```python
import functools
import math

import jax
import jax.numpy as jnp
import numpy as np
from jax import lax
from jax.experimental import pallas as pl
from jax.experimental.pallas import tpu as pltpu

F32 = jnp.float32
BF16 = jnp.bfloat16
I32 = jnp.int32

D_MODEL = 1024
DEPTH = 2
CHUNK = 64
CHUNK_SHIFT = 6
A_HEADS = 8
A_HEAD_DIM = 64
A_WIDTH = A_HEADS * A_HEAD_DIM
KV_LORA = 128
IDX_HEADS = 8
IDX_DIM = 64
INDEX_TOPK_MAX = 256
B_HEADS = 4
B_QK_DIM = 64
B_V_DIM = 128
B_WIDTH = B_HEADS * B_V_DIM
MEM_HEADS = 4
MEM_HEAD_DIM = D_MODEL // MEM_HEADS
N_EXPERTS = 64
TOP_K = 8
EXPERT_DIM = 256
SHARED_DIM = 256
ROUTED_SCALE = 2.5
ALPHA = (2 * DEPTH) ** 0.25
LN_EPS = 1e-5
RMS_EPS = 1e-6

LANES = 128
SUBLANES = 8
VMEM_LIMIT = 56 * 1024 * 1024
INT_MIN = -2 ** 31
NEG_BIG = -3.0e38
M_INIT = -1.0e30

NT_DIMS = (((1,), (1,)), ((), ()))


def _cparams(n_axes):
    return pltpu.CompilerParams(dimension_semantics=("arbitrary",) * n_axes,
                                vmem_limit_bytes=VMEM_LIMIT)


def _layer_norm(x, g, b):
    mu = jnp.mean(x, axis=-1, keepdims=True)
    xc = x - mu
    var = jnp.mean(xc * xc, axis=-1, keepdims=True)
    return xc * lax.rsqrt(var + LN_EPS) * g + b


def _silu(x):
    return x * (1.0 / (1.0 + jnp.exp(-x)))


def _ln_kernel(x_ref, g_ref, b_ref, o_ref):
    o_ref[...] = _layer_norm(x_ref[...], g_ref[...], b_ref[...])


def _ln_call(x, g, b, tm):
    n, d = x.shape
    return pl.pallas_call(
        _ln_kernel,
        grid=(n // tm,),
        in_specs=[pl.BlockSpec((tm, d), lambda i: (i, 0)),
                  pl.BlockSpec((1, d), lambda i: (0, 0)),
                  pl.BlockSpec((1, d), lambda i: (0, 0))],
        out_specs=pl.BlockSpec((tm, d), lambda i: (i, 0)),
        out_shape=jax.ShapeDtypeStruct((n, d), F32),
        compiler_params=_cparams(1),
        name="ln_in",
    )(x, g.reshape(1, d), b.reshape(1, d))


_SEG_QA = (0, 512)
_SEG_CKV = (512, 640)
_SEG_QIDX = (640, 1152)
_SEG_KK = (1152, 1280)
_SEG_WIDX = (1280, 1408)
_SEG_QB = (1408, 1920)
_SEG_KB = (1920, 2432)
_SEG_VB = (2432, 2944)
_PROJ_COLS = 2944


def _proj_kernel(h_ref, w_ref, kvn_ref, qa_ref, ckv_ref, ckvt_ref, qidx_ref, kk_ref, widx_ref,
                 qb_ref, kb_ref, vbt_ref):
    x = h_ref[...].astype(BF16)

    def seg(s):
        return jnp.dot(x, w_ref[:, s[0]:s[1]], preferred_element_type=F32)

    qa_ref[...] = (seg(_SEG_QA) * (A_HEAD_DIM ** -0.5)).astype(BF16)
    c = seg(_SEG_CKV)
    c = c * lax.rsqrt(jnp.mean(c * c, axis=-1, keepdims=True) + RMS_EPS) * kvn_ref[...]
    ckv_ref[...] = c.astype(BF16)
    ckvt_ref[...] = c.T.astype(BF16)
    qidx_ref[...] = (seg(_SEG_QIDX) * (IDX_DIM ** -0.5)).astype(BF16)
    kk_ref[...] = seg(_SEG_KK).astype(BF16)
    widx_ref[...] = seg(_SEG_WIDX) * (IDX_HEADS ** -0.5)
    qb_ref[...] = (seg(_SEG_QB) * (B_QK_DIM ** -0.5)).astype(BF16)
    kb_ref[...] = seg(_SEG_KB).astype(BF16)
    vbt_ref[...] = seg(_SEG_VB).T.astype(BF16)


def _proj_call(h, w_pad, kvn, tm):
    n, d = h.shape
    outs = [(512, BF16, False), (128, BF16, False), (128, BF16, True), (512, BF16, False),
            (128, BF16, False), (128, F32, False), (512, BF16, False), (512, BF16, False),
            (512, BF16, True)]
    spec = lambda w, t: (pl.BlockSpec((w, tm), lambda i: (0, i)) if t
                         else pl.BlockSpec((tm, w), lambda i: (i, 0)))
    shape = lambda w, dt, t: jax.ShapeDtypeStruct((w, n) if t else (n, w), dt)
    return pl.pallas_call(
        _proj_kernel,
        grid=(n // tm,),
        in_specs=[pl.BlockSpec((tm, d), lambda i: (i, 0)),
                  pl.BlockSpec((d, _PROJ_COLS), lambda i: (0, 0)),
                  pl.BlockSpec((1, KV_LORA), lambda i: (0, 0))],
        out_specs=[spec(w, t) for w, _, t in outs],
        out_shape=[shape(w, dt, t) for w, dt, t in outs],
        compiler_params=_cparams(1),
        name="proj_in",
    )(h, w_pad, kvn.reshape(1, KV_LORA))


def _pad_w_in(w_in):
    sizes = (A_WIDTH, KV_LORA, IDX_HEADS * IDX_DIM, IDX_DIM, IDX_HEADS,
             2 * B_HEADS * B_QK_DIM, 2 * B_HEADS * B_QK_DIM, B_WIDTH)
    splits = np.cumsum(sizes)[:-1].tolist()
    q_a, c_kv, q_idx, k_idx, w_idx, q_b, k_b, v_b = jnp.split(w_in, splits, axis=-1)
    w_idx = jnp.pad(w_idx, ((0, 0), (0, LANES - IDX_HEADS)))
    return jnp.concatenate([q_a, c_kv, q_idx, k_idx, k_idx, w_idx, q_b, k_b, v_b],
                           axis=-1).astype(BF16)


def _flash_probs(sl, sc, m_ref, l_ref):
    m_old = m_ref[:, sl]
    m_new = jnp.maximum(m_old, jnp.max(sc, axis=0, keepdims=True))
    alpha = jnp.exp(m_old - m_new)
    p = jnp.exp(sc - m_new)
    l_ref[:, sl] = alpha * l_ref[:, sl] + jnp.sum(p, axis=0, keepdims=True)
    m_ref[:, sl] = m_new
    return alpha, p.astype(BF16)


def _flash_init(m_ref, l_ref, acc_ref):
    m_ref[...] = jnp.full(m_ref.shape, M_INIT, F32)
    l_ref[...] = jnp.zeros(l_ref.shape, F32)
    acc_ref[...] = jnp.zeros(acc_ref.shape, F32)


def _split_halves(q_ref, n_pairs, out_ref, rows):
    lane = lax.broadcasted_iota(I32, (rows, LANES), 1)
    for j in range(n_pairs):
        qp = q_ref[:, j * LANES:(j + 1) * LANES].astype(F32)
        out_ref[2 * j * rows:(2 * j + 1) * rows, :] = jnp.where(lane < 64, qp, 0.0).astype(BF16)
        out_ref[(2 * j + 1) * rows:(2 * j + 2) * rows, :] = jnp.where(lane >= 64, qp, 0.0).astype(BF16)


def _dsa_kernel(qa_ref, qidx_ref, widx_ref, kk_ref, ckv_ref, ckvt_ref, wk_ref, wvt_ref, out_ref,
                keys_ref, qim_ref, qlat_ref, tj_ref, m_ref, l_ref, acc_ref, *, tq, tk, seq, topk):
    i = pl.program_id(1)
    n_kt = (i + 1) * (tq // tk)
    idx_bits = int(math.log2(seq))

    _split_halves(qidx_ref, IDX_HEADS // 2, qim_ref, tq)
    qlat = jnp.dot(qa_ref[...], wk_ref[...], preferred_element_type=F32)
    for h in range(A_HEADS):
        qlat_ref[h * tq:(h + 1) * tq, :] = qlat[:, h * KV_LORA:(h + 1) * KV_LORA].astype(BF16)
    w_t = widx_ref[...].T

    row = lax.broadcasted_iota(I32, (tk, tq), 0)
    tqi = i * tq + lax.broadcasted_iota(I32, (tk, tq), 1)
    qchunk = lax.shift_right_logical(tqi, CHUNK_SHIFT)

    def idx_body(kt, carry):
        off = pl.multiple_of(kt * tk, tk)
        kkt = kk_ref[pl.ds(off, tk), :]
        s_all = lax.dot_general(kkt, qim_ref[...], NT_DIMS, preferred_element_type=F32)
        acc = jnp.zeros((tk, tq), F32)
        for h in range(IDX_HEADS):
            acc = acc + w_t[h:h + 1, :] * jnp.maximum(s_all[:, h * tq:(h + 1) * tq], 0.0)
        bits = lax.bitcast_convert_type(acc + 0.0, I32)
        key = jnp.where(bits < 0, bits ^ jnp.int32(0x7FFFFFFF), bits)
        adm = lax.shift_right_logical(off + row, CHUNK_SHIFT) <= qchunk
        keys_ref[pl.ds(off, tk), :] = jnp.where(adm, key, jnp.int32(INT_MIN))
        return carry

    lax.fori_loop(0, n_kt, idx_body, 0)

    def count(pred):
        def body(kt, c):
            off = pl.multiple_of(kt * tk, tk)
            ind = jnp.where(pred(keys_ref[pl.ds(off, tk), :], off + row), 1.0, 0.0)
            return c + jnp.sum(ind, axis=0, keepdims=True)
        return lax.fori_loop(0, n_kt, body, jnp.zeros((1, tq), F32))

    def search_body(it, t):
        cand = t + lax.shift_left(jnp.int32(1), 31 - it)
        cnt = count(lambda tile, _: tile >= cand)
        return jnp.where(cnt >= topk, cand, t)

    thr = lax.fori_loop(0, 32, search_body, jnp.full((1, tq), INT_MIN, I32))

    live = thr > INT_MIN
    n_ge = count(lambda tile, _: tile >= thr)
    excess = jnp.logical_and(n_ge > topk, live)
    all_ties = jnp.where(live, seq, -1).astype(I32)
    tj_ref[0] = jnp.broadcast_to(thr, (SUBLANES, tq))
    tj_ref[1] = jnp.broadcast_to(all_ties, (SUBLANES, tq))

    @pl.when(jnp.max(jnp.where(excess, 1.0, 0.0)) > 0.0)
    def _():
        need = topk - count(lambda tile, _: tile > thr)

        def tie_body(it, hi):
            cand = hi + lax.shift_left(jnp.int32(1), idx_bits - 1 - it)
            cnt = count(lambda tile, pos: jnp.logical_and(tile == thr, pos < cand))
            return jnp.where(cnt < need, cand, hi)

        hi = lax.fori_loop(0, idx_bits, tie_body, jnp.zeros((1, tq), I32))
        tj_ref[1] = jnp.broadcast_to(jnp.where(excess, hi, all_ties), (SUBLANES, tq))

    thr_b = tj_ref[0][:1]
    tie_hi = tj_ref[1][:1]

    _flash_init(m_ref, l_ref, acc_ref)

    def att_body(kt, carry):
        off = pl.multiple_of(kt * tk, tk)
        tile = keys_ref[pl.ds(off, tk), :]
        pos = off + row
        sel = jnp.logical_or(tile > thr_b,
                             jnp.logical_and(tile == thr_b, pos <= tie_hi))
        dist = jnp.abs(tqi - pos).astype(F32)
        ckt = ckv_ref[pl.ds(off, tk), :]
        s_all = lax.dot_general(ckt, qlat_ref[...], NT_DIMS, preferred_element_type=F32)
        alphas, probs = [], []
        for h in range(A_HEADS):
            sl = slice(h * tq, (h + 1) * tq)
            sc = jnp.where(sel, s_all[:, sl] - (2.0 ** -(h + 1)) * dist, NEG_BIG)
            alpha, p = _flash_probs(sl, sc, m_ref, l_ref)
            alphas.append(alpha)
            probs.append(p)
        pv = jnp.dot(ckvt_ref[:, pl.ds(off, tk)], jnp.concatenate(probs, axis=1),
                     preferred_element_type=F32)
        acc_ref[...] = jnp.concatenate(alphas, axis=1) * acc_ref[...] + pv
        return carry

    lax.fori_loop(0, n_kt, att_body, 0)

    inv_l = 1.0 / l_ref[...]
    olat = jnp.concatenate([(acc_ref[:, h * tq:(h + 1) * tq] * inv_l[:, h * tq:(h + 1) * tq]).astype(BF16)
                            for h in range(A_HEADS)], axis=0)
    out_t = jnp.dot(wvt_ref[...], olat, preferred_element_type=F32)
    out_ref[...] = out_t.T.astype(BF16)


def _dsa_call(qa, qidx, widx, kk, ckv, ckvt, wk_bd, wvt_bd, b, s, tq, tk):
    topk = min(INDEX_TOPK_MAX, s // 4)
    assert s % tq == 0 and tq % tk == 0 and tk % LANES == 0 and s & (s - 1) == 0
    blk = lambda w: pl.BlockSpec((None, tq, w), lambda bi, i: (bi, i, 0))
    full = lambda w: pl.BlockSpec((None, s, w), lambda bi, i: (bi, 0, 0))
    const = lambda a: pl.BlockSpec(a.shape, lambda bi, i: (0, 0))
    r3 = lambda a: a.reshape(b, s, a.shape[-1])
    out = pl.pallas_call(
        functools.partial(_dsa_kernel, tq=tq, tk=tk, seq=s, topk=topk),
        grid=(b, s // tq),
        in_specs=[blk(A_WIDTH), blk(IDX_HEADS * IDX_DIM), blk(LANES), full(LANES), full(KV_LORA),
                  pl.BlockSpec((KV_LORA, s), lambda bi, i: (0, bi)),
                  const(wk_bd), const(wvt_bd)],
        out_specs=blk(A_WIDTH),
        out_shape=jax.ShapeDtypeStruct((b, s, A_WIDTH), BF16),
        scratch_shapes=[pltpu.VMEM((s, tq), I32),
                        pltpu.VMEM((IDX_HEADS * tq, LANES), BF16),
                        pltpu.VMEM((A_HEADS * tq, KV_LORA), BF16),
                        pltpu.VMEM((2, SUBLANES, tq), I32),
                        pltpu.VMEM((1, A_HEADS * tq), F32),
                        pltpu.VMEM((1, A_HEADS * tq), F32),
                        pltpu.VMEM((KV_LORA, A_HEADS * tq), F32)],
        compiler_params=_cparams(2),
        name="dsa",
    )(r3(qa), r3(qidx), r3(widx), r3(kk), r3(ckv), ckvt, wk_bd, wvt_bd)
    return out.reshape(b * s, A_WIDTH)


def _block_diag_uk(w_uk):
    h, c, d = w_uk.shape
    eye = jnp.eye(h, dtype=w_uk.dtype)
    return jnp.einsum('hcd,hg->hdgc', w_uk, eye).reshape(h * d, h * c).astype(BF16)


def _block_diag_uv_t(w_uv):
    h, c, d = w_uv.shape
    eye = jnp.eye(h, dtype=w_uv.dtype)
    return jnp.einsum('hcd,hg->hdgc', w_uv, eye).reshape(h * d, h * c).astype(BF16)


def _diff_kernel(qb_ref, kb_ref, vbt_ref, lam_ref, subln_ref, out_ref,
                 qm_ref, m_ref, l_ref, acc_ref, *, tq, tk, lambda_init):
    i = pl.program_id(1)
    _split_halves(qb_ref, B_HEADS, qm_ref, tq)

    row = lax.broadcasted_iota(I32, (tk, tq), 0)
    tqi = i * tq + lax.broadcasted_iota(I32, (tk, tq), 1)
    qchunk = lax.shift_right_logical(tqi, CHUNK_SHIFT)
    _flash_init(m_ref, l_ref, acc_ref)

    def body(kt, carry):
        off = pl.multiple_of(kt * tk, tk)
        pos = off + row
        adm = lax.shift_right_logical(pos, CHUNK_SHIFT) <= qchunk
        dist = jnp.abs(tqi - pos).astype(F32)
        s_pairs = [lax.dot_general(kb_ref[pl.ds(off, tk), h * LANES:(h + 1) * LANES],
                                   qm_ref[2 * h * tq:(2 * h + 2) * tq, :], NT_DIMS,
                                   preferred_element_type=F32) for h in range(B_HEADS)]
        alphas, probs = [], []
        for h in range(B_HEADS):
            for j in range(2):
                sl = slice((2 * h + j) * tq, (2 * h + j + 1) * tq)
                sc = jnp.where(adm, s_pairs[h][:, j * tq:(j + 1) * tq]
                               - (2.0 ** (-2 * (h + 1))) * dist, NEG_BIG)
                alpha, p = _flash_probs(sl, sc, m_ref, l_ref)
                alphas.append(alpha)
                probs.append(p)
        pv = jnp.concatenate(
            [jnp.dot(vbt_ref[h * B_V_DIM:(h + 1) * B_V_DIM, pl.ds(off, tk)],
                     jnp.concatenate(probs[2 * h:2 * h + 2], axis=1), preferred_element_type=F32)
             for h in range(B_HEADS)], axis=1)
        acc_ref[...] = jnp.concatenate(alphas, axis=1) * acc_ref[...] + pv
        return carry

    lax.fori_loop(0, (i + 1) * (tq // tk), body, 0)

    lp = lam_ref[...]
    lam = (jnp.exp(jnp.sum(lp[0:1] * lp[1:2], axis=1, keepdims=True))
           - jnp.exp(jnp.sum(lp[2:3] * lp[3:4], axis=1, keepdims=True)) + lambda_init)
    o_all = acc_ref[...] * (1.0 / l_ref[...])
    outs = []
    for h in range(B_HEADS):
        o = (o_all[:, 2 * h * tq:(2 * h + 1) * tq]
             - lam * o_all[:, (2 * h + 1) * tq:(2 * h + 2) * tq])
        o = o * lax.rsqrt(jnp.mean(o * o, axis=0, keepdims=True) + RMS_EPS) * subln_ref[...]
        outs.append(o * (1.0 - lambda_init))
    out_ref[...] = jnp.concatenate(outs, axis=0).T.astype(BF16)


def _diff_call(qb, kb, vbt, lam_rows, subln, b, s, tq, tk, lambda_init):
    blk = pl.BlockSpec((None, tq, B_WIDTH), lambda bi, i: (bi, i, 0))
    full = pl.BlockSpec((None, s, B_WIDTH), lambda bi, i: (bi, 0, 0))
    r3 = lambda a: a.reshape(b, s, a.shape[-1])
    out = pl.pallas_call(
        functools.partial(_diff_kernel, tq=tq, tk=tk, lambda_init=lambda_init),
        grid=(b, s // tq),
        in_specs=[blk, full, pl.BlockSpec((B_WIDTH, s), lambda bi, i: (0, bi)),
                  pl.BlockSpec((SUBLANES, LANES), lambda bi, i: (0, 0)),
                  pl.BlockSpec((B_V_DIM, 1), lambda bi, i: (0, 0))],
        out_specs=blk,
        out_shape=jax.ShapeDtypeStruct((b, s, B_WIDTH), BF16),
        scratch_shapes=[pltpu.VMEM((2 * B_HEADS * tq, LANES), BF16),
                        pltpu.VMEM((1, 2 * B_HEADS * tq), F32),
                        pltpu.VMEM((1, 2 * B_HEADS * tq), F32),
                        pltpu.VMEM((B_V_DIM, 2 * B_HEADS * tq), F32)],
        compiler_params=_cparams(2),
        name="diff",
    )(r3(qb), r3(kb), vbt, lam_rows, subln.reshape(B_V_DIM, 1))
    return out.reshape(b * s, B_WIDTH)


def _mix_out_kernel(h_ref, a_ref, b_ref, wa_ref, wb_ref, g_ref, beta_ref, o_ref):
    y = (jnp.dot(a_ref[...], wa_ref[...], preferred_element_type=F32)
         + jnp.dot(b_ref[...], wb_ref[...], preferred_element_type=F32))
    o_ref[...] = _layer_norm(ALPHA * h_ref[...] + y, g_ref[...], beta_ref[...])


def _mix_out_call(h, out_a, out_b, w_o, g, beta, tm):
    n, d = h.shape
    w = w_o.astype(BF16)
    row = lambda wd: pl.BlockSpec((tm, wd), lambda i: (i, 0))
    const = lambda shp: pl.BlockSpec(shp, lambda i: (0, 0))
    return pl.pallas_call(
        _mix_out_kernel,
        grid=(n // tm,),
        in_specs=[row(d), row(A_WIDTH), row(B_WIDTH), const((A_WIDTH, d)), const((B_WIDTH, d)),
                  const((1, d)), const((1, d))],
        out_specs=row(d),
        out_shape=jax.ShapeDtypeStruct((n, d), F32),
        compiler_params=_cparams(1),
        name="mix_out",
    )(h, out_a, out_b, w[:A_WIDTH], w[A_WIDTH:], g.reshape(1, d), beta.reshape(1, d))


def _mem_kv_kernel(m_ref, w_ref, k_ref, v_ref):
    x = m_ref[...].astype(BF16)
    d = k_ref.shape[-1]
    k_ref[...] = jnp.dot(x, w_ref[:, :d], preferred_element_type=F32).astype(BF16)
    v_ref[...] = jnp.dot(x, w_ref[:, d:], preferred_element_type=F32).astype(BF16)


def _mem_kv_call(mem2d, wkv, tm):
    n, d = mem2d.shape
    return pl.pallas_call(
        _mem_kv_kernel,
        grid=(n // tm,),
        in_specs=[pl.BlockSpec((tm, d), lambda i: (i, 0)),
                  pl.BlockSpec((d, 2 * d), lambda i: (0, 0))],
        out_specs=[pl.BlockSpec((tm, d), lambda i: (i, 0))] * 2,
        out_shape=[jax.ShapeDtypeStruct((n, d), BF16)] * 2,
        compiler_params=_cparams(1),
        name="mem_kv",
    )(mem2d, wkv.astype(BF16))


def _mem_attn_kernel(h_ref, wq_ref, k_ref, v_ref, wo_ref, g_ref, beta_ref, o_ref):
    h = h_ref[...]
    q = jnp.dot(h.astype(BF16), wq_ref[...], preferred_element_type=F32)
    q = (q * (MEM_HEAD_DIM ** -0.5)).astype(BF16)
    outs = []
    for hd in range(MEM_HEADS):
        sl = slice(hd * MEM_HEAD_DIM, (hd + 1) * MEM_HEAD_DIM)
        sc = lax.dot_general(q[:, sl], k_ref[:, sl], NT_DIMS, preferred_element_type=F32)
        p = jnp.exp(sc - jnp.max(sc, axis=1, keepdims=True))
        den = jnp.sum(p, axis=1, keepdims=True)
        o = jnp.dot(p.astype(BF16), v_ref[:, sl], preferred_element_type=F32) / den
        outs.append(o.astype(BF16))
    y = jnp.dot(jnp.concatenate(outs, axis=1), wo_ref[...], preferred_element_type=F32)
    o_ref[...] = _layer_norm(ALPHA * h + y, g_ref[...], beta_ref[...])


def _mem_attn_call(h, wq, k, v, wo, g, beta, b, s, tm):
    n, d = h.shape
    m = k.shape[0] // b
    const = lambda shp: pl.BlockSpec(shp, lambda bi, i: (0,) * len(shp))
    row = pl.BlockSpec((None, tm, d), lambda bi, i: (bi, i, 0))
    kv = pl.BlockSpec((None, m, d), lambda bi, i: (bi, 0, 0))
    out = pl.pallas_call(
        _mem_attn_kernel,
        grid=(b, s // tm),
        in_specs=[row, const((d, d)), kv, kv, const((d, d)), const((1, d)), const((1, d))],
        out_specs=row,
        out_shape=jax.ShapeDtypeStruct((b, s, d), F32),
        compiler_params=_cparams(2),
        name="mem_attn",
    )(h.reshape(b, s, d), wq.astype(BF16), k.reshape(b, m, d), v.reshape(b, m, d),
      wo.astype(BF16), g.reshape(1, d), beta.reshape(1, d))
    return out.reshape(n, d)


def _router_kernel(h_ref, rw_ref, bias_ref, tri_ref, e_ref, g_ref, rank_ref, cnt_ref, run_ref):
    tm = h_ref.shape[0]

    @pl.when(pl.program_id(0) == 0)
    def _():
        run_ref[...] = jnp.zeros(run_ref.shape, F32)

    logits = jnp.dot(h_ref[...].astype(BF16), rw_ref[...], preferred_element_type=F32)
    scores = 1.0 / (1.0 + jnp.exp(-logits))
    lane = lax.broadcasted_iota(I32, (tm, LANES), 1)
    lanef = lane.astype(F32)
    ninf = jnp.float32(-jnp.inf)
    cur = jnp.where(lane < N_EXPERTS, scores + bias_ref[...], ninf)
    top_e = jnp.zeros((tm, LANES), F32)
    top_s = jnp.zeros((tm, LANES), F32)
    picked = jnp.zeros((tm, LANES), F32)
    firsts = []
    for k in range(TOP_K):
        mx = jnp.max(cur, axis=1, keepdims=True)
        first = jnp.min(jnp.where(cur == mx, lanef, float(LANES)), axis=1, keepdims=True)
        hit = lanef == first
        s_k = jnp.sum(jnp.where(hit, scores, 0.0), axis=1, keepdims=True)
        top_e = jnp.where(lane == k, first, top_e)
        top_s = jnp.where(lane == k, s_k, top_s)
        picked = jnp.where(hit, 1.0, picked)
        cur = jnp.where(hit, ninf, cur)
        firsts.append(first)
    e_ref[...] = top_e.astype(I32)
    g_ref[...] = top_s / jnp.sum(top_s, axis=1, keepdims=True) * ROUTED_SCALE

    slot = jnp.dot(tri_ref[...], picked.astype(BF16), preferred_element_type=F32) + run_ref[...]
    rank = jnp.zeros((tm, LANES), F32)
    for k in range(TOP_K):
        r_k = jnp.sum(jnp.where(lanef == firsts[k], slot, 0.0), axis=1, keepdims=True)
        rank = jnp.where(lane == k, r_k, rank)
    rank_ref[...] = rank.astype(I32)
    run = run_ref[...] + jnp.sum(picked, axis=0, keepdims=True)
    run_ref[...] = run
    cnt_ref[...] = run.astype(I32)


def _router_call(h, router_w, router_bias, tm):
    n, d = h.shape
    rw = jnp.pad(router_w, ((0, 0), (0, LANES - N_EXPERTS))).astype(BF16)
    bias = jnp.pad(router_bias, (0, LANES - N_EXPERTS)).reshape(1, LANES)
    tri = jnp.tri(tm, k=-1, dtype=BF16)
    row = pl.BlockSpec((tm, LANES), lambda i: (i, 0))
    const = lambda shp: pl.BlockSpec(shp, lambda i: (0, 0))
    return pl.pallas_call(
        _router_kernel,
        grid=(n // tm,),
        in_specs=[pl.BlockSpec((tm, d), lambda i: (i, 0)), const((d, LANES)), const((1, LANES)),
                  const((tm, tm))],
        out_specs=[row, row, row, const((1, LANES))],
        out_shape=[jax.ShapeDtypeStruct((n, LANES), I32),
                   jax.ShapeDtypeStruct((n, LANES), F32),
                   jax.ShapeDtypeStruct((n, LANES), I32),
                   jax.ShapeDtypeStruct((1, LANES), I32)],
        scratch_shapes=[pltpu.VMEM((1, LANES), F32)],
        compiler_params=_cparams(1),
        name="router",
    )(h, rw, bias, tri)


def _dispatch_kernel(dest_ref, x_ref, xs_ref, sem, *, td):
    def row_copy(r, k):
        return pltpu.make_async_copy(x_ref.at[pl.ds(r, 1), :],
                                     xs_ref.at[pl.ds(dest_ref[r * TOP_K + k], 1), :], sem)

    def issue(r, carry):
        for k in range(TOP_K):
            row_copy(r, k).start()
        return carry

    lax.fori_loop(0, td, issue, 0)

    def drain(r, carry):
        for k in range(TOP_K):
            row_copy(r, k).wait()
        return carry

    lax.fori_loop(0, td, drain, 0)


def _dispatch_call(h, dest, td):
    n, d = h.shape
    m = n * TOP_K
    return pl.pallas_call(
        functools.partial(_dispatch_kernel, td=td),
        grid=(n // td,),
        in_specs=[pl.BlockSpec((td * TOP_K,), lambda i: (i,), memory_space=pltpu.SMEM),
                  pl.BlockSpec((td, d), lambda i: (i, 0))],
        out_specs=pl.BlockSpec(memory_space=pl.ANY),
        out_shape=jax.ShapeDtypeStruct((m, d), F32),
        scratch_shapes=[pltpu.SemaphoreType.DMA(())],
        compiler_params=_cparams(1),
        name="dispatch",
    )(dest, h)


def _moe_mm_kernel(ptile_ref, pexp_ref, plo_ref, phi_ref, xs_ref, wg_ref, wu_ref, wd_ref, y_ref,
                   *, tmm):
    p = pl.program_id(0)
    tile = ptile_ref[p]
    lo = plo_ref[p]
    hi = phi_ref[p]
    first = jnp.logical_or(p == 0, ptile_ref[jnp.maximum(p - 1, 0)] != tile)

    def expert_out():
        x = xs_ref[...].astype(BF16)
        g = jnp.dot(x, wg_ref[...], preferred_element_type=F32)
        u = jnp.dot(x, wu_ref[...], preferred_element_type=F32)
        hid = (_silu(g) * u).astype(BF16)
        return jnp.dot(hid, wd_ref[...], preferred_element_type=F32)

    whole = jnp.logical_and(lo <= tile * tmm, hi >= (tile + 1) * tmm)

    @pl.when(whole)
    def _():
        y_ref[...] = expert_out()

    @pl.when(jnp.logical_and(first, jnp.logical_not(whole)))
    def _():
        y_ref[...] = jnp.zeros(y_ref.shape, F32)

    @pl.when(jnp.logical_and(lo < hi, jnp.logical_not(whole)))
    def _():
        y = expert_out()
        rowg = tile * tmm + lax.broadcasted_iota(I32, (tmm, 1), 0)
        y_ref[...] += jnp.where(jnp.logical_and(rowg >= lo, rowg < hi), y, 0.0)


def _moe_mm_call(xs, pairs, wg, wu, wd, tmm):
    m = xs.shape[0]
    n_pairs = pairs[0].shape[0]
    rows = pl.BlockSpec((tmm, D_MODEL), lambda p, pt, pe, plo, phi: (pt[p], 0))
    wspec = lambda shp: pl.BlockSpec((None,) + shp, lambda p, pt, pe, plo, phi: (pe[p], 0, 0))
    return pl.pallas_call(
        functools.partial(_moe_mm_kernel, tmm=tmm),
        grid_spec=pltpu.PrefetchScalarGridSpec(
            num_scalar_prefetch=4,
            grid=(n_pairs,),
            in_specs=[rows, wspec((D_MODEL, EXPERT_DIM)), wspec((D_MODEL, EXPERT_DIM)),
                      wspec((EXPERT_DIM, D_MODEL))],
            out_specs=rows),
        out_shape=jax.ShapeDtypeStruct((m, D_MODEL), F32),
        compiler_params=_cparams(1),
        name="moe_mm",
    )(*pairs, xs, wg, wu, wd)


def _group_pairs(counts, m, tmm):
    n_tiles = m // tmm
    n_pairs = n_tiles + N_EXPERTS
    gend = jnp.cumsum(counts)
    gstart = gend - counts
    t0 = jnp.arange(n_tiles, dtype=I32) * tmm
    n_le = lambda edges, v: jnp.sum(edges[None, :] <= v[:, None], axis=1).astype(I32)
    e_first = n_le(gend, t0)
    e_last = n_le(gend, t0 + (tmm - 1))
    per_tile = e_last - e_first + 1
    pend = jnp.cumsum(per_tile)
    pstart = pend - per_tile
    p = jnp.arange(n_pairs, dtype=I32)
    valid = p < pend[-1]
    tile = jnp.minimum(n_le(pend, p), n_tiles - 1)
    e = jnp.clip(e_first[tile] + (p - pstart[tile]), 0, N_EXPERTS - 1).astype(I32)
    lo = jnp.where(valid, gstart[e], 0).astype(I32)
    hi = jnp.where(valid, gend[e], 0).astype(I32)
    return tile, e, lo, hi


def _combine_kernel(dest_ref, gate_ref, h_ref, y_ref, sg_ref, su_ref, sd_ref, g_ref, beta_ref, o_ref,
                    buf_ref, sem, *, tc):
    def row_copy(r, k):
        return pltpu.make_async_copy(y_ref.at[pl.ds(dest_ref[r * TOP_K + k], 1), :],
                                     buf_ref.at[k, pl.ds(r, 1), :], sem)

    def issue(r, carry):
        for k in range(TOP_K):
            row_copy(r, k).start()
        return carry

    lax.fori_loop(0, tc, issue, 0)

    h = h_ref[...]
    x = h.astype(BF16)
    hid = (_silu(jnp.dot(x, sg_ref[...], preferred_element_type=F32))
           * jnp.dot(x, su_ref[...], preferred_element_type=F32)).astype(BF16)
    shared = jnp.dot(hid, sd_ref[...], preferred_element_type=F32)

    def drain(r, carry):
        for k in range(TOP_K):
            row_copy(r, k).wait()
        return carry

    lax.fori_loop(0, tc, drain, 0)

    gates = gate_ref[...]
    routed = gates[:, 0:1] * buf_ref[0]
    for k in range(1, TOP_K):
        routed = routed + gates[:, k:k + 1] * buf_ref[k]
    o_ref[...] = _layer_norm(ALPHA * h + (routed + shared), g_ref[...], beta_ref[...])


def _combine_call(h, y, dest, gates, sg, su, sd, g, beta, tc):
    n, d = h.shape
    smem = pl.BlockSpec((tc * TOP_K,), lambda i: (i,), memory_space=pltpu.SMEM)
    const = lambda shp: pl.BlockSpec(shp, lambda i: (0, 0))
    return pl.pallas_call(
        functools.partial(_combine_kernel, tc=tc),
        grid=(n // tc,),
        in_specs=[smem, pl.BlockSpec((tc, LANES), lambda i: (i, 0)),
                  pl.BlockSpec((tc, d), lambda i: (i, 0)),
                  pl.BlockSpec(memory_space=pl.ANY),
                  const((d, SHARED_DIM)), const((d, SHARED_DIM)), const((SHARED_DIM, d)),
                  const((1, d)), const((1, d))],
        out_specs=pl.BlockSpec((tc, d), lambda i: (i, 0)),
        out_shape=jax.ShapeDtypeStruct((n, d), F32),
        scratch_shapes=[pltpu.VMEM((TOP_K, tc, d), F32),
                        pltpu.SemaphoreType.DMA(())],
        compiler_params=_cparams(1),
        name="combine",
    )(dest, gates, h, y, sg.astype(BF16), su.astype(BF16), sd.astype(BF16),
      g.reshape(1, d), beta.reshape(1, d))


def _tiles(b, s):
    n = b * s
    pick = lambda pref, total: next(t for t in (pref, 512, 256, 128, 64, 32, 16, 8) if t <= pref and total % t == 0)
    return dict(
        rows=pick(512, n),
        attn_q=pick(256, s),
        attn_k=pick(256, s),
        mem_rows=pick(512, s),
        mem_kv=pick(512, b * 256),
        router=pick(1024, n),
        dispatch=pick(256, n),
        moe=pick(256, n * TOP_K),
        combine=pick(128, n),
    )


def kernel(x, mem, ln_in_g, ln_in_b, w_in, a_kv_norm, a_w_uk, a_w_uv, b_lq1, b_lk1, b_lq2, b_lk2,
           b_subln, w_o, ln1_g, ln1_b, m_wq, m_wkv, m_wo, ln2_g, ln2_b, router_w, router_bias,
           e_w_gate, e_w_up, e_w_down, s_w_gate, s_w_up, s_w_down, ln3_g, ln3_b):
    b, s, d = x.shape
    n = b * s
    m = n * TOP_K
    t = _tiles(b, s)
    mem2d = mem.reshape(-1, d)

    h = _ln_call(x.reshape(n, d), ln_in_g, ln_in_b, t["rows"])
    for l in range(DEPTH):
        lambda_init = 0.8 - 0.6 * math.exp(-0.3 * l)

        qa, ckv, ckvt, qidx, kk, widx, qb, kb, vbt = _proj_call(h, _pad_w_in(w_in[l]), a_kv_norm[l],
                                                                  t["rows"])
        out_a = _dsa_call(qa, qidx, widx, kk, ckv, ckvt, _block_diag_uk(a_w_uk[l]),
                          _block_diag_uv_t(a_w_uv[l]), b, s, t["attn_q"], t["attn_k"])
        lam_rows = jnp.pad(jnp.stack([b_lq1[l], b_lk1[l], b_lq2[l], b_lk2[l]]),
                           ((0, SUBLANES - 4), (0, LANES - B_QK_DIM)))
        out_b = _diff_call(qb, kb, vbt, lam_rows, b_subln[l], b, s, t["attn_q"], t["attn_k"], lambda_init)
        h = _mix_out_call(h, out_a, out_b, w_o[l], ln1_g[l], ln1_b[l], t["rows"])

        mk, mv = _mem_kv_call(mem2d, m_wkv[l], t["mem_kv"])
        h = _mem_attn_call(h, m_wq[l], mk, mv, m_wo[l], ln2_g[l], ln2_b[l], b, s, t["mem_rows"])

        top_e, gates, rank, counts = _router_call(h, router_w[l], router_bias[l], t["router"])
        counts = counts[0, :N_EXPERTS]
        gstart = jnp.cumsum(counts) - counts
        top_e = top_e[:, :TOP_K]
        dest = (rank[:, :TOP_K] + gstart[top_e]).reshape(m)
        xs = _dispatch_call(h, dest, t["dispatch"])
        pairs = _group_pairs(counts, m, t["moe"])
        y = _moe_mm_call(xs, pairs, e_w_gate[l].astype(BF16), e_w_up[l].astype(BF16),
                         e_w_down[l].astype(BF16), t["moe"])
        h = _combine_call(h, y, dest, gates, s_w_gate[l], s_w_up[l], s_w_down[l],
                          ln3_g[l], ln3_b[l], t["combine"])
    return h.reshape(b, s, d)
```

```python
import functools
import math

import jax
import jax.numpy as jnp
import numpy as np
from jax import lax
from jax.experimental import pallas as pl
from jax.experimental.pallas import tpu as pltpu

F32 = jnp.float32
BF16 = jnp.bfloat16
I32 = jnp.int32
I16 = jnp.int16

D_MODEL = 1024
DEPTH = 2
CHUNK = 64
CHUNK_SHIFT = 6
A_HEADS = 8
A_HEAD_DIM = 64
A_WIDTH = A_HEADS * A_HEAD_DIM
KV_LORA = 128
IDX_HEADS = 8
IDX_DIM = 64
INDEX_TOPK_MAX = 256
B_HEADS = 4
B_QK_DIM = 64
B_V_DIM = 128
B_WIDTH = B_HEADS * B_V_DIM
MEM_HEADS = 4
MEM_HEAD_DIM = D_MODEL // MEM_HEADS
N_EXPERTS = 64
TOP_K = 8
EXPERT_DIM = 256
SHARED_DIM = 256
ROUTED_SCALE = 2.5
ALPHA = (2 * DEPTH) ** 0.25
LN_EPS = 1e-5
RMS_EPS = 1e-6

LANES = 128
SUBLANES = 8
VMEM_LIMIT = 56 * 1024 * 1024
INT_MIN = -2 ** 31
I16_MIN = -2 ** 15
NEG_BIG = -3.0e38
M_INIT = -1.0e30

NT_DIMS = (((1,), (1,)), ((), ()))


def _cparams(n_axes):
    return pltpu.CompilerParams(dimension_semantics=("arbitrary",) * n_axes,
                                vmem_limit_bytes=VMEM_LIMIT)


def _layer_norm(x, g, b):
    mu = jnp.mean(x, axis=-1, keepdims=True)
    xc = x - mu
    var = jnp.mean(xc * xc, axis=-1, keepdims=True)
    return xc * lax.rsqrt(var + LN_EPS) * g + b


def _silu(x):
    return x * (1.0 / (1.0 + jnp.exp(-x)))


def _ln_kernel(x_ref, g_ref, b_ref, o_ref):
    o_ref[...] = _layer_norm(x_ref[...], g_ref[...], b_ref[...])


def _ln_call(x, g, b, tm):
    n, d = x.shape
    return pl.pallas_call(
        _ln_kernel,
        grid=(n // tm,),
        in_specs=[pl.BlockSpec((tm, d), lambda i: (i, 0)),
                  pl.BlockSpec((1, d), lambda i: (0, 0)),
                  pl.BlockSpec((1, d), lambda i: (0, 0))],
        out_specs=pl.BlockSpec((tm, d), lambda i: (i, 0)),
        out_shape=jax.ShapeDtypeStruct((n, d), F32),
        compiler_params=_cparams(1),
        name="ln_in",
    )(x, g.reshape(1, d), b.reshape(1, d))


_SEG_QA = (0, 512)
_SEG_CKV = (512, 640)
_SEG_QIDX = (640, 1152)
_SEG_KK = (1152, 1280)
_SEG_WIDX = (1280, 1408)
_SEG_QB = (1408, 1920)
_SEG_KB = (1920, 2432)
_SEG_VB = (2432, 2944)
_PROJ_COLS = 2944


def _proj_kernel(h_ref, w_ref, kvn_ref, qa_ref, ckv_ref, ckvt_ref, qidx_ref, kk_ref, widx_ref,
                 qb_ref, kb_ref, vbt_ref):
    x = h_ref[...].astype(BF16)

    def seg(s):
        return jnp.dot(x, w_ref[:, s[0]:s[1]], preferred_element_type=F32)

    qa_ref[...] = (seg(_SEG_QA) * (A_HEAD_DIM ** -0.5)).astype(BF16)
    c = seg(_SEG_CKV)
    c = c * lax.rsqrt(jnp.mean(c * c, axis=-1, keepdims=True) + RMS_EPS) * kvn_ref[...]
    ckv_ref[...] = c.astype(BF16)
    ckvt_ref[...] = c.T.astype(BF16)
    qidx_ref[...] = (seg(_SEG_QIDX) * (IDX_DIM ** -0.5)).astype(BF16)
    kk_ref[...] = seg(_SEG_KK).astype(BF16)
    widx_ref[...] = seg(_SEG_WIDX) * (IDX_HEADS ** -0.5)
    qb_ref[...] = (seg(_SEG_QB) * (B_QK_DIM ** -0.5)).astype(BF16)
    kb_ref[...] = seg(_SEG_KB).astype(BF16)
    vbt_ref[...] = seg(_SEG_VB).T.astype(BF16)


def _proj_call(h, w_pad, kvn, tm):
    n, d = h.shape
    outs = [(512, BF16, False), (128, BF16, False), (128, BF16, True), (512, BF16, False),
            (128, BF16, False), (128, F32, False), (512, BF16, False), (512, BF16, False),
            (512, BF16, True)]
    spec = lambda w, t: (pl.BlockSpec((w, tm), lambda i: (0, i)) if t
                         else pl.BlockSpec((tm, w), lambda i: (i, 0)))
    shape = lambda w, dt, t: jax.ShapeDtypeStruct((w, n) if t else (n, w), dt)
    return pl.pallas_call(
        _proj_kernel,
        grid=(n // tm,),
        in_specs=[pl.BlockSpec((tm, d), lambda i: (i, 0)),
                  pl.BlockSpec((d, _PROJ_COLS), lambda i: (0, 0)),
                  pl.BlockSpec((1, KV_LORA), lambda i: (0, 0))],
        out_specs=[spec(w, t) for w, _, t in outs],
        out_shape=[shape(w, dt, t) for w, dt, t in outs],
        compiler_params=_cparams(1),
        name="proj_in",
    )(h, w_pad, kvn.reshape(1, KV_LORA))


def _pad_w_in(w_in):
    sizes = (A_WIDTH, KV_LORA, IDX_HEADS * IDX_DIM, IDX_DIM, IDX_HEADS,
             2 * B_HEADS * B_QK_DIM, 2 * B_HEADS * B_QK_DIM, B_WIDTH)
    splits = np.cumsum(sizes)[:-1].tolist()
    q_a, c_kv, q_idx, k_idx, w_idx, q_b, k_b, v_b = jnp.split(w_in, splits, axis=-1)
    w_idx = jnp.pad(w_idx, ((0, 0), (0, LANES - IDX_HEADS)))
    return jnp.concatenate([q_a, c_kv, q_idx, k_idx, k_idx, w_idx, q_b, k_b, v_b],
                           axis=-1).astype(BF16)


def _flash_probs(sl, sc, m_ref, l_ref):
    m_old = m_ref[:, sl]
    m_new = jnp.maximum(m_old, jnp.max(sc, axis=0, keepdims=True))
    alpha = jnp.exp(m_old - m_new)
    p = jnp.exp(sc - m_new)
    l_ref[:, sl] = alpha * l_ref[:, sl] + jnp.sum(p, axis=0, keepdims=True)
    m_ref[:, sl] = m_new
    return alpha, p.astype(BF16)


def _flash_init(m_ref, l_ref, acc_ref):
    m_ref[...] = jnp.full(m_ref.shape, M_INIT, F32)
    l_ref[...] = jnp.zeros(l_ref.shape, F32)
    acc_ref[...] = jnp.zeros(acc_ref.shape, F32)


def _split_halves(q_ref, n_pairs, out_ref, rows):
    lane = lax.broadcasted_iota(I32, (rows, LANES), 1)
    for j in range(n_pairs):
        qp = q_ref[:, j * LANES:(j + 1) * LANES].astype(F32)
        out_ref[2 * j * rows:(2 * j + 1) * rows, :] = jnp.where(lane < 64, qp, 0.0).astype(BF16)
        out_ref[(2 * j + 1) * rows:(2 * j + 2) * rows, :] = jnp.where(lane >= 64, qp, 0.0).astype(BF16)


def _dsa_kernel(qa_ref, qidx_ref, widx_ref, kk_ref, ckv_ref, ckvt_ref, wk_ref, wvt_ref, out_ref,
                keys_ref, khi_ref, klo_ref, qim_ref, qlat_ref, tj_ref, m_ref, l_ref, acc_ref, *, tq, tk, seq, topk):
    i = pl.program_id(1)
    n_kt = (i + 1) * (tq // tk)
    idx_bits = int(math.log2(seq))

    _split_halves(qidx_ref, IDX_HEADS // 2, qim_ref, tq)
    qlat = jnp.dot(qa_ref[...], wk_ref[...], preferred_element_type=F32)
    for h in range(A_HEADS):
        qlat_ref[h * tq:(h + 1) * tq, :] = qlat[:, h * KV_LORA:(h + 1) * KV_LORA].astype(BF16)
    w_t = widx_ref[...].T

    row = lax.broadcasted_iota(I32, (tk, tq), 0)
    tqi = i * tq + lax.broadcasted_iota(I32, (tk, tq), 1)
    qchunk = lax.shift_right_logical(tqi, CHUNK_SHIFT)

    def idx_body(kt, carry):
        off = pl.multiple_of(kt * tk, tk)
        kkt = kk_ref[pl.ds(off, tk), :]
        s_all = lax.dot_general(kkt, qim_ref[...], NT_DIMS, preferred_element_type=F32)
        acc = jnp.zeros((tk, tq), F32)
        for h in range(IDX_HEADS):
            acc = acc + w_t[h:h + 1, :] * jnp.maximum(s_all[:, h * tq:(h + 1) * tq], 0.0)
        bits = lax.bitcast_convert_type(acc + 0.0, I32)
        key = jnp.where(bits < 0, bits ^ jnp.int32(0x7FFFFFFF), bits)
        adm = lax.shift_right_logical(off + row, CHUNK_SHIFT) <= qchunk
        key = jnp.where(adm, key, jnp.int32(INT_MIN))
        keys_ref[pl.ds(off, tk), :] = key
        khi_ref[pl.ds(off, tk), :] = lax.shift_right_arithmetic(key, 16).astype(I16)
        klo_ref[pl.ds(off, tk), :] = ((key & 0xFFFF) + I16_MIN).astype(I16)
        return carry

    lax.fori_loop(0, n_kt, idx_body, 0)

    def count(pred):
        def body(kt, c):
            off = pl.multiple_of(kt * tk, tk)
            ind = jnp.where(pred(keys_ref[pl.ds(off, tk), :], off + row), 1.0, 0.0)
            return c + jnp.sum(ind, axis=0, keepdims=True)
        return lax.fori_loop(0, n_kt, body, jnp.zeros((1, tq), F32))

    def count16(ref, pred):
        pack = 2 * SUBLANES
        def body(kt, c):
            off = pl.multiple_of(kt * tk, tk)
            ind = jnp.where(pred(ref[pl.ds(off, tk), :]), jnp.int16(1), jnp.int16(0))
            parts = [ind[j * pack:(j + 1) * pack] for j in range(tk // pack)]
            while len(parts) > 1:
                parts = [parts[j] + parts[j + 1] for j in range(0, len(parts), 2)]
            return c + parts[0]
        c = lax.fori_loop(0, n_kt, body, jnp.zeros((pack, tq), I16))
        return jnp.sum(c.astype(F32), axis=0, keepdims=True)

    def search16(ref, need):
        def body(it, t):
            cand = t + lax.shift_left(jnp.int32(1), 15 - it)
            cand16 = cand.astype(I16)
            cnt = count16(ref, lambda tile: tile >= cand16)
            return jnp.where(cnt >= need, cand, t)
        return lax.fori_loop(0, 16, body, jnp.full((1, tq), I16_MIN, I32))

    t_hi = search16(khi_ref, topk)
    t_hi16 = t_hi.astype(I16)
    n_above = count16(khi_ref, lambda tile: tile > t_hi16)

    def park_body(kt, carry):
        off = pl.multiple_of(kt * tk, tk)
        klo_ref[pl.ds(off, tk), :] = jnp.where(khi_ref[pl.ds(off, tk), :] == t_hi16,
                                               klo_ref[pl.ds(off, tk), :], jnp.int16(I16_MIN))
        return carry

    lax.fori_loop(0, n_kt, park_body, 0)
    t_lo = search16(klo_ref, topk - n_above)
    thr = t_hi * 65536 + (t_lo - I16_MIN)

    live = thr > INT_MIN
    n_ge = count(lambda tile, _: tile >= thr)
    excess = jnp.logical_and(n_ge > topk, live)
    all_ties = jnp.where(live, seq, -1).astype(I32)
    tj_ref[0] = jnp.broadcast_to(thr, (SUBLANES, tq))
    tj_ref[1] = jnp.broadcast_to(all_ties, (SUBLANES, tq))

    @pl.when(jnp.max(jnp.where(excess, 1.0, 0.0)) > 0.0)
    def _():
        need = topk - count(lambda tile, _: tile > thr)

        def tie_body(it, hi):
            cand = hi + lax.shift_left(jnp.int32(1), idx_bits - 1 - it)
            cnt = count(lambda tile, pos: jnp.logical_and(tile == thr, pos < cand))
            return jnp.where(cnt < need, cand, hi)

        hi = lax.fori_loop(0, idx_bits, tie_body, jnp.zeros((1, tq), I32))
        tj_ref[1] = jnp.broadcast_to(jnp.where(excess, hi, all_ties), (SUBLANES, tq))

    thr_b = tj_ref[0][:1]
    tie_hi = tj_ref[1][:1]

    _flash_init(m_ref, l_ref, acc_ref)

    def att_body(kt, carry):
        off = pl.multiple_of(kt * tk, tk)
        tile = keys_ref[pl.ds(off, tk), :]
        pos = off + row
        sel = jnp.logical_or(tile > thr_b,
                             jnp.logical_and(tile == thr_b, pos <= tie_hi))
        dist = jnp.abs(tqi - pos).astype(F32)
        ckt = ckv_ref[pl.ds(off, tk), :]
        s_all = lax.dot_general(ckt, qlat_ref[...], NT_DIMS, preferred_element_type=F32)
        alphas, probs = [], []
        for h in range(A_HEADS):
            sl = slice(h * tq, (h + 1) * tq)
            sc = jnp.where(sel, s_all[:, sl] - (2.0 ** -(h + 1)) * dist, NEG_BIG)
            alpha, p = _flash_probs(sl, sc, m_ref, l_ref)
            alphas.append(alpha)
            probs.append(p)
        pv = jnp.dot(ckvt_ref[:, pl.ds(off, tk)], jnp.concatenate(probs, axis=1),
                     preferred_element_type=F32)
        acc_ref[...] = jnp.concatenate(alphas, axis=1) * acc_ref[...] + pv
        return carry

    lax.fori_loop(0, n_kt, att_body, 0)

    inv_l = 1.0 / l_ref[...]
    olat = jnp.concatenate([(acc_ref[:, h * tq:(h + 1) * tq] * inv_l[:, h * tq:(h + 1) * tq]).astype(BF16)
                            for h in range(A_HEADS)], axis=0)
    out_t = jnp.dot(wvt_ref[...], olat, preferred_element_type=F32)
    out_ref[...] = out_t.T.astype(BF16)


def _dsa_call(qa, qidx, widx, kk, ckv, ckvt, wk_bd, wvt_bd, b, s, tq, tk):
    topk = min(INDEX_TOPK_MAX, s // 4)
    assert s % tq == 0 and tq % tk == 0 and tk % LANES == 0 and s & (s - 1) == 0
    blk = lambda w: pl.BlockSpec((None, tq, w), lambda bi, i: (bi, i, 0))
    full = lambda w: pl.BlockSpec((None, s, w), lambda bi, i: (bi, 0, 0))
    const = lambda a: pl.BlockSpec(a.shape, lambda bi, i: (0, 0))
    r3 = lambda a: a.reshape(b, s, a.shape[-1])
    out = pl.pallas_call(
        functools.partial(_dsa_kernel, tq=tq, tk=tk, seq=s, topk=topk),
        grid=(b, s // tq),
        in_specs=[blk(A_WIDTH), blk(IDX_HEADS * IDX_DIM), blk(LANES), full(LANES), full(KV_LORA),
                  pl.BlockSpec((KV_LORA, s), lambda bi, i: (0, bi)),
                  const(wk_bd), const(wvt_bd)],
        out_specs=blk(A_WIDTH),
        out_shape=jax.ShapeDtypeStruct((b, s, A_WIDTH), BF16),
        scratch_shapes=[pltpu.VMEM((s, tq), I32),
                        pltpu.VMEM((s, tq), I16),
                        pltpu.VMEM((s, tq), I16),
                        pltpu.VMEM((IDX_HEADS * tq, LANES), BF16),
                        pltpu.VMEM((A_HEADS * tq, KV_LORA), BF16),
                        pltpu.VMEM((2, SUBLANES, tq), I32),
                        pltpu.VMEM((1, A_HEADS * tq), F32),
                        pltpu.VMEM((1, A_HEADS * tq), F32),
                        pltpu.VMEM((KV_LORA, A_HEADS * tq), F32)],
        compiler_params=_cparams(2),
        name="dsa",
    )(r3(qa), r3(qidx), r3(widx), r3(kk), r3(ckv), ckvt, wk_bd, wvt_bd)
    return out.reshape(b * s, A_WIDTH)


def _block_diag_uk(w_uk):
    h, c, d = w_uk.shape
    eye = jnp.eye(h, dtype=w_uk.dtype)
    return jnp.einsum('hcd,hg->hdgc', w_uk, eye).reshape(h * d, h * c).astype(BF16)


def _block_diag_uv_t(w_uv):
    h, c, d = w_uv.shape
    eye = jnp.eye(h, dtype=w_uv.dtype)
    return jnp.einsum('hcd,hg->hdgc', w_uv, eye).reshape(h * d, h * c).astype(BF16)


def _diff_kernel(qb_ref, kb_ref, vbt_ref, lam_ref, subln_ref, out_ref,
                 qm_ref, m_ref, l_ref, acc_ref, *, tq, tk, lambda_init):
    i = pl.program_id(1)
    _split_halves(qb_ref, B_HEADS, qm_ref, tq)

    row = lax.broadcasted_iota(I32, (tk, tq), 0)
    tqi = i * tq + lax.broadcasted_iota(I32, (tk, tq), 1)
    qchunk = lax.shift_right_logical(tqi, CHUNK_SHIFT)
    _flash_init(m_ref, l_ref, acc_ref)

    def body(kt, carry):
        off = pl.multiple_of(kt * tk, tk)
        pos = off + row
        adm = lax.shift_right_logical(pos, CHUNK_SHIFT) <= qchunk
        dist = jnp.abs(tqi - pos).astype(F32)
        s_pairs = [lax.dot_general(kb_ref[pl.ds(off, tk), h * LANES:(h + 1) * LANES],
                                   qm_ref[2 * h * tq:(2 * h + 2) * tq, :], NT_DIMS,
                                   preferred_element_type=F32) for h in range(B_HEADS)]
        alphas, probs = [], []
        for h in range(B_HEADS):
            for j in range(2):
                sl = slice((2 * h + j) * tq, (2 * h + j + 1) * tq)
                sc = jnp.where(adm, s_pairs[h][:, j * tq:(j + 1) * tq]
                               - (2.0 ** (-2 * (h + 1))) * dist, NEG_BIG)
                alpha, p = _flash_probs(sl, sc, m_ref, l_ref)
                alphas.append(alpha)
                probs.append(p)
        pv = jnp.concatenate(
            [jnp.dot(vbt_ref[h * B_V_DIM:(h + 1) * B_V_DIM, pl.ds(off, tk)],
                     jnp.concatenate(probs[2 * h:2 * h + 2], axis=1), preferred_element_type=F32)
             for h in range(B_HEADS)], axis=1)
        acc_ref[...] = jnp.concatenate(alphas, axis=1) * acc_ref[...] + pv
        return carry

    lax.fori_loop(0, (i + 1) * (tq // tk), body, 0)

    lp = lam_ref[...]
    lam = (jnp.exp(jnp.sum(lp[0:1] * lp[1:2], axis=1, keepdims=True))
           - jnp.exp(jnp.sum(lp[2:3] * lp[3:4], axis=1, keepdims=True)) + lambda_init)
    o_all = acc_ref[...] * (1.0 / l_ref[...])
    outs = []
    for h in range(B_HEADS):
        o = (o_all[:, 2 * h * tq:(2 * h + 1) * tq]
             - lam * o_all[:, (2 * h + 1) * tq:(2 * h + 2) * tq])
        o = o * lax.rsqrt(jnp.mean(o * o, axis=0, keepdims=True) + RMS_EPS) * subln_ref[...]
        outs.append(o * (1.0 - lambda_init))
    out_ref[...] = jnp.concatenate(outs, axis=0).T.astype(BF16)


def _diff_call(qb, kb, vbt, lam_rows, subln, b, s, tq, tk, lambda_init):
    blk = pl.BlockSpec((None, tq, B_WIDTH), lambda bi, i: (bi, i, 0))
    full = pl.BlockSpec((None, s, B_WIDTH), lambda bi, i: (bi, 0, 0))
    r3 = lambda a: a.reshape(b, s, a.shape[-1])
    out = pl.pallas_call(
        functools.partial(_diff_kernel, tq=tq, tk=tk, lambda_init=lambda_init),
        grid=(b, s // tq),
        in_specs=[blk, full, pl.BlockSpec((B_WIDTH, s), lambda bi, i: (0, bi)),
                  pl.BlockSpec((SUBLANES, LANES), lambda bi, i: (0, 0)),
                  pl.BlockSpec((B_V_DIM, 1), lambda bi, i: (0, 0))],
        out_specs=blk,
        out_shape=jax.ShapeDtypeStruct((b, s, B_WIDTH), BF16),
        scratch_shapes=[pltpu.VMEM((2 * B_HEADS * tq, LANES), BF16),
                        pltpu.VMEM((1, 2 * B_HEADS * tq), F32),
                        pltpu.VMEM((1, 2 * B_HEADS * tq), F32),
                        pltpu.VMEM((B_V_DIM, 2 * B_HEADS * tq), F32)],
        compiler_params=_cparams(2),
        name="diff",
    )(r3(qb), r3(kb), vbt, lam_rows, subln.reshape(B_V_DIM, 1))
    return out.reshape(b * s, B_WIDTH)


def _mix_out_kernel(h_ref, a_ref, b_ref, wa_ref, wb_ref, g_ref, beta_ref, o_ref):
    y = (jnp.dot(a_ref[...], wa_ref[...], preferred_element_type=F32)
         + jnp.dot(b_ref[...], wb_ref[...], preferred_element_type=F32))
    o_ref[...] = _layer_norm(ALPHA * h_ref[...] + y, g_ref[...], beta_ref[...])


def _mix_out_call(h, out_a, out_b, w_o, g, beta, tm):
    n, d = h.shape
    w = w_o.astype(BF16)
    row = lambda wd: pl.BlockSpec((tm, wd), lambda i: (i, 0))
    const = lambda shp: pl.BlockSpec(shp, lambda i: (0, 0))
    return pl.pallas_call(
        _mix_out_kernel,
        grid=(n // tm,),
        in_specs=[row(d), row(A_WIDTH), row(B_WIDTH), const((A_WIDTH, d)), const((B_WIDTH, d)),
                  const((1, d)), const((1, d))],
        out_specs=row(d),
        out_shape=jax.ShapeDtypeStruct((n, d), F32),
        compiler_params=_cparams(1),
        name="mix_out",
    )(h, out_a, out_b, w[:A_WIDTH], w[A_WIDTH:], g.reshape(1, d), beta.reshape(1, d))


def _mem_kv_kernel(m_ref, w_ref, k_ref, v_ref):
    x = m_ref[...].astype(BF16)
    d = k_ref.shape[-1]
    k_ref[...] = jnp.dot(x, w_ref[:, :d], preferred_element_type=F32).astype(BF16)
    v_ref[...] = jnp.dot(x, w_ref[:, d:], preferred_element_type=F32).astype(BF16)


def _mem_kv_call(mem2d, wkv, tm):
    n, d = mem2d.shape
    return pl.pallas_call(
        _mem_kv_kernel,
        grid=(n // tm,),
        in_specs=[pl.BlockSpec((tm, d), lambda i: (i, 0)),
                  pl.BlockSpec((d, 2 * d), lambda i: (0, 0))],
        out_specs=[pl.BlockSpec((tm, d), lambda i: (i, 0))] * 2,
        out_shape=[jax.ShapeDtypeStruct((n, d), BF16)] * 2,
        compiler_params=_cparams(1),
        name="mem_kv",
    )(mem2d, wkv.astype(BF16))


def _mem_attn_kernel(h_ref, wq_ref, k_ref, v_ref, wo_ref, g_ref, beta_ref, o_ref):
    h = h_ref[...]
    q = jnp.dot(h.astype(BF16), wq_ref[...], preferred_element_type=F32)
    q = (q * (MEM_HEAD_DIM ** -0.5)).astype(BF16)
    outs = []
    for hd in range(MEM_HEADS):
        sl = slice(hd * MEM_HEAD_DIM, (hd + 1) * MEM_HEAD_DIM)
        sc = lax.dot_general(q[:, sl], k_ref[:, sl], NT_DIMS, preferred_element_type=F32)
        p = jnp.exp(sc - jnp.max(sc, axis=1, keepdims=True))
        den = jnp.sum(p, axis=1, keepdims=True)
        o = jnp.dot(p.astype(BF16), v_ref[:, sl], preferred_element_type=F32) / den
        outs.append(o.astype(BF16))
    y = jnp.dot(jnp.concatenate(outs, axis=1), wo_ref[...], preferred_element_type=F32)
    o_ref[...] = _layer_norm(ALPHA * h + y, g_ref[...], beta_ref[...])


def _mem_attn_call(h, wq, k, v, wo, g, beta, b, s, tm):
    n, d = h.shape
    m = k.shape[0] // b
    const = lambda shp: pl.BlockSpec(shp, lambda bi, i: (0,) * len(shp))
    row = pl.BlockSpec((None, tm, d), lambda bi, i: (bi, i, 0))
    kv = pl.BlockSpec((None, m, d), lambda bi, i: (bi, 0, 0))
    out = pl.pallas_call(
        _mem_attn_kernel,
        grid=(b, s // tm),
        in_specs=[row, const((d, d)), kv, kv, const((d, d)), const((1, d)), const((1, d))],
        out_specs=row,
        out_shape=jax.ShapeDtypeStruct((b, s, d), F32),
        compiler_params=_cparams(2),
        name="mem_attn",
    )(h.reshape(b, s, d), wq.astype(BF16), k.reshape(b, m, d), v.reshape(b, m, d),
      wo.astype(BF16), g.reshape(1, d), beta.reshape(1, d))
    return out.reshape(n, d)


def _router_kernel(h_ref, rw_ref, bias_ref, tri_ref, e_ref, g_ref, rank_ref, cnt_ref, run_ref):
    tm = h_ref.shape[0]

    @pl.when(pl.program_id(0) == 0)
    def _():
        run_ref[...] = jnp.zeros(run_ref.shape, F32)

    logits = jnp.dot(h_ref[...].astype(BF16), rw_ref[...], preferred_element_type=F32)
    scores = 1.0 / (1.0 + jnp.exp(-logits))
    lane = lax.broadcasted_iota(I32, (tm, LANES), 1)
    lanef = lane.astype(F32)
    ninf = jnp.float32(-jnp.inf)
    cur = jnp.where(lane < N_EXPERTS, scores + bias_ref[...], ninf)
    top_e = jnp.zeros((tm, LANES), F32)
    top_s = jnp.zeros((tm, LANES), F32)
    picked = jnp.zeros((tm, LANES), F32)
    firsts = []
    for k in range(TOP_K):
        mx = jnp.max(cur, axis=1, keepdims=True)
        first = jnp.min(jnp.where(cur == mx, lanef, float(LANES)), axis=1, keepdims=True)
        hit = lanef == first
        s_k = jnp.sum(jnp.where(hit, scores, 0.0), axis=1, keepdims=True)
        top_e = jnp.where(lane == k, first, top_e)
        top_s = jnp.where(lane == k, s_k, top_s)
        picked = jnp.where(hit, 1.0, picked)
        cur = jnp.where(hit, ninf, cur)
        firsts.append(first)
    e_ref[...] = top_e.astype(I32)
    g_ref[...] = top_s / jnp.sum(top_s, axis=1, keepdims=True) * ROUTED_SCALE

    slot = jnp.dot(tri_ref[...], picked.astype(BF16), preferred_element_type=F32) + run_ref[...]
    rank = jnp.zeros((tm, LANES), F32)
    for k in range(TOP_K):
        r_k = jnp.sum(jnp.where(lanef == firsts[k], slot, 0.0), axis=1, keepdims=True)
        rank = jnp.where(lane == k, r_k, rank)
    rank_ref[...] = rank.astype(I32)
    run = run_ref[...] + jnp.sum(picked, axis=0, keepdims=True)
    run_ref[...] = run
    cnt_ref[...] = run.astype(I32)


def _router_call(h, router_w, router_bias, tm):
    n, d = h.shape
    rw = jnp.pad(router_w, ((0, 0), (0, LANES - N_EXPERTS))).astype(BF16)
    bias = jnp.pad(router_bias, (0, LANES - N_EXPERTS)).reshape(1, LANES)
    tri = jnp.tri(tm, k=-1, dtype=BF16)
    row = pl.BlockSpec((tm, LANES), lambda i: (i, 0))
    const = lambda shp: pl.BlockSpec(shp, lambda i: (0, 0))
    return pl.pallas_call(
        _router_kernel,
        grid=(n // tm,),
        in_specs=[pl.BlockSpec((tm, d), lambda i: (i, 0)), const((d, LANES)), const((1, LANES)),
                  const((tm, tm))],
        out_specs=[row, row, row, const((1, LANES))],
        out_shape=[jax.ShapeDtypeStruct((n, LANES), I32),
                   jax.ShapeDtypeStruct((n, LANES), F32),
                   jax.ShapeDtypeStruct((n, LANES), I32),
                   jax.ShapeDtypeStruct((1, LANES), I32)],
        scratch_shapes=[pltpu.VMEM((1, LANES), F32)],
        compiler_params=_cparams(1),
        name="router",
    )(h, rw, bias, tri)


def _dispatch_kernel(dest_ref, x_ref, xs_ref, sem, *, td):
    def row_copy(g, rr, k):
        r0 = pl.multiple_of(g * SUBLANES, SUBLANES)
        return pltpu.make_async_copy(
            x_ref.at[pl.ds(r0 + rr, 1), :],
            xs_ref.at[pl.ds(dest_ref[g * (SUBLANES * TOP_K) + rr * TOP_K + k], 1), :], sem)

    def issue(g, carry):
        for rr in range(SUBLANES):
            for k in range(TOP_K):
                row_copy(g, rr, k).start()
        return carry

    lax.fori_loop(0, td // SUBLANES, issue, 0)

    def drain(g, carry):
        for rr in range(SUBLANES):
            for k in range(TOP_K):
                row_copy(g, rr, k).wait()
        return carry

    lax.fori_loop(0, td // SUBLANES, drain, 0)


def _dispatch_call(h, dest, td):
    n, d = h.shape
    m = n * TOP_K
    return pl.pallas_call(
        functools.partial(_dispatch_kernel, td=td),
        grid=(n // td,),
        in_specs=[pl.BlockSpec((td * TOP_K,), lambda i: (i,), memory_space=pltpu.SMEM),
                  pl.BlockSpec((td, d), lambda i: (i, 0))],
        out_specs=pl.BlockSpec(memory_space=pl.ANY),
        out_shape=jax.ShapeDtypeStruct((m, d), F32),
        scratch_shapes=[pltpu.SemaphoreType.DMA(())],
        compiler_params=_cparams(1),
        name="dispatch",
    )(dest, h)


def _moe_mm_kernel(ptile_ref, pexp_ref, plo_ref, phi_ref, xs_ref, wg_ref, wu_ref, wd_ref, y_ref,
                   *, tmm):
    p = pl.program_id(0)
    tile = ptile_ref[p]
    lo = plo_ref[p]
    hi = phi_ref[p]
    first = jnp.logical_or(p == 0, ptile_ref[jnp.maximum(p - 1, 0)] != tile)

    def expert_out():
        x = xs_ref[...].astype(BF16)
        g = jnp.dot(x, wg_ref[...], preferred_element_type=F32)
        u = jnp.dot(x, wu_ref[...], preferred_element_type=F32)
        hid = (_silu(g) * u).astype(BF16)
        return jnp.dot(hid, wd_ref[...], preferred_element_type=F32)

    whole = jnp.logical_and(lo <= tile * tmm, hi >= (tile + 1) * tmm)

    @pl.when(whole)
    def _():
        y_ref[...] = expert_out()

    @pl.when(jnp.logical_and(first, jnp.logical_not(whole)))
    def _():
        y_ref[...] = jnp.zeros(y_ref.shape, F32)

    @pl.when(jnp.logical_and(lo < hi, jnp.logical_not(whole)))
    def _():
        y = expert_out()
        rowg = tile * tmm + lax.broadcasted_iota(I32, (tmm, 1), 0)
        y_ref[...] += jnp.where(jnp.logical_and(rowg >= lo, rowg < hi), y, 0.0)


def _moe_mm_call(xs, pairs, wg, wu, wd, tmm):
    m = xs.shape[0]
    n_pairs = pairs[0].shape[0]
    rows = pl.BlockSpec((tmm, D_MODEL), lambda p, pt, pe, plo, phi: (pt[p], 0))
    wspec = lambda shp: pl.BlockSpec((None,) + shp, lambda p, pt, pe, plo, phi: (pe[p], 0, 0))
    return pl.pallas_call(
        functools.partial(_moe_mm_kernel, tmm=tmm),
        grid_spec=pltpu.PrefetchScalarGridSpec(
            num_scalar_prefetch=4,
            grid=(n_pairs,),
            in_specs=[rows, wspec((D_MODEL, EXPERT_DIM)), wspec((D_MODEL, EXPERT_DIM)),
                      wspec((EXPERT_DIM, D_MODEL))],
            out_specs=rows),
        out_shape=jax.ShapeDtypeStruct((m, D_MODEL), F32),
        compiler_params=_cparams(1),
        name="moe_mm",
    )(*pairs, xs, wg, wu, wd)


def _group_pairs(counts, m, tmm):
    n_tiles = m // tmm
    n_pairs = n_tiles + N_EXPERTS
    gend = jnp.cumsum(counts)
    gstart = gend - counts
    t0 = jnp.arange(n_tiles, dtype=I32) * tmm
    n_le = lambda edges, v: jnp.sum(edges[None, :] <= v[:, None], axis=1).astype(I32)
    e_first = n_le(gend, t0)
    e_last = n_le(gend, t0 + (tmm - 1))
    per_tile = e_last - e_first + 1
    pend = jnp.cumsum(per_tile)
    pstart = pend - per_tile
    p = jnp.arange(n_pairs, dtype=I32)
    valid = p < pend[-1]
    tile = jnp.minimum(n_le(pend, p), n_tiles - 1)
    e = jnp.clip(e_first[tile] + (p - pstart[tile]), 0, N_EXPERTS - 1).astype(I32)
    lo = jnp.where(valid, gstart[e], 0).astype(I32)
    hi = jnp.where(valid, gend[e], 0).astype(I32)
    return tile, e, lo, hi


def _combine_kernel(dest_ref, gate_ref, h_ref, y_ref, sg_ref, su_ref, sd_ref, g_ref, beta_ref, o_ref,
                    buf_ref, sem, *, tc):
    def row_copy(g, rr, k):
        r0 = pl.multiple_of(g * SUBLANES, SUBLANES)
        return pltpu.make_async_copy(
            y_ref.at[pl.ds(dest_ref[g * (SUBLANES * TOP_K) + rr * TOP_K + k], 1), :],
            buf_ref.at[k, pl.ds(r0 + rr, 1), :], sem)

    def issue(g, carry):
        for rr in range(SUBLANES):
            for k in range(TOP_K):
                row_copy(g, rr, k).start()
        return carry

    lax.fori_loop(0, tc // SUBLANES, issue, 0)

    h = h_ref[...]
    x = h.astype(BF16)
    hid = (_silu(jnp.dot(x, sg_ref[...], preferred_element_type=F32))
           * jnp.dot(x, su_ref[...], preferred_element_type=F32)).astype(BF16)
    shared = jnp.dot(hid, sd_ref[...], preferred_element_type=F32)

    def drain(g, carry):
        for rr in range(SUBLANES):
            for k in range(TOP_K):
                row_copy(g, rr, k).wait()
        return carry

    lax.fori_loop(0, tc // SUBLANES, drain, 0)

    gates = gate_ref[...]
    routed = gates[:, 0:1] * buf_ref[0]
    for k in range(1, TOP_K):
        routed = routed + gates[:, k:k + 1] * buf_ref[k]
    o_ref[...] = _layer_norm(ALPHA * h + (routed + shared), g_ref[...], beta_ref[...])


def _combine_call(h, y, dest, gates, sg, su, sd, g, beta, tc):
    n, d = h.shape
    smem = pl.BlockSpec((tc * TOP_K,), lambda i: (i,), memory_space=pltpu.SMEM)
    const = lambda shp: pl.BlockSpec(shp, lambda i: (0, 0))
    return pl.pallas_call(
        functools.partial(_combine_kernel, tc=tc),
        grid=(n // tc,),
        in_specs=[smem, pl.BlockSpec((tc, LANES), lambda i: (i, 0)),
                  pl.BlockSpec((tc, d), lambda i: (i, 0)),
                  pl.BlockSpec(memory_space=pl.ANY),
                  const((d, SHARED_DIM)), const((d, SHARED_DIM)), const((SHARED_DIM, d)),
                  const((1, d)), const((1, d))],
        out_specs=pl.BlockSpec((tc, d), lambda i: (i, 0)),
        out_shape=jax.ShapeDtypeStruct((n, d), F32),
        scratch_shapes=[pltpu.VMEM((TOP_K, tc, d), F32),
                        pltpu.SemaphoreType.DMA(())],
        compiler_params=_cparams(1),
        name="combine",
    )(dest, gates, h, y, sg.astype(BF16), su.astype(BF16), sd.astype(BF16),
      g.reshape(1, d), beta.reshape(1, d))


def _tiles(b, s):
    n = b * s
    pick = lambda pref, total: next(t for t in (pref, 512, 256, 128, 64, 32, 16, 8) if t <= pref and total % t == 0)
    return dict(
        rows=pick(512, n),
        attn_q=pick(256, s),
        attn_k=pick(256, s),
        mem_rows=pick(512, s),
        mem_kv=pick(512, b * 256),
        router=pick(1024, n),
        dispatch=pick(256, n),
        moe=pick(256, n * TOP_K),
        combine=pick(128, n),
    )


def kernel(x, mem, ln_in_g, ln_in_b, w_in, a_kv_norm, a_w_uk, a_w_uv, b_lq1, b_lk1, b_lq2, b_lk2,
           b_subln, w_o, ln1_g, ln1_b, m_wq, m_wkv, m_wo, ln2_g, ln2_b, router_w, router_bias,
           e_w_gate, e_w_up, e_w_down, s_w_gate, s_w_up, s_w_down, ln3_g, ln3_b):
    b, s, d = x.shape
    n = b * s
    m = n * TOP_K
    t = _tiles(b, s)
    mem2d = mem.reshape(-1, d)

    h = _ln_call(x.reshape(n, d), ln_in_g, ln_in_b, t["rows"])
    for l in range(DEPTH):
        lambda_init = 0.8 - 0.6 * math.exp(-0.3 * l)

        qa, ckv, ckvt, qidx, kk, widx, qb, kb, vbt = _proj_call(h, _pad_w_in(w_in[l]), a_kv_norm[l],
                                                                  t["rows"])
        out_a = _dsa_call(qa, qidx, widx, kk, ckv, ckvt, _block_diag_uk(a_w_uk[l]),
                          _block_diag_uv_t(a_w_uv[l]), b, s, t["attn_q"], t["attn_k"])
        lam_rows = jnp.pad(jnp.stack([b_lq1[l], b_lk1[l], b_lq2[l], b_lk2[l]]),
                           ((0, SUBLANES - 4), (0, LANES - B_QK_DIM)))
        out_b = _diff_call(qb, kb, vbt, lam_rows, b_subln[l], b, s, t["attn_q"], t["attn_k"], lambda_init)
        h = _mix_out_call(h, out_a, out_b, w_o[l], ln1_g[l], ln1_b[l], t["rows"])

        mk, mv = _mem_kv_call(mem2d, m_wkv[l], t["mem_kv"])
        h = _mem_attn_call(h, m_wq[l], mk, mv, m_wo[l], ln2_g[l], ln2_b[l], b, s, t["mem_rows"])

        top_e, gates, rank, counts = _router_call(h, router_w[l], router_bias[l], t["router"])
        counts = counts[0, :N_EXPERTS]
        gstart = jnp.cumsum(counts) - counts
        top_e = top_e[:, :TOP_K]
        dest = (rank[:, :TOP_K] + gstart[top_e]).reshape(m)
        xs = _dispatch_call(h, dest, t["dispatch"])
        pairs = _group_pairs(counts, m, t["moe"])
        y = _moe_mm_call(xs, pairs, e_w_gate[l].astype(BF16), e_w_up[l].astype(BF16),
                         e_w_down[l].astype(BF16), t["moe"])
        h = _combine_call(h, y, dest, gates, s_w_gate[l], s_w_up[l], s_w_down[l],
                          ln3_g[l], ln3_b[l], t["combine"])
    return h.reshape(b, s, d)
```

```python
import functools
import math

import jax
import jax.numpy as jnp
import numpy as np
from jax import lax
from jax.experimental import pallas as pl
from jax.experimental.pallas import tpu as pltpu

F32 = jnp.float32
BF16 = jnp.bfloat16
I32 = jnp.int32
I16 = jnp.int16

D_MODEL = 1024
DEPTH = 2
CHUNK = 64
CHUNK_SHIFT = 6
A_HEADS = 8
A_HEAD_DIM = 64
A_WIDTH = A_HEADS * A_HEAD_DIM
KV_LORA = 128
IDX_HEADS = 8
IDX_DIM = 64
INDEX_TOPK_MAX = 256
B_HEADS = 4
B_QK_DIM = 64
B_V_DIM = 128
B_WIDTH = B_HEADS * B_V_DIM
MEM_HEADS = 4
MEM_HEAD_DIM = D_MODEL // MEM_HEADS
N_EXPERTS = 64
TOP_K = 8
EXPERT_DIM = 256
SHARED_DIM = 256
ROUTED_SCALE = 2.5
ALPHA = (2 * DEPTH) ** 0.25
LN_EPS = 1e-5
RMS_EPS = 1e-6

LANES = 128
SUBLANES = 8
VMEM_LIMIT = 56 * 1024 * 1024
INT_MIN = -2 ** 31
I16_MIN = -2 ** 15
V_PAD = 16
NEG_BIG = -3.0e38
M_INIT = -1.0e30

NT_DIMS = (((1,), (1,)), ((), ()))


def _cparams(n_axes):
    return pltpu.CompilerParams(dimension_semantics=("arbitrary",) * n_axes,
                                vmem_limit_bytes=VMEM_LIMIT)


def _layer_norm(x, g, b):
    mu = jnp.mean(x, axis=-1, keepdims=True)
    xc = x - mu
    var = jnp.mean(xc * xc, axis=-1, keepdims=True)
    return xc * lax.rsqrt(var + LN_EPS) * g + b


def _silu(x):
    return x * (1.0 / (1.0 + jnp.exp(-x)))


def _ln_kernel(x_ref, g_ref, b_ref, o_ref):
    o_ref[...] = _layer_norm(x_ref[...], g_ref[...], b_ref[...])


def _ln_call(x, g, b, tm):
    n, d = x.shape
    return pl.pallas_call(
        _ln_kernel,
        grid=(n // tm,),
        in_specs=[pl.BlockSpec((tm, d), lambda i: (i, 0)),
                  pl.BlockSpec((1, d), lambda i: (0, 0)),
                  pl.BlockSpec((1, d), lambda i: (0, 0))],
        out_specs=pl.BlockSpec((tm, d), lambda i: (i, 0)),
        out_shape=jax.ShapeDtypeStruct((n, d), F32),
        compiler_params=_cparams(1),
        name="ln_in",
    )(x, g.reshape(1, d), b.reshape(1, d))


_SEG_QA = (0, 512)
_SEG_CKV = (512, 640)
_SEG_QIDX = (640, 1152)
_SEG_KK = (1152, 1280)
_SEG_WIDX = (1280, 1408)
_SEG_QB = (1408, 1920)
_SEG_KB = (1920, 2432)
_SEG_VB = (2432, 2944)
_PROJ_COLS = 2944


def _pos_features(pos, shape):
    lane = lax.broadcasted_iota(I32, shape, 1)
    hi = lax.shift_right_logical(pos, CHUNK_SHIFT).astype(F32)
    lo = (pos & (CHUNK - 1)).astype(F32)
    return jnp.where(lane < 2, 1.0, jnp.where(lane == 2, hi, jnp.where(lane == 3, lo, 0.0)))


def _slope_features(tqi, slope, shape):
    lane = lax.broadcasted_iota(I32, shape, 1)
    hi = lax.shift_right_logical(tqi, CHUNK_SHIFT).astype(F32)
    lo = (tqi & (CHUNK - 1)).astype(F32)
    return jnp.where(lane == 0, -slope * CHUNK * hi,
                     jnp.where(lane == 1, -slope * lo,
                               jnp.where(lane == 2, slope * CHUNK,
                                         jnp.where(lane == 3, slope, 0.0))))


def _with_ones_rows(vt):
    t = vt.shape[1]
    sub = lax.broadcasted_iota(I32, (V_PAD, t), 0)
    return jnp.concatenate([vt, jnp.where(sub == 0, 1.0, 0.0)], axis=0)


def _proj_kernel(h_ref, w_ref, kvn_ref, qa_ref, ckv_ref, ckvt_ref, qidx_ref, kk_ref, widx_ref,
                 qb_ref, kb_ref, vbt_ref, *, seq):
    tm = h_ref.shape[0]
    x = h_ref[...].astype(BF16)

    def seg(s):
        return jnp.dot(x, w_ref[:, s[0]:s[1]], preferred_element_type=F32)

    row = pl.program_id(0) * tm + lax.broadcasted_iota(I32, (tm, LANES), 0)
    feat = _pos_features(row & (seq - 1), (tm, LANES)).astype(BF16)

    qa_ref[...] = (seg(_SEG_QA) * (A_HEAD_DIM ** -0.5)).astype(BF16)
    c = seg(_SEG_CKV)
    c = c * lax.rsqrt(jnp.mean(c * c, axis=-1, keepdims=True) + RMS_EPS) * kvn_ref[...]
    ckv_ref[:, :KV_LORA] = c.astype(BF16)
    ckv_ref[:, KV_LORA:] = feat
    ckvt_ref[...] = _with_ones_rows(c.T).astype(BF16)
    qidx_ref[...] = (seg(_SEG_QIDX) * (IDX_DIM ** -0.5)).astype(BF16)
    kk_ref[...] = seg(_SEG_KK).astype(BF16)
    widx_ref[...] = seg(_SEG_WIDX) * (IDX_HEADS ** -0.5)
    qb_ref[...] = (seg(_SEG_QB) * (B_QK_DIM ** -0.5)).astype(BF16)
    kb = seg(_SEG_KB).astype(BF16)
    vt = seg(_SEG_VB).T
    for h in range(B_HEADS):
        kb_ref[:, 2 * h * LANES:(2 * h + 1) * LANES] = kb[:, h * LANES:(h + 1) * LANES]
        kb_ref[:, (2 * h + 1) * LANES:(2 * h + 2) * LANES] = feat
        vbt_ref[h * (B_V_DIM + V_PAD):(h + 1) * (B_V_DIM + V_PAD), :] = _with_ones_rows(
            vt[h * B_V_DIM:(h + 1) * B_V_DIM]).astype(BF16)


def _proj_call(h, w_pad, kvn, tm, seq):
    n, d = h.shape
    assert seq & (seq - 1) == 0 and seq <= CHUNK * 256 and seq % tm == 0
    outs = [(512, BF16, False), (2 * KV_LORA, BF16, False), (KV_LORA + V_PAD, BF16, True),
            (512, BF16, False), (128, BF16, False), (128, F32, False), (512, BF16, False),
            (2 * B_HEADS * LANES, BF16, False), (B_HEADS * (B_V_DIM + V_PAD), BF16, True)]
    spec = lambda w, t: (pl.BlockSpec((w, tm), lambda i: (0, i)) if t
                         else pl.BlockSpec((tm, w), lambda i: (i, 0)))
    shape = lambda w, dt, t: jax.ShapeDtypeStruct((w, n) if t else (n, w), dt)
    return pl.pallas_call(
        functools.partial(_proj_kernel, seq=seq),
        grid=(n // tm,),
        in_specs=[pl.BlockSpec((tm, d), lambda i: (i, 0)),
                  pl.BlockSpec((d, _PROJ_COLS), lambda i: (0, 0)),
                  pl.BlockSpec((1, KV_LORA), lambda i: (0, 0))],
        out_specs=[spec(w, t) for w, _, t in outs],
        out_shape=[shape(w, dt, t) for w, dt, t in outs],
        compiler_params=_cparams(1),
        name="proj_in",
    )(h, w_pad, kvn.reshape(1, KV_LORA))


def _pad_w_in(w_in):
    sizes = (A_WIDTH, KV_LORA, IDX_HEADS * IDX_DIM, IDX_DIM, IDX_HEADS,
             2 * B_HEADS * B_QK_DIM, 2 * B_HEADS * B_QK_DIM, B_WIDTH)
    splits = np.cumsum(sizes)[:-1].tolist()
    q_a, c_kv, q_idx, k_idx, w_idx, q_b, k_b, v_b = jnp.split(w_in, splits, axis=-1)
    w_idx = jnp.pad(w_idx, ((0, 0), (0, LANES - IDX_HEADS)))
    return jnp.concatenate([q_a, c_kv, q_idx, k_idx, k_idx, w_idx, q_b, k_b, v_b],
                           axis=-1).astype(BF16)


def _flash_probs(sl, sc, m_ref):
    m_old = m_ref[:, sl]
    m_new = jnp.maximum(m_old, jnp.max(sc, axis=0, keepdims=True))
    m_ref[:, sl] = m_new
    return jnp.exp(m_old - m_new), jnp.exp(sc - m_new).astype(BF16)


def _flash_init(m_ref, acc_ref):
    m_ref[...] = jnp.full(m_ref.shape, M_INIT, F32)
    acc_ref[...] = jnp.zeros(acc_ref.shape, F32)


def _split_halves(q_ref, n_pairs, out_ref, rows):
    lane = lax.broadcasted_iota(I32, (rows, LANES), 1)
    for j in range(n_pairs):
        qp = q_ref[:, j * LANES:(j + 1) * LANES].astype(F32)
        out_ref[2 * j * rows:(2 * j + 1) * rows, :LANES] = jnp.where(lane < 64, qp, 0.0).astype(BF16)
        out_ref[(2 * j + 1) * rows:(2 * j + 2) * rows, :LANES] = jnp.where(lane >= 64, qp, 0.0).astype(BF16)


def _dsa_kernel(qa_ref, qidx_ref, widx_ref, kk_ref, ckv_ref, ckvt_ref, wk_ref, wvt_ref, out_ref,
                keys_ref, khi_ref, klo_ref, qim_ref, qlat_ref, tj_ref, m_ref, acc_ref,
                *, tq, tk, seq, topk):
    i = pl.program_id(1)
    n_off = i * (tq // tk)
    n_kt = n_off + tq // tk
    idx_bits = int(math.log2(seq))

    _split_halves(qidx_ref, IDX_HEADS // 2, qim_ref, tq)
    qlat = jnp.dot(qa_ref[...], wk_ref[...], preferred_element_type=F32)
    t_col = i * tq + lax.broadcasted_iota(I32, (tq, LANES), 0)
    for h in range(A_HEADS):
        qlat_ref[h * tq:(h + 1) * tq, :KV_LORA] = qlat[:, h * KV_LORA:(h + 1) * KV_LORA].astype(BF16)
        qlat_ref[h * tq:(h + 1) * tq, KV_LORA:] = _slope_features(
            t_col, 2.0 ** -(h + 1), (tq, LANES)).astype(BF16)
    w_t = widx_ref[...].T

    row = lax.broadcasted_iota(I32, (tk, tq), 0)
    tqi = i * tq + lax.broadcasted_iota(I32, (tk, tq), 1)
    qchunk = lax.shift_right_logical(tqi, CHUNK_SHIFT)

    def idx_body(kt, carry):
        off = pl.multiple_of(kt * tk, tk)
        kkt = kk_ref[pl.ds(off, tk), :]
        s_all = lax.dot_general(kkt, qim_ref[...], NT_DIMS, preferred_element_type=F32)
        acc = jnp.zeros((tk, tq), F32)
        for h in range(IDX_HEADS):
            acc = acc + w_t[h:h + 1, :] * jnp.maximum(s_all[:, h * tq:(h + 1) * tq], 0.0)
        bits = lax.bitcast_convert_type(acc + 0.0, I32)
        key = jnp.where(bits < 0, bits ^ jnp.int32(0x7FFFFFFF), bits)
        adm = lax.shift_right_logical(off + row, CHUNK_SHIFT) <= qchunk
        key = jnp.where(adm, key, jnp.int32(INT_MIN))
        keys_ref[pl.ds(off, tk), :] = key
        khi_ref[pl.ds(off, tk), :] = lax.shift_right_arithmetic(key, 16).astype(I16)
        klo_ref[pl.ds(off, tk), :] = ((key & 0xFFFF) + I16_MIN).astype(I16)
        return carry

    lax.fori_loop(0, n_kt, idx_body, 0)

    def count(pred):
        def body(kt, c):
            off = pl.multiple_of(kt * tk, tk)
            ind = jnp.where(pred(keys_ref[pl.ds(off, tk), :], off + row), 1.0, 0.0)
            return c + jnp.sum(ind, axis=0, keepdims=True)
        return lax.fori_loop(0, n_kt, body, jnp.zeros((1, tq), F32))

    def count16(ref, pred):
        pack = 2 * SUBLANES
        def body(kt, c):
            off = pl.multiple_of(kt * tk, tk)
            ind = jnp.where(pred(ref[pl.ds(off, tk), :]), jnp.int16(1), jnp.int16(0))
            parts = [ind[j * pack:(j + 1) * pack] for j in range(tk // pack)]
            while len(parts) > 1:
                parts = [parts[j] + parts[j + 1] for j in range(0, len(parts), 2)]
            return c + parts[0]
        c = lax.fori_loop(0, n_kt, body, jnp.zeros((pack, tq), I16))
        return jnp.sum(c.astype(F32), axis=0, keepdims=True)

    def search16(ref, need):
        def body(it, t):
            cand = t + lax.shift_left(jnp.int32(1), 15 - it)
            cand16 = cand.astype(I16)
            cnt = count16(ref, lambda tile: tile >= cand16)
            return jnp.where(cnt >= need, cand, t)
        return lax.fori_loop(0, 16, body, jnp.full((1, tq), I16_MIN, I32))

    t_hi = search16(khi_ref, topk)
    t_hi16 = t_hi.astype(I16)
    n_above = count16(khi_ref, lambda tile: tile > t_hi16)

    def park_body(kt, carry):
        off = pl.multiple_of(kt * tk, tk)
        klo_ref[pl.ds(off, tk), :] = jnp.where(khi_ref[pl.ds(off, tk), :] == t_hi16,
                                               klo_ref[pl.ds(off, tk), :], jnp.int16(I16_MIN))
        return carry

    lax.fori_loop(0, n_kt, park_body, 0)
    t_lo = search16(klo_ref, topk - n_above)
    thr = t_hi * 65536 + (t_lo - I16_MIN)

    live = thr > INT_MIN
    n_ge = count(lambda tile, _: tile >= thr)
    excess = jnp.logical_and(n_ge > topk, live)
    all_ties = jnp.where(live, seq, -1).astype(I32)
    tj_ref[0] = jnp.broadcast_to(thr, (SUBLANES, tq))
    tj_ref[1] = jnp.broadcast_to(all_ties, (SUBLANES, tq))

    @pl.when(jnp.max(jnp.where(excess, 1.0, 0.0)) > 0.0)
    def _():
        need = topk - count(lambda tile, _: tile > thr)

        def tie_body(it, hi):
            cand = hi + lax.shift_left(jnp.int32(1), idx_bits - 1 - it)
            cnt = count(lambda tile, pos: jnp.logical_and(tile == thr, pos < cand))
            return jnp.where(cnt < need, cand, hi)

        hi = lax.fori_loop(0, idx_bits, tie_body, jnp.zeros((1, tq), I32))
        tj_ref[1] = jnp.broadcast_to(jnp.where(excess, hi, all_ties), (SUBLANES, tq))

    thr_b = tj_ref[0][:1]
    tie_hi = tj_ref[1][:1]

    _flash_init(m_ref, acc_ref)

    def att_tile(kt, late_fix):
        off = pl.multiple_of(kt * tk, tk)
        tile = keys_ref[pl.ds(off, tk), :]
        pos = off + row
        sel = jnp.logical_or(tile > thr_b,
                             jnp.logical_and(tile == thr_b, pos <= tie_hi))
        s_all = lax.dot_general(ckv_ref[pl.ds(off, tk), :], qlat_ref[...], NT_DIMS,
                                preferred_element_type=F32)
        if late_fix:
            late = jnp.maximum(pos - tqi, 0).astype(F32)
        alphas, probs = [], []
        for h in range(A_HEADS):
            sl = slice(h * tq, (h + 1) * tq)
            sc = s_all[:, sl]
            if late_fix:
                sc = sc - (2.0 * 2.0 ** -(h + 1)) * late
            alpha, p = _flash_probs(sl, jnp.where(sel, sc, NEG_BIG), m_ref)
            alphas.append(alpha)
            probs.append(p)
        pv = jnp.dot(ckvt_ref[:, pl.ds(off, tk)], jnp.concatenate(probs, axis=1),
                     preferred_element_type=F32)
        acc_ref[...] = jnp.concatenate(alphas, axis=1) * acc_ref[...] + pv

    def off_body(kt, carry):
        att_tile(kt, False)
        return carry

    def diag_body(kt, carry):
        att_tile(kt, True)
        return carry

    lax.fori_loop(0, n_off, off_body, 0)
    lax.fori_loop(n_off, n_kt, diag_body, 0)

    inv_l = 1.0 / acc_ref[KV_LORA:KV_LORA + 1, :]
    olat = jnp.concatenate([(acc_ref[:KV_LORA, h * tq:(h + 1) * tq]
                             * inv_l[:, h * tq:(h + 1) * tq]).astype(BF16)
                            for h in range(A_HEADS)], axis=0)
    out_t = jnp.dot(wvt_ref[...], olat, preferred_element_type=F32)
    out_ref[...] = out_t.T.astype(BF16)


def _dsa_call(qa, qidx, widx, kk, ckv, ckvt, wk_bd, wvt_bd, b, s, tq, tk):
    topk = min(INDEX_TOPK_MAX, s // 4)
    assert s % tq == 0 and tq % tk == 0 and tk % LANES == 0 and s & (s - 1) == 0
    blk = lambda w: pl.BlockSpec((None, tq, w), lambda bi, i: (bi, i, 0))
    full = lambda w: pl.BlockSpec((None, s, w), lambda bi, i: (bi, 0, 0))
    const = lambda a: pl.BlockSpec(a.shape, lambda bi, i: (0, 0))
    r3 = lambda a: a.reshape(b, s, a.shape[-1])
    out = pl.pallas_call(
        functools.partial(_dsa_kernel, tq=tq, tk=tk, seq=s, topk=topk),
        grid=(b, s // tq),
        in_specs=[blk(A_WIDTH), blk(IDX_HEADS * IDX_DIM), blk(LANES), full(LANES),
                  full(2 * KV_LORA),
                  pl.BlockSpec((KV_LORA + V_PAD, s), lambda bi, i: (0, bi)),
                  const(wk_bd), const(wvt_bd)],
        out_specs=blk(A_WIDTH),
        out_shape=jax.ShapeDtypeStruct((b, s, A_WIDTH), BF16),
        scratch_shapes=[pltpu.VMEM((s, tq), I32),
                        pltpu.VMEM((s, tq), I16),
                        pltpu.VMEM((s, tq), I16),
                        pltpu.VMEM((IDX_HEADS * tq, LANES), BF16),
                        pltpu.VMEM((A_HEADS * tq, 2 * KV_LORA), BF16),
                        pltpu.VMEM((2, SUBLANES, tq), I32),
                        pltpu.VMEM((1, A_HEADS * tq), F32),
                        pltpu.VMEM((KV_LORA + V_PAD, A_HEADS * tq), F32)],
        compiler_params=_cparams(2),
        name="dsa",
    )(r3(qa), r3(qidx), r3(widx), r3(kk), r3(ckv), ckvt, wk_bd, wvt_bd)
    return out.reshape(b * s, A_WIDTH)


def _block_diag_uk(w_uk):
    h, c, d = w_uk.shape
    eye = jnp.eye(h, dtype=w_uk.dtype)
    return jnp.einsum('hcd,hg->hdgc', w_uk, eye).reshape(h * d, h * c).astype(BF16)


def _block_diag_uv_t(w_uv):
    h, c, d = w_uv.shape
    eye = jnp.eye(h, dtype=w_uv.dtype)
    return jnp.einsum('hcd,hg->hdgc', w_uv, eye).reshape(h * d, h * c).astype(BF16)


def _diff_kernel(qb_ref, kb_ref, vbt_ref, lam_ref, subln_ref, out_ref,
                 qm_ref, m_ref, acc_ref, *, tq, tk, lambda_init):
    i = pl.program_id(1)
    n_off = i * (tq // tk)
    dvp = B_V_DIM + V_PAD
    _split_halves(qb_ref, B_HEADS, qm_ref, tq)
    t_col = i * tq + lax.broadcasted_iota(I32, (tq, LANES), 0)
    for h in range(B_HEADS):
        feat = _slope_features(t_col, 2.0 ** (-2 * (h + 1)), (tq, LANES)).astype(BF16)
        qm_ref[2 * h * tq:(2 * h + 1) * tq, LANES:] = feat
        qm_ref[(2 * h + 1) * tq:(2 * h + 2) * tq, LANES:] = feat

    row = lax.broadcasted_iota(I32, (tk, tq), 0)
    tqi = i * tq + lax.broadcasted_iota(I32, (tk, tq), 1)
    qchunk = lax.shift_right_logical(tqi, CHUNK_SHIFT)
    _flash_init(m_ref, acc_ref)

    def tile_step(kt, diag):
        off = pl.multiple_of(kt * tk, tk)
        s_pairs = [lax.dot_general(kb_ref[pl.ds(off, tk), 2 * h * LANES:(2 * h + 2) * LANES],
                                   qm_ref[2 * h * tq:(2 * h + 2) * tq, :], NT_DIMS,
                                   preferred_element_type=F32) for h in range(B_HEADS)]
        if diag:
            pos = off + row
            adm = lax.shift_right_logical(pos, CHUNK_SHIFT) <= qchunk
            late = jnp.maximum(pos - tqi, 0).astype(F32)
        alphas, probs = [], []
        for h in range(B_HEADS):
            for j in range(2):
                sl = slice((2 * h + j) * tq, (2 * h + j + 1) * tq)
                sc = s_pairs[h][:, j * tq:(j + 1) * tq]
                if diag:
                    sc = jnp.where(adm, sc - (2.0 * 2.0 ** (-2 * (h + 1))) * late, NEG_BIG)
                alpha, p = _flash_probs(sl, sc, m_ref)
                alphas.append(alpha)
                probs.append(p)
        pv = jnp.concatenate(
            [jnp.dot(vbt_ref[h * dvp:(h + 1) * dvp, pl.ds(off, tk)],
                     jnp.concatenate(probs[2 * h:2 * h + 2], axis=1), preferred_element_type=F32)
             for h in range(B_HEADS)], axis=1)
        acc_ref[...] = jnp.concatenate(alphas, axis=1) * acc_ref[...] + pv

    def off_body(kt, carry):
        tile_step(kt, False)
        return carry

    def diag_body(kt, carry):
        tile_step(kt, True)
        return carry

    lax.fori_loop(0, n_off, off_body, 0)
    lax.fori_loop(n_off, n_off + tq // tk, diag_body, 0)

    lp = lam_ref[...]
    lam = (jnp.exp(jnp.sum(lp[0:1] * lp[1:2], axis=1, keepdims=True))
           - jnp.exp(jnp.sum(lp[2:3] * lp[3:4], axis=1, keepdims=True)) + lambda_init)
    o_all = acc_ref[:B_V_DIM, :] * (1.0 / acc_ref[B_V_DIM:B_V_DIM + 1, :])
    outs = []
    for h in range(B_HEADS):
        o = (o_all[:, 2 * h * tq:(2 * h + 1) * tq]
             - lam * o_all[:, (2 * h + 1) * tq:(2 * h + 2) * tq])
        o = o * lax.rsqrt(jnp.mean(o * o, axis=0, keepdims=True) + RMS_EPS) * subln_ref[...]
        outs.append(o * (1.0 - lambda_init))
    out_ref[...] = jnp.concatenate(outs, axis=0).T.astype(BF16)


def _diff_call(qb, kb, vbt, lam_rows, subln, b, s, tq, tk, lambda_init):
    blk = pl.BlockSpec((None, tq, B_WIDTH), lambda bi, i: (bi, i, 0))
    kw = 2 * B_HEADS * LANES
    dvp = B_V_DIM + V_PAD
    out = pl.pallas_call(
        functools.partial(_diff_kernel, tq=tq, tk=tk, lambda_init=lambda_init),
        grid=(b, s // tq),
        in_specs=[blk, pl.BlockSpec((None, s, kw), lambda bi, i: (bi, 0, 0)),
                  pl.BlockSpec((B_HEADS * dvp, s), lambda bi, i: (0, bi)),
                  pl.BlockSpec((SUBLANES, LANES), lambda bi, i: (0, 0)),
                  pl.BlockSpec((B_V_DIM, 1), lambda bi, i: (0, 0))],
        out_specs=blk,
        out_shape=jax.ShapeDtypeStruct((b, s, B_WIDTH), BF16),
        scratch_shapes=[pltpu.VMEM((2 * B_HEADS * tq, 2 * LANES), BF16),
                        pltpu.VMEM((1, 2 * B_HEADS * tq), F32),
                        pltpu.VMEM((dvp, 2 * B_HEADS * tq), F32)],
        compiler_params=_cparams(2),
        name="diff",
    )(qb.reshape(b, s, B_WIDTH), kb.reshape(b, s, kw), vbt, lam_rows, subln.reshape(B_V_DIM, 1))
    return out.reshape(b * s, B_WIDTH)


def _mix_out_kernel(h_ref, a_ref, b_ref, wa_ref, wb_ref, g_ref, beta_ref, o_ref):
    y = (jnp.dot(a_ref[...], wa_ref[...], preferred_element_type=F32)
         + jnp.dot(b_ref[...], wb_ref[...], preferred_element_type=F32))
    o_ref[...] = _layer_norm(ALPHA * h_ref[...] + y, g_ref[...], beta_ref[...])


def _mix_out_call(h, out_a, out_b, w_o, g, beta, tm):
    n, d = h.shape
    w = w_o.astype(BF16)
    row = lambda wd: pl.BlockSpec((tm, wd), lambda i: (i, 0))
    const = lambda shp: pl.BlockSpec(shp, lambda i: (0, 0))
    return pl.pallas_call(
        _mix_out_kernel,
        grid=(n // tm,),
        in_specs=[row(d), row(A_WIDTH), row(B_WIDTH), const((A_WIDTH, d)), const((B_WIDTH, d)),
                  const((1, d)), const((1, d))],
        out_specs=row(d),
        out_shape=jax.ShapeDtypeStruct((n, d), F32),
        compiler_params=_cparams(1),
        name="mix_out",
    )(h, out_a, out_b, w[:A_WIDTH], w[A_WIDTH:], g.reshape(1, d), beta.reshape(1, d))


def _mem_kv_kernel(m_ref, w_ref, k_ref, v_ref):
    x = m_ref[...].astype(BF16)
    d = k_ref.shape[-1]
    k_ref[...] = jnp.dot(x, w_ref[:, :d], preferred_element_type=F32).astype(BF16)
    v_ref[...] = jnp.dot(x, w_ref[:, d:], preferred_element_type=F32).astype(BF16)


def _mem_kv_call(mem2d, wkv, tm):
    n, d = mem2d.shape
    return pl.pallas_call(
        _mem_kv_kernel,
        grid=(n // tm,),
        in_specs=[pl.BlockSpec((tm, d), lambda i: (i, 0)),
                  pl.BlockSpec((d, 2 * d), lambda i: (0, 0))],
        out_specs=[pl.BlockSpec((tm, d), lambda i: (i, 0))] * 2,
        out_shape=[jax.ShapeDtypeStruct((n, d), BF16)] * 2,
        compiler_params=_cparams(1),
        name="mem_kv",
    )(mem2d, wkv.astype(BF16))


def _mem_attn_kernel(h_ref, wq_ref, k_ref, v_ref, wo_ref, g_ref, beta_ref, o_ref):
    h = h_ref[...]
    q = jnp.dot(h.astype(BF16), wq_ref[...], preferred_element_type=F32)
    q = (q * (MEM_HEAD_DIM ** -0.5)).astype(BF16)
    outs = []
    for hd in range(MEM_HEADS):
        sl = slice(hd * MEM_HEAD_DIM, (hd + 1) * MEM_HEAD_DIM)
        sc = lax.dot_general(q[:, sl], k_ref[:, sl], NT_DIMS, preferred_element_type=F32)
        p = jnp.exp(sc - jnp.max(sc, axis=1, keepdims=True))
        den = jnp.sum(p, axis=1, keepdims=True)
        o = jnp.dot(p.astype(BF16), v_ref[:, sl], preferred_element_type=F32) / den
        outs.append(o.astype(BF16))
    y = jnp.dot(jnp.concatenate(outs, axis=1), wo_ref[...], preferred_element_type=F32)
    o_ref[...] = _layer_norm(ALPHA * h + y, g_ref[...], beta_ref[...])


def _mem_attn_call(h, wq, k, v, wo, g, beta, b, s, tm):
    n, d = h.shape
    m = k.shape[0] // b
    const = lambda shp: pl.BlockSpec(shp, lambda bi, i: (0,) * len(shp))
    row = pl.BlockSpec((None, tm, d), lambda bi, i: (bi, i, 0))
    kv = pl.BlockSpec((None, m, d), lambda bi, i: (bi, 0, 0))
    out = pl.pallas_call(
        _mem_attn_kernel,
        grid=(b, s // tm),
        in_specs=[row, const((d, d)), kv, kv, const((d, d)), const((1, d)), const((1, d))],
        out_specs=row,
        out_shape=jax.ShapeDtypeStruct((b, s, d), F32),
        compiler_params=_cparams(2),
        name="mem_attn",
    )(h.reshape(b, s, d), wq.astype(BF16), k.reshape(b, m, d), v.reshape(b, m, d),
      wo.astype(BF16), g.reshape(1, d), beta.reshape(1, d))
    return out.reshape(n, d)


def _router_kernel(h_ref, rw_ref, bias_ref, tri_ref, e_ref, g_ref, rank_ref, cnt_ref, run_ref):
    tm = h_ref.shape[0]

    @pl.when(pl.program_id(0) == 0)
    def _():
        run_ref[...] = jnp.zeros(run_ref.shape, F32)

    logits = jnp.dot(h_ref[...].astype(BF16), rw_ref[...], preferred_element_type=F32)
    scores = 1.0 / (1.0 + jnp.exp(-logits))
    lane = lax.broadcasted_iota(I32, (tm, LANES), 1)
    lanef = lane.astype(F32)
    ninf = jnp.float32(-jnp.inf)
    cur = jnp.where(lane < N_EXPERTS, scores + bias_ref[...], ninf)
    top_e = jnp.zeros((tm, LANES), F32)
    top_s = jnp.zeros((tm, LANES), F32)
    picked = jnp.zeros((tm, LANES), F32)
    firsts = []
    for k in range(TOP_K):
        mx = jnp.max(cur, axis=1, keepdims=True)
        first = jnp.min(jnp.where(cur == mx, lanef, float(LANES)), axis=1, keepdims=True)
        hit = lanef == first
        s_k = jnp.sum(jnp.where(hit, scores, 0.0), axis=1, keepdims=True)
        top_e = jnp.where(lane == k, first, top_e)
        top_s = jnp.where(lane == k, s_k, top_s)
        picked = jnp.where(hit, 1.0, picked)
        cur = jnp.where(hit, ninf, cur)
        firsts.append(first)
    e_ref[...] = top_e.astype(I32)
    g_ref[...] = top_s / jnp.sum(top_s, axis=1, keepdims=True) * ROUTED_SCALE

    slot = jnp.dot(tri_ref[...], picked.astype(BF16), preferred_element_type=F32) + run_ref[...]
    rank = jnp.zeros((tm, LANES), F32)
    for k in range(TOP_K):
        r_k = jnp.sum(jnp.where(lanef == firsts[k], slot, 0.0), axis=1, keepdims=True)
        rank = jnp.where(lane == k, r_k, rank)
    rank_ref[...] = rank.astype(I32)
    run = run_ref[...] + jnp.sum(picked, axis=0, keepdims=True)
    run_ref[...] = run
    cnt_ref[...] = run.astype(I32)


def _router_call(h, router_w, router_bias, tm):
    n, d = h.shape
    rw = jnp.pad(router_w, ((0, 0), (0, LANES - N_EXPERTS))).astype(BF16)
    bias = jnp.pad(router_bias, (0, LANES - N_EXPERTS)).reshape(1, LANES)
    tri = jnp.tri(tm, k=-1, dtype=BF16)
    row = pl.BlockSpec((tm, LANES), lambda i: (i, 0))
    const = lambda shp: pl.BlockSpec(shp, lambda i: (0, 0))
    return pl.pallas_call(
        _router_kernel,
        grid=(n // tm,),
        in_specs=[pl.BlockSpec((tm, d), lambda i: (i, 0)), const((d, LANES)), const((1, LANES)),
                  const((tm, tm))],
        out_specs=[row, row, row, const((1, LANES))],
        out_shape=[jax.ShapeDtypeStruct((n, LANES), I32),
                   jax.ShapeDtypeStruct((n, LANES), F32),
                   jax.ShapeDtypeStruct((n, LANES), I32),
                   jax.ShapeDtypeStruct((1, LANES), I32)],
        scratch_shapes=[pltpu.VMEM((1, LANES), F32)],
        compiler_params=_cparams(1),
        name="router",
    )(h, rw, bias, tri)


def _dispatch_kernel(dest_ref, x_ref, xs_ref, sem, *, td):
    def row_copy(g, rr, k):
        r0 = pl.multiple_of(g * SUBLANES, SUBLANES)
        return pltpu.make_async_copy(
            x_ref.at[pl.ds(r0 + rr, 1), :],
            xs_ref.at[pl.ds(dest_ref[g * (SUBLANES * TOP_K) + rr * TOP_K + k], 1), :], sem)

    def issue(g, carry):
        for rr in range(SUBLANES):
            for k in range(TOP_K):
                row_copy(g, rr, k).start()
        return carry

    lax.fori_loop(0, td // SUBLANES, issue, 0)

    def drain(g, carry):
        for rr in range(SUBLANES):
            for k in range(TOP_K):
                row_copy(g, rr, k).wait()
        return carry

    lax.fori_loop(0, td // SUBLANES, drain, 0)


def _dispatch_call(h, dest, td):
    n, d = h.shape
    m = n * TOP_K
    return pl.pallas_call(
        functools.partial(_dispatch_kernel, td=td),
        grid=(n // td,),
        in_specs=[pl.BlockSpec((td * TOP_K,), lambda i: (i,), memory_space=pltpu.SMEM),
                  pl.BlockSpec((td, d), lambda i: (i, 0))],
        out_specs=pl.BlockSpec(memory_space=pl.ANY),
        out_shape=jax.ShapeDtypeStruct((m, d), F32),
        scratch_shapes=[pltpu.SemaphoreType.DMA(())],
        compiler_params=_cparams(1),
        name="dispatch",
    )(dest, h)


def _moe_mm_kernel(ptile_ref, pexp_ref, plo_ref, phi_ref, xs_ref, wg_ref, wu_ref, wd_ref, y_ref,
                   *, tmm):
    p = pl.program_id(0)
    tile = ptile_ref[p]
    lo = plo_ref[p]
    hi = phi_ref[p]
    first = jnp.logical_or(p == 0, ptile_ref[jnp.maximum(p - 1, 0)] != tile)

    def expert_out():
        x = xs_ref[...].astype(BF16)
        g = jnp.dot(x, wg_ref[...], preferred_element_type=F32)
        u = jnp.dot(x, wu_ref[...], preferred_element_type=F32)
        hid = (_silu(g) * u).astype(BF16)
        return jnp.dot(hid, wd_ref[...], preferred_element_type=F32)

    whole = jnp.logical_and(lo <= tile * tmm, hi >= (tile + 1) * tmm)

    @pl.when(whole)
    def _():
        y_ref[...] = expert_out()

    @pl.when(jnp.logical_and(first, jnp.logical_not(whole)))
    def _():
        y_ref[...] = jnp.zeros(y_ref.shape, F32)

    @pl.when(jnp.logical_and(lo < hi, jnp.logical_not(whole)))
    def _():
        y = expert_out()
        rowg = tile * tmm + lax.broadcasted_iota(I32, (tmm, 1), 0)
        y_ref[...] += jnp.where(jnp.logical_and(rowg >= lo, rowg < hi), y, 0.0)


def _moe_mm_call(xs, pairs, wg, wu, wd, tmm):
    m = xs.shape[0]
    n_pairs = pairs[0].shape[0]
    rows = pl.BlockSpec((tmm, D_MODEL), lambda p, pt, pe, plo, phi: (pt[p], 0))
    wspec = lambda shp: pl.BlockSpec((None,) + shp, lambda p, pt, pe, plo, phi: (pe[p], 0, 0))
    return pl.pallas_call(
        functools.partial(_moe_mm_kernel, tmm=tmm),
        grid_spec=pltpu.PrefetchScalarGridSpec(
            num_scalar_prefetch=4,
            grid=(n_pairs,),
            in_specs=[rows, wspec((D_MODEL, EXPERT_DIM)), wspec((D_MODEL, EXPERT_DIM)),
                      wspec((EXPERT_DIM, D_MODEL))],
            out_specs=rows),
        out_shape=jax.ShapeDtypeStruct((m, D_MODEL), F32),
        compiler_params=_cparams(1),
        name="moe_mm",
    )(*pairs, xs, wg, wu, wd)


def _group_pairs(counts, m, tmm):
    n_tiles = m // tmm
    n_pairs = n_tiles + N_EXPERTS
    gend = jnp.cumsum(counts)
    gstart = gend - counts
    t0 = jnp.arange(n_tiles, dtype=I32) * tmm
    n_le = lambda edges, v: jnp.sum(edges[None, :] <= v[:, None], axis=1).astype(I32)
    e_first = n_le(gend, t0)
    e_last = n_le(gend, t0 + (tmm - 1))
    per_tile = e_last - e_first + 1
    pend = jnp.cumsum(per_tile)
    pstart = pend - per_tile
    p = jnp.arange(n_pairs, dtype=I32)
    valid = p < pend[-1]
    tile = jnp.minimum(n_le(pend, p), n_tiles - 1)
    e = jnp.clip(e_first[tile] + (p - pstart[tile]), 0, N_EXPERTS - 1).astype(I32)
    lo = jnp.where(valid, gstart[e], 0).astype(I32)
    hi = jnp.where(valid, gend[e], 0).astype(I32)
    return tile, e, lo, hi


def _combine_kernel(dest_ref, gate_ref, h_ref, y_ref, sg_ref, su_ref, sd_ref, g_ref, beta_ref, o_ref,
                    buf_ref, sem, *, tc):
    def row_copy(g, rr, k):
        r0 = pl.multiple_of(g * SUBLANES, SUBLANES)
        return pltpu.make_async_copy(
            y_ref.at[pl.ds(dest_ref[g * (SUBLANES * TOP_K) + rr * TOP_K + k], 1), :],
            buf_ref.at[k, pl.ds(r0 + rr, 1), :], sem)

    def issue(g, carry):
        for rr in range(SUBLANES):
            for k in range(TOP_K):
                row_copy(g, rr, k).start()
        return carry

    lax.fori_loop(0, tc // SUBLANES, issue, 0)

    h = h_ref[...]
    x = h.astype(BF16)
    hid = (_silu(jnp.dot(x, sg_ref[...], preferred_element_type=F32))
           * jnp.dot(x, su_ref[...], preferred_element_type=F32)).astype(BF16)
    shared = jnp.dot(hid, sd_ref[...], preferred_element_type=F32)

    def drain(g, carry):
        for rr in range(SUBLANES):
            for k in range(TOP_K):
                row_copy(g, rr, k).wait()
        return carry

    lax.fori_loop(0, tc // SUBLANES, drain, 0)

    gates = gate_ref[...]
    routed = gates[:, 0:1] * buf_ref[0]
    for k in range(1, TOP_K):
        routed = routed + gates[:, k:k + 1] * buf_ref[k]
    o_ref[...] = _layer_norm(ALPHA * h + (routed + shared), g_ref[...], beta_ref[...])


def _combine_call(h, y, dest, gates, sg, su, sd, g, beta, tc):
    n, d = h.shape
    smem = pl.BlockSpec((tc * TOP_K,), lambda i: (i,), memory_space=pltpu.SMEM)
    const = lambda shp: pl.BlockSpec(shp, lambda i: (0, 0))
    return pl.pallas_call(
        functools.partial(_combine_kernel, tc=tc),
        grid=(n // tc,),
        in_specs=[smem, pl.BlockSpec((tc, LANES), lambda i: (i, 0)),
                  pl.BlockSpec((tc, d), lambda i: (i, 0)),
                  pl.BlockSpec(memory_space=pl.ANY),
                  const((d, SHARED_DIM)), const((d, SHARED_DIM)), const((SHARED_DIM, d)),
                  const((1, d)), const((1, d))],
        out_specs=pl.BlockSpec((tc, d), lambda i: (i, 0)),
        out_shape=jax.ShapeDtypeStruct((n, d), F32),
        scratch_shapes=[pltpu.VMEM((TOP_K, tc, d), F32),
                        pltpu.SemaphoreType.DMA(())],
        compiler_params=_cparams(1),
        name="combine",
    )(dest, gates, h, y, sg.astype(BF16), su.astype(BF16), sd.astype(BF16),
      g.reshape(1, d), beta.reshape(1, d))


def _tiles(b, s):
    n = b * s
    pick = lambda pref, total: next(t for t in (pref, 512, 256, 128, 64, 32, 16, 8) if t <= pref and total % t == 0)
    return dict(
        rows=pick(512, n),
        attn_q=pick(256, s),
        attn_k=pick(256, s),
        mem_rows=pick(512, s),
        mem_kv=pick(512, b * 256),
        router=pick(1024, n),
        dispatch=pick(256, n),
        moe=pick(512, n * TOP_K),
        combine=pick(128, n),
    )


def kernel(x, mem, ln_in_g, ln_in_b, w_in, a_kv_norm, a_w_uk, a_w_uv, b_lq1, b_lk1, b_lq2, b_lk2,
           b_subln, w_o, ln1_g, ln1_b, m_wq, m_wkv, m_wo, ln2_g, ln2_b, router_w, router_bias,
           e_w_gate, e_w_up, e_w_down, s_w_gate, s_w_up, s_w_down, ln3_g, ln3_b):
    b, s, d = x.shape
    n = b * s
    m = n * TOP_K
    t = _tiles(b, s)
    mem2d = mem.reshape(-1, d)

    h = _ln_call(x.reshape(n, d), ln_in_g, ln_in_b, t["rows"])
    for l in range(DEPTH):
        lambda_init = 0.8 - 0.6 * math.exp(-0.3 * l)

        qa, ckv, ckvt, qidx, kk, widx, qb, kb, vbt = _proj_call(h, _pad_w_in(w_in[l]), a_kv_norm[l],
                                                                  t["rows"], s)
        out_a = _dsa_call(qa, qidx, widx, kk, ckv, ckvt, _block_diag_uk(a_w_uk[l]),
                          _block_diag_uv_t(a_w_uv[l]), b, s, t["attn_q"], t["attn_k"])
        lam_rows = jnp.pad(jnp.stack([b_lq1[l], b_lk1[l], b_lq2[l], b_lk2[l]]),
                           ((0, SUBLANES - 4), (0, LANES - B_QK_DIM)))
        out_b = _diff_call(qb, kb, vbt, lam_rows, b_subln[l], b, s, t["attn_q"], t["attn_k"], lambda_init)
        h = _mix_out_call(h, out_a, out_b, w_o[l], ln1_g[l], ln1_b[l], t["rows"])

        mk, mv = _mem_kv_call(mem2d, m_wkv[l], t["mem_kv"])
        h = _mem_attn_call(h, m_wq[l], mk, mv, m_wo[l], ln2_g[l], ln2_b[l], b, s, t["mem_rows"])

        top_e, gates, rank, counts = _router_call(h, router_w[l], router_bias[l], t["router"])
        counts = counts[0, :N_EXPERTS]
        gstart = jnp.cumsum(counts) - counts
        top_e = top_e[:, :TOP_K]
        dest = (rank[:, :TOP_K] + gstart[top_e]).reshape(m)
        xs = _dispatch_call(h, dest, t["dispatch"])
        pairs = _group_pairs(counts, m, t["moe"])
        y = _moe_mm_call(xs, pairs, e_w_gate[l].astype(BF16), e_w_up[l].astype(BF16),
                         e_w_down[l].astype(BF16), t["moe"])
        h = _combine_call(h, y, dest, gates, s_w_gate[l], s_w_up[l], s_w_down[l],
                          ln3_g[l], ln3_b[l], t["combine"])
    return h.reshape(b, s, d)
```

```python
import functools
import math

import jax
import jax.numpy as jnp
import numpy as np
from jax import lax
from jax.experimental import pallas as pl
from jax.experimental.pallas import tpu as pltpu

F32 = jnp.float32
BF16 = jnp.bfloat16
I32 = jnp.int32
I16 = jnp.int16

D_MODEL = 1024
DEPTH = 2
CHUNK = 64
CHUNK_SHIFT = 6
A_HEADS = 8
A_HEAD_DIM = 64
A_WIDTH = A_HEADS * A_HEAD_DIM
KV_LORA = 128
IDX_HEADS = 8
IDX_DIM = 64
INDEX_TOPK_MAX = 256
B_HEADS = 4
B_QK_DIM = 64
B_V_DIM = 128
B_WIDTH = B_HEADS * B_V_DIM
MEM_HEADS = 4
MEM_HEAD_DIM = D_MODEL // MEM_HEADS
N_EXPERTS = 64
TOP_K = 8
EXPERT_DIM = 256
SHARED_DIM = 256
ROUTED_SCALE = 2.5
ALPHA = (2 * DEPTH) ** 0.25
LN_EPS = 1e-5
RMS_EPS = 1e-6

LANES = 128
SUBLANES = 8
VMEM_LIMIT = 56 * 1024 * 1024
INT_MIN = -2 ** 31
I16_MIN = -2 ** 15
V_PAD = 16
NEG_BIG = -3.0e38
M_INIT = -1.0e30

NT_DIMS = (((1,), (1,)), ((), ()))


def _cparams(n_axes):
    return pltpu.CompilerParams(dimension_semantics=("arbitrary",) * n_axes,
                                vmem_limit_bytes=VMEM_LIMIT)


def _layer_norm(x, g, b):
    mu = jnp.mean(x, axis=-1, keepdims=True)
    xc = x - mu
    var = jnp.mean(xc * xc, axis=-1, keepdims=True)
    return xc * lax.rsqrt(var + LN_EPS) * g + b


def _silu(x):
    return x * (1.0 / (1.0 + jnp.exp(-x)))


def _ln_kernel(x_ref, g_ref, b_ref, o_ref):
    o_ref[...] = _layer_norm(x_ref[...], g_ref[...], b_ref[...])


def _ln_call(x, g, b, tm):
    n, d = x.shape
    return pl.pallas_call(
        _ln_kernel,
        grid=(n // tm,),
        in_specs=[pl.BlockSpec((tm, d), lambda i: (i, 0)),
                  pl.BlockSpec((1, d), lambda i: (0, 0)),
                  pl.BlockSpec((1, d), lambda i: (0, 0))],
        out_specs=pl.BlockSpec((tm, d), lambda i: (i, 0)),
        out_shape=jax.ShapeDtypeStruct((n, d), F32),
        compiler_params=_cparams(1),
        name="ln_in",
    )(x, g.reshape(1, d), b.reshape(1, d))


_SEG_QA = (0, 512)
_SEG_CKV = (512, 640)
_SEG_QIDX = (640, 1152)
_SEG_KK = (1152, 1280)
_SEG_WIDX = (1280, 1408)
_SEG_QB = (1408, 1920)
_SEG_KB = (1920, 2432)
_SEG_VB = (2432, 2944)
_PROJ_COLS = 2944


def _pos_features(pos, shape):
    lane = lax.broadcasted_iota(I32, shape, 1)
    hi = lax.shift_right_logical(pos, CHUNK_SHIFT).astype(F32)
    lo = (pos & (CHUNK - 1)).astype(F32)
    return jnp.where(lane < 2, 1.0, jnp.where(lane == 2, hi, jnp.where(lane == 3, lo, 0.0)))


def _slope_features(tqi, slope, shape):
    lane = lax.broadcasted_iota(I32, shape, 1)
    hi = lax.shift_right_logical(tqi, CHUNK_SHIFT).astype(F32)
    lo = (tqi & (CHUNK - 1)).astype(F32)
    return jnp.where(lane == 0, -slope * CHUNK * hi,
                     jnp.where(lane == 1, -slope * lo,
                               jnp.where(lane == 2, slope * CHUNK,
                                         jnp.where(lane == 3, slope, 0.0))))


def _with_ones_rows(vt):
    t = vt.shape[1]
    sub = lax.broadcasted_iota(I32, (V_PAD, t), 0)
    return jnp.concatenate([vt, jnp.where(sub == 0, 1.0, 0.0)], axis=0)


def _proj_kernel(h_ref, w_ref, kvn_ref, qa_ref, ckv_ref, ckvt_ref, qidx_ref, kk_ref, widx_ref,
                 qb_ref, kb_ref, vbt_ref, *, seq):
    tm = h_ref.shape[0]
    x = h_ref[...].astype(BF16)

    def seg(s):
        return jnp.dot(x, w_ref[:, s[0]:s[1]], preferred_element_type=F32)

    row = pl.program_id(0) * tm + lax.broadcasted_iota(I32, (tm, LANES), 0)
    feat = _pos_features(row & (seq - 1), (tm, LANES)).astype(BF16)

    qa_ref[...] = (seg(_SEG_QA) * (A_HEAD_DIM ** -0.5)).astype(BF16)
    c = seg(_SEG_CKV)
    c = c * lax.rsqrt(jnp.mean(c * c, axis=-1, keepdims=True) + RMS_EPS) * kvn_ref[...]
    ckv_ref[:, :KV_LORA] = c.astype(BF16)
    ckv_ref[:, KV_LORA:] = feat
    ckvt_ref[...] = _with_ones_rows(c.T).astype(BF16)
    qidx_ref[...] = (seg(_SEG_QIDX) * (IDX_DIM ** -0.5)).astype(BF16)
    kk_ref[...] = seg(_SEG_KK).astype(BF16)
    widx_ref[...] = seg(_SEG_WIDX) * (IDX_HEADS ** -0.5)
    qb_ref[...] = (seg(_SEG_QB) * (B_QK_DIM ** -0.5)).astype(BF16)
    kb = seg(_SEG_KB).astype(BF16)
    vt = seg(_SEG_VB).T
    for h in range(B_HEADS):
        kb_ref[:, 2 * h * LANES:(2 * h + 1) * LANES] = kb[:, h * LANES:(h + 1) * LANES]
        kb_ref[:, (2 * h + 1) * LANES:(2 * h + 2) * LANES] = feat
        vbt_ref[h * (B_V_DIM + V_PAD):(h + 1) * (B_V_DIM + V_PAD), :] = _with_ones_rows(
            vt[h * B_V_DIM:(h + 1) * B_V_DIM]).astype(BF16)


def _proj_call(h, w_pad, kvn, tm, seq):
    n, d = h.shape
    assert seq & (seq - 1) == 0 and seq <= CHUNK * 256 and seq % tm == 0
    outs = [(512, BF16, False), (2 * KV_LORA, BF16, False), (KV_LORA + V_PAD, BF16, True),
            (512, BF16, False), (128, BF16, False), (128, F32, False), (512, BF16, False),
            (2 * B_HEADS * LANES, BF16, False), (B_HEADS * (B_V_DIM + V_PAD), BF16, True)]
    spec = lambda w, t: (pl.BlockSpec((w, tm), lambda i: (0, i)) if t
                         else pl.BlockSpec((tm, w), lambda i: (i, 0)))
    shape = lambda w, dt, t: jax.ShapeDtypeStruct((w, n) if t else (n, w), dt)
    return pl.pallas_call(
        functools.partial(_proj_kernel, seq=seq),
        grid=(n // tm,),
        in_specs=[pl.BlockSpec((tm, d), lambda i: (i, 0)),
                  pl.BlockSpec((d, _PROJ_COLS), lambda i: (0, 0)),
                  pl.BlockSpec((1, KV_LORA), lambda i: (0, 0))],
        out_specs=[spec(w, t) for w, _, t in outs],
        out_shape=[shape(w, dt, t) for w, dt, t in outs],
        compiler_params=_cparams(1),
        name="proj_in",
    )(h, w_pad, kvn.reshape(1, KV_LORA))


def _pad_w_in(w_in):
    sizes = (A_WIDTH, KV_LORA, IDX_HEADS * IDX_DIM, IDX_DIM, IDX_HEADS,
             2 * B_HEADS * B_QK_DIM, 2 * B_HEADS * B_QK_DIM, B_WIDTH)
    splits = np.cumsum(sizes)[:-1].tolist()
    q_a, c_kv, q_idx, k_idx, w_idx, q_b, k_b, v_b = jnp.split(w_in, splits, axis=-1)
    w_idx = jnp.pad(w_idx, ((0, 0), (0, LANES - IDX_HEADS)))
    return jnp.concatenate([q_a, c_kv, q_idx, k_idx, k_idx, w_idx, q_b, k_b, v_b],
                           axis=-1).astype(BF16)


def _flash_probs(sl, sc, m_ref):
    m_old = m_ref[:, sl]
    m_new = jnp.maximum(m_old, jnp.max(sc, axis=0, keepdims=True))
    m_ref[:, sl] = m_new
    return jnp.exp(m_old - m_new), jnp.exp(sc - m_new).astype(BF16)


def _flash_init(m_ref, acc_ref):
    m_ref[...] = jnp.full(m_ref.shape, M_INIT, F32)
    acc_ref[...] = jnp.zeros(acc_ref.shape, F32)


def _split_halves(q_ref, n_pairs, out_ref, rows):
    lane = lax.broadcasted_iota(I32, (rows, LANES), 1)
    for j in range(n_pairs):
        qp = q_ref[:, j * LANES:(j + 1) * LANES].astype(F32)
        out_ref[2 * j * rows:(2 * j + 1) * rows, :LANES] = jnp.where(lane < 64, qp, 0.0).astype(BF16)
        out_ref[(2 * j + 1) * rows:(2 * j + 2) * rows, :LANES] = jnp.where(lane >= 64, qp, 0.0).astype(BF16)


def _dsa_kernel(qa_ref, qidx_ref, widx_ref, kk_ref, ckv_ref, ckvt_ref, wk_ref, wvt_ref, out_ref,
                keys_ref, khi_ref, klo_ref, qim_ref, qlat_ref, tj_ref, m_ref, acc_ref,
                *, tq, tk, seq, topk):
    i = pl.program_id(1)
    n_off = i * (tq // tk)
    n_kt = n_off + tq // tk
    idx_bits = int(math.log2(seq))

    _split_halves(qidx_ref, IDX_HEADS // 2, qim_ref, tq)
    qlat = jnp.dot(qa_ref[...], wk_ref[...], preferred_element_type=F32)
    t_col = i * tq + lax.broadcasted_iota(I32, (tq, LANES), 0)
    for h in range(A_HEADS):
        qlat_ref[h * tq:(h + 1) * tq, :KV_LORA] = qlat[:, h * KV_LORA:(h + 1) * KV_LORA].astype(BF16)
        qlat_ref[h * tq:(h + 1) * tq, KV_LORA:] = _slope_features(
            t_col, 2.0 ** -(h + 1), (tq, LANES)).astype(BF16)
    w_t = widx_ref[...].T

    row = lax.broadcasted_iota(I32, (tk, tq), 0)
    tqi = i * tq + lax.broadcasted_iota(I32, (tk, tq), 1)
    qchunk = lax.shift_right_logical(tqi, CHUNK_SHIFT)

    def idx_body(kt, carry):
        off = pl.multiple_of(kt * tk, tk)
        kkt = kk_ref[pl.ds(off, tk), :]
        s_all = lax.dot_general(kkt, qim_ref[...], NT_DIMS, preferred_element_type=F32)
        acc = jnp.zeros((tk, tq), F32)
        for h in range(IDX_HEADS):
            acc = acc + w_t[h:h + 1, :] * jnp.maximum(s_all[:, h * tq:(h + 1) * tq], 0.0)
        bits = lax.bitcast_convert_type(acc + 0.0, I32)
        key = jnp.where(bits < 0, bits ^ jnp.int32(0x7FFFFFFF), bits)
        adm = lax.shift_right_logical(off + row, CHUNK_SHIFT) <= qchunk
        key = jnp.where(adm, key, jnp.int32(INT_MIN))
        keys_ref[pl.ds(off, tk), :] = key
        khi_ref[pl.ds(off, tk), :] = lax.shift_right_arithmetic(key, 16).astype(I16)
        klo_ref[pl.ds(off, tk), :] = ((key & 0xFFFF) + I16_MIN).astype(I16)
        return carry

    lax.fori_loop(0, n_kt, idx_body, 0)

    def count(pred):
        def body(kt, c):
            off = pl.multiple_of(kt * tk, tk)
            ind = jnp.where(pred(keys_ref[pl.ds(off, tk), :], off + row), 1.0, 0.0)
            return c + jnp.sum(ind, axis=0, keepdims=True)
        return lax.fori_loop(0, n_kt, body, jnp.zeros((1, tq), F32))

    def count16(ref, pred):
        pack = 2 * SUBLANES
        def body(kt, c):
            off = pl.multiple_of(kt * tk, tk)
            ind = jnp.where(pred(ref[pl.ds(off, tk), :]), jnp.int16(1), jnp.int16(0))
            parts = [ind[j * pack:(j + 1) * pack] for j in range(tk // pack)]
            while len(parts) > 1:
                parts = [parts[j] + parts[j + 1] for j in range(0, len(parts), 2)]
            return c + parts[0]
        c = lax.fori_loop(0, n_kt, body, jnp.zeros((pack, tq), I16))
        return jnp.sum(c.astype(F32), axis=0, keepdims=True)

    def search16(ref, need):
        def body(it, t):
            cand = t + lax.shift_left(jnp.int32(1), 15 - it)
            cand16 = cand.astype(I16)
            cnt = count16(ref, lambda tile: tile >= cand16)
            return jnp.where(cnt >= need, cand, t)
        return lax.fori_loop(0, 16, body, jnp.full((1, tq), I16_MIN, I32))

    t_hi = search16(khi_ref, topk)
    t_hi16 = t_hi.astype(I16)
    n_above = count16(khi_ref, lambda tile: tile > t_hi16)

    def park_body(kt, carry):
        off = pl.multiple_of(kt * tk, tk)
        klo_ref[pl.ds(off, tk), :] = jnp.where(khi_ref[pl.ds(off, tk), :] == t_hi16,
                                               klo_ref[pl.ds(off, tk), :], jnp.int16(I16_MIN))
        return carry

    lax.fori_loop(0, n_kt, park_body, 0)
    t_lo = search16(klo_ref, topk - n_above)
    thr = t_hi * 65536 + (t_lo - I16_MIN)

    live = thr > INT_MIN
    n_ge = count(lambda tile, _: tile >= thr)
    excess = jnp.logical_and(n_ge > topk, live)
    all_ties = jnp.where(live, seq, -1).astype(I32)
    tj_ref[0] = jnp.broadcast_to(thr, (SUBLANES, tq))
    tj_ref[1] = jnp.broadcast_to(all_ties, (SUBLANES, tq))

    @pl.when(jnp.max(jnp.where(excess, 1.0, 0.0)) > 0.0)
    def _():
        need = topk - count(lambda tile, _: tile > thr)

        def tie_body(it, hi):
            cand = hi + lax.shift_left(jnp.int32(1), idx_bits - 1 - it)
            cnt = count(lambda tile, pos: jnp.logical_and(tile == thr, pos < cand))
            return jnp.where(cnt < need, cand, hi)

        hi = lax.fori_loop(0, idx_bits, tie_body, jnp.zeros((1, tq), I32))
        tj_ref[1] = jnp.broadcast_to(jnp.where(excess, hi, all_ties), (SUBLANES, tq))

    thr_b = tj_ref[0][:1]
    tie_hi = tj_ref[1][:1]

    _flash_init(m_ref, acc_ref)

    def att_tile(kt, late_fix):
        off = pl.multiple_of(kt * tk, tk)
        tile = keys_ref[pl.ds(off, tk), :]
        pos = off + row
        sel = jnp.logical_or(tile > thr_b,
                             jnp.logical_and(tile == thr_b, pos <= tie_hi))
        s_all = lax.dot_general(ckv_ref[pl.ds(off, tk), :], qlat_ref[...], NT_DIMS,
                                preferred_element_type=F32)
        if late_fix:
            late = jnp.maximum(pos - tqi, 0).astype(F32)
        alphas, probs = [], []
        for h in range(A_HEADS):
            sl = slice(h * tq, (h + 1) * tq)
            sc = s_all[:, sl]
            if late_fix:
                sc = sc - (2.0 * 2.0 ** -(h + 1)) * late
            alpha, p = _flash_probs(sl, jnp.where(sel, sc, NEG_BIG), m_ref)
            alphas.append(alpha)
            probs.append(p)
        pv = jnp.dot(ckvt_ref[:, pl.ds(off, tk)], jnp.concatenate(probs, axis=1),
                     preferred_element_type=F32)
        acc_ref[...] = jnp.concatenate(alphas, axis=1) * acc_ref[...] + pv

    def off_body(kt, carry):
        att_tile(kt, False)
        return carry

    def diag_body(kt, carry):
        att_tile(kt, True)
        return carry

    lax.fori_loop(0, n_off, off_body, 0)
    lax.fori_loop(n_off, n_kt, diag_body, 0)

    inv_l = 1.0 / acc_ref[KV_LORA:KV_LORA + 1, :]
    olat = jnp.concatenate([(acc_ref[:KV_LORA, h * tq:(h + 1) * tq]
                             * inv_l[:, h * tq:(h + 1) * tq]).astype(BF16)
                            for h in range(A_HEADS)], axis=0)
    out_t = jnp.dot(wvt_ref[...], olat, preferred_element_type=F32)
    out_ref[...] = out_t.T.astype(BF16)


def _dsa_call(qa, qidx, widx, kk, ckv, ckvt, wk_bd, wvt_bd, b, s, tq, tk):
    topk = min(INDEX_TOPK_MAX, s // 4)
    assert s % tq == 0 and tq % tk == 0 and tk % LANES == 0 and s & (s - 1) == 0
    blk = lambda w: pl.BlockSpec((None, tq, w), lambda bi, i: (bi, i, 0))
    full = lambda w: pl.BlockSpec((None, s, w), lambda bi, i: (bi, 0, 0))
    const = lambda a: pl.BlockSpec(a.shape, lambda bi, i: (0, 0))
    r3 = lambda a: a.reshape(b, s, a.shape[-1])
    out = pl.pallas_call(
        functools.partial(_dsa_kernel, tq=tq, tk=tk, seq=s, topk=topk),
        grid=(b, s // tq),
        in_specs=[blk(A_WIDTH), blk(IDX_HEADS * IDX_DIM), blk(LANES), full(LANES),
                  full(2 * KV_LORA),
                  pl.BlockSpec((KV_LORA + V_PAD, s), lambda bi, i: (0, bi)),
                  const(wk_bd), const(wvt_bd)],
        out_specs=blk(A_WIDTH),
        out_shape=jax.ShapeDtypeStruct((b, s, A_WIDTH), BF16),
        scratch_shapes=[pltpu.VMEM((s, tq), I32),
                        pltpu.VMEM((s, tq), I16),
                        pltpu.VMEM((s, tq), I16),
                        pltpu.VMEM((IDX_HEADS * tq, LANES), BF16),
                        pltpu.VMEM((A_HEADS * tq, 2 * KV_LORA), BF16),
                        pltpu.VMEM((2, SUBLANES, tq), I32),
                        pltpu.VMEM((1, A_HEADS * tq), F32),
                        pltpu.VMEM((KV_LORA + V_PAD, A_HEADS * tq), F32)],
        compiler_params=_cparams(2),
        name="dsa",
    )(r3(qa), r3(qidx), r3(widx), r3(kk), r3(ckv), ckvt, wk_bd, wvt_bd)
    return out.reshape(b * s, A_WIDTH)


def _block_diag_uk(w_uk):
    h, c, d = w_uk.shape
    eye = jnp.eye(h, dtype=w_uk.dtype)
    return jnp.einsum('hcd,hg->hdgc', w_uk, eye).reshape(h * d, h * c).astype(BF16)


def _block_diag_uv_t(w_uv):
    h, c, d = w_uv.shape
    eye = jnp.eye(h, dtype=w_uv.dtype)
    return jnp.einsum('hcd,hg->hdgc', w_uv, eye).reshape(h * d, h * c).astype(BF16)


def _diff_kernel(qb_ref, kb_ref, vbt_ref, lam_ref, subln_ref, out_ref,
                 qm_ref, m_ref, acc_ref, *, tq, tk, lambda_init):
    i = pl.program_id(1)
    n_off = i * (tq // tk)
    dvp = B_V_DIM + V_PAD
    _split_halves(qb_ref, B_HEADS, qm_ref, tq)
    t_col = i * tq + lax.broadcasted_iota(I32, (tq, LANES), 0)
    for h in range(B_HEADS):
        feat = _slope_features(t_col, 2.0 ** (-2 * (h + 1)), (tq, LANES)).astype(BF16)
        qm_ref[2 * h * tq:(2 * h + 1) * tq, LANES:] = feat
        qm_ref[(2 * h + 1) * tq:(2 * h + 2) * tq, LANES:] = feat

    row = lax.broadcasted_iota(I32, (tk, tq), 0)
    tqi = i * tq + lax.broadcasted_iota(I32, (tk, tq), 1)
    qchunk = lax.shift_right_logical(tqi, CHUNK_SHIFT)
    _flash_init(m_ref, acc_ref)

    def tile_step(kt, diag):
        off = pl.multiple_of(kt * tk, tk)
        s_pairs = [lax.dot_general(kb_ref[pl.ds(off, tk), 2 * h * LANES:(2 * h + 2) * LANES],
                                   qm_ref[2 * h * tq:(2 * h + 2) * tq, :], NT_DIMS,
                                   preferred_element_type=F32) for h in range(B_HEADS)]
        if diag:
            pos = off + row
            adm = lax.shift_right_logical(pos, CHUNK_SHIFT) <= qchunk
            late = jnp.maximum(pos - tqi, 0).astype(F32)
        alphas, probs = [], []
        for h in range(B_HEADS):
            for j in range(2):
                sl = slice((2 * h + j) * tq, (2 * h + j + 1) * tq)
                sc = s_pairs[h][:, j * tq:(j + 1) * tq]
                if diag:
                    sc = jnp.where(adm, sc - (2.0 * 2.0 ** (-2 * (h + 1))) * late, NEG_BIG)
                alpha, p = _flash_probs(sl, sc, m_ref)
                alphas.append(alpha)
                probs.append(p)
        pv = jnp.concatenate(
            [jnp.dot(vbt_ref[h * dvp:(h + 1) * dvp, pl.ds(off, tk)],
                     jnp.concatenate(probs[2 * h:2 * h + 2], axis=1), preferred_element_type=F32)
             for h in range(B_HEADS)], axis=1)
        acc_ref[...] = jnp.concatenate(alphas, axis=1) * acc_ref[...] + pv

    def off_body(kt, carry):
        tile_step(kt, False)
        return carry

    def diag_body(kt, carry):
        tile_step(kt, True)
        return carry

    lax.fori_loop(0, n_off, off_body, 0)
    lax.fori_loop(n_off, n_off + tq // tk, diag_body, 0)

    lp = lam_ref[...]
    lam = (jnp.exp(jnp.sum(lp[0:1] * lp[1:2], axis=1, keepdims=True))
           - jnp.exp(jnp.sum(lp[2:3] * lp[3:4], axis=1, keepdims=True)) + lambda_init)
    o_all = acc_ref[:B_V_DIM, :] * (1.0 / acc_ref[B_V_DIM:B_V_DIM + 1, :])
    outs = []
    for h in range(B_HEADS):
        o = (o_all[:, 2 * h * tq:(2 * h + 1) * tq]
             - lam * o_all[:, (2 * h + 1) * tq:(2 * h + 2) * tq])
        o = o * lax.rsqrt(jnp.mean(o * o, axis=0, keepdims=True) + RMS_EPS) * subln_ref[...]
        outs.append(o * (1.0 - lambda_init))
    out_ref[...] = jnp.concatenate(outs, axis=0).T.astype(BF16)


def _diff_call(qb, kb, vbt, lam_rows, subln, b, s, tq, tk, lambda_init):
    blk = pl.BlockSpec((None, tq, B_WIDTH), lambda bi, i: (bi, i, 0))
    kw = 2 * B_HEADS * LANES
    dvp = B_V_DIM + V_PAD
    out = pl.pallas_call(
        functools.partial(_diff_kernel, tq=tq, tk=tk, lambda_init=lambda_init),
        grid=(b, s // tq),
        in_specs=[blk, pl.BlockSpec((None, s, kw), lambda bi, i: (bi, 0, 0)),
                  pl.BlockSpec((B_HEADS * dvp, s), lambda bi, i: (0, bi)),
                  pl.BlockSpec((SUBLANES, LANES), lambda bi, i: (0, 0)),
                  pl.BlockSpec((B_V_DIM, 1), lambda bi, i: (0, 0))],
        out_specs=blk,
        out_shape=jax.ShapeDtypeStruct((b, s, B_WIDTH), BF16),
        scratch_shapes=[pltpu.VMEM((2 * B_HEADS * tq, 2 * LANES), BF16),
                        pltpu.VMEM((1, 2 * B_HEADS * tq), F32),
                        pltpu.VMEM((dvp, 2 * B_HEADS * tq), F32)],
        compiler_params=_cparams(2),
        name="diff",
    )(qb.reshape(b, s, B_WIDTH), kb.reshape(b, s, kw), vbt, lam_rows, subln.reshape(B_V_DIM, 1))
    return out.reshape(b * s, B_WIDTH)


def _mix_out_kernel(h_ref, a_ref, b_ref, wa_ref, wb_ref, g_ref, beta_ref, o_ref):
    y = (jnp.dot(a_ref[...], wa_ref[...], preferred_element_type=F32)
         + jnp.dot(b_ref[...], wb_ref[...], preferred_element_type=F32))
    o_ref[...] = _layer_norm(ALPHA * h_ref[...] + y, g_ref[...], beta_ref[...])


def _mix_out_call(h, out_a, out_b, w_o, g, beta, tm):
    n, d = h.shape
    w = w_o.astype(BF16)
    row = lambda wd: pl.BlockSpec((tm, wd), lambda i: (i, 0))
    const = lambda shp: pl.BlockSpec(shp, lambda i: (0, 0))
    return pl.pallas_call(
        _mix_out_kernel,
        grid=(n // tm,),
        in_specs=[row(d), row(A_WIDTH), row(B_WIDTH), const((A_WIDTH, d)), const((B_WIDTH, d)),
                  const((1, d)), const((1, d))],
        out_specs=row(d),
        out_shape=jax.ShapeDtypeStruct((n, d), F32),
        compiler_params=_cparams(1),
        name="mix_out",
    )(h, out_a, out_b, w[:A_WIDTH], w[A_WIDTH:], g.reshape(1, d), beta.reshape(1, d))


def _mem_kv_kernel(m_ref, w_ref, k_ref, v_ref):
    x = m_ref[...].astype(BF16)
    d = k_ref.shape[-1]
    k_ref[...] = jnp.dot(x, w_ref[:, :d], preferred_element_type=F32).astype(BF16)
    v_ref[...] = jnp.dot(x, w_ref[:, d:], preferred_element_type=F32).astype(BF16)


def _mem_kv_call(mem2d, wkv, tm):
    n, d = mem2d.shape
    return pl.pallas_call(
        _mem_kv_kernel,
        grid=(n // tm,),
        in_specs=[pl.BlockSpec((tm, d), lambda i: (i, 0)),
                  pl.BlockSpec((d, 2 * d), lambda i: (0, 0))],
        out_specs=[pl.BlockSpec((tm, d), lambda i: (i, 0))] * 2,
        out_shape=[jax.ShapeDtypeStruct((n, d), BF16)] * 2,
        compiler_params=_cparams(1),
        name="mem_kv",
    )(mem2d, wkv.astype(BF16))


def _mem_attn_kernel(h_ref, wq_ref, k_ref, v_ref, wo_ref, g_ref, beta_ref, o_ref):
    h = h_ref[...]
    q = jnp.dot(h.astype(BF16), wq_ref[...], preferred_element_type=F32)
    q = (q * (MEM_HEAD_DIM ** -0.5)).astype(BF16)
    outs = []
    for hd in range(MEM_HEADS):
        sl = slice(hd * MEM_HEAD_DIM, (hd + 1) * MEM_HEAD_DIM)
        sc = lax.dot_general(q[:, sl], k_ref[:, sl], NT_DIMS, preferred_element_type=F32)
        p = jnp.exp(sc - jnp.max(sc, axis=1, keepdims=True))
        den = jnp.sum(p, axis=1, keepdims=True)
        o = jnp.dot(p.astype(BF16), v_ref[:, sl], preferred_element_type=F32) / den
        outs.append(o.astype(BF16))
    y = jnp.dot(jnp.concatenate(outs, axis=1), wo_ref[...], preferred_element_type=F32)
    o_ref[...] = _layer_norm(ALPHA * h + y, g_ref[...], beta_ref[...])


def _mem_attn_call(h, wq, k, v, wo, g, beta, b, s, tm):
    n, d = h.shape
    m = k.shape[0] // b
    const = lambda shp: pl.BlockSpec(shp, lambda bi, i: (0,) * len(shp))
    row = pl.BlockSpec((None, tm, d), lambda bi, i: (bi, i, 0))
    kv = pl.BlockSpec((None, m, d), lambda bi, i: (bi, 0, 0))
    out = pl.pallas_call(
        _mem_attn_kernel,
        grid=(b, s // tm),
        in_specs=[row, const((d, d)), kv, kv, const((d, d)), const((1, d)), const((1, d))],
        out_specs=row,
        out_shape=jax.ShapeDtypeStruct((b, s, d), F32),
        compiler_params=_cparams(2),
        name="mem_attn",
    )(h.reshape(b, s, d), wq.astype(BF16), k.reshape(b, m, d), v.reshape(b, m, d),
      wo.astype(BF16), g.reshape(1, d), beta.reshape(1, d))
    return out.reshape(n, d)


def _router_kernel(h_ref, rw_ref, bias_ref, tri_ref, e_ref, g_ref, rank_ref, cnt_ref, run_ref):
    tm = h_ref.shape[0]

    @pl.when(pl.program_id(0) == 0)
    def _():
        run_ref[...] = jnp.zeros(run_ref.shape, F32)

    logits = jnp.dot(h_ref[...].astype(BF16), rw_ref[...], preferred_element_type=F32)
    scores = 1.0 / (1.0 + jnp.exp(-logits))
    lane = lax.broadcasted_iota(I32, (tm, LANES), 1)
    lanef = lane.astype(F32)
    ninf = jnp.float32(-jnp.inf)
    cur = jnp.where(lane < N_EXPERTS, scores + bias_ref[...], ninf)
    top_e = jnp.zeros((tm, LANES), F32)
    top_s = jnp.zeros((tm, LANES), F32)
    picked = jnp.zeros((tm, LANES), F32)
    firsts = []
    for k in range(TOP_K):
        mx = jnp.max(cur, axis=1, keepdims=True)
        first = jnp.min(jnp.where(cur == mx, lanef, float(LANES)), axis=1, keepdims=True)
        hit = lanef == first
        s_k = jnp.sum(jnp.where(hit, scores, 0.0), axis=1, keepdims=True)
        top_e = jnp.where(lane == k, first, top_e)
        top_s = jnp.where(lane == k, s_k, top_s)
        picked = jnp.where(hit, 1.0, picked)
        cur = jnp.where(hit, ninf, cur)
        firsts.append(first)
    e_ref[...] = top_e.astype(I32)
    g_ref[...] = top_s / jnp.sum(top_s, axis=1, keepdims=True) * ROUTED_SCALE

    slot = jnp.dot(tri_ref[...], picked.astype(BF16), preferred_element_type=F32) + run_ref[...]
    rank = jnp.zeros((tm, LANES), F32)
    for k in range(TOP_K):
        r_k = jnp.sum(jnp.where(lanef == firsts[k], slot, 0.0), axis=1, keepdims=True)
        rank = jnp.where(lane == k, r_k, rank)
    rank_ref[...] = rank.astype(I32)
    run = run_ref[...] + jnp.sum(picked, axis=0, keepdims=True)
    run_ref[...] = run
    cnt_ref[...] = run.astype(I32)


def _router_call(h, router_w, router_bias, tm):
    n, d = h.shape
    rw = jnp.pad(router_w, ((0, 0), (0, LANES - N_EXPERTS))).astype(BF16)
    bias = jnp.pad(router_bias, (0, LANES - N_EXPERTS)).reshape(1, LANES)
    tri = jnp.tri(tm, k=-1, dtype=BF16)
    row = pl.BlockSpec((tm, LANES), lambda i: (i, 0))
    const = lambda shp: pl.BlockSpec(shp, lambda i: (0, 0))
    return pl.pallas_call(
        _router_kernel,
        grid=(n // tm,),
        in_specs=[pl.BlockSpec((tm, d), lambda i: (i, 0)), const((d, LANES)), const((1, LANES)),
                  const((tm, tm))],
        out_specs=[row, row, row, const((1, LANES))],
        out_shape=[jax.ShapeDtypeStruct((n, LANES), I32),
                   jax.ShapeDtypeStruct((n, LANES), F32),
                   jax.ShapeDtypeStruct((n, LANES), I32),
                   jax.ShapeDtypeStruct((1, LANES), I32)],
        scratch_shapes=[pltpu.VMEM((1, LANES), F32)],
        compiler_params=_cparams(1),
        name="router",
    )(h, rw, bias, tri)


def _dispatch_kernel(dest_ref, x_ref, xs_ref, sem, *, td):
    def row_copy(g, rr, k):
        r0 = pl.multiple_of(g * SUBLANES, SUBLANES)
        return pltpu.make_async_copy(
            x_ref.at[pl.ds(r0 + rr, 1), :],
            xs_ref.at[pl.ds(dest_ref[g * (SUBLANES * TOP_K) + rr * TOP_K + k], 1), :], sem)

    def issue(g, carry):
        for rr in range(SUBLANES):
            for k in range(TOP_K):
                row_copy(g, rr, k).start()
        return carry

    lax.fori_loop(0, td // SUBLANES, issue, 0)

    def drain(g, carry):
        for rr in range(SUBLANES):
            for k in range(TOP_K):
                row_copy(g, rr, k).wait()
        return carry

    lax.fori_loop(0, td // SUBLANES, drain, 0)


def _dispatch_call(h, dest, td):
    n, d = h.shape
    m = n * TOP_K
    return pl.pallas_call(
        functools.partial(_dispatch_kernel, td=td),
        grid=(n // td,),
        in_specs=[pl.BlockSpec((td * TOP_K,), lambda i: (i,), memory_space=pltpu.SMEM),
                  pl.BlockSpec((td, d), lambda i: (i, 0))],
        out_specs=pl.BlockSpec(memory_space=pl.ANY),
        out_shape=jax.ShapeDtypeStruct((m, d), F32),
        scratch_shapes=[pltpu.SemaphoreType.DMA(())],
        compiler_params=_cparams(1),
        name="dispatch",
    )(dest, h)


def _moe_mm_kernel(ptile_ref, pexp_ref, plo_ref, phi_ref, xs_ref, wg_ref, wu_ref, wd_ref, y_ref,
                   *, tmm):
    p = pl.program_id(0)
    tile = ptile_ref[p]
    lo = plo_ref[p]
    hi = phi_ref[p]
    first = jnp.logical_or(p == 0, ptile_ref[jnp.maximum(p - 1, 0)] != tile)

    def expert_out():
        x = xs_ref[...].astype(BF16)
        g = jnp.dot(x, wg_ref[...], preferred_element_type=F32)
        u = jnp.dot(x, wu_ref[...], preferred_element_type=F32)
        hid = (_silu(g) * u).astype(BF16)
        return jnp.dot(hid, wd_ref[...], preferred_element_type=F32)

    whole = jnp.logical_and(lo <= tile * tmm, hi >= (tile + 1) * tmm)

    @pl.when(whole)
    def _():
        y_ref[...] = expert_out()

    @pl.when(jnp.logical_and(first, jnp.logical_not(whole)))
    def _():
        y_ref[...] = jnp.zeros(y_ref.shape, F32)

    @pl.when(jnp.logical_and(lo < hi, jnp.logical_not(whole)))
    def _():
        y = expert_out()
        rowg = tile * tmm + lax.broadcasted_iota(I32, (tmm, 1), 0)
        y_ref[...] += jnp.where(jnp.logical_and(rowg >= lo, rowg < hi), y, 0.0)


def _moe_mm_call(xs, pairs, wg, wu, wd, tmm):
    m = xs.shape[0]
    n_pairs = pairs[0].shape[0]
    rows = pl.BlockSpec((tmm, D_MODEL), lambda p, pt, pe, plo, phi: (pt[p], 0))
    wspec = lambda shp: pl.BlockSpec((None,) + shp, lambda p, pt, pe, plo, phi: (pe[p], 0, 0))
    return pl.pallas_call(
        functools.partial(_moe_mm_kernel, tmm=tmm),
        grid_spec=pltpu.PrefetchScalarGridSpec(
            num_scalar_prefetch=4,
            grid=(n_pairs,),
            in_specs=[rows, wspec((D_MODEL, EXPERT_DIM)), wspec((D_MODEL, EXPERT_DIM)),
                      wspec((EXPERT_DIM, D_MODEL))],
            out_specs=rows),
        out_shape=jax.ShapeDtypeStruct((m, D_MODEL), F32),
        compiler_params=_cparams(1),
        name="moe_mm",
    )(*pairs, xs, wg, wu, wd)


def _group_pairs(counts, m, tmm):
    n_tiles = m // tmm
    n_pairs = n_tiles + N_EXPERTS
    gend = jnp.cumsum(counts)
    gstart = gend - counts
    t0 = jnp.arange(n_tiles, dtype=I32) * tmm
    n_le = lambda edges, v: jnp.sum(edges[None, :] <= v[:, None], axis=1).astype(I32)
    e_first = n_le(gend, t0)
    e_last = n_le(gend, t0 + (tmm - 1))
    per_tile = e_last - e_first + 1
    pend = jnp.cumsum(per_tile)
    pstart = pend - per_tile
    p = jnp.arange(n_pairs, dtype=I32)
    valid = p < pend[-1]
    tile = jnp.minimum(n_le(pend, p), n_tiles - 1)
    e = jnp.clip(e_first[tile] + (p - pstart[tile]), 0, N_EXPERTS - 1).astype(I32)
    lo = jnp.where(valid, gstart[e], 0).astype(I32)
    hi = jnp.where(valid, gend[e], 0).astype(I32)
    return tile, e, lo, hi


def _combine_kernel(dest_ref, gate_ref, h_ref, y_ref, sg_ref, su_ref, sd_ref, g_ref, beta_ref, o_ref,
                    buf_ref, sem, *, tc):
    def row_copy(g, rr, k):
        r0 = pl.multiple_of(g * SUBLANES, SUBLANES)
        return pltpu.make_async_copy(
            y_ref.at[pl.ds(dest_ref[g * (SUBLANES * TOP_K) + rr * TOP_K + k], 1), :],
            buf_ref.at[k, pl.ds(r0 + rr, 1), :], sem)

    def issue(g, carry):
        for rr in range(SUBLANES):
            for k in range(TOP_K):
                row_copy(g, rr, k).start()
        return carry

    lax.fori_loop(0, tc // SUBLANES, issue, 0)

    h = h_ref[...]
    x = h.astype(BF16)
    hid = (_silu(jnp.dot(x, sg_ref[...], preferred_element_type=F32))
           * jnp.dot(x, su_ref[...], preferred_element_type=F32)).astype(BF16)
    shared = jnp.dot(hid, sd_ref[...], preferred_element_type=F32)

    def drain(g, carry):
        for rr in range(SUBLANES):
            for k in range(TOP_K):
                row_copy(g, rr, k).wait()
        return carry

    lax.fori_loop(0, tc // SUBLANES, drain, 0)

    gates = gate_ref[...]
    routed = gates[:, 0:1] * buf_ref[0]
    for k in range(1, TOP_K):
        routed = routed + gates[:, k:k + 1] * buf_ref[k]
    o_ref[...] = _layer_norm(ALPHA * h + (routed + shared), g_ref[...], beta_ref[...])


def _combine_call(h, y, dest, gates, sg, su, sd, g, beta, tc):
    n, d = h.shape
    smem = pl.BlockSpec((tc * TOP_K,), lambda i: (i,), memory_space=pltpu.SMEM)
    const = lambda shp: pl.BlockSpec(shp, lambda i: (0, 0))
    return pl.pallas_call(
        functools.partial(_combine_kernel, tc=tc),
        grid=(n // tc,),
        in_specs=[smem, pl.BlockSpec((tc, LANES), lambda i: (i, 0)),
                  pl.BlockSpec((tc, d), lambda i: (i, 0)),
                  pl.BlockSpec(memory_space=pl.ANY),
                  const((d, SHARED_DIM)), const((d, SHARED_DIM)), const((SHARED_DIM, d)),
                  const((1, d)), const((1, d))],
        out_specs=pl.BlockSpec((tc, d), lambda i: (i, 0)),
        out_shape=jax.ShapeDtypeStruct((n, d), F32),
        scratch_shapes=[pltpu.VMEM((TOP_K, tc, d), F32),
                        pltpu.SemaphoreType.DMA(())],
        compiler_params=_cparams(1),
        name="combine",
    )(dest, gates, h, y, sg.astype(BF16), su.astype(BF16), sd.astype(BF16),
      g.reshape(1, d), beta.reshape(1, d))


def _tiles(b, s):
    n = b * s
    pick = lambda pref, total: next(t for t in (pref, 512, 256, 128, 64, 32, 16, 8) if t <= pref and total % t == 0)
    return dict(
        rows=pick(512, n),
        attn_q=pick(512, s),
        attn_k=pick(256, s),
        mem_rows=pick(512, s),
        mem_kv=pick(512, b * 256),
        router=pick(1024, n),
        dispatch=pick(256, n),
        moe=pick(512, n * TOP_K),
        combine=pick(128, n),
    )


def kernel(x, mem, ln_in_g, ln_in_b, w_in, a_kv_norm, a_w_uk, a_w_uv, b_lq1, b_lk1, b_lq2, b_lk2,
           b_subln, w_o, ln1_g, ln1_b, m_wq, m_wkv, m_wo, ln2_g, ln2_b, router_w, router_bias,
           e_w_gate, e_w_up, e_w_down, s_w_gate, s_w_up, s_w_down, ln3_g, ln3_b):
    b, s, d = x.shape
    n = b * s
    m = n * TOP_K
    t = _tiles(b, s)
    mem2d = mem.reshape(-1, d)

    h = _ln_call(x.reshape(n, d), ln_in_g, ln_in_b, t["rows"])
    for l in range(DEPTH):
        lambda_init = 0.8 - 0.6 * math.exp(-0.3 * l)

        qa, ckv, ckvt, qidx, kk, widx, qb, kb, vbt = _proj_call(h, _pad_w_in(w_in[l]), a_kv_norm[l],
                                                                  t["rows"], s)
        out_a = _dsa_call(qa, qidx, widx, kk, ckv, ckvt, _block_diag_uk(a_w_uk[l]),
                          _block_diag_uv_t(a_w_uv[l]), b, s, t["attn_q"], t["attn_k"])
        lam_rows = jnp.pad(jnp.stack([b_lq1[l], b_lk1[l], b_lq2[l], b_lk2[l]]),
                           ((0, SUBLANES - 4), (0, LANES - B_QK_DIM)))
        out_b = _diff_call(qb, kb, vbt, lam_rows, b_subln[l], b, s, t["attn_q"], t["attn_k"], lambda_init)
        h = _mix_out_call(h, out_a, out_b, w_o[l], ln1_g[l], ln1_b[l], t["rows"])

        mk, mv = _mem_kv_call(mem2d, m_wkv[l], t["mem_kv"])
        h = _mem_attn_call(h, m_wq[l], mk, mv, m_wo[l], ln2_g[l], ln2_b[l], b, s, t["mem_rows"])

        top_e, gates, rank, counts = _router_call(h, router_w[l], router_bias[l], t["router"])
        counts = counts[0, :N_EXPERTS]
        gstart = jnp.cumsum(counts) - counts
        top_e = top_e[:, :TOP_K]
        dest = (rank[:, :TOP_K] + gstart[top_e]).reshape(m)
        xs = _dispatch_call(h, dest, t["dispatch"])
        pairs = _group_pairs(counts, m, t["moe"])
        y = _moe_mm_call(xs, pairs, e_w_gate[l].astype(BF16), e_w_up[l].astype(BF16),
                         e_w_down[l].astype(BF16), t["moe"])
        h = _combine_call(h, y, dest, gates, s_w_gate[l], s_w_up[l], s_w_down[l],
                          ln3_g[l], ln3_b[l], t["combine"])
    return h.reshape(b, s, d)
```

```python
import functools
import math

import jax
import jax.numpy as jnp
import numpy as np
from jax import lax
from jax.experimental import pallas as pl
from jax.experimental.pallas import tpu as pltpu

F32 = jnp.float32
BF16 = jnp.bfloat16
I32 = jnp.int32
I16 = jnp.int16

D_MODEL = 1024
DEPTH = 2
CHUNK = 64
CHUNK_SHIFT = 6
A_HEADS = 8
A_HEAD_DIM = 64
A_WIDTH = A_HEADS * A_HEAD_DIM
KV_LORA = 128
IDX_HEADS = 8
IDX_DIM = 64
INDEX_TOPK_MAX = 256
B_HEADS = 4
B_QK_DIM = 64
B_V_DIM = 128
B_WIDTH = B_HEADS * B_V_DIM
MEM_HEADS = 4
MEM_HEAD_DIM = D_MODEL // MEM_HEADS
N_EXPERTS = 64
TOP_K = 8
EXPERT_DIM = 256
SHARED_DIM = 256
ROUTED_SCALE = 2.5
ALPHA = (2 * DEPTH) ** 0.25
LN_EPS = 1e-5
RMS_EPS = 1e-6

LANES = 128
SUBLANES = 8
VMEM_LIMIT = 56 * 1024 * 1024
INT_MIN = -2 ** 31
I16_MIN = -2 ** 15
V_PAD = 16
N_DMA_QUEUES = 2
NEG_BIG = -3.0e38
M_INIT = -1.0e30

NT_DIMS = (((1,), (1,)), ((), ()))


def _cparams(n_axes):
    return pltpu.CompilerParams(dimension_semantics=("arbitrary",) * n_axes,
                                vmem_limit_bytes=VMEM_LIMIT)


def _layer_norm(x, g, b):
    mu = jnp.mean(x, axis=-1, keepdims=True)
    xc = x - mu
    var = jnp.mean(xc * xc, axis=-1, keepdims=True)
    return xc * lax.rsqrt(var + LN_EPS) * g + b


def _silu(x):
    return x * (1.0 / (1.0 + jnp.exp(-x)))


def _ln_kernel(x_ref, g_ref, b_ref, o_ref):
    o_ref[...] = _layer_norm(x_ref[...], g_ref[...], b_ref[...])


def _ln_call(x, g, b, tm):
    n, d = x.shape
    return pl.pallas_call(
        _ln_kernel,
        grid=(n // tm,),
        in_specs=[pl.BlockSpec((tm, d), lambda i: (i, 0)),
                  pl.BlockSpec((1, d), lambda i: (0, 0)),
                  pl.BlockSpec((1, d), lambda i: (0, 0))],
        out_specs=pl.BlockSpec((tm, d), lambda i: (i, 0)),
        out_shape=jax.ShapeDtypeStruct((n, d), F32),
        compiler_params=_cparams(1),
        name="ln_in",
    )(x, g.reshape(1, d), b.reshape(1, d))


_SEG_QA = (0, 512)
_SEG_CKV = (512, 640)
_SEG_QIDX = (640, 1152)
_SEG_KK = (1152, 1280)
_SEG_WIDX = (1280, 1408)
_SEG_QB = (1408, 1920)
_SEG_KB = (1920, 2432)
_SEG_VB = (2432, 2944)
_PROJ_COLS = 2944


def _pos_features(pos, shape):
    lane = lax.broadcasted_iota(I32, shape, 1)
    hi = lax.shift_right_logical(pos, CHUNK_SHIFT).astype(F32)
    lo = (pos & (CHUNK - 1)).astype(F32)
    return jnp.where(lane < 2, 1.0, jnp.where(lane == 2, hi, jnp.where(lane == 3, lo, 0.0)))


def _slope_features(tqi, slope, shape):
    lane = lax.broadcasted_iota(I32, shape, 1)
    hi = lax.shift_right_logical(tqi, CHUNK_SHIFT).astype(F32)
    lo = (tqi & (CHUNK - 1)).astype(F32)
    return jnp.where(lane == 0, -slope * CHUNK * hi,
                     jnp.where(lane == 1, -slope * lo,
                               jnp.where(lane == 2, slope * CHUNK,
                                         jnp.where(lane == 3, slope, 0.0))))


def _with_ones_rows(vt):
    t = vt.shape[1]
    sub = lax.broadcasted_iota(I32, (V_PAD, t), 0)
    return jnp.concatenate([vt, jnp.where(sub == 0, 1.0, 0.0)], axis=0)


def _proj_kernel(h_ref, w_ref, kvn_ref, qa_ref, ckv_ref, ckvt_ref, qidx_ref, kk_ref, widx_ref,
                 qb_ref, kb_ref, vbt_ref, *, seq):
    tm = h_ref.shape[0]
    x = h_ref[...].astype(BF16)

    def seg(s):
        return jnp.dot(x, w_ref[:, s[0]:s[1]], preferred_element_type=F32)

    row = pl.program_id(0) * tm + lax.broadcasted_iota(I32, (tm, LANES), 0)
    feat = _pos_features(row & (seq - 1), (tm, LANES)).astype(BF16)

    qa_ref[...] = (seg(_SEG_QA) * (A_HEAD_DIM ** -0.5)).astype(BF16)
    c = seg(_SEG_CKV)
    c = c * lax.rsqrt(jnp.mean(c * c, axis=-1, keepdims=True) + RMS_EPS) * kvn_ref[...]
    ckv_ref[:, :KV_LORA] = c.astype(BF16)
    ckv_ref[:, KV_LORA:] = feat
    ckvt_ref[...] = _with_ones_rows(c.T).astype(BF16)
    qidx_ref[...] = (seg(_SEG_QIDX) * (IDX_DIM ** -0.5)).astype(BF16)
    kk_ref[...] = seg(_SEG_KK).astype(BF16)
    widx_ref[...] = seg(_SEG_WIDX) * (IDX_HEADS ** -0.5)
    qb_ref[...] = (seg(_SEG_QB) * (B_QK_DIM ** -0.5)).astype(BF16)
    kb = seg(_SEG_KB).astype(BF16)
    vt = seg(_SEG_VB).T
    for h in range(B_HEADS):
        kb_ref[:, 2 * h * LANES:(2 * h + 1) * LANES] = kb[:, h * LANES:(h + 1) * LANES]
        kb_ref[:, (2 * h + 1) * LANES:(2 * h + 2) * LANES] = feat
        vbt_ref[h * (B_V_DIM + V_PAD):(h + 1) * (B_V_DIM + V_PAD), :] = _with_ones_rows(
            vt[h * B_V_DIM:(h + 1) * B_V_DIM]).astype(BF16)


def _proj_call(h, w_pad, kvn, tm, seq):
    n, d = h.shape
    assert seq & (seq - 1) == 0 and seq <= CHUNK * 256 and seq % tm == 0
    outs = [(512, BF16, False), (2 * KV_LORA, BF16, False), (KV_LORA + V_PAD, BF16, True),
            (512, BF16, False), (128, BF16, False), (128, F32, False), (512, BF16, False),
            (2 * B_HEADS * LANES, BF16, False), (B_HEADS * (B_V_DIM + V_PAD), BF16, True)]
    spec = lambda w, t: (pl.BlockSpec((w, tm), lambda i: (0, i)) if t
                         else pl.BlockSpec((tm, w), lambda i: (i, 0)))
    shape = lambda w, dt, t: jax.ShapeDtypeStruct((w, n) if t else (n, w), dt)
    return pl.pallas_call(
        functools.partial(_proj_kernel, seq=seq),
        grid=(n // tm,),
        in_specs=[pl.BlockSpec((tm, d), lambda i: (i, 0)),
                  pl.BlockSpec((d, _PROJ_COLS), lambda i: (0, 0)),
                  pl.BlockSpec((1, KV_LORA), lambda i: (0, 0))],
        out_specs=[spec(w, t) for w, _, t in outs],
        out_shape=[shape(w, dt, t) for w, dt, t in outs],
        compiler_params=_cparams(1),
        name="proj_in",
    )(h, w_pad, kvn.reshape(1, KV_LORA))


def _pad_w_in(w_in):
    sizes = (A_WIDTH, KV_LORA, IDX_HEADS * IDX_DIM, IDX_DIM, IDX_HEADS,
             2 * B_HEADS * B_QK_DIM, 2 * B_HEADS * B_QK_DIM, B_WIDTH)
    splits = np.cumsum(sizes)[:-1].tolist()
    q_a, c_kv, q_idx, k_idx, w_idx, q_b, k_b, v_b = jnp.split(w_in, splits, axis=-1)
    w_idx = jnp.pad(w_idx, ((0, 0), (0, LANES - IDX_HEADS)))
    return jnp.concatenate([q_a, c_kv, q_idx, k_idx, k_idx, w_idx, q_b, k_b, v_b],
                           axis=-1).astype(BF16)


def _flash_probs(sl, sc, m_ref):
    m_old = m_ref[:, sl]
    m_new = jnp.maximum(m_old, jnp.max(sc, axis=0, keepdims=True))
    m_ref[:, sl] = m_new
    return jnp.exp(m_old - m_new), jnp.exp(sc - m_new).astype(BF16)


def _flash_init(m_ref, acc_ref):
    m_ref[...] = jnp.full(m_ref.shape, M_INIT, F32)
    acc_ref[...] = jnp.zeros(acc_ref.shape, F32)


def _split_halves(q_ref, n_pairs, out_ref, rows):
    lane = lax.broadcasted_iota(I32, (rows, LANES), 1)
    for j in range(n_pairs):
        qp = q_ref[:, j * LANES:(j + 1) * LANES].astype(F32)
        out_ref[2 * j * rows:(2 * j + 1) * rows, :LANES] = jnp.where(lane < 64, qp, 0.0).astype(BF16)
        out_ref[(2 * j + 1) * rows:(2 * j + 2) * rows, :LANES] = jnp.where(lane >= 64, qp, 0.0).astype(BF16)


def _dsa_kernel(qa_ref, qidx_ref, widx_ref, kk_ref, ckv_ref, ckvt_ref, wk_ref, wvt_ref, out_ref,
                keys_ref, khi_ref, klo_ref, qim_ref, qlat_ref, tj_ref, m_ref, acc_ref,
                *, tq, tk, seq, topk):
    i = pl.program_id(1)
    n_off = i * (tq // tk)
    n_kt = n_off + tq // tk
    idx_bits = int(math.log2(seq))

    _split_halves(qidx_ref, IDX_HEADS // 2, qim_ref, tq)
    qlat = jnp.dot(qa_ref[...], wk_ref[...], preferred_element_type=F32)
    t_col = i * tq + lax.broadcasted_iota(I32, (tq, LANES), 0)
    for h in range(A_HEADS):
        qlat_ref[h * tq:(h + 1) * tq, :KV_LORA] = qlat[:, h * KV_LORA:(h + 1) * KV_LORA].astype(BF16)
        qlat_ref[h * tq:(h + 1) * tq, KV_LORA:] = _slope_features(
            t_col, 2.0 ** -(h + 1), (tq, LANES)).astype(BF16)
    w_t = widx_ref[...].T

    row = lax.broadcasted_iota(I32, (tk, tq), 0)
    tqi = i * tq + lax.broadcasted_iota(I32, (tk, tq), 1)
    qchunk = lax.shift_right_logical(tqi, CHUNK_SHIFT)

    def idx_body(kt, carry):
        off = pl.multiple_of(kt * tk, tk)
        kkt = kk_ref[pl.ds(off, tk), :]
        s_all = lax.dot_general(kkt, qim_ref[...], NT_DIMS, preferred_element_type=F32)
        acc = jnp.zeros((tk, tq), F32)
        for h in range(IDX_HEADS):
            acc = acc + w_t[h:h + 1, :] * jnp.maximum(s_all[:, h * tq:(h + 1) * tq], 0.0)
        bits = lax.bitcast_convert_type(acc + 0.0, I32)
        key = jnp.where(bits < 0, bits ^ jnp.int32(0x7FFFFFFF), bits)
        adm = lax.shift_right_logical(off + row, CHUNK_SHIFT) <= qchunk
        key = jnp.where(adm, key, jnp.int32(INT_MIN))
        keys_ref[pl.ds(off, tk), :] = key
        khi_ref[pl.ds(off, tk), :] = lax.shift_right_arithmetic(key, 16).astype(I16)
        klo_ref[pl.ds(off, tk), :] = ((key & 0xFFFF) + I16_MIN).astype(I16)
        return carry

    lax.fori_loop(0, n_kt, idx_body, 0)

    def count(pred):
        def body(kt, c):
            off = pl.multiple_of(kt * tk, tk)
            ind = jnp.where(pred(keys_ref[pl.ds(off, tk), :], off + row), 1.0, 0.0)
            return c + jnp.sum(ind, axis=0, keepdims=True)
        return lax.fori_loop(0, n_kt, body, jnp.zeros((1, tq), F32))

    def count16(ref, pred):
        pack = 2 * SUBLANES
        def body(kt, c):
            off = pl.multiple_of(kt * tk, tk)
            ind = jnp.where(pred(ref[pl.ds(off, tk), :]), jnp.int16(1), jnp.int16(0))
            parts = [ind[j * pack:(j + 1) * pack] for j in range(tk // pack)]
            while len(parts) > 1:
                parts = [parts[j] + parts[j + 1] for j in range(0, len(parts), 2)]
            return c + parts[0]
        c = lax.fori_loop(0, n_kt, body, jnp.zeros((pack, tq), I16))
        return jnp.sum(c.astype(F32), axis=0, keepdims=True)

    def search16(ref, need):
        def body(it, t):
            cand = t + lax.shift_left(jnp.int32(1), 15 - it)
            cand16 = cand.astype(I16)
            cnt = count16(ref, lambda tile: tile >= cand16)
            return jnp.where(cnt >= need, cand, t)
        return lax.fori_loop(0, 16, body, jnp.full((1, tq), I16_MIN, I32))

    t_hi = search16(khi_ref, topk)
    t_hi16 = t_hi.astype(I16)
    n_above = count16(khi_ref, lambda tile: tile > t_hi16)

    def park_body(kt, carry):
        off = pl.multiple_of(kt * tk, tk)
        klo_ref[pl.ds(off, tk), :] = jnp.where(khi_ref[pl.ds(off, tk), :] == t_hi16,
                                               klo_ref[pl.ds(off, tk), :], jnp.int16(I16_MIN))
        return carry

    lax.fori_loop(0, n_kt, park_body, 0)
    t_lo = search16(klo_ref, topk - n_above)
    thr = t_hi * 65536 + (t_lo - I16_MIN)

    live = thr > INT_MIN
    n_ge = count(lambda tile, _: tile >= thr)
    excess = jnp.logical_and(n_ge > topk, live)
    all_ties = jnp.where(live, seq, -1).astype(I32)
    tj_ref[0] = jnp.broadcast_to(thr, (SUBLANES, tq))
    tj_ref[1] = jnp.broadcast_to(all_ties, (SUBLANES, tq))

    @pl.when(jnp.max(jnp.where(excess, 1.0, 0.0)) > 0.0)
    def _():
        need = topk - count(lambda tile, _: tile > thr)

        def tie_body(it, hi):
            cand = hi + lax.shift_left(jnp.int32(1), idx_bits - 1 - it)
            cnt = count(lambda tile, pos: jnp.logical_and(tile == thr, pos < cand))
            return jnp.where(cnt < need, cand, hi)

        hi = lax.fori_loop(0, idx_bits, tie_body, jnp.zeros((1, tq), I32))
        tj_ref[1] = jnp.broadcast_to(jnp.where(excess, hi, all_ties), (SUBLANES, tq))

    thr_b = tj_ref[0][:1]
    tie_hi = tj_ref[1][:1]

    _flash_init(m_ref, acc_ref)

    def att_tile(kt, late_fix):
        off = pl.multiple_of(kt * tk, tk)
        tile = keys_ref[pl.ds(off, tk), :]
        pos = off + row
        sel = jnp.logical_or(tile > thr_b,
                             jnp.logical_and(tile == thr_b, pos <= tie_hi))
        s_all = lax.dot_general(ckv_ref[pl.ds(off, tk), :], qlat_ref[...], NT_DIMS,
                                preferred_element_type=F32)
        if late_fix:
            late = jnp.maximum(pos - tqi, 0).astype(F32)
        alphas, probs = [], []
        for h in range(A_HEADS):
            sl = slice(h * tq, (h + 1) * tq)
            sc = s_all[:, sl]
            if late_fix:
                sc = sc - (2.0 * 2.0 ** -(h + 1)) * late
            alpha, p = _flash_probs(sl, jnp.where(sel, sc, NEG_BIG), m_ref)
            alphas.append(alpha)
            probs.append(p)
        pv = jnp.dot(ckvt_ref[:, pl.ds(off, tk)], jnp.concatenate(probs, axis=1),
                     preferred_element_type=F32)
        acc_ref[...] = jnp.concatenate(alphas, axis=1) * acc_ref[...] + pv

    def off_body(kt, carry):
        att_tile(kt, False)
        return carry

    def diag_body(kt, carry):
        att_tile(kt, True)
        return carry

    lax.fori_loop(0, n_off, off_body, 0)
    lax.fori_loop(n_off, n_kt, diag_body, 0)

    inv_l = 1.0 / acc_ref[KV_LORA:KV_LORA + 1, :]
    olat = jnp.concatenate([(acc_ref[:KV_LORA, h * tq:(h + 1) * tq]
                             * inv_l[:, h * tq:(h + 1) * tq]).astype(BF16)
                            for h in range(A_HEADS)], axis=0)
    out_t = jnp.dot(wvt_ref[...], olat, preferred_element_type=F32)
    out_ref[...] = out_t.T.astype(BF16)


def _dsa_call(qa, qidx, widx, kk, ckv, ckvt, wk_bd, wvt_bd, b, s, tq, tk):
    topk = min(INDEX_TOPK_MAX, s // 4)
    assert s % tq == 0 and tq % tk == 0 and tk % LANES == 0 and s & (s - 1) == 0
    blk = lambda w: pl.BlockSpec((None, tq, w), lambda bi, i: (bi, i, 0))
    full = lambda w: pl.BlockSpec((None, s, w), lambda bi, i: (bi, 0, 0))
    const = lambda a: pl.BlockSpec(a.shape, lambda bi, i: (0, 0))
    r3 = lambda a: a.reshape(b, s, a.shape[-1])
    out = pl.pallas_call(
        functools.partial(_dsa_kernel, tq=tq, tk=tk, seq=s, topk=topk),
        grid=(b, s // tq),
        in_specs=[blk(A_WIDTH), blk(IDX_HEADS * IDX_DIM), blk(LANES), full(LANES),
                  full(2 * KV_LORA),
                  pl.BlockSpec((KV_LORA + V_PAD, s), lambda bi, i: (0, bi)),
                  const(wk_bd), const(wvt_bd)],
        out_specs=blk(A_WIDTH),
        out_shape=jax.ShapeDtypeStruct((b, s, A_WIDTH), BF16),
        scratch_shapes=[pltpu.VMEM((s, tq), I32),
                        pltpu.VMEM((s, tq), I16),
                        pltpu.VMEM((s, tq), I16),
                        pltpu.VMEM((IDX_HEADS * tq, LANES), BF16),
                        pltpu.VMEM((A_HEADS * tq, 2 * KV_LORA), BF16),
                        pltpu.VMEM((2, SUBLANES, tq), I32),
                        pltpu.VMEM((1, A_HEADS * tq), F32),
                        pltpu.VMEM((KV_LORA + V_PAD, A_HEADS * tq), F32)],
        compiler_params=_cparams(2),
        name="dsa",
    )(r3(qa), r3(qidx), r3(widx), r3(kk), r3(ckv), ckvt, wk_bd, wvt_bd)
    return out.reshape(b * s, A_WIDTH)


def _block_diag_uk(w_uk):
    h, c, d = w_uk.shape
    eye = jnp.eye(h, dtype=w_uk.dtype)
    return jnp.einsum('hcd,hg->hdgc', w_uk, eye).reshape(h * d, h * c).astype(BF16)


def _block_diag_uv_t(w_uv):
    h, c, d = w_uv.shape
    eye = jnp.eye(h, dtype=w_uv.dtype)
    return jnp.einsum('hcd,hg->hdgc', w_uv, eye).reshape(h * d, h * c).astype(BF16)


def _diff_kernel(qb_ref, kb_ref, vbt_ref, lam_ref, subln_ref, out_ref,
                 qm_ref, m_ref, acc_ref, *, tq, tk, lambda_init):
    i = pl.program_id(1)
    n_off = i * (tq // tk)
    dvp = B_V_DIM + V_PAD
    _split_halves(qb_ref, B_HEADS, qm_ref, tq)
    t_col = i * tq + lax.broadcasted_iota(I32, (tq, LANES), 0)
    for h in range(B_HEADS):
        feat = _slope_features(t_col, 2.0 ** (-2 * (h + 1)), (tq, LANES)).astype(BF16)
        qm_ref[2 * h * tq:(2 * h + 1) * tq, LANES:] = feat
        qm_ref[(2 * h + 1) * tq:(2 * h + 2) * tq, LANES:] = feat

    row = lax.broadcasted_iota(I32, (tk, tq), 0)
    tqi = i * tq + lax.broadcasted_iota(I32, (tk, tq), 1)
    qchunk = lax.shift_right_logical(tqi, CHUNK_SHIFT)
    _flash_init(m_ref, acc_ref)

    def tile_step(kt, diag):
        off = pl.multiple_of(kt * tk, tk)
        s_pairs = [lax.dot_general(kb_ref[pl.ds(off, tk), 2 * h * LANES:(2 * h + 2) * LANES],
                                   qm_ref[2 * h * tq:(2 * h + 2) * tq, :], NT_DIMS,
                                   preferred_element_type=F32) for h in range(B_HEADS)]
        if diag:
            pos = off + row
            adm = lax.shift_right_logical(pos, CHUNK_SHIFT) <= qchunk
            late = jnp.maximum(pos - tqi, 0).astype(F32)
        alphas, probs = [], []
        for h in range(B_HEADS):
            for j in range(2):
                sl = slice((2 * h + j) * tq, (2 * h + j + 1) * tq)
                sc = s_pairs[h][:, j * tq:(j + 1) * tq]
                if diag:
                    sc = jnp.where(adm, sc - (2.0 * 2.0 ** (-2 * (h + 1))) * late, NEG_BIG)
                alpha, p = _flash_probs(sl, sc, m_ref)
                alphas.append(alpha)
                probs.append(p)
        pv = jnp.concatenate(
            [jnp.dot(vbt_ref[h * dvp:(h + 1) * dvp, pl.ds(off, tk)],
                     jnp.concatenate(probs[2 * h:2 * h + 2], axis=1), preferred_element_type=F32)
             for h in range(B_HEADS)], axis=1)
        acc_ref[...] = jnp.concatenate(alphas, axis=1) * acc_ref[...] + pv

    def off_body(kt, carry):
        tile_step(kt, False)
        return carry

    def diag_body(kt, carry):
        tile_step(kt, True)
        return carry

    lax.fori_loop(0, n_off, off_body, 0)
    lax.fori_loop(n_off, n_off + tq // tk, diag_body, 0)

    lp = lam_ref[...]
    lam = (jnp.exp(jnp.sum(lp[0:1] * lp[1:2], axis=1, keepdims=True))
           - jnp.exp(jnp.sum(lp[2:3] * lp[3:4], axis=1, keepdims=True)) + lambda_init)
    o_all = acc_ref[:B_V_DIM, :] * (1.0 / acc_ref[B_V_DIM:B_V_DIM + 1, :])
    outs = []
    for h in range(B_HEADS):
        o = (o_all[:, 2 * h * tq:(2 * h + 1) * tq]
             - lam * o_all[:, (2 * h + 1) * tq:(2 * h + 2) * tq])
        o = o * lax.rsqrt(jnp.mean(o * o, axis=0, keepdims=True) + RMS_EPS) * subln_ref[...]
        outs.append(o * (1.0 - lambda_init))
    out_ref[...] = jnp.concatenate(outs, axis=0).T.astype(BF16)


def _diff_call(qb, kb, vbt, lam_rows, subln, b, s, tq, tk, lambda_init):
    blk = pl.BlockSpec((None, tq, B_WIDTH), lambda bi, i: (bi, i, 0))
    kw = 2 * B_HEADS * LANES
    dvp = B_V_DIM + V_PAD
    out = pl.pallas_call(
        functools.partial(_diff_kernel, tq=tq, tk=tk, lambda_init=lambda_init),
        grid=(b, s // tq),
        in_specs=[blk, pl.BlockSpec((None, s, kw), lambda bi, i: (bi, 0, 0)),
                  pl.BlockSpec((B_HEADS * dvp, s), lambda bi, i: (0, bi)),
                  pl.BlockSpec((SUBLANES, LANES), lambda bi, i: (0, 0)),
                  pl.BlockSpec((B_V_DIM, 1), lambda bi, i: (0, 0))],
        out_specs=blk,
        out_shape=jax.ShapeDtypeStruct((b, s, B_WIDTH), BF16),
        scratch_shapes=[pltpu.VMEM((2 * B_HEADS * tq, 2 * LANES), BF16),
                        pltpu.VMEM((1, 2 * B_HEADS * tq), F32),
                        pltpu.VMEM((dvp, 2 * B_HEADS * tq), F32)],
        compiler_params=_cparams(2),
        name="diff",
    )(qb.reshape(b, s, B_WIDTH), kb.reshape(b, s, kw), vbt, lam_rows, subln.reshape(B_V_DIM, 1))
    return out.reshape(b * s, B_WIDTH)


def _mix_out_kernel(h_ref, a_ref, b_ref, wa_ref, wb_ref, g_ref, beta_ref, o_ref):
    y = (jnp.dot(a_ref[...], wa_ref[...], preferred_element_type=F32)
         + jnp.dot(b_ref[...], wb_ref[...], preferred_element_type=F32))
    o_ref[...] = _layer_norm(ALPHA * h_ref[...] + y, g_ref[...], beta_ref[...])


def _mix_out_call(h, out_a, out_b, w_o, g, beta, tm):
    n, d = h.shape
    w = w_o.astype(BF16)
    row = lambda wd: pl.BlockSpec((tm, wd), lambda i: (i, 0))
    const = lambda shp: pl.BlockSpec(shp, lambda i: (0, 0))
    return pl.pallas_call(
        _mix_out_kernel,
        grid=(n // tm,),
        in_specs=[row(d), row(A_WIDTH), row(B_WIDTH), const((A_WIDTH, d)), const((B_WIDTH, d)),
                  const((1, d)), const((1, d))],
        out_specs=row(d),
        out_shape=jax.ShapeDtypeStruct((n, d), F32),
        compiler_params=_cparams(1),
        name="mix_out",
    )(h, out_a, out_b, w[:A_WIDTH], w[A_WIDTH:], g.reshape(1, d), beta.reshape(1, d))


def _mem_kv_kernel(m_ref, w_ref, k_ref, v_ref):
    x = m_ref[...].astype(BF16)
    d = k_ref.shape[-1]
    k_ref[...] = jnp.dot(x, w_ref[:, :d], preferred_element_type=F32).astype(BF16)
    v_ref[...] = jnp.dot(x, w_ref[:, d:], preferred_element_type=F32).astype(BF16)


def _mem_kv_call(mem2d, wkv, tm):
    n, d = mem2d.shape
    return pl.pallas_call(
        _mem_kv_kernel,
        grid=(n // tm,),
        in_specs=[pl.BlockSpec((tm, d), lambda i: (i, 0)),
                  pl.BlockSpec((d, 2 * d), lambda i: (0, 0))],
        out_specs=[pl.BlockSpec((tm, d), lambda i: (i, 0))] * 2,
        out_shape=[jax.ShapeDtypeStruct((n, d), BF16)] * 2,
        compiler_params=_cparams(1),
        name="mem_kv",
    )(mem2d, wkv.astype(BF16))


def _mem_attn_kernel(h_ref, wq_ref, k_ref, v_ref, wo_ref, g_ref, beta_ref, o_ref):
    h = h_ref[...]
    q = jnp.dot(h.astype(BF16), wq_ref[...], preferred_element_type=F32)
    q = (q * (MEM_HEAD_DIM ** -0.5)).astype(BF16)
    outs = []
    for hd in range(MEM_HEADS):
        sl = slice(hd * MEM_HEAD_DIM, (hd + 1) * MEM_HEAD_DIM)
        sc = lax.dot_general(q[:, sl], k_ref[:, sl], NT_DIMS, preferred_element_type=F32)
        p = jnp.exp(sc - jnp.max(sc, axis=1, keepdims=True))
        den = jnp.sum(p, axis=1, keepdims=True)
        o = jnp.dot(p.astype(BF16), v_ref[:, sl], preferred_element_type=F32) / den
        outs.append(o.astype(BF16))
    y = jnp.dot(jnp.concatenate(outs, axis=1), wo_ref[...], preferred_element_type=F32)
    o_ref[...] = _layer_norm(ALPHA * h + y, g_ref[...], beta_ref[...])


def _mem_attn_call(h, wq, k, v, wo, g, beta, b, s, tm):
    n, d = h.shape
    m = k.shape[0] // b
    const = lambda shp: pl.BlockSpec(shp, lambda bi, i: (0,) * len(shp))
    row = pl.BlockSpec((None, tm, d), lambda bi, i: (bi, i, 0))
    kv = pl.BlockSpec((None, m, d), lambda bi, i: (bi, 0, 0))
    out = pl.pallas_call(
        _mem_attn_kernel,
        grid=(b, s // tm),
        in_specs=[row, const((d, d)), kv, kv, const((d, d)), const((1, d)), const((1, d))],
        out_specs=row,
        out_shape=jax.ShapeDtypeStruct((b, s, d), F32),
        compiler_params=_cparams(2),
        name="mem_attn",
    )(h.reshape(b, s, d), wq.astype(BF16), k.reshape(b, m, d), v.reshape(b, m, d),
      wo.astype(BF16), g.reshape(1, d), beta.reshape(1, d))
    return out.reshape(n, d)


def _router_kernel(h_ref, rw_ref, bias_ref, tri_ref, e_ref, g_ref, rank_ref, cnt_ref, run_ref):
    tm = h_ref.shape[0]

    @pl.when(pl.program_id(0) == 0)
    def _():
        run_ref[...] = jnp.zeros(run_ref.shape, F32)

    logits = jnp.dot(h_ref[...].astype(BF16), rw_ref[...], preferred_element_type=F32)
    scores = 1.0 / (1.0 + jnp.exp(-logits))
    lane = lax.broadcasted_iota(I32, (tm, LANES), 1)
    lanef = lane.astype(F32)
    ninf = jnp.float32(-jnp.inf)
    cur = jnp.where(lane < N_EXPERTS, scores + bias_ref[...], ninf)
    top_e = jnp.zeros((tm, LANES), F32)
    top_s = jnp.zeros((tm, LANES), F32)
    picked = jnp.zeros((tm, LANES), F32)
    firsts = []
    for k in range(TOP_K):
        mx = jnp.max(cur, axis=1, keepdims=True)
        first = jnp.min(jnp.where(cur == mx, lanef, float(LANES)), axis=1, keepdims=True)
        hit = lanef == first
        s_k = jnp.sum(jnp.where(hit, scores, 0.0), axis=1, keepdims=True)
        top_e = jnp.where(lane == k, first, top_e)
        top_s = jnp.where(lane == k, s_k, top_s)
        picked = jnp.where(hit, 1.0, picked)
        cur = jnp.where(hit, ninf, cur)
        firsts.append(first)
    e_ref[...] = top_e.astype(I32)
    g_ref[...] = top_s / jnp.sum(top_s, axis=1, keepdims=True) * ROUTED_SCALE

    slot = jnp.dot(tri_ref[...], picked.astype(BF16), preferred_element_type=F32) + run_ref[...]
    rank = jnp.zeros((tm, LANES), F32)
    for k in range(TOP_K):
        r_k = jnp.sum(jnp.where(lanef == firsts[k], slot, 0.0), axis=1, keepdims=True)
        rank = jnp.where(lane == k, r_k, rank)
    rank_ref[...] = rank.astype(I32)
    run = run_ref[...] + jnp.sum(picked, axis=0, keepdims=True)
    run_ref[...] = run
    cnt_ref[...] = run.astype(I32)


def _router_call(h, router_w, router_bias, tm):
    n, d = h.shape
    rw = jnp.pad(router_w, ((0, 0), (0, LANES - N_EXPERTS))).astype(BF16)
    bias = jnp.pad(router_bias, (0, LANES - N_EXPERTS)).reshape(1, LANES)
    tri = jnp.tri(tm, k=-1, dtype=BF16)
    row = pl.BlockSpec((tm, LANES), lambda i: (i, 0))
    const = lambda shp: pl.BlockSpec(shp, lambda i: (0, 0))
    return pl.pallas_call(
        _router_kernel,
        grid=(n // tm,),
        in_specs=[pl.BlockSpec((tm, d), lambda i: (i, 0)), const((d, LANES)), const((1, LANES)),
                  const((tm, tm))],
        out_specs=[row, row, row, const((1, LANES))],
        out_shape=[jax.ShapeDtypeStruct((n, LANES), I32),
                   jax.ShapeDtypeStruct((n, LANES), F32),
                   jax.ShapeDtypeStruct((n, LANES), I32),
                   jax.ShapeDtypeStruct((1, LANES), I32)],
        scratch_shapes=[pltpu.VMEM((1, LANES), F32)],
        compiler_params=_cparams(1),
        name="router",
    )(h, rw, bias, tri)


def _dispatch_kernel(dest_ref, x_ref, xs_ref, sem, *, td):
    def row_copy(g, rr, k):
        r0 = pl.multiple_of(g * SUBLANES, SUBLANES)
        return pltpu.make_async_copy(
            x_ref.at[pl.ds(r0 + rr, 1), :],
            xs_ref.at[pl.ds(dest_ref[g * (SUBLANES * TOP_K) + rr * TOP_K + k], 1), :], sem)

    def issue(g, carry):
        for rr in range(SUBLANES):
            for k in range(TOP_K):
                row_copy(g, rr, k).start(priority=k % N_DMA_QUEUES)
        return carry

    lax.fori_loop(0, td // SUBLANES, issue, 0)

    def drain(g, carry):
        for rr in range(SUBLANES):
            for k in range(TOP_K):
                row_copy(g, rr, k).wait()
        return carry

    lax.fori_loop(0, td // SUBLANES, drain, 0)


def _dispatch_call(h, dest, td):
    n, d = h.shape
    m = n * TOP_K
    return pl.pallas_call(
        functools.partial(_dispatch_kernel, td=td),
        grid=(n // td,),
        in_specs=[pl.BlockSpec((td * TOP_K,), lambda i: (i,), memory_space=pltpu.SMEM),
                  pl.BlockSpec((td, d), lambda i: (i, 0))],
        out_specs=pl.BlockSpec(memory_space=pl.ANY),
        out_shape=jax.ShapeDtypeStruct((m, d), F32),
        scratch_shapes=[pltpu.SemaphoreType.DMA(())],
        compiler_params=_cparams(1),
        name="dispatch",
    )(dest, h)


def _moe_mm_kernel(ptile_ref, pexp_ref, plo_ref, phi_ref, xs_ref, wg_ref, wu_ref, wd_ref, y_ref,
                   *, tmm):
    p = pl.program_id(0)
    tile = ptile_ref[p]
    lo = plo_ref[p]
    hi = phi_ref[p]
    first = jnp.logical_or(p == 0, ptile_ref[jnp.maximum(p - 1, 0)] != tile)

    def expert_out():
        x = xs_ref[...].astype(BF16)
        g = jnp.dot(x, wg_ref[...], preferred_element_type=F32)
        u = jnp.dot(x, wu_ref[...], preferred_element_type=F32)
        hid = (_silu(g) * u).astype(BF16)
        return jnp.dot(hid, wd_ref[...], preferred_element_type=F32)

    whole = jnp.logical_and(lo <= tile * tmm, hi >= (tile + 1) * tmm)

    @pl.when(whole)
    def _():
        y_ref[...] = expert_out()

    @pl.when(jnp.logical_and(first, jnp.logical_not(whole)))
    def _():
        y_ref[...] = jnp.zeros(y_ref.shape, F32)

    @pl.when(jnp.logical_and(lo < hi, jnp.logical_not(whole)))
    def _():
        y = expert_out()
        rowg = tile * tmm + lax.broadcasted_iota(I32, (tmm, 1), 0)
        y_ref[...] += jnp.where(jnp.logical_and(rowg >= lo, rowg < hi), y, 0.0)


def _moe_mm_call(xs, pairs, wg, wu, wd, tmm):
    m = xs.shape[0]
    n_pairs = pairs[0].shape[0]
    rows = pl.BlockSpec((tmm, D_MODEL), lambda p, pt, pe, plo, phi: (pt[p], 0))
    wspec = lambda shp: pl.BlockSpec((None,) + shp, lambda p, pt, pe, plo, phi: (pe[p], 0, 0))
    return pl.pallas_call(
        functools.partial(_moe_mm_kernel, tmm=tmm),
        grid_spec=pltpu.PrefetchScalarGridSpec(
            num_scalar_prefetch=4,
            grid=(n_pairs,),
            in_specs=[rows, wspec((D_MODEL, EXPERT_DIM)), wspec((D_MODEL, EXPERT_DIM)),
                      wspec((EXPERT_DIM, D_MODEL))],
            out_specs=rows),
        out_shape=jax.ShapeDtypeStruct((m, D_MODEL), F32),
        compiler_params=_cparams(1),
        name="moe_mm",
    )(*pairs, xs, wg, wu, wd)


def _group_pairs(counts, m, tmm):
    n_tiles = m // tmm
    n_pairs = n_tiles + N_EXPERTS
    gend = jnp.cumsum(counts)
    gstart = gend - counts
    t0 = jnp.arange(n_tiles, dtype=I32) * tmm
    n_le = lambda edges, v: jnp.sum(edges[None, :] <= v[:, None], axis=1).astype(I32)
    e_first = n_le(gend, t0)
    e_last = n_le(gend, t0 + (tmm - 1))
    per_tile = e_last - e_first + 1
    pend = jnp.cumsum(per_tile)
    pstart = pend - per_tile
    p = jnp.arange(n_pairs, dtype=I32)
    valid = p < pend[-1]
    tile = jnp.minimum(n_le(pend, p), n_tiles - 1)
    e = jnp.clip(e_first[tile] + (p - pstart[tile]), 0, N_EXPERTS - 1).astype(I32)
    lo = jnp.where(valid, gstart[e], 0).astype(I32)
    hi = jnp.where(valid, gend[e], 0).astype(I32)
    return tile, e, lo, hi


def _combine_kernel(dest_ref, gate_ref, h_ref, y_ref, sg_ref, su_ref, sd_ref, g_ref, beta_ref, o_ref,
                    buf_ref, sem, *, tc):
    def row_copy(g, rr, k):
        r0 = pl.multiple_of(g * SUBLANES, SUBLANES)
        return pltpu.make_async_copy(
            y_ref.at[pl.ds(dest_ref[g * (SUBLANES * TOP_K) + rr * TOP_K + k], 1), :],
            buf_ref.at[k, pl.ds(r0 + rr, 1), :], sem)

    def issue(g, carry):
        for rr in range(SUBLANES):
            for k in range(TOP_K):
                row_copy(g, rr, k).start(priority=k % N_DMA_QUEUES)
        return carry

    lax.fori_loop(0, tc // SUBLANES, issue, 0)

    h = h_ref[...]
    x = h.astype(BF16)
    hid = (_silu(jnp.dot(x, sg_ref[...], preferred_element_type=F32))
           * jnp.dot(x, su_ref[...], preferred_element_type=F32)).astype(BF16)
    shared = jnp.dot(hid, sd_ref[...], preferred_element_type=F32)

    def drain(g, carry):
        for rr in range(SUBLANES):
            for k in range(TOP_K):
                row_copy(g, rr, k).wait()
        return carry

    lax.fori_loop(0, tc // SUBLANES, drain, 0)

    gates = gate_ref[...]
    routed = gates[:, 0:1] * buf_ref[0]
    for k in range(1, TOP_K):
        routed = routed + gates[:, k:k + 1] * buf_ref[k]
    o_ref[...] = _layer_norm(ALPHA * h + (routed + shared), g_ref[...], beta_ref[...])


def _combine_call(h, y, dest, gates, sg, su, sd, g, beta, tc):
    n, d = h.shape
    smem = pl.BlockSpec((tc * TOP_K,), lambda i: (i,), memory_space=pltpu.SMEM)
    const = lambda shp: pl.BlockSpec(shp, lambda i: (0, 0))
    return pl.pallas_call(
        functools.partial(_combine_kernel, tc=tc),
        grid=(n // tc,),
        in_specs=[smem, pl.BlockSpec((tc, LANES), lambda i: (i, 0)),
                  pl.BlockSpec((tc, d), lambda i: (i, 0)),
                  pl.BlockSpec(memory_space=pl.ANY),
                  const((d, SHARED_DIM)), const((d, SHARED_DIM)), const((SHARED_DIM, d)),
                  const((1, d)), const((1, d))],
        out_specs=pl.BlockSpec((tc, d), lambda i: (i, 0)),
        out_shape=jax.ShapeDtypeStruct((n, d), F32),
        scratch_shapes=[pltpu.VMEM((TOP_K, tc, d), F32),
                        pltpu.SemaphoreType.DMA(())],
        compiler_params=_cparams(1),
        name="combine",
    )(dest, gates, h, y, sg.astype(BF16), su.astype(BF16), sd.astype(BF16),
      g.reshape(1, d), beta.reshape(1, d))


def _tiles(b, s):
    n = b * s
    pick = lambda pref, total: next(t for t in (pref, 512, 256, 128, 64, 32, 16, 8) if t <= pref and total % t == 0)
    return dict(
        rows=pick(512, n),
        attn_q=pick(512, s),
        attn_k=pick(256, s),
        mem_rows=pick(512, s),
        mem_kv=pick(512, b * 256),
        router=pick(1024, n),
        dispatch=pick(256, n),
        moe=pick(512, n * TOP_K),
        combine=pick(256, n),
    )


def kernel(x, mem, ln_in_g, ln_in_b, w_in, a_kv_norm, a_w_uk, a_w_uv, b_lq1, b_lk1, b_lq2, b_lk2,
           b_subln, w_o, ln1_g, ln1_b, m_wq, m_wkv, m_wo, ln2_g, ln2_b, router_w, router_bias,
           e_w_gate, e_w_up, e_w_down, s_w_gate, s_w_up, s_w_down, ln3_g, ln3_b):
    b, s, d = x.shape
    n = b * s
    m = n * TOP_K
    t = _tiles(b, s)
    mem2d = mem.reshape(-1, d)

    h = _ln_call(x.reshape(n, d), ln_in_g, ln_in_b, t["rows"])
    for l in range(DEPTH):
        lambda_init = 0.8 - 0.6 * math.exp(-0.3 * l)

        qa, ckv, ckvt, qidx, kk, widx, qb, kb, vbt = _proj_call(h, _pad_w_in(w_in[l]), a_kv_norm[l],
                                                                  t["rows"], s)
        out_a = _dsa_call(qa, qidx, widx, kk, ckv, ckvt, _block_diag_uk(a_w_uk[l]),
                          _block_diag_uv_t(a_w_uv[l]), b, s, t["attn_q"], t["attn_k"])
        lam_rows = jnp.pad(jnp.stack([b_lq1[l], b_lk1[l], b_lq2[l], b_lk2[l]]),
                           ((0, SUBLANES - 4), (0, LANES - B_QK_DIM)))
        out_b = _diff_call(qb, kb, vbt, lam_rows, b_subln[l], b, s, t["attn_q"], t["attn_k"], lambda_init)
        h = _mix_out_call(h, out_a, out_b, w_o[l], ln1_g[l], ln1_b[l], t["rows"])

        mk, mv = _mem_kv_call(mem2d, m_wkv[l], t["mem_kv"])
        h = _mem_attn_call(h, m_wq[l], mk, mv, m_wo[l], ln2_g[l], ln2_b[l], b, s, t["mem_rows"])

        top_e, gates, rank, counts = _router_call(h, router_w[l], router_bias[l], t["router"])
        counts = counts[0, :N_EXPERTS]
        gstart = jnp.cumsum(counts) - counts
        top_e = top_e[:, :TOP_K]
        dest = (rank[:, :TOP_K] + gstart[top_e]).reshape(m)
        xs = _dispatch_call(h, dest, t["dispatch"])
        pairs = _group_pairs(counts, m, t["moe"])
        y = _moe_mm_call(xs, pairs, e_w_gate[l].astype(BF16), e_w_up[l].astype(BF16),
                         e_w_down[l].astype(BF16), t["moe"])
        h = _combine_call(h, y, dest, gates, s_w_gate[l], s_w_up[l], s_w_down[l],
                          ln3_g[l], ln3_b[l], t["combine"])
    return h.reshape(b, s, d)
```

```python
import functools
import math

import jax
import jax.numpy as jnp
import numpy as np
from jax import lax
from jax.experimental import pallas as pl
from jax.experimental.pallas import tpu as pltpu

F32 = jnp.float32
BF16 = jnp.bfloat16
I32 = jnp.int32
I16 = jnp.int16

D_MODEL = 1024
DEPTH = 2
CHUNK = 64
CHUNK_SHIFT = 6
A_HEADS = 8
A_HEAD_DIM = 64
A_WIDTH = A_HEADS * A_HEAD_DIM
KV_LORA = 128
IDX_HEADS = 8
IDX_DIM = 64
INDEX_TOPK_MAX = 256
B_HEADS = 4
B_QK_DIM = 64
B_V_DIM = 128
B_WIDTH = B_HEADS * B_V_DIM
MEM_HEADS = 4
MEM_HEAD_DIM = D_MODEL // MEM_HEADS
N_EXPERTS = 64
TOP_K = 8
EXPERT_DIM = 256
SHARED_DIM = 256
ROUTED_SCALE = 2.5
ALPHA = (2 * DEPTH) ** 0.25
LN_EPS = 1e-5
RMS_EPS = 1e-6

LANES = 128
SUBLANES = 8
VMEM_LIMIT = 56 * 1024 * 1024
INT_MIN = -2 ** 31
I16_MIN = -2 ** 15
V_PAD = 16
N_DMA_QUEUES = 2
NEG_BIG = -3.0e38
M_INIT = -1.0e30

NT_DIMS = (((1,), (1,)), ((), ()))


def _cparams(n_axes):
    return pltpu.CompilerParams(dimension_semantics=("arbitrary",) * n_axes,
                                vmem_limit_bytes=VMEM_LIMIT)


def _layer_norm(x, g, b):
    mu = jnp.mean(x, axis=-1, keepdims=True)
    xc = x - mu
    var = jnp.mean(xc * xc, axis=-1, keepdims=True)
    return xc * lax.rsqrt(var + LN_EPS) * g + b


def _silu(x):
    return x * (1.0 / (1.0 + jnp.exp(-x)))


def _ln_kernel(x_ref, g_ref, b_ref, o_ref):
    o_ref[...] = _layer_norm(x_ref[...], g_ref[...], b_ref[...])


def _ln_call(x, g, b, tm):
    n, d = x.shape
    return pl.pallas_call(
        _ln_kernel,
        grid=(n // tm,),
        in_specs=[pl.BlockSpec((tm, d), lambda i: (i, 0)),
                  pl.BlockSpec((1, d), lambda i: (0, 0)),
                  pl.BlockSpec((1, d), lambda i: (0, 0))],
        out_specs=pl.BlockSpec((tm, d), lambda i: (i, 0)),
        out_shape=jax.ShapeDtypeStruct((n, d), F32),
        compiler_params=_cparams(1),
        name="ln_in",
    )(x, g.reshape(1, d), b.reshape(1, d))


_SEG_QA = (0, 512)
_SEG_CKV = (512, 640)
_SEG_QIDX = (640, 1152)
_SEG_KK = (1152, 1280)
_SEG_WIDX = (1280, 1408)
_SEG_QB = (1408, 1920)
_SEG_KB = (1920, 2432)
_SEG_VB = (2432, 2944)
_PROJ_COLS = 2944


def _pos_features(pos, shape):
    lane = lax.broadcasted_iota(I32, shape, 1)
    hi = lax.shift_right_logical(pos, CHUNK_SHIFT).astype(F32)
    lo = (pos & (CHUNK - 1)).astype(F32)
    return jnp.where(lane < 2, 1.0, jnp.where(lane == 2, hi, jnp.where(lane == 3, lo, 0.0)))


def _slope_features(tqi, slope, shape):
    lane = lax.broadcasted_iota(I32, shape, 1)
    hi = lax.shift_right_logical(tqi, CHUNK_SHIFT).astype(F32)
    lo = (tqi & (CHUNK - 1)).astype(F32)
    return jnp.where(lane == 0, -slope * CHUNK * hi,
                     jnp.where(lane == 1, -slope * lo,
                               jnp.where(lane == 2, slope * CHUNK,
                                         jnp.where(lane == 3, slope, 0.0))))


def _with_ones_rows(vt):
    t = vt.shape[1]
    sub = lax.broadcasted_iota(I32, (V_PAD, t), 0)
    return jnp.concatenate([vt, jnp.where(sub == 0, 1.0, 0.0)], axis=0)


def _proj_kernel(h_ref, w_ref, kvn_ref, qa_ref, ckv_ref, ckvt_ref, qidx_ref, kk_ref, widx_ref,
                 qb_ref, kb_ref, vbt_ref, *, seq):
    tm = h_ref.shape[0]
    x = h_ref[...].astype(BF16)

    def seg(s):
        return jnp.dot(x, w_ref[:, s[0]:s[1]], preferred_element_type=F32)

    row = pl.program_id(0) * tm + lax.broadcasted_iota(I32, (tm, LANES), 0)
    feat = _pos_features(row & (seq - 1), (tm, LANES)).astype(BF16)

    qa_ref[...] = (seg(_SEG_QA) * (A_HEAD_DIM ** -0.5)).astype(BF16)
    c = seg(_SEG_CKV)
    c = c * lax.rsqrt(jnp.mean(c * c, axis=-1, keepdims=True) + RMS_EPS) * kvn_ref[...]
    ckv_ref[:, :KV_LORA] = c.astype(BF16)
    ckv_ref[:, KV_LORA:] = feat
    ckvt_ref[...] = _with_ones_rows(c.T).astype(BF16)
    qidx_ref[...] = (seg(_SEG_QIDX) * (IDX_DIM ** -0.5)).astype(BF16)
    kk_ref[...] = seg(_SEG_KK).astype(BF16)
    widx_ref[...] = seg(_SEG_WIDX) * (IDX_HEADS ** -0.5)
    qb_ref[...] = (seg(_SEG_QB) * (B_QK_DIM ** -0.5)).astype(BF16)
    kb = seg(_SEG_KB).astype(BF16)
    vt = seg(_SEG_VB).T
    for h in range(B_HEADS):
        kb_ref[:, 2 * h * LANES:(2 * h + 1) * LANES] = kb[:, h * LANES:(h + 1) * LANES]
        kb_ref[:, (2 * h + 1) * LANES:(2 * h + 2) * LANES] = feat
        vbt_ref[h * (B_V_DIM + V_PAD):(h + 1) * (B_V_DIM + V_PAD), :] = _with_ones_rows(
            vt[h * B_V_DIM:(h + 1) * B_V_DIM]).astype(BF16)


def _proj_call(h, w_pad, kvn, tm, seq):
    n, d = h.shape
    assert seq & (seq - 1) == 0 and seq <= CHUNK * 256 and seq % tm == 0
    outs = [(512, BF16, False), (2 * KV_LORA, BF16, False), (KV_LORA + V_PAD, BF16, True),
            (512, BF16, False), (128, BF16, False), (128, F32, False), (512, BF16, False),
            (2 * B_HEADS * LANES, BF16, False), (B_HEADS * (B_V_DIM + V_PAD), BF16, True)]
    spec = lambda w, t: (pl.BlockSpec((w, tm), lambda i: (0, i)) if t
                         else pl.BlockSpec((tm, w), lambda i: (i, 0)))
    shape = lambda w, dt, t: jax.ShapeDtypeStruct((w, n) if t else (n, w), dt)
    return pl.pallas_call(
        functools.partial(_proj_kernel, seq=seq),
        grid=(n // tm,),
        in_specs=[pl.BlockSpec((tm, d), lambda i: (i, 0)),
                  pl.BlockSpec((d, _PROJ_COLS), lambda i: (0, 0)),
                  pl.BlockSpec((1, KV_LORA), lambda i: (0, 0))],
        out_specs=[spec(w, t) for w, _, t in outs],
        out_shape=[shape(w, dt, t) for w, dt, t in outs],
        compiler_params=_cparams(1),
        name="proj_in",
    )(h, w_pad, kvn.reshape(1, KV_LORA))


def _pad_w_in(w_in):
    sizes = (A_WIDTH, KV_LORA, IDX_HEADS * IDX_DIM, IDX_DIM, IDX_HEADS,
             2 * B_HEADS * B_QK_DIM, 2 * B_HEADS * B_QK_DIM, B_WIDTH)
    splits = np.cumsum(sizes)[:-1].tolist()
    q_a, c_kv, q_idx, k_idx, w_idx, q_b, k_b, v_b = jnp.split(w_in, splits, axis=-1)
    w_idx = jnp.pad(w_idx, ((0, 0), (0, LANES - IDX_HEADS)))
    return jnp.concatenate([q_a, c_kv, q_idx, k_idx, k_idx, w_idx, q_b, k_b, v_b],
                           axis=-1).astype(BF16)


def _flash_probs(sl, sc, m_ref):
    m_old = m_ref[:, sl]
    m_new = jnp.maximum(m_old, jnp.max(sc, axis=0, keepdims=True))
    m_ref[:, sl] = m_new
    return jnp.exp(m_old - m_new), jnp.exp(sc - m_new).astype(BF16)


def _flash_init(m_ref, acc_ref):
    m_ref[...] = jnp.full(m_ref.shape, M_INIT, F32)
    acc_ref[...] = jnp.zeros(acc_ref.shape, F32)


def _split_halves(q_ref, n_pairs, out_ref, rows):
    lane = lax.broadcasted_iota(I32, (rows, LANES), 1)
    for j in range(n_pairs):
        qp = q_ref[:, j * LANES:(j + 1) * LANES].astype(F32)
        out_ref[2 * j * rows:(2 * j + 1) * rows, :LANES] = jnp.where(lane < 64, qp, 0.0).astype(BF16)
        out_ref[(2 * j + 1) * rows:(2 * j + 2) * rows, :LANES] = jnp.where(lane >= 64, qp, 0.0).astype(BF16)


def _dsa_kernel(qa_ref, qidx_ref, widx_ref, kk_ref, ckv_ref, ckvt_ref, wk_ref, wvt_ref, out_ref,
                keys_ref, khi_ref, klo_ref, qim_ref, qlat_ref, tj_ref, m_ref, acc_ref,
                *, tq, tk, seq, topk):
    i = pl.program_id(1)
    n_off = i * (tq // tk)
    n_kt = n_off + tq // tk
    idx_bits = int(math.log2(seq))

    _split_halves(qidx_ref, IDX_HEADS // 2, qim_ref, tq)
    qlat = jnp.dot(qa_ref[...], wk_ref[...], preferred_element_type=F32)
    t_col = i * tq + lax.broadcasted_iota(I32, (tq, LANES), 0)
    for h in range(A_HEADS):
        qlat_ref[h * tq:(h + 1) * tq, :KV_LORA] = qlat[:, h * KV_LORA:(h + 1) * KV_LORA].astype(BF16)
        qlat_ref[h * tq:(h + 1) * tq, KV_LORA:] = _slope_features(
            t_col, 2.0 ** -(h + 1), (tq, LANES)).astype(BF16)
    w_t = widx_ref[...].T

    row = lax.broadcasted_iota(I32, (tk, tq), 0)
    tqi = i * tq + lax.broadcasted_iota(I32, (tk, tq), 1)
    qchunk = lax.shift_right_logical(tqi, CHUNK_SHIFT)

    def idx_body(kt, carry):
        off = pl.multiple_of(kt * tk, tk)
        kkt = kk_ref[pl.ds(off, tk), :]
        s_all = lax.dot_general(kkt, qim_ref[...], NT_DIMS, preferred_element_type=F32)
        acc = jnp.zeros((tk, tq), F32)
        for h in range(IDX_HEADS):
            acc = acc + w_t[h:h + 1, :] * jnp.maximum(s_all[:, h * tq:(h + 1) * tq], 0.0)
        bits = lax.bitcast_convert_type(acc + 0.0, I32)
        key = jnp.where(bits < 0, bits ^ jnp.int32(0x7FFFFFFF), bits)
        adm = lax.shift_right_logical(off + row, CHUNK_SHIFT) <= qchunk
        key = jnp.where(adm, key, jnp.int32(INT_MIN))
        keys_ref[pl.ds(off, tk), :] = key
        khi_ref[pl.ds(off, tk), :] = lax.shift_right_arithmetic(key, 16).astype(I16)
        klo_ref[pl.ds(off, tk), :] = ((key & 0xFFFF) + I16_MIN).astype(I16)
        return carry

    lax.fori_loop(0, n_kt, idx_body, 0)

    def count(pred):
        def body(kt, c):
            off = pl.multiple_of(kt * tk, tk)
            ind = jnp.where(pred(keys_ref[pl.ds(off, tk), :], off + row), 1.0, 0.0)
            return c + jnp.sum(ind, axis=0, keepdims=True)
        return lax.fori_loop(0, n_kt, body, jnp.zeros((1, tq), F32))

    def count16(ref, pred):
        pack = 2 * SUBLANES
        def body(kt, c):
            off = pl.multiple_of(kt * tk, tk)
            ind = jnp.where(pred(ref[pl.ds(off, tk), :]), jnp.int16(1), jnp.int16(0))
            parts = [ind[j * pack:(j + 1) * pack] for j in range(tk // pack)]
            while len(parts) > 1:
                parts = [parts[j] + parts[j + 1] for j in range(0, len(parts), 2)]
            return c + parts[0]
        c = lax.fori_loop(0, n_kt, body, jnp.zeros((pack, tq), I16))
        return jnp.sum(c.astype(F32), axis=0, keepdims=True)

    def search16(ref, need):
        def body(it, t):
            cand = t + lax.shift_left(jnp.int32(1), 15 - it)
            cand16 = cand.astype(I16)
            cnt = count16(ref, lambda tile: tile >= cand16)
            return jnp.where(cnt >= need, cand, t)
        return lax.fori_loop(0, 16, body, jnp.full((1, tq), I16_MIN, I32))

    t_hi = search16(khi_ref, topk)
    t_hi16 = t_hi.astype(I16)
    n_above = count16(khi_ref, lambda tile: tile > t_hi16)

    def park_body(kt, carry):
        off = pl.multiple_of(kt * tk, tk)
        klo_ref[pl.ds(off, tk), :] = jnp.where(khi_ref[pl.ds(off, tk), :] == t_hi16,
                                               klo_ref[pl.ds(off, tk), :], jnp.int16(I16_MIN))
        return carry

    lax.fori_loop(0, n_kt, park_body, 0)
    t_lo = search16(klo_ref, topk - n_above)
    thr = t_hi * 65536 + (t_lo - I16_MIN)

    live = thr > INT_MIN
    n_ge = count(lambda tile, _: tile >= thr)
    excess = jnp.logical_and(n_ge > topk, live)
    all_ties = jnp.where(live, seq, -1).astype(I32)
    tj_ref[0] = jnp.broadcast_to(thr, (SUBLANES, tq))
    tj_ref[1] = jnp.broadcast_to(all_ties, (SUBLANES, tq))

    @pl.when(jnp.max(jnp.where(excess, 1.0, 0.0)) > 0.0)
    def _():
        need = topk - count(lambda tile, _: tile > thr)

        def tie_body(it, hi):
            cand = hi + lax.shift_left(jnp.int32(1), idx_bits - 1 - it)
            cnt = count(lambda tile, pos: jnp.logical_and(tile == thr, pos < cand))
            return jnp.where(cnt < need, cand, hi)

        hi = lax.fori_loop(0, idx_bits, tie_body, jnp.zeros((1, tq), I32))
        tj_ref[1] = jnp.broadcast_to(jnp.where(excess, hi, all_ties), (SUBLANES, tq))

    thr_b = tj_ref[0][:1]
    tie_hi = tj_ref[1][:1]

    _flash_init(m_ref, acc_ref)

    def att_tile(kt, late_fix):
        off = pl.multiple_of(kt * tk, tk)
        tile = keys_ref[pl.ds(off, tk), :]
        pos = off + row
        sel = jnp.logical_or(tile > thr_b,
                             jnp.logical_and(tile == thr_b, pos <= tie_hi))
        s_all = lax.dot_general(ckv_ref[pl.ds(off, tk), :], qlat_ref[...], NT_DIMS,
                                preferred_element_type=F32)
        if late_fix:
            late = jnp.maximum(pos - tqi, 0).astype(F32)
        alphas, probs = [], []
        for h in range(A_HEADS):
            sl = slice(h * tq, (h + 1) * tq)
            sc = s_all[:, sl]
            if late_fix:
                sc = sc - (2.0 * 2.0 ** -(h + 1)) * late
            alpha, p = _flash_probs(sl, jnp.where(sel, sc, NEG_BIG), m_ref)
            alphas.append(alpha)
            probs.append(p)
        pv = jnp.dot(ckvt_ref[:, pl.ds(off, tk)], jnp.concatenate(probs, axis=1),
                     preferred_element_type=F32)
        acc_ref[...] = jnp.concatenate(alphas, axis=1) * acc_ref[...] + pv

    def off_body(kt, carry):
        att_tile(kt, False)
        return carry

    def diag_body(kt, carry):
        att_tile(kt, True)
        return carry

    lax.fori_loop(0, n_off, off_body, 0)
    lax.fori_loop(n_off, n_kt, diag_body, 0)

    inv_l = 1.0 / acc_ref[KV_LORA:KV_LORA + 1, :]
    olat = jnp.concatenate([(acc_ref[:KV_LORA, h * tq:(h + 1) * tq]
                             * inv_l[:, h * tq:(h + 1) * tq]).astype(BF16)
                            for h in range(A_HEADS)], axis=0)
    out_t = jnp.dot(wvt_ref[...], olat, preferred_element_type=F32)
    out_ref[...] = out_t.T.astype(BF16)


def _dsa_call(qa, qidx, widx, kk, ckv, ckvt, wk_bd, wvt_bd, b, s, tq, tk):
    topk = min(INDEX_TOPK_MAX, s // 4)
    assert s % tq == 0 and tq % tk == 0 and tk % LANES == 0 and s & (s - 1) == 0
    blk = lambda w: pl.BlockSpec((None, tq, w), lambda bi, i: (bi, i, 0))
    full = lambda w: pl.BlockSpec((None, s, w), lambda bi, i: (bi, 0, 0))
    const = lambda a: pl.BlockSpec(a.shape, lambda bi, i: (0, 0))
    r3 = lambda a: a.reshape(b, s, a.shape[-1])
    out = pl.pallas_call(
        functools.partial(_dsa_kernel, tq=tq, tk=tk, seq=s, topk=topk),
        grid=(b, s // tq),
        in_specs=[blk(A_WIDTH), blk(IDX_HEADS * IDX_DIM), blk(LANES), full(LANES),
                  full(2 * KV_LORA),
                  pl.BlockSpec((KV_LORA + V_PAD, s), lambda bi, i: (0, bi)),
                  const(wk_bd), const(wvt_bd)],
        out_specs=blk(A_WIDTH),
        out_shape=jax.ShapeDtypeStruct((b, s, A_WIDTH), BF16),
        scratch_shapes=[pltpu.VMEM((s, tq), I32),
                        pltpu.VMEM((s, tq), I16),
                        pltpu.VMEM((s, tq), I16),
                        pltpu.VMEM((IDX_HEADS * tq, LANES), BF16),
                        pltpu.VMEM((A_HEADS * tq, 2 * KV_LORA), BF16),
                        pltpu.VMEM((2, SUBLANES, tq), I32),
                        pltpu.VMEM((1, A_HEADS * tq), F32),
                        pltpu.VMEM((KV_LORA + V_PAD, A_HEADS * tq), F32)],
        compiler_params=_cparams(2),
        name="dsa",
    )(r3(qa), r3(qidx), r3(widx), r3(kk), r3(ckv), ckvt, wk_bd, wvt_bd)
    return out.reshape(b * s, A_WIDTH)


def _block_diag_uk(w_uk):
    h, c, d = w_uk.shape
    eye = jnp.eye(h, dtype=w_uk.dtype)
    return jnp.einsum('hcd,hg->hdgc', w_uk, eye).reshape(h * d, h * c).astype(BF16)


def _block_diag_uv_t(w_uv):
    h, c, d = w_uv.shape
    eye = jnp.eye(h, dtype=w_uv.dtype)
    return jnp.einsum('hcd,hg->hdgc', w_uv, eye).reshape(h * d, h * c).astype(BF16)


def _diff_kernel(qb_ref, kb_ref, vbt_ref, lam_ref, subln_ref, out_ref,
                 qm_ref, m_ref, acc_ref, *, tq, tk, lambda_init):
    i = pl.program_id(1)
    n_off = i * (tq // tk)
    dvp = B_V_DIM + V_PAD
    _split_halves(qb_ref, B_HEADS, qm_ref, tq)
    t_col = i * tq + lax.broadcasted_iota(I32, (tq, LANES), 0)
    for h in range(B_HEADS):
        feat = _slope_features(t_col, 2.0 ** (-2 * (h + 1)), (tq, LANES)).astype(BF16)
        qm_ref[2 * h * tq:(2 * h + 1) * tq, LANES:] = feat
        qm_ref[(2 * h + 1) * tq:(2 * h + 2) * tq, LANES:] = feat

    row = lax.broadcasted_iota(I32, (tk, tq), 0)
    tqi = i * tq + lax.broadcasted_iota(I32, (tk, tq), 1)
    qchunk = lax.shift_right_logical(tqi, CHUNK_SHIFT)
    _flash_init(m_ref, acc_ref)

    def tile_step(kt, diag):
        off = pl.multiple_of(kt * tk, tk)
        s_pairs = [lax.dot_general(kb_ref[pl.ds(off, tk), 2 * h * LANES:(2 * h + 2) * LANES],
                                   qm_ref[2 * h * tq:(2 * h + 2) * tq, :], NT_DIMS,
                                   preferred_element_type=F32) for h in range(B_HEADS)]
        if diag:
            pos = off + row
            adm = lax.shift_right_logical(pos, CHUNK_SHIFT) <= qchunk
            late = jnp.maximum(pos - tqi, 0).astype(F32)
        alphas, probs = [], []
        for h in range(B_HEADS):
            for j in range(2):
                sl = slice((2 * h + j) * tq, (2 * h + j + 1) * tq)
                sc = s_pairs[h][:, j * tq:(j + 1) * tq]
                if diag:
                    sc = jnp.where(adm, sc - (2.0 * 2.0 ** (-2 * (h + 1))) * late, NEG_BIG)
                alpha, p = _flash_probs(sl, sc, m_ref)
                alphas.append(alpha)
                probs.append(p)
        pv = jnp.concatenate(
            [jnp.dot(vbt_ref[h * dvp:(h + 1) * dvp, pl.ds(off, tk)],
                     jnp.concatenate(probs[2 * h:2 * h + 2], axis=1), preferred_element_type=F32)
             for h in range(B_HEADS)], axis=1)
        acc_ref[...] = jnp.concatenate(alphas, axis=1) * acc_ref[...] + pv

    def off_body(kt, carry):
        tile_step(kt, False)
        return carry

    def diag_body(kt, carry):
        tile_step(kt, True)
        return carry

    lax.fori_loop(0, n_off, off_body, 0)
    lax.fori_loop(n_off, n_off + tq // tk, diag_body, 0)

    lp = lam_ref[...]
    lam = (jnp.exp(jnp.sum(lp[0:1] * lp[1:2], axis=1, keepdims=True))
           - jnp.exp(jnp.sum(lp[2:3] * lp[3:4], axis=1, keepdims=True)) + lambda_init)
    o_all = acc_ref[:B_V_DIM, :] * (1.0 / acc_ref[B_V_DIM:B_V_DIM + 1, :])
    outs = []
    for h in range(B_HEADS):
        o = (o_all[:, 2 * h * tq:(2 * h + 1) * tq]
             - lam * o_all[:, (2 * h + 1) * tq:(2 * h + 2) * tq])
        o = o * lax.rsqrt(jnp.mean(o * o, axis=0, keepdims=True) + RMS_EPS) * subln_ref[...]
        outs.append(o * (1.0 - lambda_init))
    out_ref[...] = jnp.concatenate(outs, axis=0).T.astype(BF16)


def _diff_call(qb, kb, vbt, lam_rows, subln, b, s, tq, tk, lambda_init):
    blk = pl.BlockSpec((None, tq, B_WIDTH), lambda bi, i: (bi, i, 0))
    kw = 2 * B_HEADS * LANES
    dvp = B_V_DIM + V_PAD
    out = pl.pallas_call(
        functools.partial(_diff_kernel, tq=tq, tk=tk, lambda_init=lambda_init),
        grid=(b, s // tq),
        in_specs=[blk, pl.BlockSpec((None, s, kw), lambda bi, i: (bi, 0, 0)),
                  pl.BlockSpec((B_HEADS * dvp, s), lambda bi, i: (0, bi)),
                  pl.BlockSpec((SUBLANES, LANES), lambda bi, i: (0, 0)),
                  pl.BlockSpec((B_V_DIM, 1), lambda bi, i: (0, 0))],
        out_specs=blk,
        out_shape=jax.ShapeDtypeStruct((b, s, B_WIDTH), BF16),
        scratch_shapes=[pltpu.VMEM((2 * B_HEADS * tq, 2 * LANES), BF16),
                        pltpu.VMEM((1, 2 * B_HEADS * tq), F32),
                        pltpu.VMEM((dvp, 2 * B_HEADS * tq), F32)],
        compiler_params=_cparams(2),
        name="diff",
    )(qb.reshape(b, s, B_WIDTH), kb.reshape(b, s, kw), vbt, lam_rows, subln.reshape(B_V_DIM, 1))
    return out.reshape(b * s, B_WIDTH)


def _mix_out_kernel(h_ref, a_ref, b_ref, wa_ref, wb_ref, g_ref, beta_ref, o_ref):
    y = (jnp.dot(a_ref[...], wa_ref[...], preferred_element_type=F32)
         + jnp.dot(b_ref[...], wb_ref[...], preferred_element_type=F32))
    o_ref[...] = _layer_norm(ALPHA * h_ref[...] + y, g_ref[...], beta_ref[...])


def _mix_out_call(h, out_a, out_b, w_o, g, beta, tm):
    n, d = h.shape
    w = w_o.astype(BF16)
    row = lambda wd: pl.BlockSpec((tm, wd), lambda i: (i, 0))
    const = lambda shp: pl.BlockSpec(shp, lambda i: (0, 0))
    return pl.pallas_call(
        _mix_out_kernel,
        grid=(n // tm,),
        in_specs=[row(d), row(A_WIDTH), row(B_WIDTH), const((A_WIDTH, d)), const((B_WIDTH, d)),
                  const((1, d)), const((1, d))],
        out_specs=row(d),
        out_shape=jax.ShapeDtypeStruct((n, d), F32),
        compiler_params=_cparams(1),
        name="mix_out",
    )(h, out_a, out_b, w[:A_WIDTH], w[A_WIDTH:], g.reshape(1, d), beta.reshape(1, d))


def _mem_kv_kernel(m_ref, w_ref, k_ref, v_ref):
    x = m_ref[...].astype(BF16)
    d = k_ref.shape[-1]
    k_ref[...] = jnp.dot(x, w_ref[:, :d], preferred_element_type=F32).astype(BF16)
    v_ref[...] = jnp.dot(x, w_ref[:, d:], preferred_element_type=F32).astype(BF16)


def _mem_kv_call(mem2d, wkv, tm):
    n, d = mem2d.shape
    return pl.pallas_call(
        _mem_kv_kernel,
        grid=(n // tm,),
        in_specs=[pl.BlockSpec((tm, d), lambda i: (i, 0)),
                  pl.BlockSpec((d, 2 * d), lambda i: (0, 0))],
        out_specs=[pl.BlockSpec((tm, d), lambda i: (i, 0))] * 2,
        out_shape=[jax.ShapeDtypeStruct((n, d), BF16)] * 2,
        compiler_params=_cparams(1),
        name="mem_kv",
    )(mem2d, wkv.astype(BF16))


def _mem_attn_kernel(h_ref, wq_ref, k_ref, v_ref, wo_ref, g_ref, beta_ref, o_ref):
    h = h_ref[...]
    q = jnp.dot(h.astype(BF16), wq_ref[...], preferred_element_type=F32)
    q = (q * (MEM_HEAD_DIM ** -0.5)).astype(BF16)
    outs = []
    for hd in range(MEM_HEADS):
        sl = slice(hd * MEM_HEAD_DIM, (hd + 1) * MEM_HEAD_DIM)
        sc = lax.dot_general(q[:, sl], k_ref[:, sl], NT_DIMS, preferred_element_type=F32)
        p = jnp.exp(sc - jnp.max(sc, axis=1, keepdims=True))
        den = jnp.sum(p, axis=1, keepdims=True)
        o = jnp.dot(p.astype(BF16), v_ref[:, sl], preferred_element_type=F32) / den
        outs.append(o.astype(BF16))
    y = jnp.dot(jnp.concatenate(outs, axis=1), wo_ref[...], preferred_element_type=F32)
    o_ref[...] = _layer_norm(ALPHA * h + y, g_ref[...], beta_ref[...])


def _mem_attn_call(h, wq, k, v, wo, g, beta, b, s, tm):
    n, d = h.shape
    m = k.shape[0] // b
    const = lambda shp: pl.BlockSpec(shp, lambda bi, i: (0,) * len(shp))
    row = pl.BlockSpec((None, tm, d), lambda bi, i: (bi, i, 0))
    kv = pl.BlockSpec((None, m, d), lambda bi, i: (bi, 0, 0))
    out = pl.pallas_call(
        _mem_attn_kernel,
        grid=(b, s // tm),
        in_specs=[row, const((d, d)), kv, kv, const((d, d)), const((1, d)), const((1, d))],
        out_specs=row,
        out_shape=jax.ShapeDtypeStruct((b, s, d), F32),
        compiler_params=_cparams(2),
        name="mem_attn",
    )(h.reshape(b, s, d), wq.astype(BF16), k.reshape(b, m, d), v.reshape(b, m, d),
      wo.astype(BF16), g.reshape(1, d), beta.reshape(1, d))
    return out.reshape(n, d)


def _router_kernel(h_ref, rw_ref, bias_ref, tri_ref, e_ref, g_ref, rank_ref, cnt_ref, run_ref):
    tm = h_ref.shape[0]

    @pl.when(pl.program_id(0) == 0)
    def _():
        run_ref[...] = jnp.zeros(run_ref.shape, F32)

    logits = jnp.dot(h_ref[...].astype(BF16), rw_ref[...], preferred_element_type=F32)
    scores = 1.0 / (1.0 + jnp.exp(-logits))
    lane = lax.broadcasted_iota(I32, (tm, LANES), 1)
    lanef = lane.astype(F32)
    ninf = jnp.float32(-jnp.inf)
    cur = jnp.where(lane < N_EXPERTS, scores + bias_ref[...], ninf)
    top_e = jnp.zeros((tm, LANES), F32)
    top_s = jnp.zeros((tm, LANES), F32)
    picked = jnp.zeros((tm, LANES), F32)
    firsts = []
    for k in range(TOP_K):
        mx = jnp.max(cur, axis=1, keepdims=True)
        first = jnp.min(jnp.where(cur == mx, lanef, float(LANES)), axis=1, keepdims=True)
        hit = lanef == first
        s_k = jnp.sum(jnp.where(hit, scores, 0.0), axis=1, keepdims=True)
        top_e = jnp.where(lane == k, first, top_e)
        top_s = jnp.where(lane == k, s_k, top_s)
        picked = jnp.where(hit, 1.0, picked)
        cur = jnp.where(hit, ninf, cur)
        firsts.append(first)
    e_ref[...] = top_e.astype(I32)
    g_ref[...] = top_s / jnp.sum(top_s, axis=1, keepdims=True) * ROUTED_SCALE

    slot = jnp.dot(tri_ref[...], picked.astype(BF16), preferred_element_type=F32) + run_ref[...]
    rank = jnp.zeros((tm, LANES), F32)
    for k in range(TOP_K):
        r_k = jnp.sum(jnp.where(lanef == firsts[k], slot, 0.0), axis=1, keepdims=True)
        rank = jnp.where(lane == k, r_k, rank)
    rank_ref[...] = rank.astype(I32)
    run = run_ref[...] + jnp.sum(picked, axis=0, keepdims=True)
    run_ref[...] = run
    cnt_ref[...] = run.astype(I32)


def _router_call(h, router_w, router_bias, tm):
    n, d = h.shape
    rw = jnp.pad(router_w, ((0, 0), (0, LANES - N_EXPERTS))).astype(BF16)
    bias = jnp.pad(router_bias, (0, LANES - N_EXPERTS)).reshape(1, LANES)
    tri = jnp.tri(tm, k=-1, dtype=BF16)
    row = pl.BlockSpec((tm, LANES), lambda i: (i, 0))
    const = lambda shp: pl.BlockSpec(shp, lambda i: (0, 0))
    return pl.pallas_call(
        _router_kernel,
        grid=(n // tm,),
        in_specs=[pl.BlockSpec((tm, d), lambda i: (i, 0)), const((d, LANES)), const((1, LANES)),
                  const((tm, tm))],
        out_specs=[row, row, row, const((1, LANES))],
        out_shape=[jax.ShapeDtypeStruct((n, LANES), I32),
                   jax.ShapeDtypeStruct((n, LANES), F32),
                   jax.ShapeDtypeStruct((n, LANES), I32),
                   jax.ShapeDtypeStruct((1, LANES), I32)],
        scratch_shapes=[pltpu.VMEM((1, LANES), F32)],
        compiler_params=_cparams(1),
        name="router",
    )(h, rw, bias, tri)


def _dispatch_kernel(dest_ref, x_ref, xs_ref, sem, *, td):
    def row_copy(r, k):
        return pltpu.make_async_copy(x_ref.at[pl.ds(r, 1), :],
                                     xs_ref.at[pl.ds(dest_ref[r * TOP_K + k], 1), :], sem)

    for r in range(td):
        for k in range(TOP_K):
            row_copy(r, k).start(priority=k % N_DMA_QUEUES)

    def drain(r, carry):
        for k in range(TOP_K):
            row_copy(r, k).wait()
        return carry

    lax.fori_loop(0, td, drain, 0)


def _dispatch_call(h, dest, td):
    n, d = h.shape
    m = n * TOP_K
    return pl.pallas_call(
        functools.partial(_dispatch_kernel, td=td),
        grid=(n // td,),
        in_specs=[pl.BlockSpec((td * TOP_K,), lambda i: (i,), memory_space=pltpu.SMEM),
                  pl.BlockSpec((td, d), lambda i: (i, 0))],
        out_specs=pl.BlockSpec(memory_space=pl.ANY),
        out_shape=jax.ShapeDtypeStruct((m, d), F32),
        scratch_shapes=[pltpu.SemaphoreType.DMA(())],
        compiler_params=_cparams(1),
        name="dispatch",
    )(dest, h)


def _moe_mm_kernel(ptile_ref, pexp_ref, plo_ref, phi_ref, xs_ref, wg_ref, wu_ref, wd_ref, y_ref,
                   *, tmm):
    p = pl.program_id(0)
    tile = ptile_ref[p]
    lo = plo_ref[p]
    hi = phi_ref[p]
    first = jnp.logical_or(p == 0, ptile_ref[jnp.maximum(p - 1, 0)] != tile)

    def expert_out():
        x = xs_ref[...].astype(BF16)
        g = jnp.dot(x, wg_ref[...], preferred_element_type=F32)
        u = jnp.dot(x, wu_ref[...], preferred_element_type=F32)
        hid = (_silu(g) * u).astype(BF16)
        return jnp.dot(hid, wd_ref[...], preferred_element_type=F32)

    whole = jnp.logical_and(lo <= tile * tmm, hi >= (tile + 1) * tmm)

    @pl.when(whole)
    def _():
        y_ref[...] = expert_out()

    @pl.when(jnp.logical_and(first, jnp.logical_not(whole)))
    def _():
        y_ref[...] = jnp.zeros(y_ref.shape, F32)

    @pl.when(jnp.logical_and(lo < hi, jnp.logical_not(whole)))
    def _():
        y = expert_out()
        rowg = tile * tmm + lax.broadcasted_iota(I32, (tmm, 1), 0)
        y_ref[...] += jnp.where(jnp.logical_and(rowg >= lo, rowg < hi), y, 0.0)


def _moe_mm_call(xs, pairs, wg, wu, wd, tmm):
    m = xs.shape[0]
    n_pairs = pairs[0].shape[0]
    rows = pl.BlockSpec((tmm, D_MODEL), lambda p, pt, pe, plo, phi: (pt[p], 0))
    wspec = lambda shp: pl.BlockSpec((None,) + shp, lambda p, pt, pe, plo, phi: (pe[p], 0, 0))
    return pl.pallas_call(
        functools.partial(_moe_mm_kernel, tmm=tmm),
        grid_spec=pltpu.PrefetchScalarGridSpec(
            num_scalar_prefetch=4,
            grid=(n_pairs,),
            in_specs=[rows, wspec((D_MODEL, EXPERT_DIM)), wspec((D_MODEL, EXPERT_DIM)),
                      wspec((EXPERT_DIM, D_MODEL))],
            out_specs=rows),
        out_shape=jax.ShapeDtypeStruct((m, D_MODEL), F32),
        compiler_params=_cparams(1),
        name="moe_mm",
    )(*pairs, xs, wg, wu, wd)


def _group_pairs(counts, m, tmm):
    n_tiles = m // tmm
    n_pairs = n_tiles + N_EXPERTS
    gend = jnp.cumsum(counts)
    gstart = gend - counts
    t0 = jnp.arange(n_tiles, dtype=I32) * tmm
    n_le = lambda edges, v: jnp.sum(edges[None, :] <= v[:, None], axis=1).astype(I32)
    e_first = n_le(gend, t0)
    e_last = n_le(gend, t0 + (tmm - 1))
    per_tile = e_last - e_first + 1
    pend = jnp.cumsum(per_tile)
    pstart = pend - per_tile
    p = jnp.arange(n_pairs, dtype=I32)
    valid = p < pend[-1]
    tile = jnp.minimum(n_le(pend, p), n_tiles - 1)
    e = jnp.clip(e_first[tile] + (p - pstart[tile]), 0, N_EXPERTS - 1).astype(I32)
    lo = jnp.where(valid, gstart[e], 0).astype(I32)
    hi = jnp.where(valid, gend[e], 0).astype(I32)
    return tile, e, lo, hi


def _combine_kernel(dest_ref, gate_ref, h_ref, y_ref, sg_ref, su_ref, sd_ref, g_ref, beta_ref, o_ref,
                    buf_ref, sem, *, tc):
    def row_copy(r, k):
        return pltpu.make_async_copy(y_ref.at[pl.ds(dest_ref[r * TOP_K + k], 1), :],
                                     buf_ref.at[k, pl.ds(r, 1), :], sem)

    for r in range(tc):
        for k in range(TOP_K):
            row_copy(r, k).start(priority=k % N_DMA_QUEUES)

    h = h_ref[...]
    x = h.astype(BF16)
    hid = (_silu(jnp.dot(x, sg_ref[...], preferred_element_type=F32))
           * jnp.dot(x, su_ref[...], preferred_element_type=F32)).astype(BF16)
    shared = jnp.dot(hid, sd_ref[...], preferred_element_type=F32)

    def drain(r, carry):
        for k in range(TOP_K):
            row_copy(r, k).wait()
        return carry

    lax.fori_loop(0, tc, drain, 0)

    gates = gate_ref[...]
    routed = gates[:, 0:1] * buf_ref[0]
    for k in range(1, TOP_K):
        routed = routed + gates[:, k:k + 1] * buf_ref[k]
    o_ref[...] = _layer_norm(ALPHA * h + (routed + shared), g_ref[...], beta_ref[...])


def _combine_call(h, y, dest, gates, sg, su, sd, g, beta, tc):
    n, d = h.shape
    smem = pl.BlockSpec((tc * TOP_K,), lambda i: (i,), memory_space=pltpu.SMEM)
    const = lambda shp: pl.BlockSpec(shp, lambda i: (0, 0))
    return pl.pallas_call(
        functools.partial(_combine_kernel, tc=tc),
        grid=(n // tc,),
        in_specs=[smem, pl.BlockSpec((tc, LANES), lambda i: (i, 0)),
                  pl.BlockSpec((tc, d), lambda i: (i, 0)),
                  pl.BlockSpec(memory_space=pl.ANY),
                  const((d, SHARED_DIM)), const((d, SHARED_DIM)), const((SHARED_DIM, d)),
                  const((1, d)), const((1, d))],
        out_specs=pl.BlockSpec((tc, d), lambda i: (i, 0)),
        out_shape=jax.ShapeDtypeStruct((n, d), F32),
        scratch_shapes=[pltpu.VMEM((TOP_K, tc, d), F32),
                        pltpu.SemaphoreType.DMA(())],
        compiler_params=_cparams(1),
        name="combine",
    )(dest, gates, h, y, sg.astype(BF16), su.astype(BF16), sd.astype(BF16),
      g.reshape(1, d), beta.reshape(1, d))


def _tiles(b, s):
    n = b * s
    pick = lambda pref, total: next(t for t in (pref, 512, 256, 128, 64, 32, 16, 8) if t <= pref and total % t == 0)
    return dict(
        rows=pick(512, n),
        attn_q=pick(512, s),
        attn_k=pick(256, s),
        mem_rows=pick(512, s),
        mem_kv=pick(512, b * 256),
        router=pick(1024, n),
        dispatch=pick(128, n),
        moe=pick(512, n * TOP_K),
        combine=pick(128, n),
    )


def kernel(x, mem, ln_in_g, ln_in_b, w_in, a_kv_norm, a_w_uk, a_w_uv, b_lq1, b_lk1, b_lq2, b_lk2,
           b_subln, w_o, ln1_g, ln1_b, m_wq, m_wkv, m_wo, ln2_g, ln2_b, router_w, router_bias,
           e_w_gate, e_w_up, e_w_down, s_w_gate, s_w_up, s_w_down, ln3_g, ln3_b):
    b, s, d = x.shape
    n = b * s
    m = n * TOP_K
    t = _tiles(b, s)
    mem2d = mem.reshape(-1, d)

    h = _ln_call(x.reshape(n, d), ln_in_g, ln_in_b, t["rows"])
    for l in range(DEPTH):
        lambda_init = 0.8 - 0.6 * math.exp(-0.3 * l)

        qa, ckv, ckvt, qidx, kk, widx, qb, kb, vbt = _proj_call(h, _pad_w_in(w_in[l]), a_kv_norm[l],
                                                                  t["rows"], s)
        out_a = _dsa_call(qa, qidx, widx, kk, ckv, ckvt, _block_diag_uk(a_w_uk[l]),
                          _block_diag_uv_t(a_w_uv[l]), b, s, t["attn_q"], t["attn_k"])
        lam_rows = jnp.pad(jnp.stack([b_lq1[l], b_lk1[l], b_lq2[l], b_lk2[l]]),
                           ((0, SUBLANES - 4), (0, LANES - B_QK_DIM)))
        out_b = _diff_call(qb, kb, vbt, lam_rows, b_subln[l], b, s, t["attn_q"], t["attn_k"], lambda_init)
        h = _mix_out_call(h, out_a, out_b, w_o[l], ln1_g[l], ln1_b[l], t["rows"])

        mk, mv = _mem_kv_call(mem2d, m_wkv[l], t["mem_kv"])
        h = _mem_attn_call(h, m_wq[l], mk, mv, m_wo[l], ln2_g[l], ln2_b[l], b, s, t["mem_rows"])

        top_e, gates, rank, counts = _router_call(h, router_w[l], router_bias[l], t["router"])
        counts = counts[0, :N_EXPERTS]
        gstart = jnp.cumsum(counts) - counts
        top_e = top_e[:, :TOP_K]
        dest = (rank[:, :TOP_K] + gstart[top_e]).reshape(m)
        xs = _dispatch_call(h, dest, t["dispatch"])
        pairs = _group_pairs(counts, m, t["moe"])
        y = _moe_mm_call(xs, pairs, e_w_gate[l].astype(BF16), e_w_up[l].astype(BF16),
                         e_w_down[l].astype(BF16), t["moe"])
        h = _combine_call(h, y, dest, gates, s_w_gate[l], s_w_up[l], s_w_down[l],
                          ln3_g[l], ln3_b[l], t["combine"])
    return h.reshape(b, s, d)
```

```python
import functools
import math

import jax
import jax.numpy as jnp
import numpy as np
from jax import lax
from jax.experimental import pallas as pl
from jax.experimental.pallas import tpu as pltpu

F32 = jnp.float32
BF16 = jnp.bfloat16
I32 = jnp.int32
I16 = jnp.int16

D_MODEL = 1024
DEPTH = 2
CHUNK = 64
CHUNK_SHIFT = 6
A_HEADS = 8
A_HEAD_DIM = 64
A_WIDTH = A_HEADS * A_HEAD_DIM
KV_LORA = 128
IDX_HEADS = 8
IDX_DIM = 64
INDEX_TOPK_MAX = 256
B_HEADS = 4
B_QK_DIM = 64
B_V_DIM = 128
B_WIDTH = B_HEADS * B_V_DIM
MEM_HEADS = 4
MEM_HEAD_DIM = D_MODEL // MEM_HEADS
N_EXPERTS = 64
TOP_K = 8
EXPERT_DIM = 256
SHARED_DIM = 256
ROUTED_SCALE = 2.5
ALPHA = (2 * DEPTH) ** 0.25
LN_EPS = 1e-5
RMS_EPS = 1e-6

LANES = 128
SUBLANES = 8
VMEM_LIMIT = 56 * 1024 * 1024
INT_MIN = -2 ** 31
I16_MIN = -2 ** 15
V_PAD = 16
N_DMA_QUEUES = 2
NEG_BIG = -3.0e38
M_INIT = -1.0e30

NT_DIMS = (((1,), (1,)), ((), ()))


def _cparams(n_axes):
    return pltpu.CompilerParams(dimension_semantics=("arbitrary",) * n_axes,
                                vmem_limit_bytes=VMEM_LIMIT)


def _layer_norm(x, g, b):
    mu = jnp.mean(x, axis=-1, keepdims=True)
    xc = x - mu
    var = jnp.mean(xc * xc, axis=-1, keepdims=True)
    return xc * lax.rsqrt(var + LN_EPS) * g + b


def _silu(x):
    return x * (1.0 / (1.0 + jnp.exp(-x)))


def _ln_kernel(x_ref, g_ref, b_ref, o_ref):
    o_ref[...] = _layer_norm(x_ref[...], g_ref[...], b_ref[...])


def _ln_call(x, g, b, tm):
    n, d = x.shape
    return pl.pallas_call(
        _ln_kernel,
        grid=(n // tm,),
        in_specs=[pl.BlockSpec((tm, d), lambda i: (i, 0)),
                  pl.BlockSpec((1, d), lambda i: (0, 0)),
                  pl.BlockSpec((1, d), lambda i: (0, 0))],
        out_specs=pl.BlockSpec((tm, d), lambda i: (i, 0)),
        out_shape=jax.ShapeDtypeStruct((n, d), F32),
        compiler_params=_cparams(1),
        name="ln_in",
    )(x, g.reshape(1, d), b.reshape(1, d))


_SEG_QA = (0, 512)
_SEG_CKV = (512, 640)
_SEG_QIDX = (640, 1152)
_SEG_KK = (1152, 1280)
_SEG_WIDX = (1280, 1408)
_SEG_QB = (1408, 1920)
_SEG_KB = (1920, 2432)
_SEG_VB = (2432, 2944)
_PROJ_COLS = 2944


def _pos_features(pos, shape):
    lane = lax.broadcasted_iota(I32, shape, 1)
    hi = lax.shift_right_logical(pos, CHUNK_SHIFT).astype(F32)
    lo = (pos & (CHUNK - 1)).astype(F32)
    return jnp.where(lane < 2, 1.0, jnp.where(lane == 2, hi, jnp.where(lane == 3, lo, 0.0)))


def _slope_features(tqi, slope, shape):
    lane = lax.broadcasted_iota(I32, shape, 1)
    hi = lax.shift_right_logical(tqi, CHUNK_SHIFT).astype(F32)
    lo = (tqi & (CHUNK - 1)).astype(F32)
    return jnp.where(lane == 0, -slope * CHUNK * hi,
                     jnp.where(lane == 1, -slope * lo,
                               jnp.where(lane == 2, slope * CHUNK,
                                         jnp.where(lane == 3, slope, 0.0))))


def _with_ones_rows(vt):
    t = vt.shape[1]
    sub = lax.broadcasted_iota(I32, (V_PAD, t), 0)
    return jnp.concatenate([vt, jnp.where(sub == 0, 1.0, 0.0)], axis=0)


def _proj_kernel(h_ref, w_ref, kvn_ref, qa_ref, ckv_ref, ckvt_ref, qidx_ref, kk_ref, widx_ref,
                 qb_ref, kb_ref, vbt_ref, *, seq):
    tm = h_ref.shape[0]
    x = h_ref[...].astype(BF16)

    def seg(s):
        return jnp.dot(x, w_ref[:, s[0]:s[1]], preferred_element_type=F32)

    row = pl.program_id(0) * tm + lax.broadcasted_iota(I32, (tm, LANES), 0)
    feat = _pos_features(row & (seq - 1), (tm, LANES)).astype(BF16)

    qa_ref[...] = (seg(_SEG_QA) * (A_HEAD_DIM ** -0.5)).astype(BF16)
    c = seg(_SEG_CKV)
    c = c * lax.rsqrt(jnp.mean(c * c, axis=-1, keepdims=True) + RMS_EPS) * kvn_ref[...]
    ckv_ref[:, :KV_LORA] = c.astype(BF16)
    ckv_ref[:, KV_LORA:] = feat
    ckvt_ref[...] = _with_ones_rows(c.T).astype(BF16)
    qidx_ref[...] = (seg(_SEG_QIDX) * (IDX_DIM ** -0.5)).astype(BF16)
    kk_ref[...] = seg(_SEG_KK).astype(BF16)
    widx_ref[...] = seg(_SEG_WIDX) * (IDX_HEADS ** -0.5)
    qb_ref[...] = (seg(_SEG_QB) * (B_QK_DIM ** -0.5)).astype(BF16)
    kb = seg(_SEG_KB).astype(BF16)
    vt = seg(_SEG_VB).T
    for h in range(B_HEADS):
        kb_ref[:, 2 * h * LANES:(2 * h + 1) * LANES] = kb[:, h * LANES:(h + 1) * LANES]
        kb_ref[:, (2 * h + 1) * LANES:(2 * h + 2) * LANES] = feat
        vbt_ref[h * (B_V_DIM + V_PAD):(h + 1) * (B_V_DIM + V_PAD), :] = _with_ones_rows(
            vt[h * B_V_DIM:(h + 1) * B_V_DIM]).astype(BF16)


def _proj_call(h, w_pad, kvn, tm, seq):
    n, d = h.shape
    assert seq & (seq - 1) == 0 and seq <= CHUNK * 256 and seq % tm == 0
    outs = [(512, BF16, False), (2 * KV_LORA, BF16, False), (KV_LORA + V_PAD, BF16, True),
            (512, BF16, False), (128, BF16, False), (128, F32, False), (512, BF16, False),
            (2 * B_HEADS * LANES, BF16, False), (B_HEADS * (B_V_DIM + V_PAD), BF16, True)]
    spec = lambda w, t: (pl.BlockSpec((w, tm), lambda i: (0, i)) if t
                         else pl.BlockSpec((tm, w), lambda i: (i, 0)))
    shape = lambda w, dt, t: jax.ShapeDtypeStruct((w, n) if t else (n, w), dt)
    return pl.pallas_call(
        functools.partial(_proj_kernel, seq=seq),
        grid=(n // tm,),
        in_specs=[pl.BlockSpec((tm, d), lambda i: (i, 0)),
                  pl.BlockSpec((d, _PROJ_COLS), lambda i: (0, 0)),
                  pl.BlockSpec((1, KV_LORA), lambda i: (0, 0))],
        out_specs=[spec(w, t) for w, _, t in outs],
        out_shape=[shape(w, dt, t) for w, dt, t in outs],
        compiler_params=_cparams(1),
        name="proj_in",
    )(h, w_pad, kvn.reshape(1, KV_LORA))


def _pad_w_in(w_in):
    sizes = (A_WIDTH, KV_LORA, IDX_HEADS * IDX_DIM, IDX_DIM, IDX_HEADS,
             2 * B_HEADS * B_QK_DIM, 2 * B_HEADS * B_QK_DIM, B_WIDTH)
    splits = np.cumsum(sizes)[:-1].tolist()
    q_a, c_kv, q_idx, k_idx, w_idx, q_b, k_b, v_b = jnp.split(w_in, splits, axis=-1)
    w_idx = jnp.pad(w_idx, ((0, 0), (0, LANES - IDX_HEADS)))
    return jnp.concatenate([q_a, c_kv, q_idx, k_idx, k_idx, w_idx, q_b, k_b, v_b],
                           axis=-1).astype(BF16)


def _flash_probs(sl, sc, m_ref):
    m_old = m_ref[:, sl]
    m_new = jnp.maximum(m_old, jnp.max(sc, axis=0, keepdims=True))
    m_ref[:, sl] = m_new
    return jnp.exp(m_old - m_new), jnp.exp(sc - m_new).astype(BF16)


def _flash_init(m_ref, acc_ref):
    m_ref[...] = jnp.full(m_ref.shape, M_INIT, F32)
    acc_ref[...] = jnp.zeros(acc_ref.shape, F32)


def _split_halves(q_ref, n_pairs, out_ref, rows):
    lane = lax.broadcasted_iota(I32, (rows, LANES), 1)
    for j in range(n_pairs):
        qp = q_ref[:, j * LANES:(j + 1) * LANES].astype(F32)
        out_ref[2 * j * rows:(2 * j + 1) * rows, :LANES] = jnp.where(lane < 64, qp, 0.0).astype(BF16)
        out_ref[(2 * j + 1) * rows:(2 * j + 2) * rows, :LANES] = jnp.where(lane >= 64, qp, 0.0).astype(BF16)


def _dsa_kernel(qa_ref, qidx_ref, widx_ref, kk_ref, ckv_ref, ckvt_ref, wk_ref, wvt_ref, out_ref,
                keys_ref, khi_ref, klo_ref, qim_ref, qlat_ref, tj_ref, m_ref, acc_ref,
                *, tq, tk, seq, topk):
    i = pl.program_id(1)
    n_off = i * (tq // tk)
    n_kt = n_off + tq // tk
    idx_bits = int(math.log2(seq))

    _split_halves(qidx_ref, IDX_HEADS // 2, qim_ref, tq)
    qlat = jnp.dot(qa_ref[...], wk_ref[...], preferred_element_type=F32)
    t_col = i * tq + lax.broadcasted_iota(I32, (tq, LANES), 0)
    for h in range(A_HEADS):
        qlat_ref[h * tq:(h + 1) * tq, :KV_LORA] = qlat[:, h * KV_LORA:(h + 1) * KV_LORA].astype(BF16)
        qlat_ref[h * tq:(h + 1) * tq, KV_LORA:] = _slope_features(
            t_col, 2.0 ** -(h + 1), (tq, LANES)).astype(BF16)
    w_t = widx_ref[...].T

    row = lax.broadcasted_iota(I32, (tk, tq), 0)
    tqi = i * tq + lax.broadcasted_iota(I32, (tk, tq), 1)
    qchunk = lax.shift_right_logical(tqi, CHUNK_SHIFT)

    def idx_body(kt, carry):
        off = pl.multiple_of(kt * tk, tk)
        kkt = kk_ref[pl.ds(off, tk), :]
        s_all = lax.dot_general(kkt, qim_ref[...], NT_DIMS, preferred_element_type=F32)
        acc = jnp.zeros((tk, tq), F32)
        for h in range(IDX_HEADS):
            acc = acc + w_t[h:h + 1, :] * jnp.maximum(s_all[:, h * tq:(h + 1) * tq], 0.0)
        bits = lax.bitcast_convert_type(acc + 0.0, I32)
        key = jnp.where(bits < 0, bits ^ jnp.int32(0x7FFFFFFF), bits)
        adm = lax.shift_right_logical(off + row, CHUNK_SHIFT) <= qchunk
        key = jnp.where(adm, key, jnp.int32(INT_MIN))
        keys_ref[pl.ds(off, tk), :] = key
        khi_ref[pl.ds(off, tk), :] = lax.shift_right_arithmetic(key, 16).astype(I16)
        klo_ref[pl.ds(off, tk), :] = ((key & 0xFFFF) + I16_MIN).astype(I16)
        return carry

    lax.fori_loop(0, n_kt, idx_body, 0)

    def count(pred):
        def body(kt, c):
            off = pl.multiple_of(kt * tk, tk)
            ind = jnp.where(pred(keys_ref[pl.ds(off, tk), :], off + row), 1.0, 0.0)
            return c + jnp.sum(ind, axis=0, keepdims=True)
        return lax.fori_loop(0, n_kt, body, jnp.zeros((1, tq), F32))

    def count16(ref, pred):
        pack = 2 * SUBLANES
        def body(kt, c):
            off = pl.multiple_of(kt * tk, tk)
            ind = jnp.where(pred(ref[pl.ds(off, tk), :]), jnp.int16(1), jnp.int16(0))
            parts = [ind[j * pack:(j + 1) * pack] for j in range(tk // pack)]
            while len(parts) > 1:
                parts = [parts[j] + parts[j + 1] for j in range(0, len(parts), 2)]
            return c + parts[0]
        c = lax.fori_loop(0, n_kt, body, jnp.zeros((pack, tq), I16))
        return jnp.sum(c.astype(F32), axis=0, keepdims=True)

    def search16(ref, need):
        def body(it, t):
            cand = t + lax.shift_left(jnp.int32(1), 15 - it)
            cand16 = cand.astype(I16)
            cnt = count16(ref, lambda tile: tile >= cand16)
            return jnp.where(cnt >= need, cand, t)
        return lax.fori_loop(0, 16, body, jnp.full((1, tq), I16_MIN, I32))

    t_hi = search16(khi_ref, topk)
    t_hi16 = t_hi.astype(I16)
    n_above = count16(khi_ref, lambda tile: tile > t_hi16)

    def park_body(kt, carry):
        off = pl.multiple_of(kt * tk, tk)
        klo_ref[pl.ds(off, tk), :] = jnp.where(khi_ref[pl.ds(off, tk), :] == t_hi16,
                                               klo_ref[pl.ds(off, tk), :], jnp.int16(I16_MIN))
        return carry

    lax.fori_loop(0, n_kt, park_body, 0)
    t_lo = search16(klo_ref, topk - n_above)
    thr = t_hi * 65536 + (t_lo - I16_MIN)

    live = thr > INT_MIN
    n_ge = count(lambda tile, _: tile >= thr)
    excess = jnp.logical_and(n_ge > topk, live)
    all_ties = jnp.where(live, seq, -1).astype(I32)
    tj_ref[0] = jnp.broadcast_to(thr, (SUBLANES, tq))
    tj_ref[1] = jnp.broadcast_to(all_ties, (SUBLANES, tq))

    @pl.when(jnp.max(jnp.where(excess, 1.0, 0.0)) > 0.0)
    def _():
        need = topk - count(lambda tile, _: tile > thr)

        def tie_body(it, hi):
            cand = hi + lax.shift_left(jnp.int32(1), idx_bits - 1 - it)
            cnt = count(lambda tile, pos: jnp.logical_and(tile == thr, pos < cand))
            return jnp.where(cnt < need, cand, hi)

        hi = lax.fori_loop(0, idx_bits, tie_body, jnp.zeros((1, tq), I32))
        tj_ref[1] = jnp.broadcast_to(jnp.where(excess, hi, all_ties), (SUBLANES, tq))

    thr_b = tj_ref[0][:1]
    tie_hi = tj_ref[1][:1]

    _flash_init(m_ref, acc_ref)

    def att_tile(kt, late_fix):
        off = pl.multiple_of(kt * tk, tk)
        tile = keys_ref[pl.ds(off, tk), :]
        pos = off + row
        sel = jnp.logical_or(tile > thr_b,
                             jnp.logical_and(tile == thr_b, pos <= tie_hi))
        s_all = lax.dot_general(ckv_ref[pl.ds(off, tk), :], qlat_ref[...], NT_DIMS,
                                preferred_element_type=F32)
        if late_fix:
            late = jnp.maximum(pos - tqi, 0).astype(F32)
        alphas, probs = [], []
        for h in range(A_HEADS):
            sl = slice(h * tq, (h + 1) * tq)
            sc = s_all[:, sl]
            if late_fix:
                sc = sc - (2.0 * 2.0 ** -(h + 1)) * late
            alpha, p = _flash_probs(sl, jnp.where(sel, sc, NEG_BIG), m_ref)
            alphas.append(alpha)
            probs.append(p)
        pv = jnp.dot(ckvt_ref[:, pl.ds(off, tk)], jnp.concatenate(probs, axis=1),
                     preferred_element_type=F32)
        acc_ref[...] = jnp.concatenate(alphas, axis=1) * acc_ref[...] + pv

    def off_body(kt, carry):
        att_tile(kt, False)
        return carry

    def diag_body(kt, carry):
        att_tile(kt, True)
        return carry

    lax.fori_loop(0, n_off, off_body, 0)
    lax.fori_loop(n_off, n_kt, diag_body, 0)

    inv_l = 1.0 / acc_ref[KV_LORA:KV_LORA + 1, :]
    olat = jnp.concatenate([(acc_ref[:KV_LORA, h * tq:(h + 1) * tq]
                             * inv_l[:, h * tq:(h + 1) * tq]).astype(BF16)
                            for h in range(A_HEADS)], axis=0)
    out_t = jnp.dot(wvt_ref[...], olat, preferred_element_type=F32)
    out_ref[...] = out_t.T.astype(BF16)


def _dsa_call(qa, qidx, widx, kk, ckv, ckvt, wk_bd, wvt_bd, b, s, tq, tk):
    topk = min(INDEX_TOPK_MAX, s // 4)
    assert s % tq == 0 and tq % tk == 0 and tk % LANES == 0 and s & (s - 1) == 0
    blk = lambda w: pl.BlockSpec((None, tq, w), lambda bi, i: (bi, i, 0))
    full = lambda w: pl.BlockSpec((None, s, w), lambda bi, i: (bi, 0, 0))
    const = lambda a: pl.BlockSpec(a.shape, lambda bi, i: (0, 0))
    r3 = lambda a: a.reshape(b, s, a.shape[-1])
    out = pl.pallas_call(
        functools.partial(_dsa_kernel, tq=tq, tk=tk, seq=s, topk=topk),
        grid=(b, s // tq),
        in_specs=[blk(A_WIDTH), blk(IDX_HEADS * IDX_DIM), blk(LANES), full(LANES),
                  full(2 * KV_LORA),
                  pl.BlockSpec((KV_LORA + V_PAD, s), lambda bi, i: (0, bi)),
                  const(wk_bd), const(wvt_bd)],
        out_specs=blk(A_WIDTH),
        out_shape=jax.ShapeDtypeStruct((b, s, A_WIDTH), BF16),
        scratch_shapes=[pltpu.VMEM((s, tq), I32),
                        pltpu.VMEM((s, tq), I16),
                        pltpu.VMEM((s, tq), I16),
                        pltpu.VMEM((IDX_HEADS * tq, LANES), BF16),
                        pltpu.VMEM((A_HEADS * tq, 2 * KV_LORA), BF16),
                        pltpu.VMEM((2, SUBLANES, tq), I32),
                        pltpu.VMEM((1, A_HEADS * tq), F32),
                        pltpu.VMEM((KV_LORA + V_PAD, A_HEADS * tq), F32)],
        compiler_params=_cparams(2),
        name="dsa",
    )(r3(qa), r3(qidx), r3(widx), r3(kk), r3(ckv), ckvt, wk_bd, wvt_bd)
    return out.reshape(b * s, A_WIDTH)


def _block_diag_uk(w_uk):
    h, c, d = w_uk.shape
    eye = jnp.eye(h, dtype=w_uk.dtype)
    return jnp.einsum('hcd,hg->hdgc', w_uk, eye).reshape(h * d, h * c).astype(BF16)


def _block_diag_uv_t(w_uv):
    h, c, d = w_uv.shape
    eye = jnp.eye(h, dtype=w_uv.dtype)
    return jnp.einsum('hcd,hg->hdgc', w_uv, eye).reshape(h * d, h * c).astype(BF16)


def _diff_kernel(qb_ref, kb_ref, vbt_ref, lam_ref, subln_ref, out_ref,
                 qm_ref, m_ref, acc_ref, *, tq, tk, lambda_init):
    i = pl.program_id(1)
    n_off = i * (tq // tk)
    dvp = B_V_DIM + V_PAD
    _split_halves(qb_ref, B_HEADS, qm_ref, tq)
    t_col = i * tq + lax.broadcasted_iota(I32, (tq, LANES), 0)
    for h in range(B_HEADS):
        feat = _slope_features(t_col, 2.0 ** (-2 * (h + 1)), (tq, LANES)).astype(BF16)
        qm_ref[2 * h * tq:(2 * h + 1) * tq, LANES:] = feat
        qm_ref[(2 * h + 1) * tq:(2 * h + 2) * tq, LANES:] = feat

    row = lax.broadcasted_iota(I32, (tk, tq), 0)
    tqi = i * tq + lax.broadcasted_iota(I32, (tk, tq), 1)
    qchunk = lax.shift_right_logical(tqi, CHUNK_SHIFT)
    _flash_init(m_ref, acc_ref)

    def tile_step(kt, diag):
        off = pl.multiple_of(kt * tk, tk)
        s_pairs = [lax.dot_general(kb_ref[pl.ds(off, tk), 2 * h * LANES:(2 * h + 2) * LANES],
                                   qm_ref[2 * h * tq:(2 * h + 2) * tq, :], NT_DIMS,
                                   preferred_element_type=F32) for h in range(B_HEADS)]
        if diag:
            pos = off + row
            adm = lax.shift_right_logical(pos, CHUNK_SHIFT) <= qchunk
            late = jnp.maximum(pos - tqi, 0).astype(F32)
        alphas, probs = [], []
        for h in range(B_HEADS):
            for j in range(2):
                sl = slice((2 * h + j) * tq, (2 * h + j + 1) * tq)
                sc = s_pairs[h][:, j * tq:(j + 1) * tq]
                if diag:
                    sc = jnp.where(adm, sc - (2.0 * 2.0 ** (-2 * (h + 1))) * late, NEG_BIG)
                alpha, p = _flash_probs(sl, sc, m_ref)
                alphas.append(alpha)
                probs.append(p)
        pv = jnp.concatenate(
            [jnp.dot(vbt_ref[h * dvp:(h + 1) * dvp, pl.ds(off, tk)],
                     jnp.concatenate(probs[2 * h:2 * h + 2], axis=1), preferred_element_type=F32)
             for h in range(B_HEADS)], axis=1)
        acc_ref[...] = jnp.concatenate(alphas, axis=1) * acc_ref[...] + pv

    def off_body(kt, carry):
        tile_step(kt, False)
        return carry

    def diag_body(kt, carry):
        tile_step(kt, True)
        return carry

    lax.fori_loop(0, n_off, off_body, 0)
    lax.fori_loop(n_off, n_off + tq // tk, diag_body, 0)

    lp = lam_ref[...]
    lam = (jnp.exp(jnp.sum(lp[0:1] * lp[1:2], axis=1, keepdims=True))
           - jnp.exp(jnp.sum(lp[2:3] * lp[3:4], axis=1, keepdims=True)) + lambda_init)
    o_all = acc_ref[:B_V_DIM, :] * (1.0 / acc_ref[B_V_DIM:B_V_DIM + 1, :])
    outs = []
    for h in range(B_HEADS):
        o = (o_all[:, 2 * h * tq:(2 * h + 1) * tq]
             - lam * o_all[:, (2 * h + 1) * tq:(2 * h + 2) * tq])
        o = o * lax.rsqrt(jnp.mean(o * o, axis=0, keepdims=True) + RMS_EPS) * subln_ref[...]
        outs.append(o * (1.0 - lambda_init))
    out_ref[...] = jnp.concatenate(outs, axis=0).T.astype(BF16)


def _diff_call(qb, kb, vbt, lam_rows, subln, b, s, tq, tk, lambda_init):
    blk = pl.BlockSpec((None, tq, B_WIDTH), lambda bi, i: (bi, i, 0))
    kw = 2 * B_HEADS * LANES
    dvp = B_V_DIM + V_PAD
    out = pl.pallas_call(
        functools.partial(_diff_kernel, tq=tq, tk=tk, lambda_init=lambda_init),
        grid=(b, s // tq),
        in_specs=[blk, pl.BlockSpec((None, s, kw), lambda bi, i: (bi, 0, 0)),
                  pl.BlockSpec((B_HEADS * dvp, s), lambda bi, i: (0, bi)),
                  pl.BlockSpec((SUBLANES, LANES), lambda bi, i: (0, 0)),
                  pl.BlockSpec((B_V_DIM, 1), lambda bi, i: (0, 0))],
        out_specs=blk,
        out_shape=jax.ShapeDtypeStruct((b, s, B_WIDTH), BF16),
        scratch_shapes=[pltpu.VMEM((2 * B_HEADS * tq, 2 * LANES), BF16),
                        pltpu.VMEM((1, 2 * B_HEADS * tq), F32),
                        pltpu.VMEM((dvp, 2 * B_HEADS * tq), F32)],
        compiler_params=_cparams(2),
        name="diff",
    )(qb.reshape(b, s, B_WIDTH), kb.reshape(b, s, kw), vbt, lam_rows, subln.reshape(B_V_DIM, 1))
    return out.reshape(b * s, B_WIDTH)


def _mix_out_kernel(h_ref, a_ref, b_ref, wa_ref, wb_ref, g_ref, beta_ref, o_ref):
    y = (jnp.dot(a_ref[...], wa_ref[...], preferred_element_type=F32)
         + jnp.dot(b_ref[...], wb_ref[...], preferred_element_type=F32))
    o_ref[...] = _layer_norm(ALPHA * h_ref[...] + y, g_ref[...], beta_ref[...])


def _mix_out_call(h, out_a, out_b, w_o, g, beta, tm):
    n, d = h.shape
    w = w_o.astype(BF16)
    row = lambda wd: pl.BlockSpec((tm, wd), lambda i: (i, 0))
    const = lambda shp: pl.BlockSpec(shp, lambda i: (0, 0))
    return pl.pallas_call(
        _mix_out_kernel,
        grid=(n // tm,),
        in_specs=[row(d), row(A_WIDTH), row(B_WIDTH), const((A_WIDTH, d)), const((B_WIDTH, d)),
                  const((1, d)), const((1, d))],
        out_specs=row(d),
        out_shape=jax.ShapeDtypeStruct((n, d), F32),
        compiler_params=_cparams(1),
        name="mix_out",
    )(h, out_a, out_b, w[:A_WIDTH], w[A_WIDTH:], g.reshape(1, d), beta.reshape(1, d))


def _mem_kv_kernel(m_ref, w_ref, k_ref, v_ref):
    x = m_ref[...].astype(BF16)
    d = k_ref.shape[-1]
    k_ref[...] = jnp.dot(x, w_ref[:, :d], preferred_element_type=F32).astype(BF16)
    v_ref[...] = jnp.dot(x, w_ref[:, d:], preferred_element_type=F32).astype(BF16)


def _mem_kv_call(mem2d, wkv, tm):
    n, d = mem2d.shape
    return pl.pallas_call(
        _mem_kv_kernel,
        grid=(n // tm,),
        in_specs=[pl.BlockSpec((tm, d), lambda i: (i, 0)),
                  pl.BlockSpec((d, 2 * d), lambda i: (0, 0))],
        out_specs=[pl.BlockSpec((tm, d), lambda i: (i, 0))] * 2,
        out_shape=[jax.ShapeDtypeStruct((n, d), BF16)] * 2,
        compiler_params=_cparams(1),
        name="mem_kv",
    )(mem2d, wkv.astype(BF16))


def _mem_attn_kernel(h_ref, wq_ref, k_ref, v_ref, wo_ref, g_ref, beta_ref, o_ref):
    h = h_ref[...]
    q = jnp.dot(h.astype(BF16), wq_ref[...], preferred_element_type=F32)
    q = (q * (MEM_HEAD_DIM ** -0.5)).astype(BF16)
    outs = []
    for hd in range(MEM_HEADS):
        sl = slice(hd * MEM_HEAD_DIM, (hd + 1) * MEM_HEAD_DIM)
        sc = lax.dot_general(q[:, sl], k_ref[:, sl], NT_DIMS, preferred_element_type=F32)
        p = jnp.exp(sc - jnp.max(sc, axis=1, keepdims=True))
        den = jnp.sum(p, axis=1, keepdims=True)
        o = jnp.dot(p.astype(BF16), v_ref[:, sl], preferred_element_type=F32) / den
        outs.append(o.astype(BF16))
    y = jnp.dot(jnp.concatenate(outs, axis=1), wo_ref[...], preferred_element_type=F32)
    o_ref[...] = _layer_norm(ALPHA * h + y, g_ref[...], beta_ref[...])


def _mem_attn_call(h, wq, k, v, wo, g, beta, b, s, tm):
    n, d = h.shape
    m = k.shape[0] // b
    const = lambda shp: pl.BlockSpec(shp, lambda bi, i: (0,) * len(shp))
    row = pl.BlockSpec((None, tm, d), lambda bi, i: (bi, i, 0))
    kv = pl.BlockSpec((None, m, d), lambda bi, i: (bi, 0, 0))
    out = pl.pallas_call(
        _mem_attn_kernel,
        grid=(b, s // tm),
        in_specs=[row, const((d, d)), kv, kv, const((d, d)), const((1, d)), const((1, d))],
        out_specs=row,
        out_shape=jax.ShapeDtypeStruct((b, s, d), F32),
        compiler_params=_cparams(2),
        name="mem_attn",
    )(h.reshape(b, s, d), wq.astype(BF16), k.reshape(b, m, d), v.reshape(b, m, d),
      wo.astype(BF16), g.reshape(1, d), beta.reshape(1, d))
    return out.reshape(n, d)


def _router_kernel(h_ref, rw_ref, bias_ref, tri_ref, e_ref, g_ref, rank_ref, cnt_ref, run_ref):
    tm = h_ref.shape[0]

    @pl.when(pl.program_id(0) == 0)
    def _():
        run_ref[...] = jnp.zeros(run_ref.shape, F32)

    logits = jnp.dot(h_ref[...].astype(BF16), rw_ref[...], preferred_element_type=F32)
    scores = 1.0 / (1.0 + jnp.exp(-logits))
    lane = lax.broadcasted_iota(I32, (tm, LANES), 1)
    lanef = lane.astype(F32)
    ninf = jnp.float32(-jnp.inf)
    cur = jnp.where(lane < N_EXPERTS, scores + bias_ref[...], ninf)
    top_e = jnp.zeros((tm, LANES), F32)
    top_s = jnp.zeros((tm, LANES), F32)
    picked = jnp.zeros((tm, LANES), F32)
    firsts = []
    for k in range(TOP_K):
        mx = jnp.max(cur, axis=1, keepdims=True)
        first = jnp.min(jnp.where(cur == mx, lanef, float(LANES)), axis=1, keepdims=True)
        hit = lanef == first
        s_k = jnp.sum(jnp.where(hit, scores, 0.0), axis=1, keepdims=True)
        top_e = jnp.where(lane == k, first, top_e)
        top_s = jnp.where(lane == k, s_k, top_s)
        picked = jnp.where(hit, 1.0, picked)
        cur = jnp.where(hit, ninf, cur)
        firsts.append(first)
    e_ref[...] = top_e.astype(I32)
    g_ref[...] = top_s / jnp.sum(top_s, axis=1, keepdims=True) * ROUTED_SCALE

    slot = jnp.dot(tri_ref[...], picked.astype(BF16), preferred_element_type=F32) + run_ref[...]
    rank = jnp.zeros((tm, LANES), F32)
    for k in range(TOP_K):
        r_k = jnp.sum(jnp.where(lanef == firsts[k], slot, 0.0), axis=1, keepdims=True)
        rank = jnp.where(lane == k, r_k, rank)
    rank_ref[...] = rank.astype(I32)
    run = run_ref[...] + jnp.sum(picked, axis=0, keepdims=True)
    run_ref[...] = run
    cnt_ref[...] = run.astype(I32)


def _router_call(h, router_w, router_bias, tm):
    n, d = h.shape
    rw = jnp.pad(router_w, ((0, 0), (0, LANES - N_EXPERTS))).astype(BF16)
    bias = jnp.pad(router_bias, (0, LANES - N_EXPERTS)).reshape(1, LANES)
    tri = jnp.tri(tm, k=-1, dtype=BF16)
    row = pl.BlockSpec((tm, LANES), lambda i: (i, 0))
    const = lambda shp: pl.BlockSpec(shp, lambda i: (0, 0))
    return pl.pallas_call(
        _router_kernel,
        grid=(n // tm,),
        in_specs=[pl.BlockSpec((tm, d), lambda i: (i, 0)), const((d, LANES)), const((1, LANES)),
                  const((tm, tm))],
        out_specs=[row, row, row, const((1, LANES))],
        out_shape=[jax.ShapeDtypeStruct((n, LANES), I32),
                   jax.ShapeDtypeStruct((n, LANES), F32),
                   jax.ShapeDtypeStruct((n, LANES), I32),
                   jax.ShapeDtypeStruct((1, LANES), I32)],
        scratch_shapes=[pltpu.VMEM((1, LANES), F32)],
        compiler_params=_cparams(1),
        name="router",
    )(h, rw, bias, tri)


def _dest_kernel(e_ref, rank_ref, gs_ref, o_ref):
    tm = e_ref.shape[0]
    lane = lax.broadcasted_iota(I32, (tm, LANES), 1)
    e = e_ref[...]
    gs = gs_ref[...].astype(F32)
    start = jnp.zeros((tm, LANES), F32)
    for k in range(TOP_K):
        s_k = jnp.sum(jnp.where(lane == e[:, k:k + 1], gs, 0.0), axis=1, keepdims=True)
        start = jnp.where(lane == k, s_k, start)
    o_ref[...] = start.astype(I32) + rank_ref[...]


def _dest_call(top_e, rank, gstart, tm):
    n = top_e.shape[0]
    row = pl.BlockSpec((tm, LANES), lambda i: (i, 0))
    return pl.pallas_call(
        _dest_kernel,
        grid=(n // tm,),
        in_specs=[row, row, pl.BlockSpec((1, LANES), lambda i: (0, 0))],
        out_specs=row,
        out_shape=jax.ShapeDtypeStruct((n, LANES), I32),
        compiler_params=_cparams(1),
        name="dest",
    )(top_e, rank, gstart)


def _dispatch_kernel(dest_ref, x_ref, xs_ref, sem, *, td):
    def row_copy(r, k):
        return pltpu.make_async_copy(x_ref.at[pl.ds(r, 1), :],
                                     xs_ref.at[pl.ds(dest_ref[r * TOP_K + k], 1), :], sem)

    for r in range(td):
        for k in range(TOP_K):
            row_copy(r, k).start(priority=k % N_DMA_QUEUES)

    def drain(r, carry):
        for k in range(TOP_K):
            row_copy(r, k).wait()
        return carry

    lax.fori_loop(0, td, drain, 0)


def _dispatch_call(h, dest, td):
    n, d = h.shape
    m = n * TOP_K
    return pl.pallas_call(
        functools.partial(_dispatch_kernel, td=td),
        grid=(n // td,),
        in_specs=[pl.BlockSpec((td * TOP_K,), lambda i: (i,), memory_space=pltpu.SMEM),
                  pl.BlockSpec((td, d), lambda i: (i, 0))],
        out_specs=pl.BlockSpec(memory_space=pl.ANY),
        out_shape=jax.ShapeDtypeStruct((m, d), F32),
        scratch_shapes=[pltpu.SemaphoreType.DMA(())],
        compiler_params=_cparams(1),
        name="dispatch",
    )(dest, h)


def _moe_mm_kernel(ptile_ref, pexp_ref, plo_ref, phi_ref, xs_ref, wg_ref, wu_ref, wd_ref, y_ref,
                   *, tmm):
    p = pl.program_id(0)
    tile = ptile_ref[p]
    lo = plo_ref[p]
    hi = phi_ref[p]
    first = jnp.logical_or(p == 0, ptile_ref[jnp.maximum(p - 1, 0)] != tile)

    def expert_out():
        x = xs_ref[...].astype(BF16)
        g = jnp.dot(x, wg_ref[...].astype(BF16), preferred_element_type=F32)
        u = jnp.dot(x, wu_ref[...].astype(BF16), preferred_element_type=F32)
        hid = (_silu(g) * u).astype(BF16)
        return jnp.dot(hid, wd_ref[...].astype(BF16), preferred_element_type=F32)

    whole = jnp.logical_and(lo <= tile * tmm, hi >= (tile + 1) * tmm)

    @pl.when(whole)
    def _():
        y_ref[...] = expert_out()

    @pl.when(jnp.logical_and(first, jnp.logical_not(whole)))
    def _():
        y_ref[...] = jnp.zeros(y_ref.shape, F32)

    @pl.when(jnp.logical_and(lo < hi, jnp.logical_not(whole)))
    def _():
        y = expert_out()
        rowg = tile * tmm + lax.broadcasted_iota(I32, (tmm, 1), 0)
        y_ref[...] += jnp.where(jnp.logical_and(rowg >= lo, rowg < hi), y, 0.0)


def _moe_mm_call(xs, pairs, wg, wu, wd, tmm):
    m = xs.shape[0]
    n_pairs = pairs[0].shape[0]
    rows = pl.BlockSpec((tmm, D_MODEL), lambda p, pt, pe, plo, phi: (pt[p], 0))
    wspec = lambda shp: pl.BlockSpec((None,) + shp, lambda p, pt, pe, plo, phi: (pe[p], 0, 0))
    return pl.pallas_call(
        functools.partial(_moe_mm_kernel, tmm=tmm),
        grid_spec=pltpu.PrefetchScalarGridSpec(
            num_scalar_prefetch=4,
            grid=(n_pairs,),
            in_specs=[rows, wspec((D_MODEL, EXPERT_DIM)), wspec((D_MODEL, EXPERT_DIM)),
                      wspec((EXPERT_DIM, D_MODEL))],
            out_specs=rows),
        out_shape=jax.ShapeDtypeStruct((m, D_MODEL), F32),
        compiler_params=_cparams(1),
        name="moe_mm",
    )(*pairs, xs, wg, wu, wd)


def _group_pairs(counts, m, tmm):
    n_tiles = m // tmm
    n_pairs = n_tiles + N_EXPERTS
    gend = jnp.cumsum(counts)
    gstart = gend - counts
    t0 = jnp.arange(n_tiles, dtype=I32) * tmm
    n_le = lambda edges, v: jnp.sum(edges[None, :] <= v[:, None], axis=1).astype(I32)
    e_first = n_le(gend, t0)
    e_last = n_le(gend, t0 + (tmm - 1))
    per_tile = e_last - e_first + 1
    pend = jnp.cumsum(per_tile)
    pstart = pend - per_tile
    p = jnp.arange(n_pairs, dtype=I32)
    valid = p < pend[-1]
    tile = jnp.minimum(n_le(pend, p), n_tiles - 1)
    e = jnp.clip(e_first[tile] + (p - pstart[tile]), 0, N_EXPERTS - 1).astype(I32)
    lo = jnp.where(valid, gstart[e], 0).astype(I32)
    hi = jnp.where(valid, gend[e], 0).astype(I32)
    return tile, e, lo, hi


def _combine_kernel(dest_ref, gate_ref, h_ref, y_ref, sg_ref, su_ref, sd_ref, g_ref, beta_ref, o_ref,
                    buf_ref, sem, *, tc):
    def row_copy(r, k):
        return pltpu.make_async_copy(y_ref.at[pl.ds(dest_ref[r * TOP_K + k], 1), :],
                                     buf_ref.at[k, pl.ds(r, 1), :], sem)

    for r in range(tc):
        for k in range(TOP_K):
            row_copy(r, k).start(priority=k % N_DMA_QUEUES)

    h = h_ref[...]
    x = h.astype(BF16)
    hid = (_silu(jnp.dot(x, sg_ref[...], preferred_element_type=F32))
           * jnp.dot(x, su_ref[...], preferred_element_type=F32)).astype(BF16)
    shared = jnp.dot(hid, sd_ref[...], preferred_element_type=F32)

    def drain(r, carry):
        for k in range(TOP_K):
            row_copy(r, k).wait()
        return carry

    lax.fori_loop(0, tc, drain, 0)

    gates = gate_ref[...]
    routed = gates[:, 0:1] * buf_ref[0]
    for k in range(1, TOP_K):
        routed = routed + gates[:, k:k + 1] * buf_ref[k]
    o_ref[...] = _layer_norm(ALPHA * h + (routed + shared), g_ref[...], beta_ref[...])


def _combine_call(h, y, dest, gates, sg, su, sd, g, beta, tc):
    n, d = h.shape
    smem = pl.BlockSpec((tc * TOP_K,), lambda i: (i,), memory_space=pltpu.SMEM)
    const = lambda shp: pl.BlockSpec(shp, lambda i: (0, 0))
    return pl.pallas_call(
        functools.partial(_combine_kernel, tc=tc),
        grid=(n // tc,),
        in_specs=[smem, pl.BlockSpec((tc, LANES), lambda i: (i, 0)),
                  pl.BlockSpec((tc, d), lambda i: (i, 0)),
                  pl.BlockSpec(memory_space=pl.ANY),
                  const((d, SHARED_DIM)), const((d, SHARED_DIM)), const((SHARED_DIM, d)),
                  const((1, d)), const((1, d))],
        out_specs=pl.BlockSpec((tc, d), lambda i: (i, 0)),
        out_shape=jax.ShapeDtypeStruct((n, d), F32),
        scratch_shapes=[pltpu.VMEM((TOP_K, tc, d), F32),
                        pltpu.SemaphoreType.DMA(())],
        compiler_params=_cparams(1),
        name="combine",
    )(dest, gates, h, y, sg.astype(BF16), su.astype(BF16), sd.astype(BF16),
      g.reshape(1, d), beta.reshape(1, d))


def _tiles(b, s):
    n = b * s
    pick = lambda pref, total: next(t for t in (pref, 512, 256, 128, 64, 32, 16, 8) if t <= pref and total % t == 0)
    return dict(
        rows=pick(512, n),
        attn_q=pick(512, s),
        attn_k=pick(256, s),
        mem_rows=pick(512, s),
        mem_kv=pick(512, b * 256),
        router=pick(1024, n),
        dispatch=pick(128, n),
        moe=pick(512, n * TOP_K),
        combine=pick(128, n),
    )


def kernel(x, mem, ln_in_g, ln_in_b, w_in, a_kv_norm, a_w_uk, a_w_uv, b_lq1, b_lk1, b_lq2, b_lk2,
           b_subln, w_o, ln1_g, ln1_b, m_wq, m_wkv, m_wo, ln2_g, ln2_b, router_w, router_bias,
           e_w_gate, e_w_up, e_w_down, s_w_gate, s_w_up, s_w_down, ln3_g, ln3_b):
    b, s, d = x.shape
    n = b * s
    m = n * TOP_K
    t = _tiles(b, s)
    mem2d = mem.reshape(-1, d)

    h = _ln_call(x.reshape(n, d), ln_in_g, ln_in_b, t["rows"])
    for l in range(DEPTH):
        lambda_init = 0.8 - 0.6 * math.exp(-0.3 * l)

        qa, ckv, ckvt, qidx, kk, widx, qb, kb, vbt = _proj_call(h, _pad_w_in(w_in[l]), a_kv_norm[l],
                                                                  t["rows"], s)
        out_a = _dsa_call(qa, qidx, widx, kk, ckv, ckvt, _block_diag_uk(a_w_uk[l]),
                          _block_diag_uv_t(a_w_uv[l]), b, s, t["attn_q"], t["attn_k"])
        lam_rows = jnp.pad(jnp.stack([b_lq1[l], b_lk1[l], b_lq2[l], b_lk2[l]]),
                           ((0, SUBLANES - 4), (0, LANES - B_QK_DIM)))
        out_b = _diff_call(qb, kb, vbt, lam_rows, b_subln[l], b, s, t["attn_q"], t["attn_k"], lambda_init)
        h = _mix_out_call(h, out_a, out_b, w_o[l], ln1_g[l], ln1_b[l], t["rows"])

        mk, mv = _mem_kv_call(mem2d, m_wkv[l], t["mem_kv"])
        h = _mem_attn_call(h, m_wq[l], mk, mv, m_wo[l], ln2_g[l], ln2_b[l], b, s, t["mem_rows"])

        top_e, gates, rank, counts = _router_call(h, router_w[l], router_bias[l], t["router"])
        gstart = jnp.cumsum(counts, axis=1) - counts
        dest = _dest_call(top_e, rank, gstart, t["router"])[:, :TOP_K].reshape(m)
        counts = counts[0, :N_EXPERTS]
        xs = _dispatch_call(h, dest, t["dispatch"])
        pairs = _group_pairs(counts, m, t["moe"])
        y = _moe_mm_call(xs, pairs, e_w_gate[l], e_w_up[l], e_w_down[l], t["moe"])
        h = _combine_call(h, y, dest, gates, s_w_gate[l], s_w_up[l], s_w_down[l],
                          ln3_g[l], ln3_b[l], t["combine"])
    return h.reshape(b, s, d)
```

```python
import functools
import math

import jax
import jax.numpy as jnp
import numpy as np
from jax import lax
from jax.experimental import pallas as pl
from jax.experimental.pallas import tpu as pltpu

F32 = jnp.float32
BF16 = jnp.bfloat16
I32 = jnp.int32
I16 = jnp.int16

D_MODEL = 1024
DEPTH = 2
CHUNK = 64
CHUNK_SHIFT = 6
A_HEADS = 8
A_HEAD_DIM = 64
A_WIDTH = A_HEADS * A_HEAD_DIM
KV_LORA = 128
IDX_HEADS = 8
IDX_DIM = 64
INDEX_TOPK_MAX = 256
B_HEADS = 4
B_QK_DIM = 64
B_V_DIM = 128
B_WIDTH = B_HEADS * B_V_DIM
MEM_HEADS = 4
MEM_HEAD_DIM = D_MODEL // MEM_HEADS
N_EXPERTS = 64
TOP_K = 8
EXPERT_DIM = 256
SHARED_DIM = 256
ROUTED_SCALE = 2.5
ALPHA = (2 * DEPTH) ** 0.25
LN_EPS = 1e-5
RMS_EPS = 1e-6

LANES = 128
SUBLANES = 8
VMEM_LIMIT = 56 * 1024 * 1024
INT_MIN = -2 ** 31
I16_MIN = -2 ** 15
V_PAD = 16
N_DMA_QUEUES = 2
MOE_SUB_ROWS = 128
NEG_BIG = -3.0e38
M_INIT = -1.0e30

NT_DIMS = (((1,), (1,)), ((), ()))


def _cparams(n_axes):
    return pltpu.CompilerParams(dimension_semantics=("arbitrary",) * n_axes,
                                vmem_limit_bytes=VMEM_LIMIT)


def _layer_norm(x, g, b):
    mu = jnp.mean(x, axis=-1, keepdims=True)
    xc = x - mu
    var = jnp.mean(xc * xc, axis=-1, keepdims=True)
    return xc * lax.rsqrt(var + LN_EPS) * g + b


def _silu(x):
    return x * (1.0 / (1.0 + jnp.exp(-x)))


def _ln_kernel(x_ref, g_ref, b_ref, o_ref):
    o_ref[...] = _layer_norm(x_ref[...], g_ref[...], b_ref[...])


def _ln_call(x, g, b, tm):
    n, d = x.shape
    return pl.pallas_call(
        _ln_kernel,
        grid=(n // tm,),
        in_specs=[pl.BlockSpec((tm, d), lambda i: (i, 0)),
                  pl.BlockSpec((1, d), lambda i: (0, 0)),
                  pl.BlockSpec((1, d), lambda i: (0, 0))],
        out_specs=pl.BlockSpec((tm, d), lambda i: (i, 0)),
        out_shape=jax.ShapeDtypeStruct((n, d), F32),
        compiler_params=_cparams(1),
        name="ln_in",
    )(x, g.reshape(1, d), b.reshape(1, d))


_SEG_QA = (0, 512)
_SEG_CKV = (512, 640)
_SEG_QIDX = (640, 1152)
_SEG_KK = (1152, 1280)
_SEG_WIDX = (1280, 1408)
_SEG_QB = (1408, 1920)
_SEG_KB = (1920, 2432)
_SEG_VB = (2432, 2944)
_PROJ_COLS = 2944


def _pos_features(pos, shape):
    lane = lax.broadcasted_iota(I32, shape, 1)
    hi = lax.shift_right_logical(pos, CHUNK_SHIFT).astype(F32)
    lo = (pos & (CHUNK - 1)).astype(F32)
    return jnp.where(lane < 2, 1.0, jnp.where(lane == 2, hi, jnp.where(lane == 3, lo, 0.0)))


def _slope_features(tqi, slope, shape):
    lane = lax.broadcasted_iota(I32, shape, 1)
    hi = lax.shift_right_logical(tqi, CHUNK_SHIFT).astype(F32)
    lo = (tqi & (CHUNK - 1)).astype(F32)
    return jnp.where(lane == 0, -slope * CHUNK * hi,
                     jnp.where(lane == 1, -slope * lo,
                               jnp.where(lane == 2, slope * CHUNK,
                                         jnp.where(lane == 3, slope, 0.0))))


def _with_ones_rows(vt):
    t = vt.shape[1]
    sub = lax.broadcasted_iota(I32, (V_PAD, t), 0)
    return jnp.concatenate([vt, jnp.where(sub == 0, 1.0, 0.0)], axis=0)


def _proj_kernel(h_ref, w_ref, kvn_ref, qa_ref, ckv_ref, ckvt_ref, qidx_ref, kk_ref, widx_ref,
                 qb_ref, kb_ref, vbt_ref, *, seq):
    tm = h_ref.shape[0]
    x = h_ref[...].astype(BF16)

    def seg(s):
        return jnp.dot(x, w_ref[:, s[0]:s[1]], preferred_element_type=F32)

    row = pl.program_id(0) * tm + lax.broadcasted_iota(I32, (tm, LANES), 0)
    feat = _pos_features(row & (seq - 1), (tm, LANES)).astype(BF16)

    qa_ref[...] = (seg(_SEG_QA) * (A_HEAD_DIM ** -0.5)).astype(BF16)
    c = seg(_SEG_CKV)
    c = c * lax.rsqrt(jnp.mean(c * c, axis=-1, keepdims=True) + RMS_EPS) * kvn_ref[...]
    ckv_ref[:, :KV_LORA] = c.astype(BF16)
    ckv_ref[:, KV_LORA:] = feat
    ckvt_ref[...] = _with_ones_rows(c.T).astype(BF16)
    qidx_ref[...] = (seg(_SEG_QIDX) * (IDX_DIM ** -0.5)).astype(BF16)
    kk_ref[...] = seg(_SEG_KK).astype(BF16)
    widx_ref[...] = seg(_SEG_WIDX) * (IDX_HEADS ** -0.5)
    qb_ref[...] = (seg(_SEG_QB) * (B_QK_DIM ** -0.5)).astype(BF16)
    kb = seg(_SEG_KB).astype(BF16)
    vt = seg(_SEG_VB).T
    for h in range(B_HEADS):
        kb_ref[:, 2 * h * LANES:(2 * h + 1) * LANES] = kb[:, h * LANES:(h + 1) * LANES]
        kb_ref[:, (2 * h + 1) * LANES:(2 * h + 2) * LANES] = feat
        vbt_ref[h * (B_V_DIM + V_PAD):(h + 1) * (B_V_DIM + V_PAD), :] = _with_ones_rows(
            vt[h * B_V_DIM:(h + 1) * B_V_DIM]).astype(BF16)


def _proj_call(h, w_pad, kvn, tm, seq):
    n, d = h.shape
    assert seq & (seq - 1) == 0 and seq <= CHUNK * 256 and seq % tm == 0
    outs = [(512, BF16, False), (2 * KV_LORA, BF16, False), (KV_LORA + V_PAD, BF16, True),
            (512, BF16, False), (128, BF16, False), (128, F32, False), (512, BF16, False),
            (2 * B_HEADS * LANES, BF16, False), (B_HEADS * (B_V_DIM + V_PAD), BF16, True)]
    spec = lambda w, t: (pl.BlockSpec((w, tm), lambda i: (0, i)) if t
                         else pl.BlockSpec((tm, w), lambda i: (i, 0)))
    shape = lambda w, dt, t: jax.ShapeDtypeStruct((w, n) if t else (n, w), dt)
    return pl.pallas_call(
        functools.partial(_proj_kernel, seq=seq),
        grid=(n // tm,),
        in_specs=[pl.BlockSpec((tm, d), lambda i: (i, 0)),
                  pl.BlockSpec((d, _PROJ_COLS), lambda i: (0, 0)),
                  pl.BlockSpec((1, KV_LORA), lambda i: (0, 0))],
        out_specs=[spec(w, t) for w, _, t in outs],
        out_shape=[shape(w, dt, t) for w, dt, t in outs],
        compiler_params=_cparams(1),
        name="proj_in",
    )(h, w_pad, kvn.reshape(1, KV_LORA))


def _pad_w_in(w_in):
    sizes = (A_WIDTH, KV_LORA, IDX_HEADS * IDX_DIM, IDX_DIM, IDX_HEADS,
             2 * B_HEADS * B_QK_DIM, 2 * B_HEADS * B_QK_DIM, B_WIDTH)
    splits = np.cumsum(sizes)[:-1].tolist()
    q_a, c_kv, q_idx, k_idx, w_idx, q_b, k_b, v_b = jnp.split(w_in, splits, axis=-1)
    w_idx = jnp.pad(w_idx, ((0, 0), (0, LANES - IDX_HEADS)))
    return jnp.concatenate([q_a, c_kv, q_idx, k_idx, k_idx, w_idx, q_b, k_b, v_b],
                           axis=-1).astype(BF16)


def _flash_probs(sl, sc, m_ref):
    m_old = m_ref[:, sl]
    m_new = jnp.maximum(m_old, jnp.max(sc, axis=0, keepdims=True))
    m_ref[:, sl] = m_new
    return jnp.exp(m_old - m_new), jnp.exp(sc - m_new).astype(BF16)


def _flash_init(m_ref, acc_ref):
    m_ref[...] = jnp.full(m_ref.shape, M_INIT, F32)
    acc_ref[...] = jnp.zeros(acc_ref.shape, F32)


def _split_halves(q_ref, n_pairs, out_ref, rows):
    lane = lax.broadcasted_iota(I32, (rows, LANES), 1)
    for j in range(n_pairs):
        qp = q_ref[:, j * LANES:(j + 1) * LANES].astype(F32)
        out_ref[2 * j * rows:(2 * j + 1) * rows, :LANES] = jnp.where(lane < 64, qp, 0.0).astype(BF16)
        out_ref[(2 * j + 1) * rows:(2 * j + 2) * rows, :LANES] = jnp.where(lane >= 64, qp, 0.0).astype(BF16)


def _dsa_kernel(qa_ref, qidx_ref, widx_ref, kk_ref, ckv_ref, ckvt_ref, wk_ref, wvt_ref, out_ref,
                keys_ref, khi_ref, klo_ref, qim_ref, qlat_ref, tj_ref, m_ref, acc_ref,
                *, tq, tk, seq, topk):
    i = pl.program_id(1)
    n_off = i * (tq // tk)
    n_kt = n_off + tq // tk
    idx_bits = int(math.log2(seq))

    _split_halves(qidx_ref, IDX_HEADS // 2, qim_ref, tq)
    qlat = jnp.dot(qa_ref[...], wk_ref[...], preferred_element_type=F32)
    t_col = i * tq + lax.broadcasted_iota(I32, (tq, LANES), 0)
    for h in range(A_HEADS):
        qlat_ref[h * tq:(h + 1) * tq, :KV_LORA] = qlat[:, h * KV_LORA:(h + 1) * KV_LORA].astype(BF16)
        qlat_ref[h * tq:(h + 1) * tq, KV_LORA:] = _slope_features(
            t_col, 2.0 ** -(h + 1), (tq, LANES)).astype(BF16)
    w_t = widx_ref[...].T

    row = lax.broadcasted_iota(I32, (tk, tq), 0)
    tqi = i * tq + lax.broadcasted_iota(I32, (tk, tq), 1)
    qchunk = lax.shift_right_logical(tqi, CHUNK_SHIFT)

    def idx_body(kt, carry):
        off = pl.multiple_of(kt * tk, tk)
        kkt = kk_ref[pl.ds(off, tk), :]
        s_all = lax.dot_general(kkt, qim_ref[...], NT_DIMS, preferred_element_type=F32)
        acc = jnp.zeros((tk, tq), F32)
        for h in range(IDX_HEADS):
            acc = acc + w_t[h:h + 1, :] * jnp.maximum(s_all[:, h * tq:(h + 1) * tq], 0.0)
        bits = lax.bitcast_convert_type(acc + 0.0, I32)
        key = jnp.where(bits < 0, bits ^ jnp.int32(0x7FFFFFFF), bits)
        adm = lax.shift_right_logical(off + row, CHUNK_SHIFT) <= qchunk
        key = jnp.where(adm, key, jnp.int32(INT_MIN))
        keys_ref[pl.ds(off, tk), :] = key
        khi_ref[pl.ds(off, tk), :] = lax.shift_right_arithmetic(key, 16).astype(I16)
        klo_ref[pl.ds(off, tk), :] = ((key & 0xFFFF) + I16_MIN).astype(I16)
        return carry

    lax.fori_loop(0, n_kt, idx_body, 0)

    def count(pred):
        def body(kt, c):
            off = pl.multiple_of(kt * tk, tk)
            ind = jnp.where(pred(keys_ref[pl.ds(off, tk), :], off + row), 1.0, 0.0)
            return c + jnp.sum(ind, axis=0, keepdims=True)
        return lax.fori_loop(0, n_kt, body, jnp.zeros((1, tq), F32))

    def count16(ref, pred):
        pack = 2 * SUBLANES
        def body(kt, c):
            off = pl.multiple_of(kt * tk, tk)
            ind = jnp.where(pred(ref[pl.ds(off, tk), :]), jnp.int16(1), jnp.int16(0))
            parts = [ind[j * pack:(j + 1) * pack] for j in range(tk // pack)]
            while len(parts) > 1:
                parts = [parts[j] + parts[j + 1] for j in range(0, len(parts), 2)]
            return c + parts[0]
        c = lax.fori_loop(0, n_kt, body, jnp.zeros((pack, tq), I16))
        return jnp.sum(c.astype(F32), axis=0, keepdims=True)

    def search16(ref, need):
        def body(it, t):
            cand = t + lax.shift_left(jnp.int32(1), 15 - it)
            cand16 = cand.astype(I16)
            cnt = count16(ref, lambda tile: tile >= cand16)
            return jnp.where(cnt >= need, cand, t)
        return lax.fori_loop(0, 16, body, jnp.full((1, tq), I16_MIN, I32))

    t_hi = search16(khi_ref, topk)
    t_hi16 = t_hi.astype(I16)
    n_above = count16(khi_ref, lambda tile: tile > t_hi16)

    def park_body(kt, carry):
        off = pl.multiple_of(kt * tk, tk)
        klo_ref[pl.ds(off, tk), :] = jnp.where(khi_ref[pl.ds(off, tk), :] == t_hi16,
                                               klo_ref[pl.ds(off, tk), :], jnp.int16(I16_MIN))
        return carry

    lax.fori_loop(0, n_kt, park_body, 0)
    t_lo = search16(klo_ref, topk - n_above)
    thr = t_hi * 65536 + (t_lo - I16_MIN)

    live = thr > INT_MIN
    n_ge = count(lambda tile, _: tile >= thr)
    excess = jnp.logical_and(n_ge > topk, live)
    all_ties = jnp.where(live, seq, -1).astype(I32)
    tj_ref[0] = jnp.broadcast_to(thr, (SUBLANES, tq))
    tj_ref[1] = jnp.broadcast_to(all_ties, (SUBLANES, tq))

    @pl.when(jnp.max(jnp.where(excess, 1.0, 0.0)) > 0.0)
    def _():
        need = topk - count(lambda tile, _: tile > thr)

        def tie_body(it, hi):
            cand = hi + lax.shift_left(jnp.int32(1), idx_bits - 1 - it)
            cnt = count(lambda tile, pos: jnp.logical_and(tile == thr, pos < cand))
            return jnp.where(cnt < need, cand, hi)

        hi = lax.fori_loop(0, idx_bits, tie_body, jnp.zeros((1, tq), I32))
        tj_ref[1] = jnp.broadcast_to(jnp.where(excess, hi, all_ties), (SUBLANES, tq))

    thr_b = tj_ref[0][:1]
    tie_hi = tj_ref[1][:1]

    _flash_init(m_ref, acc_ref)

    def att_tile(kt, late_fix):
        off = pl.multiple_of(kt * tk, tk)
        tile = keys_ref[pl.ds(off, tk), :]
        pos = off + row
        sel = jnp.logical_or(tile > thr_b,
                             jnp.logical_and(tile == thr_b, pos <= tie_hi))
        s_all = lax.dot_general(ckv_ref[pl.ds(off, tk), :], qlat_ref[...], NT_DIMS,
                                preferred_element_type=F32)
        if late_fix:
            late = jnp.maximum(pos - tqi, 0).astype(F32)
        alphas, probs = [], []
        for h in range(A_HEADS):
            sl = slice(h * tq, (h + 1) * tq)
            sc = s_all[:, sl]
            if late_fix:
                sc = sc - (2.0 * 2.0 ** -(h + 1)) * late
            alpha, p = _flash_probs(sl, jnp.where(sel, sc, NEG_BIG), m_ref)
            alphas.append(alpha)
            probs.append(p)
        pv = jnp.dot(ckvt_ref[:, pl.ds(off, tk)], jnp.concatenate(probs, axis=1),
                     preferred_element_type=F32)
        acc_ref[...] = jnp.concatenate(alphas, axis=1) * acc_ref[...] + pv

    def off_body(kt, carry):
        att_tile(kt, False)
        return carry

    def diag_body(kt, carry):
        att_tile(kt, True)
        return carry

    lax.fori_loop(0, n_off, off_body, 0)
    lax.fori_loop(n_off, n_kt, diag_body, 0)

    inv_l = 1.0 / acc_ref[KV_LORA:KV_LORA + 1, :]
    olat = jnp.concatenate([(acc_ref[:KV_LORA, h * tq:(h + 1) * tq]
                             * inv_l[:, h * tq:(h + 1) * tq]).astype(BF16)
                            for h in range(A_HEADS)], axis=0)
    out_t = jnp.dot(wvt_ref[...], olat, preferred_element_type=F32)
    out_ref[...] = out_t.T.astype(BF16)


def _dsa_call(qa, qidx, widx, kk, ckv, ckvt, wk_bd, wvt_bd, b, s, tq, tk):
    topk = min(INDEX_TOPK_MAX, s // 4)
    assert s % tq == 0 and tq % tk == 0 and tk % LANES == 0 and s & (s - 1) == 0
    blk = lambda w: pl.BlockSpec((None, tq, w), lambda bi, i: (bi, i, 0))
    full = lambda w: pl.BlockSpec((None, s, w), lambda bi, i: (bi, 0, 0))
    const = lambda a: pl.BlockSpec(a.shape, lambda bi, i: (0, 0))
    r3 = lambda a: a.reshape(b, s, a.shape[-1])
    out = pl.pallas_call(
        functools.partial(_dsa_kernel, tq=tq, tk=tk, seq=s, topk=topk),
        grid=(b, s // tq),
        in_specs=[blk(A_WIDTH), blk(IDX_HEADS * IDX_DIM), blk(LANES), full(LANES),
                  full(2 * KV_LORA),
                  pl.BlockSpec((KV_LORA + V_PAD, s), lambda bi, i: (0, bi)),
                  const(wk_bd), const(wvt_bd)],
        out_specs=blk(A_WIDTH),
        out_shape=jax.ShapeDtypeStruct((b, s, A_WIDTH), BF16),
        scratch_shapes=[pltpu.VMEM((s, tq), I32),
                        pltpu.VMEM((s, tq), I16),
                        pltpu.VMEM((s, tq), I16),
                        pltpu.VMEM((IDX_HEADS * tq, LANES), BF16),
                        pltpu.VMEM((A_HEADS * tq, 2 * KV_LORA), BF16),
                        pltpu.VMEM((2, SUBLANES, tq), I32),
                        pltpu.VMEM((1, A_HEADS * tq), F32),
                        pltpu.VMEM((KV_LORA + V_PAD, A_HEADS * tq), F32)],
        compiler_params=_cparams(2),
        name="dsa",
    )(r3(qa), r3(qidx), r3(widx), r3(kk), r3(ckv), ckvt, wk_bd, wvt_bd)
    return out.reshape(b * s, A_WIDTH)


def _block_diag_uk(w_uk):
    h, c, d = w_uk.shape
    eye = jnp.eye(h, dtype=w_uk.dtype)
    return jnp.einsum('hcd,hg->hdgc', w_uk, eye).reshape(h * d, h * c).astype(BF16)


def _block_diag_uv_t(w_uv):
    h, c, d = w_uv.shape
    eye = jnp.eye(h, dtype=w_uv.dtype)
    return jnp.einsum('hcd,hg->hdgc', w_uv, eye).reshape(h * d, h * c).astype(BF16)


def _diff_kernel(qb_ref, kb_ref, vbt_ref, lam_ref, subln_ref, out_ref,
                 qm_ref, m_ref, acc_ref, *, tq, tk, lambda_init):
    i = pl.program_id(1)
    n_off = i * (tq // tk)
    dvp = B_V_DIM + V_PAD
    _split_halves(qb_ref, B_HEADS, qm_ref, tq)
    t_col = i * tq + lax.broadcasted_iota(I32, (tq, LANES), 0)
    for h in range(B_HEADS):
        feat = _slope_features(t_col, 2.0 ** (-2 * (h + 1)), (tq, LANES)).astype(BF16)
        qm_ref[2 * h * tq:(2 * h + 1) * tq, LANES:] = feat
        qm_ref[(2 * h + 1) * tq:(2 * h + 2) * tq, LANES:] = feat

    row = lax.broadcasted_iota(I32, (tk, tq), 0)
    tqi = i * tq + lax.broadcasted_iota(I32, (tk, tq), 1)
    qchunk = lax.shift_right_logical(tqi, CHUNK_SHIFT)
    _flash_init(m_ref, acc_ref)

    def tile_step(kt, diag):
        off = pl.multiple_of(kt * tk, tk)
        s_pairs = [lax.dot_general(kb_ref[pl.ds(off, tk), 2 * h * LANES:(2 * h + 2) * LANES],
                                   qm_ref[2 * h * tq:(2 * h + 2) * tq, :], NT_DIMS,
                                   preferred_element_type=F32) for h in range(B_HEADS)]
        if diag:
            pos = off + row
            adm = lax.shift_right_logical(pos, CHUNK_SHIFT) <= qchunk
            late = jnp.maximum(pos - tqi, 0).astype(F32)
        alphas, probs = [], []
        for h in range(B_HEADS):
            for j in range(2):
                sl = slice((2 * h + j) * tq, (2 * h + j + 1) * tq)
                sc = s_pairs[h][:, j * tq:(j + 1) * tq]
                if diag:
                    sc = jnp.where(adm, sc - (2.0 * 2.0 ** (-2 * (h + 1))) * late, NEG_BIG)
                alpha, p = _flash_probs(sl, sc, m_ref)
                alphas.append(alpha)
                probs.append(p)
        pv = jnp.concatenate(
            [jnp.dot(vbt_ref[h * dvp:(h + 1) * dvp, pl.ds(off, tk)],
                     jnp.concatenate(probs[2 * h:2 * h + 2], axis=1), preferred_element_type=F32)
             for h in range(B_HEADS)], axis=1)
        acc_ref[...] = jnp.concatenate(alphas, axis=1) * acc_ref[...] + pv

    def off_body(kt, carry):
        tile_step(kt, False)
        return carry

    def diag_body(kt, carry):
        tile_step(kt, True)
        return carry

    lax.fori_loop(0, n_off, off_body, 0)
    lax.fori_loop(n_off, n_off + tq // tk, diag_body, 0)

    lp = lam_ref[...]
    lam = (jnp.exp(jnp.sum(lp[0:1] * lp[1:2], axis=1, keepdims=True))
           - jnp.exp(jnp.sum(lp[2:3] * lp[3:4], axis=1, keepdims=True)) + lambda_init)
    o_all = acc_ref[:B_V_DIM, :] * (1.0 / acc_ref[B_V_DIM:B_V_DIM + 1, :])
    outs = []
    for h in range(B_HEADS):
        o = (o_all[:, 2 * h * tq:(2 * h + 1) * tq]
             - lam * o_all[:, (2 * h + 1) * tq:(2 * h + 2) * tq])
        o = o * lax.rsqrt(jnp.mean(o * o, axis=0, keepdims=True) + RMS_EPS) * subln_ref[...]
        outs.append(o * (1.0 - lambda_init))
    out_ref[...] = jnp.concatenate(outs, axis=0).T.astype(BF16)


def _diff_call(qb, kb, vbt, lam_rows, subln, b, s, tq, tk, lambda_init):
    blk = pl.BlockSpec((None, tq, B_WIDTH), lambda bi, i: (bi, i, 0))
    kw = 2 * B_HEADS * LANES
    dvp = B_V_DIM + V_PAD
    out = pl.pallas_call(
        functools.partial(_diff_kernel, tq=tq, tk=tk, lambda_init=lambda_init),
        grid=(b, s // tq),
        in_specs=[blk, pl.BlockSpec((None, s, kw), lambda bi, i: (bi, 0, 0)),
                  pl.BlockSpec((B_HEADS * dvp, s), lambda bi, i: (0, bi)),
                  pl.BlockSpec((SUBLANES, LANES), lambda bi, i: (0, 0)),
                  pl.BlockSpec((B_V_DIM, 1), lambda bi, i: (0, 0))],
        out_specs=blk,
        out_shape=jax.ShapeDtypeStruct((b, s, B_WIDTH), BF16),
        scratch_shapes=[pltpu.VMEM((2 * B_HEADS * tq, 2 * LANES), BF16),
                        pltpu.VMEM((1, 2 * B_HEADS * tq), F32),
                        pltpu.VMEM((dvp, 2 * B_HEADS * tq), F32)],
        compiler_params=_cparams(2),
        name="diff",
    )(qb.reshape(b, s, B_WIDTH), kb.reshape(b, s, kw), vbt, lam_rows, subln.reshape(B_V_DIM, 1))
    return out.reshape(b * s, B_WIDTH)


def _mix_out_kernel(h_ref, a_ref, b_ref, wa_ref, wb_ref, g_ref, beta_ref, o_ref):
    y = (jnp.dot(a_ref[...], wa_ref[...], preferred_element_type=F32)
         + jnp.dot(b_ref[...], wb_ref[...], preferred_element_type=F32))
    o_ref[...] = _layer_norm(ALPHA * h_ref[...] + y, g_ref[...], beta_ref[...])


def _mix_out_call(h, out_a, out_b, w_o, g, beta, tm):
    n, d = h.shape
    w = w_o.astype(BF16)
    row = lambda wd: pl.BlockSpec((tm, wd), lambda i: (i, 0))
    const = lambda shp: pl.BlockSpec(shp, lambda i: (0, 0))
    return pl.pallas_call(
        _mix_out_kernel,
        grid=(n // tm,),
        in_specs=[row(d), row(A_WIDTH), row(B_WIDTH), const((A_WIDTH, d)), const((B_WIDTH, d)),
                  const((1, d)), const((1, d))],
        out_specs=row(d),
        out_shape=jax.ShapeDtypeStruct((n, d), F32),
        compiler_params=_cparams(1),
        name="mix_out",
    )(h, out_a, out_b, w[:A_WIDTH], w[A_WIDTH:], g.reshape(1, d), beta.reshape(1, d))


def _mem_kv_kernel(m_ref, w_ref, k_ref, v_ref):
    x = m_ref[...].astype(BF16)
    d = k_ref.shape[-1]
    k_ref[...] = jnp.dot(x, w_ref[:, :d], preferred_element_type=F32).astype(BF16)
    v_ref[...] = jnp.dot(x, w_ref[:, d:], preferred_element_type=F32).astype(BF16)


def _mem_kv_call(mem2d, wkv, tm):
    n, d = mem2d.shape
    return pl.pallas_call(
        _mem_kv_kernel,
        grid=(n // tm,),
        in_specs=[pl.BlockSpec((tm, d), lambda i: (i, 0)),
                  pl.BlockSpec((d, 2 * d), lambda i: (0, 0))],
        out_specs=[pl.BlockSpec((tm, d), lambda i: (i, 0))] * 2,
        out_shape=[jax.ShapeDtypeStruct((n, d), BF16)] * 2,
        compiler_params=_cparams(1),
        name="mem_kv",
    )(mem2d, wkv.astype(BF16))


def _mem_attn_kernel(h_ref, wq_ref, k_ref, v_ref, wo_ref, g_ref, beta_ref, o_ref):
    h = h_ref[...]
    q = jnp.dot(h.astype(BF16), wq_ref[...], preferred_element_type=F32)
    q = (q * (MEM_HEAD_DIM ** -0.5)).astype(BF16)
    outs = []
    for hd in range(MEM_HEADS):
        sl = slice(hd * MEM_HEAD_DIM, (hd + 1) * MEM_HEAD_DIM)
        sc = lax.dot_general(q[:, sl], k_ref[:, sl], NT_DIMS, preferred_element_type=F32)
        p = jnp.exp(sc - jnp.max(sc, axis=1, keepdims=True))
        den = jnp.sum(p, axis=1, keepdims=True)
        o = jnp.dot(p.astype(BF16), v_ref[:, sl], preferred_element_type=F32) / den
        outs.append(o.astype(BF16))
    y = jnp.dot(jnp.concatenate(outs, axis=1), wo_ref[...], preferred_element_type=F32)
    o_ref[...] = _layer_norm(ALPHA * h + y, g_ref[...], beta_ref[...])


def _mem_attn_call(h, wq, k, v, wo, g, beta, b, s, tm):
    n, d = h.shape
    m = k.shape[0] // b
    const = lambda shp: pl.BlockSpec(shp, lambda bi, i: (0,) * len(shp))
    row = pl.BlockSpec((None, tm, d), lambda bi, i: (bi, i, 0))
    kv = pl.BlockSpec((None, m, d), lambda bi, i: (bi, 0, 0))
    out = pl.pallas_call(
        _mem_attn_kernel,
        grid=(b, s // tm),
        in_specs=[row, const((d, d)), kv, kv, const((d, d)), const((1, d)), const((1, d))],
        out_specs=row,
        out_shape=jax.ShapeDtypeStruct((b, s, d), F32),
        compiler_params=_cparams(2),
        name="mem_attn",
    )(h.reshape(b, s, d), wq.astype(BF16), k.reshape(b, m, d), v.reshape(b, m, d),
      wo.astype(BF16), g.reshape(1, d), beta.reshape(1, d))
    return out.reshape(n, d)


def _router_kernel(h_ref, rw_ref, bias_ref, tri_ref, e_ref, g_ref, rank_ref, cnt_ref, run_ref):
    tm = h_ref.shape[0]

    @pl.when(pl.program_id(0) == 0)
    def _():
        run_ref[...] = jnp.zeros(run_ref.shape, F32)

    logits = jnp.dot(h_ref[...].astype(BF16), rw_ref[...], preferred_element_type=F32)
    scores = 1.0 / (1.0 + jnp.exp(-logits))
    lane = lax.broadcasted_iota(I32, (tm, LANES), 1)
    lanef = lane.astype(F32)
    ninf = jnp.float32(-jnp.inf)
    cur = jnp.where(lane < N_EXPERTS, scores + bias_ref[...], ninf)
    top_e = jnp.zeros((tm, LANES), F32)
    top_s = jnp.zeros((tm, LANES), F32)
    picked = jnp.zeros((tm, LANES), F32)
    firsts = []
    for k in range(TOP_K):
        mx = jnp.max(cur, axis=1, keepdims=True)
        first = jnp.min(jnp.where(cur == mx, lanef, float(LANES)), axis=1, keepdims=True)
        hit = lanef == first
        s_k = jnp.sum(jnp.where(hit, scores, 0.0), axis=1, keepdims=True)
        top_e = jnp.where(lane == k, first, top_e)
        top_s = jnp.where(lane == k, s_k, top_s)
        picked = jnp.where(hit, 1.0, picked)
        cur = jnp.where(hit, ninf, cur)
        firsts.append(first)
    e_ref[...] = top_e.astype(I32)
    g_ref[...] = top_s / jnp.sum(top_s, axis=1, keepdims=True) * ROUTED_SCALE

    slot = jnp.dot(tri_ref[...], picked.astype(BF16), preferred_element_type=F32) + run_ref[...]
    rank = jnp.zeros((tm, LANES), F32)
    for k in range(TOP_K):
        r_k = jnp.sum(jnp.where(lanef == firsts[k], slot, 0.0), axis=1, keepdims=True)
        rank = jnp.where(lane == k, r_k, rank)
    rank_ref[...] = rank.astype(I32)
    run = run_ref[...] + jnp.sum(picked, axis=0, keepdims=True)
    run_ref[...] = run
    cnt_ref[...] = run.astype(I32)


def _router_call(h, router_w, router_bias, tm):
    n, d = h.shape
    rw = jnp.pad(router_w, ((0, 0), (0, LANES - N_EXPERTS))).astype(BF16)
    bias = jnp.pad(router_bias, (0, LANES - N_EXPERTS)).reshape(1, LANES)
    tri = jnp.tri(tm, k=-1, dtype=BF16)
    row = pl.BlockSpec((tm, LANES), lambda i: (i, 0))
    const = lambda shp: pl.BlockSpec(shp, lambda i: (0, 0))
    return pl.pallas_call(
        _router_kernel,
        grid=(n // tm,),
        in_specs=[pl.BlockSpec((tm, d), lambda i: (i, 0)), const((d, LANES)), const((1, LANES)),
                  const((tm, tm))],
        out_specs=[row, row, row, const((1, LANES))],
        out_shape=[jax.ShapeDtypeStruct((n, LANES), I32),
                   jax.ShapeDtypeStruct((n, LANES), F32),
                   jax.ShapeDtypeStruct((n, LANES), I32),
                   jax.ShapeDtypeStruct((1, LANES), I32)],
        scratch_shapes=[pltpu.VMEM((1, LANES), F32)],
        compiler_params=_cparams(1),
        name="router",
    )(h, rw, bias, tri)


def _dest_kernel(e_ref, rank_ref, gs_ref, o_ref):
    tm = e_ref.shape[0]
    lane = lax.broadcasted_iota(I32, (tm, LANES), 1)
    e = e_ref[...]
    gs = gs_ref[...].astype(F32)
    start = jnp.zeros((tm, LANES), F32)
    for k in range(TOP_K):
        s_k = jnp.sum(jnp.where(lane == e[:, k:k + 1], gs, 0.0), axis=1, keepdims=True)
        start = jnp.where(lane == k, s_k, start)
    o_ref[...] = start.astype(I32) + rank_ref[...]


def _dest_call(top_e, rank, gstart, tm):
    n = top_e.shape[0]
    row = pl.BlockSpec((tm, LANES), lambda i: (i, 0))
    return pl.pallas_call(
        _dest_kernel,
        grid=(n // tm,),
        in_specs=[row, row, pl.BlockSpec((1, LANES), lambda i: (0, 0))],
        out_specs=row,
        out_shape=jax.ShapeDtypeStruct((n, LANES), I32),
        compiler_params=_cparams(1),
        name="dest",
    )(top_e, rank, gstart)


def _dispatch_kernel(dest_ref, x_ref, xs_ref, sem, *, td):
    def row_copy(r, k):
        return pltpu.make_async_copy(x_ref.at[pl.ds(r, 1), :],
                                     xs_ref.at[pl.ds(dest_ref[r, k], 1), :], sem)

    for r in range(td):
        for k in range(TOP_K):
            row_copy(r, k).start(priority=k % N_DMA_QUEUES)

    def drain(r, carry):
        for k in range(TOP_K):
            row_copy(r, k).wait()
        return carry

    lax.fori_loop(0, td, drain, 0)


def _dispatch_call(h, dest, td):
    n, d = h.shape
    m = n * TOP_K
    return pl.pallas_call(
        functools.partial(_dispatch_kernel, td=td),
        grid=(n // td,),
        in_specs=[pl.BlockSpec((td, LANES), lambda i: (i, 0), memory_space=pltpu.SMEM),
                  pl.BlockSpec((td, d), lambda i: (i, 0))],
        out_specs=pl.BlockSpec(memory_space=pl.ANY),
        out_shape=jax.ShapeDtypeStruct((m, d), F32),
        scratch_shapes=[pltpu.SemaphoreType.DMA(())],
        compiler_params=_cparams(1),
        name="dispatch",
    )(dest, h)


def _moe_mm_kernel(ptile_ref, pexp_ref, plo_ref, phi_ref, xs_ref, wg_ref, wu_ref, wd_ref, y_ref,
                   *, tmm):
    p = pl.program_id(0)
    tile = ptile_ref[p]
    lo = plo_ref[p]
    hi = phi_ref[p]
    first = jnp.logical_or(p == 0, ptile_ref[jnp.maximum(p - 1, 0)] != tile)

    def expert_out(rows):
        x = xs_ref[rows, :].astype(BF16)
        g = jnp.dot(x, wg_ref[...].astype(BF16), preferred_element_type=F32)
        u = jnp.dot(x, wu_ref[...].astype(BF16), preferred_element_type=F32)
        hid = (_silu(g) * u).astype(BF16)
        return jnp.dot(hid, wd_ref[...].astype(BF16), preferred_element_type=F32)

    whole = jnp.logical_and(lo <= tile * tmm, hi >= (tile + 1) * tmm)

    @pl.when(whole)
    def _():
        y_ref[...] = expert_out(slice(None))

    @pl.when(jnp.logical_and(first, jnp.logical_not(whole)))
    def _():
        y_ref[...] = jnp.zeros(y_ref.shape, F32)

    sub = min(tmm, MOE_SUB_ROWS)
    for j in range(tmm // sub):
        r0 = tile * tmm + j * sub

        @pl.when(jnp.logical_and(jnp.logical_not(whole),
                                 jnp.logical_and(lo < r0 + sub, hi > r0)))
        def _():
            rows = slice(j * sub, (j + 1) * sub)
            rowg = r0 + lax.broadcasted_iota(I32, (sub, 1), 0)
            y_ref[rows, :] += jnp.where(jnp.logical_and(rowg >= lo, rowg < hi),
                                        expert_out(rows), 0.0)


def _moe_mm_call(xs, pairs, wg, wu, wd, tmm):
    m = xs.shape[0]
    n_pairs = pairs[0].shape[0]
    rows = pl.BlockSpec((tmm, D_MODEL), lambda p, pt, pe, plo, phi: (pt[p], 0))
    wspec = lambda shp: pl.BlockSpec((None,) + shp, lambda p, pt, pe, plo, phi: (pe[p], 0, 0))
    return pl.pallas_call(
        functools.partial(_moe_mm_kernel, tmm=tmm),
        grid_spec=pltpu.PrefetchScalarGridSpec(
            num_scalar_prefetch=4,
            grid=(n_pairs,),
            in_specs=[rows, wspec((D_MODEL, EXPERT_DIM)), wspec((D_MODEL, EXPERT_DIM)),
                      wspec((EXPERT_DIM, D_MODEL))],
            out_specs=rows),
        out_shape=jax.ShapeDtypeStruct((m, D_MODEL), F32),
        compiler_params=_cparams(1),
        name="moe_mm",
    )(*pairs, xs, wg, wu, wd)


def _group_pairs(counts, m, tmm):
    n_tiles = m // tmm
    n_pairs = n_tiles + N_EXPERTS
    gend = jnp.cumsum(counts)
    gstart = gend - counts
    t0 = jnp.arange(n_tiles, dtype=I32) * tmm
    n_le = lambda edges, v: jnp.sum(edges[None, :] <= v[:, None], axis=1).astype(I32)
    e_first = n_le(gend, t0)
    e_last = n_le(gend, t0 + (tmm - 1))
    per_tile = e_last - e_first + 1
    pend = jnp.cumsum(per_tile)
    pstart = pend - per_tile
    p = jnp.arange(n_pairs, dtype=I32)
    valid = p < pend[-1]
    tile = jnp.minimum(n_le(pend, p), n_tiles - 1)
    e = jnp.clip(e_first[tile] + (p - pstart[tile]), 0, N_EXPERTS - 1).astype(I32)
    lo = jnp.where(valid, gstart[e], 0).astype(I32)
    hi = jnp.where(valid, gend[e], 0).astype(I32)
    return tile, e, lo, hi


def _combine_kernel(dest_ref, gate_ref, h_ref, y_ref, sg_ref, su_ref, sd_ref, g_ref, beta_ref, o_ref,
                    buf_ref, sem, *, tc):
    def row_copy(r, k):
        return pltpu.make_async_copy(y_ref.at[pl.ds(dest_ref[r, k], 1), :],
                                     buf_ref.at[k, pl.ds(r, 1), :], sem)

    for r in range(tc):
        for k in range(TOP_K):
            row_copy(r, k).start(priority=k % N_DMA_QUEUES)

    h = h_ref[...]
    x = h.astype(BF16)
    hid = (_silu(jnp.dot(x, sg_ref[...], preferred_element_type=F32))
           * jnp.dot(x, su_ref[...], preferred_element_type=F32)).astype(BF16)
    shared = jnp.dot(hid, sd_ref[...], preferred_element_type=F32)

    def drain(r, carry):
        for k in range(TOP_K):
            row_copy(r, k).wait()
        return carry

    lax.fori_loop(0, tc, drain, 0)

    gates = gate_ref[...]
    routed = gates[:, 0:1] * buf_ref[0]
    for k in range(1, TOP_K):
        routed = routed + gates[:, k:k + 1] * buf_ref[k]
    o_ref[...] = _layer_norm(ALPHA * h + (routed + shared), g_ref[...], beta_ref[...])


def _combine_call(h, y, dest, gates, sg, su, sd, g, beta, tc):
    n, d = h.shape
    smem = pl.BlockSpec((tc, LANES), lambda i: (i, 0), memory_space=pltpu.SMEM)
    const = lambda shp: pl.BlockSpec(shp, lambda i: (0, 0))
    return pl.pallas_call(
        functools.partial(_combine_kernel, tc=tc),
        grid=(n // tc,),
        in_specs=[smem, pl.BlockSpec((tc, LANES), lambda i: (i, 0)),
                  pl.BlockSpec((tc, d), lambda i: (i, 0)),
                  pl.BlockSpec(memory_space=pl.ANY),
                  const((d, SHARED_DIM)), const((d, SHARED_DIM)), const((SHARED_DIM, d)),
                  const((1, d)), const((1, d))],
        out_specs=pl.BlockSpec((tc, d), lambda i: (i, 0)),
        out_shape=jax.ShapeDtypeStruct((n, d), F32),
        scratch_shapes=[pltpu.VMEM((TOP_K, tc, d), F32),
                        pltpu.SemaphoreType.DMA(())],
        compiler_params=_cparams(1),
        name="combine",
    )(dest, gates, h, y, sg.astype(BF16), su.astype(BF16), sd.astype(BF16),
      g.reshape(1, d), beta.reshape(1, d))


def _tiles(b, s):
    n = b * s
    pick = lambda pref, total: next(t for t in (pref, 512, 256, 128, 64, 32, 16, 8) if t <= pref and total % t == 0)
    return dict(
        rows=pick(512, n),
        attn_q=pick(512, s),
        attn_k=pick(256, s),
        mem_rows=pick(512, s),
        mem_kv=pick(512, b * 256),
        router=pick(1024, n),
        dispatch=pick(128, n),
        moe=pick(512, n * TOP_K),
        combine=pick(128, n),
    )


def kernel(x, mem, ln_in_g, ln_in_b, w_in, a_kv_norm, a_w_uk, a_w_uv, b_lq1, b_lk1, b_lq2, b_lk2,
           b_subln, w_o, ln1_g, ln1_b, m_wq, m_wkv, m_wo, ln2_g, ln2_b, router_w, router_bias,
           e_w_gate, e_w_up, e_w_down, s_w_gate, s_w_up, s_w_down, ln3_g, ln3_b):
    b, s, d = x.shape
    n = b * s
    m = n * TOP_K
    t = _tiles(b, s)
    mem2d = mem.reshape(-1, d)

    h = _ln_call(x.reshape(n, d), ln_in_g, ln_in_b, t["rows"])
    for l in range(DEPTH):
        lambda_init = 0.8 - 0.6 * math.exp(-0.3 * l)

        qa, ckv, ckvt, qidx, kk, widx, qb, kb, vbt = _proj_call(h, _pad_w_in(w_in[l]), a_kv_norm[l],
                                                                  t["rows"], s)
        out_a = _dsa_call(qa, qidx, widx, kk, ckv, ckvt, _block_diag_uk(a_w_uk[l]),
                          _block_diag_uv_t(a_w_uv[l]), b, s, t["attn_q"], t["attn_k"])
        lam_rows = jnp.pad(jnp.stack([b_lq1[l], b_lk1[l], b_lq2[l], b_lk2[l]]),
                           ((0, SUBLANES - 4), (0, LANES - B_QK_DIM)))
        out_b = _diff_call(qb, kb, vbt, lam_rows, b_subln[l], b, s, t["attn_q"], t["attn_k"], lambda_init)
        h = _mix_out_call(h, out_a, out_b, w_o[l], ln1_g[l], ln1_b[l], t["rows"])

        mk, mv = _mem_kv_call(mem2d, m_wkv[l], t["mem_kv"])
        h = _mem_attn_call(h, m_wq[l], mk, mv, m_wo[l], ln2_g[l], ln2_b[l], b, s, t["mem_rows"])

        top_e, gates, rank, counts = _router_call(h, router_w[l], router_bias[l], t["router"])
        gstart = jnp.cumsum(counts, axis=1) - counts
        dest = _dest_call(top_e, rank, gstart, t["router"])
        counts = counts[0, :N_EXPERTS]
        xs = _dispatch_call(h, dest, t["dispatch"])
        pairs = _group_pairs(counts, m, t["moe"])
        y = _moe_mm_call(xs, pairs, e_w_gate[l], e_w_up[l], e_w_down[l], t["moe"])
        h = _combine_call(h, y, dest, gates, s_w_gate[l], s_w_up[l], s_w_down[l],
                          ln3_g[l], ln3_b[l], t["combine"])
    return h.reshape(b, s, d)
```

```python
import functools
import math

import jax
import jax.numpy as jnp
import numpy as np
from jax import lax
from jax.experimental import pallas as pl
from jax.experimental.pallas import tpu as pltpu

F32 = jnp.float32
BF16 = jnp.bfloat16
I32 = jnp.int32
I16 = jnp.int16

D_MODEL = 1024
DEPTH = 2
CHUNK = 64
CHUNK_SHIFT = 6
A_HEADS = 8
A_HEAD_DIM = 64
A_WIDTH = A_HEADS * A_HEAD_DIM
KV_LORA = 128
IDX_HEADS = 8
IDX_DIM = 64
INDEX_TOPK_MAX = 256
B_HEADS = 4
B_QK_DIM = 64
B_V_DIM = 128
B_WIDTH = B_HEADS * B_V_DIM
MEM_HEADS = 4
MEM_HEAD_DIM = D_MODEL // MEM_HEADS
N_EXPERTS = 64
TOP_K = 8
EXPERT_DIM = 256
SHARED_DIM = 256
ROUTED_SCALE = 2.5
ALPHA = (2 * DEPTH) ** 0.25
LN_EPS = 1e-5
RMS_EPS = 1e-6

LANES = 128
SUBLANES = 8
VMEM_LIMIT = 56 * 1024 * 1024
INT_MIN = -2 ** 31
I16_MIN = -2 ** 15
V_PAD = 16
N_DMA_QUEUES = 2
NEG_BIG = -3.0e38
M_INIT = -1.0e30

NT_DIMS = (((1,), (1,)), ((), ()))


def _cparams(n_axes):
    return pltpu.CompilerParams(dimension_semantics=("arbitrary",) * n_axes,
                                vmem_limit_bytes=VMEM_LIMIT)


def _layer_norm(x, g, b):
    mu = jnp.mean(x, axis=-1, keepdims=True)
    xc = x - mu
    var = jnp.mean(xc * xc, axis=-1, keepdims=True)
    return xc * lax.rsqrt(var + LN_EPS) * g + b


def _silu(x):
    return x * (1.0 / (1.0 + jnp.exp(-x)))


def _ln_kernel(x_ref, g_ref, b_ref, o_ref):
    o_ref[...] = _layer_norm(x_ref[...], g_ref[...], b_ref[...])


def _ln_call(x, g, b, tm):
    n, d = x.shape
    return pl.pallas_call(
        _ln_kernel,
        grid=(n // tm,),
        in_specs=[pl.BlockSpec((tm, d), lambda i: (i, 0)),
                  pl.BlockSpec((1, d), lambda i: (0, 0)),
                  pl.BlockSpec((1, d), lambda i: (0, 0))],
        out_specs=pl.BlockSpec((tm, d), lambda i: (i, 0)),
        out_shape=jax.ShapeDtypeStruct((n, d), F32),
        compiler_params=_cparams(1),
        name="ln_in",
    )(x, g.reshape(1, d), b.reshape(1, d))


_SEG_QA = (0, 512)
_SEG_CKV = (512, 640)
_SEG_QIDX = (640, 1152)
_SEG_KK = (1152, 1280)
_SEG_WIDX = (1280, 1408)
_SEG_QB = (1408, 1920)
_SEG_KB = (1920, 2432)
_SEG_VB = (2432, 2944)
_PROJ_COLS = 2944


def _pos_features(pos, shape):
    lane = lax.broadcasted_iota(I32, shape, 1)
    hi = lax.shift_right_logical(pos, CHUNK_SHIFT).astype(F32)
    lo = (pos & (CHUNK - 1)).astype(F32)
    return jnp.where(lane < 2, 1.0, jnp.where(lane == 2, hi, jnp.where(lane == 3, lo, 0.0)))


def _slope_features(tqi, slope, shape):
    lane = lax.broadcasted_iota(I32, shape, 1)
    hi = lax.shift_right_logical(tqi, CHUNK_SHIFT).astype(F32)
    lo = (tqi & (CHUNK - 1)).astype(F32)
    return jnp.where(lane == 0, -slope * CHUNK * hi,
                     jnp.where(lane == 1, -slope * lo,
                               jnp.where(lane == 2, slope * CHUNK,
                                         jnp.where(lane == 3, slope, 0.0))))


def _with_ones_rows(vt):
    t = vt.shape[1]
    sub = lax.broadcasted_iota(I32, (V_PAD, t), 0)
    return jnp.concatenate([vt, jnp.where(sub == 0, 1.0, 0.0)], axis=0)


def _proj_kernel(h_ref, w_ref, kvn_ref, qa_ref, ckv_ref, ckvt_ref, qidx_ref, kk_ref, widx_ref,
                 qb_ref, kb_ref, vbt_ref, *, seq):
    tm = h_ref.shape[0]
    x = h_ref[...].astype(BF16)

    def seg(s):
        return jnp.dot(x, w_ref[:, s[0]:s[1]], preferred_element_type=F32)

    row = pl.program_id(0) * tm + lax.broadcasted_iota(I32, (tm, LANES), 0)
    feat = _pos_features(row & (seq - 1), (tm, LANES)).astype(BF16)

    qa_ref[...] = (seg(_SEG_QA) * (A_HEAD_DIM ** -0.5)).astype(BF16)
    c = seg(_SEG_CKV)
    c = c * lax.rsqrt(jnp.mean(c * c, axis=-1, keepdims=True) + RMS_EPS) * kvn_ref[...]
    ckv_ref[:, :KV_LORA] = c.astype(BF16)
    ckv_ref[:, KV_LORA:] = feat
    ckvt_ref[...] = _with_ones_rows(c.T).astype(BF16)
    qidx_ref[...] = (seg(_SEG_QIDX) * (IDX_DIM ** -0.5)).astype(BF16)
    kk_ref[...] = seg(_SEG_KK).astype(BF16)
    widx_ref[...] = seg(_SEG_WIDX) * (IDX_HEADS ** -0.5)
    qb_ref[...] = (seg(_SEG_QB) * (B_QK_DIM ** -0.5)).astype(BF16)
    kb = seg(_SEG_KB).astype(BF16)
    vt = seg(_SEG_VB).T
    for h in range(B_HEADS):
        kb_ref[:, 2 * h * LANES:(2 * h + 1) * LANES] = kb[:, h * LANES:(h + 1) * LANES]
        kb_ref[:, (2 * h + 1) * LANES:(2 * h + 2) * LANES] = feat
        vbt_ref[h * (B_V_DIM + V_PAD):(h + 1) * (B_V_DIM + V_PAD), :] = _with_ones_rows(
            vt[h * B_V_DIM:(h + 1) * B_V_DIM]).astype(BF16)


def _proj_call(h, w_pad, kvn, tm, seq):
    n, d = h.shape
    assert seq & (seq - 1) == 0 and seq <= CHUNK * 256 and seq % tm == 0
    outs = [(512, BF16, False), (2 * KV_LORA, BF16, False), (KV_LORA + V_PAD, BF16, True),
            (512, BF16, False), (128, BF16, False), (128, F32, False), (512, BF16, False),
            (2 * B_HEADS * LANES, BF16, False), (B_HEADS * (B_V_DIM + V_PAD), BF16, True)]
    spec = lambda w, t: (pl.BlockSpec((w, tm), lambda i: (0, i)) if t
                         else pl.BlockSpec((tm, w), lambda i: (i, 0)))
    shape = lambda w, dt, t: jax.ShapeDtypeStruct((w, n) if t else (n, w), dt)
    return pl.pallas_call(
        functools.partial(_proj_kernel, seq=seq),
        grid=(n // tm,),
        in_specs=[pl.BlockSpec((tm, d), lambda i: (i, 0)),
                  pl.BlockSpec((d, _PROJ_COLS), lambda i: (0, 0)),
                  pl.BlockSpec((1, KV_LORA), lambda i: (0, 0))],
        out_specs=[spec(w, t) for w, _, t in outs],
        out_shape=[shape(w, dt, t) for w, dt, t in outs],
        compiler_params=_cparams(1),
        name="proj_in",
    )(h, w_pad, kvn.reshape(1, KV_LORA))


def _pad_w_in(w_in):
    sizes = (A_WIDTH, KV_LORA, IDX_HEADS * IDX_DIM, IDX_DIM, IDX_HEADS,
             2 * B_HEADS * B_QK_DIM, 2 * B_HEADS * B_QK_DIM, B_WIDTH)
    splits = np.cumsum(sizes)[:-1].tolist()
    q_a, c_kv, q_idx, k_idx, w_idx, q_b, k_b, v_b = jnp.split(w_in, splits, axis=-1)
    w_idx = jnp.pad(w_idx, ((0, 0), (0, LANES - IDX_HEADS)))
    return jnp.concatenate([q_a, c_kv, q_idx, k_idx, k_idx, w_idx, q_b, k_b, v_b],
                           axis=-1).astype(BF16)


def _flash_probs(sl, sc, m_ref):
    m_old = m_ref[:, sl]
    m_new = jnp.maximum(m_old, jnp.max(sc, axis=0, keepdims=True))
    m_ref[:, sl] = m_new
    return jnp.exp(m_old - m_new), jnp.exp(sc - m_new).astype(BF16)


def _flash_init(m_ref, acc_ref):
    m_ref[...] = jnp.full(m_ref.shape, M_INIT, F32)
    acc_ref[...] = jnp.zeros(acc_ref.shape, F32)


def _split_halves(q_ref, n_pairs, out_ref, rows):
    lane = lax.broadcasted_iota(I32, (rows, LANES), 1)
    for j in range(n_pairs):
        qp = q_ref[:, j * LANES:(j + 1) * LANES].astype(F32)
        out_ref[2 * j * rows:(2 * j + 1) * rows, :LANES] = jnp.where(lane < 64, qp, 0.0).astype(BF16)
        out_ref[(2 * j + 1) * rows:(2 * j + 2) * rows, :LANES] = jnp.where(lane >= 64, qp, 0.0).astype(BF16)


def _dsa_kernel(qa_ref, qidx_ref, widx_ref, kk_ref, ckv_ref, ckvt_ref, wk_ref, wvt_ref, out_ref,
                keys_ref, khi_ref, klo_ref, qim_ref, qlat_ref, tj_ref, m_ref, acc_ref,
                *, tq, tk, seq, topk):
    i = pl.program_id(1)
    n_off = i * (tq // tk)
    n_kt = n_off + tq // tk
    idx_bits = int(math.log2(seq))

    _split_halves(qidx_ref, IDX_HEADS // 2, qim_ref, tq)
    qlat = jnp.dot(qa_ref[...], wk_ref[...], preferred_element_type=F32)
    t_col = i * tq + lax.broadcasted_iota(I32, (tq, LANES), 0)
    for h in range(A_HEADS):
        qlat_ref[h * tq:(h + 1) * tq, :KV_LORA] = qlat[:, h * KV_LORA:(h + 1) * KV_LORA].astype(BF16)
        qlat_ref[h * tq:(h + 1) * tq, KV_LORA:] = _slope_features(
            t_col, 2.0 ** -(h + 1), (tq, LANES)).astype(BF16)
    w_t = widx_ref[...].T

    row = lax.broadcasted_iota(I32, (tk, tq), 0)
    tqi = i * tq + lax.broadcasted_iota(I32, (tk, tq), 1)
    qchunk = lax.shift_right_logical(tqi, CHUNK_SHIFT)

    def idx_body(kt, carry):
        off = pl.multiple_of(kt * tk, tk)
        kkt = kk_ref[pl.ds(off, tk), :]
        s_all = lax.dot_general(kkt, qim_ref[...], NT_DIMS, preferred_element_type=F32)
        acc = jnp.zeros((tk, tq), F32)
        for h in range(IDX_HEADS):
            acc = acc + w_t[h:h + 1, :] * jnp.maximum(s_all[:, h * tq:(h + 1) * tq], 0.0)
        bits = lax.bitcast_convert_type(acc + 0.0, I32)
        key = jnp.where(bits < 0, bits ^ jnp.int32(0x7FFFFFFF), bits)
        adm = lax.shift_right_logical(off + row, CHUNK_SHIFT) <= qchunk
        key = jnp.where(adm, key, jnp.int32(INT_MIN))
        keys_ref[pl.ds(off, tk), :] = key
        khi_ref[pl.ds(off, tk), :] = lax.shift_right_arithmetic(key, 16).astype(I16)
        klo_ref[pl.ds(off, tk), :] = ((key & 0xFFFF) + I16_MIN).astype(I16)
        return carry

    lax.fori_loop(0, n_kt, idx_body, 0)

    def count(pred):
        def body(kt, c):
            off = pl.multiple_of(kt * tk, tk)
            ind = jnp.where(pred(keys_ref[pl.ds(off, tk), :], off + row), 1.0, 0.0)
            return c + jnp.sum(ind, axis=0, keepdims=True)
        return lax.fori_loop(0, n_kt, body, jnp.zeros((1, tq), F32))

    def count16(ref, pred):
        pack = 2 * SUBLANES
        def body(kt, c):
            off = pl.multiple_of(kt * tk, tk)
            ind = jnp.where(pred(ref[pl.ds(off, tk), :]), jnp.int16(1), jnp.int16(0))
            parts = [ind[j * pack:(j + 1) * pack] for j in range(tk // pack)]
            while len(parts) > 1:
                parts = [parts[j] + parts[j + 1] for j in range(0, len(parts), 2)]
            return c + parts[0]
        c = lax.fori_loop(0, n_kt, body, jnp.zeros((pack, tq), I16))
        return jnp.sum(c.astype(F32), axis=0, keepdims=True)

    def search16(ref, need):
        def body(it, t):
            cand = t + lax.shift_left(jnp.int32(1), 15 - it)
            cand16 = cand.astype(I16)
            cnt = count16(ref, lambda tile: tile >= cand16)
            return jnp.where(cnt >= need, cand, t)
        return lax.fori_loop(0, 16, body, jnp.full((1, tq), I16_MIN, I32))

    t_hi = search16(khi_ref, topk)
    t_hi16 = t_hi.astype(I16)
    n_above = count16(khi_ref, lambda tile: tile > t_hi16)

    def park_body(kt, carry):
        off = pl.multiple_of(kt * tk, tk)
        klo_ref[pl.ds(off, tk), :] = jnp.where(khi_ref[pl.ds(off, tk), :] == t_hi16,
                                               klo_ref[pl.ds(off, tk), :], jnp.int16(I16_MIN))
        return carry

    lax.fori_loop(0, n_kt, park_body, 0)
    t_lo = search16(klo_ref, topk - n_above)
    thr = t_hi * 65536 + (t_lo - I16_MIN)

    live = thr > INT_MIN
    n_ge = count(lambda tile, _: tile >= thr)
    excess = jnp.logical_and(n_ge > topk, live)
    all_ties = jnp.where(live, seq, -1).astype(I32)
    tj_ref[0] = jnp.broadcast_to(thr, (SUBLANES, tq))
    tj_ref[1] = jnp.broadcast_to(all_ties, (SUBLANES, tq))

    @pl.when(jnp.max(jnp.where(excess, 1.0, 0.0)) > 0.0)
    def _():
        need = topk - count(lambda tile, _: tile > thr)

        def tie_body(it, hi):
            cand = hi + lax.shift_left(jnp.int32(1), idx_bits - 1 - it)
            cnt = count(lambda tile, pos: jnp.logical_and(tile == thr, pos < cand))
            return jnp.where(cnt < need, cand, hi)

        hi = lax.fori_loop(0, idx_bits, tie_body, jnp.zeros((1, tq), I32))
        tj_ref[1] = jnp.broadcast_to(jnp.where(excess, hi, all_ties), (SUBLANES, tq))

    thr_b = tj_ref[0][:1]
    tie_hi = tj_ref[1][:1]

    _flash_init(m_ref, acc_ref)

    def att_tile(kt, late_fix):
        off = pl.multiple_of(kt * tk, tk)
        tile = keys_ref[pl.ds(off, tk), :]
        pos = off + row
        sel = jnp.logical_or(tile > thr_b,
                             jnp.logical_and(tile == thr_b, pos <= tie_hi))
        s_all = lax.dot_general(ckv_ref[pl.ds(off, tk), :], qlat_ref[...], NT_DIMS,
                                preferred_element_type=F32)
        if late_fix:
            late = jnp.maximum(pos - tqi, 0).astype(F32)
        alphas, probs = [], []
        for h in range(A_HEADS):
            sl = slice(h * tq, (h + 1) * tq)
            sc = s_all[:, sl]
            if late_fix:
                sc = sc - (2.0 * 2.0 ** -(h + 1)) * late
            alpha, p = _flash_probs(sl, jnp.where(sel, sc, NEG_BIG), m_ref)
            alphas.append(alpha)
            probs.append(p)
        pv = jnp.dot(ckvt_ref[:, pl.ds(off, tk)], jnp.concatenate(probs, axis=1),
                     preferred_element_type=F32)
        acc_ref[...] = jnp.concatenate(alphas, axis=1) * acc_ref[...] + pv

    def off_body(kt, carry):
        att_tile(kt, False)
        return carry

    def diag_body(kt, carry):
        att_tile(kt, True)
        return carry

    lax.fori_loop(0, n_off, off_body, 0)
    lax.fori_loop(n_off, n_kt, diag_body, 0)

    inv_l = 1.0 / acc_ref[KV_LORA:KV_LORA + 1, :]
    olat = jnp.concatenate([(acc_ref[:KV_LORA, h * tq:(h + 1) * tq]
                             * inv_l[:, h * tq:(h + 1) * tq]).astype(BF16)
                            for h in range(A_HEADS)], axis=0)
    out_t = jnp.dot(wvt_ref[...], olat, preferred_element_type=F32)
    out_ref[...] = out_t.T.astype(BF16)


def _dsa_call(qa, qidx, widx, kk, ckv, ckvt, wk_bd, wvt_bd, b, s, tq, tk):
    topk = min(INDEX_TOPK_MAX, s // 4)
    assert s % tq == 0 and tq % tk == 0 and tk % LANES == 0 and s & (s - 1) == 0
    blk = lambda w: pl.BlockSpec((None, tq, w), lambda bi, i: (bi, i, 0))
    full = lambda w: pl.BlockSpec((None, s, w), lambda bi, i: (bi, 0, 0))
    const = lambda a: pl.BlockSpec(a.shape, lambda bi, i: (0, 0))
    r3 = lambda a: a.reshape(b, s, a.shape[-1])
    out = pl.pallas_call(
        functools.partial(_dsa_kernel, tq=tq, tk=tk, seq=s, topk=topk),
        grid=(b, s // tq),
        in_specs=[blk(A_WIDTH), blk(IDX_HEADS * IDX_DIM), blk(LANES), full(LANES),
                  full(2 * KV_LORA),
                  pl.BlockSpec((KV_LORA + V_PAD, s), lambda bi, i: (0, bi)),
                  const(wk_bd), const(wvt_bd)],
        out_specs=blk(A_WIDTH),
        out_shape=jax.ShapeDtypeStruct((b, s, A_WIDTH), BF16),
        scratch_shapes=[pltpu.VMEM((s, tq), I32),
                        pltpu.VMEM((s, tq), I16),
                        pltpu.VMEM((s, tq), I16),
                        pltpu.VMEM((IDX_HEADS * tq, LANES), BF16),
                        pltpu.VMEM((A_HEADS * tq, 2 * KV_LORA), BF16),
                        pltpu.VMEM((2, SUBLANES, tq), I32),
                        pltpu.VMEM((1, A_HEADS * tq), F32),
                        pltpu.VMEM((KV_LORA + V_PAD, A_HEADS * tq), F32)],
        compiler_params=_cparams(2),
        name="dsa",
    )(r3(qa), r3(qidx), r3(widx), r3(kk), r3(ckv), ckvt, wk_bd, wvt_bd)
    return out.reshape(b * s, A_WIDTH)


def _block_diag_uk(w_uk):
    h, c, d = w_uk.shape
    eye = jnp.eye(h, dtype=w_uk.dtype)
    return jnp.einsum('hcd,hg->hdgc', w_uk, eye).reshape(h * d, h * c).astype(BF16)


def _block_diag_uv_t(w_uv):
    h, c, d = w_uv.shape
    eye = jnp.eye(h, dtype=w_uv.dtype)
    return jnp.einsum('hcd,hg->hdgc', w_uv, eye).reshape(h * d, h * c).astype(BF16)


def _diff_kernel(qb_ref, kb_ref, vbt_ref, lam_ref, subln_ref, out_ref,
                 qm_ref, m_ref, acc_ref, *, tq, tk, lambda_init):
    i = pl.program_id(1)
    n_off = i * (tq // tk)
    dvp = B_V_DIM + V_PAD
    _split_halves(qb_ref, B_HEADS, qm_ref, tq)
    t_col = i * tq + lax.broadcasted_iota(I32, (tq, LANES), 0)
    for h in range(B_HEADS):
        feat = _slope_features(t_col, 2.0 ** (-2 * (h + 1)), (tq, LANES)).astype(BF16)
        qm_ref[2 * h * tq:(2 * h + 1) * tq, LANES:] = feat
        qm_ref[(2 * h + 1) * tq:(2 * h + 2) * tq, LANES:] = feat

    row = lax.broadcasted_iota(I32, (tk, tq), 0)
    tqi = i * tq + lax.broadcasted_iota(I32, (tk, tq), 1)
    qchunk = lax.shift_right_logical(tqi, CHUNK_SHIFT)
    _flash_init(m_ref, acc_ref)

    def tile_step(kt, diag):
        off = pl.multiple_of(kt * tk, tk)
        s_pairs = [lax.dot_general(kb_ref[pl.ds(off, tk), 2 * h * LANES:(2 * h + 2) * LANES],
                                   qm_ref[2 * h * tq:(2 * h + 2) * tq, :], NT_DIMS,
                                   preferred_element_type=F32) for h in range(B_HEADS)]
        if diag:
            pos = off + row
            adm = lax.shift_right_logical(pos, CHUNK_SHIFT) <= qchunk
            late = jnp.maximum(pos - tqi, 0).astype(F32)
        alphas, probs = [], []
        for h in range(B_HEADS):
            for j in range(2):
                sl = slice((2 * h + j) * tq, (2 * h + j + 1) * tq)
                sc = s_pairs[h][:, j * tq:(j + 1) * tq]
                if diag:
                    sc = jnp.where(adm, sc - (2.0 * 2.0 ** (-2 * (h + 1))) * late, NEG_BIG)
                alpha, p = _flash_probs(sl, sc, m_ref)
                alphas.append(alpha)
                probs.append(p)
        pv = jnp.concatenate(
            [jnp.dot(vbt_ref[h * dvp:(h + 1) * dvp, pl.ds(off, tk)],
                     jnp.concatenate(probs[2 * h:2 * h + 2], axis=1), preferred_element_type=F32)
             for h in range(B_HEADS)], axis=1)
        acc_ref[...] = jnp.concatenate(alphas, axis=1) * acc_ref[...] + pv

    def off_body(kt, carry):
        tile_step(kt, False)
        return carry

    def diag_body(kt, carry):
        tile_step(kt, True)
        return carry

    lax.fori_loop(0, n_off, off_body, 0)
    lax.fori_loop(n_off, n_off + tq // tk, diag_body, 0)

    lp = lam_ref[...]
    lam = (jnp.exp(jnp.sum(lp[0:1] * lp[1:2], axis=1, keepdims=True))
           - jnp.exp(jnp.sum(lp[2:3] * lp[3:4], axis=1, keepdims=True)) + lambda_init)
    o_all = acc_ref[:B_V_DIM, :] * (1.0 / acc_ref[B_V_DIM:B_V_DIM + 1, :])
    outs = []
    for h in range(B_HEADS):
        o = (o_all[:, 2 * h * tq:(2 * h + 1) * tq]
             - lam * o_all[:, (2 * h + 1) * tq:(2 * h + 2) * tq])
        o = o * lax.rsqrt(jnp.mean(o * o, axis=0, keepdims=True) + RMS_EPS) * subln_ref[...]
        outs.append(o * (1.0 - lambda_init))
    out_ref[...] = jnp.concatenate(outs, axis=0).T.astype(BF16)


def _diff_call(qb, kb, vbt, lam_rows, subln, b, s, tq, tk, lambda_init):
    blk = pl.BlockSpec((None, tq, B_WIDTH), lambda bi, i: (bi, i, 0))
    kw = 2 * B_HEADS * LANES
    dvp = B_V_DIM + V_PAD
    out = pl.pallas_call(
        functools.partial(_diff_kernel, tq=tq, tk=tk, lambda_init=lambda_init),
        grid=(b, s // tq),
        in_specs=[blk, pl.BlockSpec((None, s, kw), lambda bi, i: (bi, 0, 0)),
                  pl.BlockSpec((B_HEADS * dvp, s), lambda bi, i: (0, bi)),
                  pl.BlockSpec((SUBLANES, LANES), lambda bi, i: (0, 0)),
                  pl.BlockSpec((B_V_DIM, 1), lambda bi, i: (0, 0))],
        out_specs=blk,
        out_shape=jax.ShapeDtypeStruct((b, s, B_WIDTH), BF16),
        scratch_shapes=[pltpu.VMEM((2 * B_HEADS * tq, 2 * LANES), BF16),
                        pltpu.VMEM((1, 2 * B_HEADS * tq), F32),
                        pltpu.VMEM((dvp, 2 * B_HEADS * tq), F32)],
        compiler_params=_cparams(2),
        name="diff",
    )(qb.reshape(b, s, B_WIDTH), kb.reshape(b, s, kw), vbt, lam_rows, subln.reshape(B_V_DIM, 1))
    return out.reshape(b * s, B_WIDTH)


def _mix_out_kernel(h_ref, a_ref, b_ref, wa_ref, wb_ref, g_ref, beta_ref, o_ref):
    y = (jnp.dot(a_ref[...], wa_ref[...], preferred_element_type=F32)
         + jnp.dot(b_ref[...], wb_ref[...], preferred_element_type=F32))
    o_ref[...] = _layer_norm(ALPHA * h_ref[...] + y, g_ref[...], beta_ref[...])


def _mix_out_call(h, out_a, out_b, w_o, g, beta, tm):
    n, d = h.shape
    w = w_o.astype(BF16)
    row = lambda wd: pl.BlockSpec((tm, wd), lambda i: (i, 0))
    const = lambda shp: pl.BlockSpec(shp, lambda i: (0, 0))
    return pl.pallas_call(
        _mix_out_kernel,
        grid=(n // tm,),
        in_specs=[row(d), row(A_WIDTH), row(B_WIDTH), const((A_WIDTH, d)), const((B_WIDTH, d)),
                  const((1, d)), const((1, d))],
        out_specs=row(d),
        out_shape=jax.ShapeDtypeStruct((n, d), F32),
        compiler_params=_cparams(1),
        name="mix_out",
    )(h, out_a, out_b, w[:A_WIDTH], w[A_WIDTH:], g.reshape(1, d), beta.reshape(1, d))


def _mem_kv_kernel(m_ref, w_ref, k_ref, v_ref):
    x = m_ref[...].astype(BF16)
    d = k_ref.shape[-1]
    k_ref[...] = jnp.dot(x, w_ref[:, :d], preferred_element_type=F32).astype(BF16)
    v_ref[...] = jnp.dot(x, w_ref[:, d:], preferred_element_type=F32).astype(BF16)


def _mem_kv_call(mem2d, wkv, tm):
    n, d = mem2d.shape
    return pl.pallas_call(
        _mem_kv_kernel,
        grid=(n // tm,),
        in_specs=[pl.BlockSpec((tm, d), lambda i: (i, 0)),
                  pl.BlockSpec((d, 2 * d), lambda i: (0, 0))],
        out_specs=[pl.BlockSpec((tm, d), lambda i: (i, 0))] * 2,
        out_shape=[jax.ShapeDtypeStruct((n, d), BF16)] * 2,
        compiler_params=_cparams(1),
        name="mem_kv",
    )(mem2d, wkv.astype(BF16))


def _mem_attn_kernel(h_ref, wq_ref, k_ref, v_ref, wo_ref, g_ref, beta_ref, o_ref):
    h = h_ref[...]
    q = jnp.dot(h.astype(BF16), wq_ref[...], preferred_element_type=F32)
    q = (q * (MEM_HEAD_DIM ** -0.5)).astype(BF16)
    outs = []
    for hd in range(MEM_HEADS):
        sl = slice(hd * MEM_HEAD_DIM, (hd + 1) * MEM_HEAD_DIM)
        sc = lax.dot_general(q[:, sl], k_ref[:, sl], NT_DIMS, preferred_element_type=F32)
        p = jnp.exp(sc - jnp.max(sc, axis=1, keepdims=True))
        den = jnp.sum(p, axis=1, keepdims=True)
        o = jnp.dot(p.astype(BF16), v_ref[:, sl], preferred_element_type=F32) / den
        outs.append(o.astype(BF16))
    y = jnp.dot(jnp.concatenate(outs, axis=1), wo_ref[...], preferred_element_type=F32)
    o_ref[...] = _layer_norm(ALPHA * h + y, g_ref[...], beta_ref[...])


def _mem_attn_call(h, wq, k, v, wo, g, beta, b, s, tm):
    n, d = h.shape
    m = k.shape[0] // b
    const = lambda shp: pl.BlockSpec(shp, lambda bi, i: (0,) * len(shp))
    row = pl.BlockSpec((None, tm, d), lambda bi, i: (bi, i, 0))
    kv = pl.BlockSpec((None, m, d), lambda bi, i: (bi, 0, 0))
    out = pl.pallas_call(
        _mem_attn_kernel,
        grid=(b, s // tm),
        in_specs=[row, const((d, d)), kv, kv, const((d, d)), const((1, d)), const((1, d))],
        out_specs=row,
        out_shape=jax.ShapeDtypeStruct((b, s, d), F32),
        compiler_params=_cparams(2),
        name="mem_attn",
    )(h.reshape(b, s, d), wq.astype(BF16), k.reshape(b, m, d), v.reshape(b, m, d),
      wo.astype(BF16), g.reshape(1, d), beta.reshape(1, d))
    return out.reshape(n, d)


def _router_kernel(h_ref, rw_ref, bias_ref, tri_ref, e_ref, g_ref, rank_ref, cnt_ref, run_ref):
    tm = h_ref.shape[0]

    @pl.when(pl.program_id(0) == 0)
    def _():
        run_ref[...] = jnp.zeros(run_ref.shape, F32)

    logits = jnp.dot(h_ref[...].astype(BF16), rw_ref[...], preferred_element_type=F32)
    scores = 1.0 / (1.0 + jnp.exp(-logits))
    lane = lax.broadcasted_iota(I32, (tm, LANES), 1)
    lanef = lane.astype(F32)
    ninf = jnp.float32(-jnp.inf)
    cur = jnp.where(lane < N_EXPERTS, scores + bias_ref[...], ninf)
    top_e = jnp.zeros((tm, LANES), F32)
    top_s = jnp.zeros((tm, LANES), F32)
    picked = jnp.zeros((tm, LANES), F32)
    firsts = []
    for k in range(TOP_K):
        mx = jnp.max(cur, axis=1, keepdims=True)
        first = jnp.min(jnp.where(cur == mx, lanef, float(LANES)), axis=1, keepdims=True)
        hit = lanef == first
        s_k = jnp.sum(jnp.where(hit, scores, 0.0), axis=1, keepdims=True)
        top_e = jnp.where(lane == k, first, top_e)
        top_s = jnp.where(lane == k, s_k, top_s)
        picked = jnp.where(hit, 1.0, picked)
        cur = jnp.where(hit, ninf, cur)
        firsts.append(first)
    e_ref[...] = top_e.astype(I32)
    g_ref[...] = top_s / jnp.sum(top_s, axis=1, keepdims=True) * ROUTED_SCALE

    slot = jnp.dot(tri_ref[...], picked.astype(BF16), preferred_element_type=F32) + run_ref[...]
    rank = jnp.zeros((tm, LANES), F32)
    for k in range(TOP_K):
        r_k = jnp.sum(jnp.where(lanef == firsts[k], slot, 0.0), axis=1, keepdims=True)
        rank = jnp.where(lane == k, r_k, rank)
    rank_ref[...] = rank.astype(I32)
    run = run_ref[...] + jnp.sum(picked, axis=0, keepdims=True)
    run_ref[...] = run
    cnt_ref[...] = run.astype(I32)


def _router_call(h, router_w, router_bias, tm):
    n, d = h.shape
    rw = jnp.pad(router_w, ((0, 0), (0, LANES - N_EXPERTS))).astype(BF16)
    bias = jnp.pad(router_bias, (0, LANES - N_EXPERTS)).reshape(1, LANES)
    tri = jnp.tri(tm, k=-1, dtype=BF16)
    row = pl.BlockSpec((tm, LANES), lambda i: (i, 0))
    const = lambda shp: pl.BlockSpec(shp, lambda i: (0, 0))
    return pl.pallas_call(
        _router_kernel,
        grid=(n // tm,),
        in_specs=[pl.BlockSpec((tm, d), lambda i: (i, 0)), const((d, LANES)), const((1, LANES)),
                  const((tm, tm))],
        out_specs=[row, row, row, const((1, LANES))],
        out_shape=[jax.ShapeDtypeStruct((n, LANES), I32),
                   jax.ShapeDtypeStruct((n, LANES), F32),
                   jax.ShapeDtypeStruct((n, LANES), I32),
                   jax.ShapeDtypeStruct((1, LANES), I32)],
        scratch_shapes=[pltpu.VMEM((1, LANES), F32)],
        compiler_params=_cparams(1),
        name="router",
    )(h, rw, bias, tri)


def _dest_kernel(e_ref, rank_ref, gs_ref, o_ref):
    tm = e_ref.shape[0]
    lane = lax.broadcasted_iota(I32, (tm, LANES), 1)
    e = e_ref[...]
    gs = gs_ref[...].astype(F32)
    start = jnp.zeros((tm, LANES), F32)
    for k in range(TOP_K):
        s_k = jnp.sum(jnp.where(lane == e[:, k:k + 1], gs, 0.0), axis=1, keepdims=True)
        start = jnp.where(lane == k, s_k, start)
    o_ref[...] = start.astype(I32) + rank_ref[...]


def _dest_call(top_e, rank, gstart, tm):
    n = top_e.shape[0]
    row = pl.BlockSpec((tm, LANES), lambda i: (i, 0))
    return pl.pallas_call(
        _dest_kernel,
        grid=(n // tm,),
        in_specs=[row, row, pl.BlockSpec((1, LANES), lambda i: (0, 0))],
        out_specs=row,
        out_shape=jax.ShapeDtypeStruct((n, LANES), I32),
        compiler_params=_cparams(1),
        name="dest",
    )(top_e, rank, gstart)


def _dispatch_kernel(dest_ref, x_ref, xs_ref, sem, *, td):
    def row_copy(r, k):
        return pltpu.make_async_copy(x_ref.at[pl.ds(r, 1), :],
                                     xs_ref.at[pl.ds(dest_ref[r, k], 1), :], sem)

    for r in range(td):
        for k in range(TOP_K):
            row_copy(r, k).start(priority=k % N_DMA_QUEUES)

    def drain(r, carry):
        for k in range(TOP_K):
            row_copy(r, k).wait()
        return carry

    lax.fori_loop(0, td, drain, 0)


def _dispatch_call(h, dest, td):
    n, d = h.shape
    m = n * TOP_K
    return pl.pallas_call(
        functools.partial(_dispatch_kernel, td=td),
        grid=(n // td,),
        in_specs=[pl.BlockSpec((td, LANES), lambda i: (i, 0), memory_space=pltpu.SMEM),
                  pl.BlockSpec((td, d), lambda i: (i, 0))],
        out_specs=pl.BlockSpec(memory_space=pl.ANY),
        out_shape=jax.ShapeDtypeStruct((m, d), F32),
        scratch_shapes=[pltpu.SemaphoreType.DMA(())],
        compiler_params=_cparams(1),
        name="dispatch",
    )(dest, h)


def _moe_mm_kernel(ptile_ref, pexp_ref, plo_ref, phi_ref, xs_ref, wg_ref, wu_ref, wd_ref, y_ref,
                   *, tmm):
    p = pl.program_id(0)
    tile = ptile_ref[p]
    lo = plo_ref[p]
    hi = phi_ref[p]
    first = jnp.logical_or(p == 0, ptile_ref[jnp.maximum(p - 1, 0)] != tile)

    def expert_out():
        x = xs_ref[...].astype(BF16)
        g = jnp.dot(x, wg_ref[...].astype(BF16), preferred_element_type=F32)
        u = jnp.dot(x, wu_ref[...].astype(BF16), preferred_element_type=F32)
        hid = (_silu(g) * u).astype(BF16)
        return jnp.dot(hid, wd_ref[...].astype(BF16), preferred_element_type=F32)

    whole = jnp.logical_and(lo <= tile * tmm, hi >= (tile + 1) * tmm)

    @pl.when(whole)
    def _():
        y_ref[...] = expert_out()

    @pl.when(jnp.logical_and(first, jnp.logical_not(whole)))
    def _():
        y_ref[...] = jnp.zeros(y_ref.shape, F32)

    @pl.when(jnp.logical_and(lo < hi, jnp.logical_not(whole)))
    def _():
        y = expert_out()
        rowg = tile * tmm + lax.broadcasted_iota(I32, (tmm, 1), 0)
        y_ref[...] += jnp.where(jnp.logical_and(rowg >= lo, rowg < hi), y, 0.0)


def _moe_mm_call(xs, pairs, wg, wu, wd, layer, tmm):
    m = xs.shape[0]
    n_pairs = pairs[0].shape[0]
    rows = pl.BlockSpec((tmm, D_MODEL), lambda p, pt, pe, plo, phi: (pt[p], 0))
    wspec = lambda shp: pl.BlockSpec((None, None) + shp,
                                     lambda p, pt, pe, plo, phi: (layer, pe[p], 0, 0))
    return pl.pallas_call(
        functools.partial(_moe_mm_kernel, tmm=tmm),
        grid_spec=pltpu.PrefetchScalarGridSpec(
            num_scalar_prefetch=4,
            grid=(n_pairs,),
            in_specs=[rows, wspec((D_MODEL, EXPERT_DIM)), wspec((D_MODEL, EXPERT_DIM)),
                      wspec((EXPERT_DIM, D_MODEL))],
            out_specs=rows),
        out_shape=jax.ShapeDtypeStruct((m, D_MODEL), F32),
        compiler_params=_cparams(1),
        name="moe_mm",
    )(*pairs, xs, wg, wu, wd)


def _group_pairs(counts, m, tmm):
    n_tiles = m // tmm
    n_pairs = n_tiles + N_EXPERTS
    gend = jnp.cumsum(counts)
    gstart = gend - counts
    t0 = jnp.arange(n_tiles, dtype=I32) * tmm
    n_le = lambda edges, v: jnp.sum(edges[None, :] <= v[:, None], axis=1).astype(I32)
    e_first = n_le(gend, t0)
    e_last = n_le(gend, t0 + (tmm - 1))
    per_tile = e_last - e_first + 1
    pend = jnp.cumsum(per_tile)
    pstart = pend - per_tile
    p = jnp.arange(n_pairs, dtype=I32)
    valid = p < pend[-1]
    tile = jnp.minimum(n_le(pend, p), n_tiles - 1)
    e = jnp.clip(e_first[tile] + (p - pstart[tile]), 0, N_EXPERTS - 1).astype(I32)
    lo = jnp.where(valid, gstart[e], 0).astype(I32)
    hi = jnp.where(valid, gend[e], 0).astype(I32)
    return tile, e, lo, hi


def _combine_kernel(dest_ref, gate_ref, h_ref, y_ref, sg_ref, su_ref, sd_ref, g_ref, beta_ref, o_ref,
                    buf_ref, sem, *, tc):
    def row_copy(r, k):
        return pltpu.make_async_copy(y_ref.at[pl.ds(dest_ref[r, k], 1), :],
                                     buf_ref.at[k, pl.ds(r, 1), :], sem)

    for r in range(tc):
        for k in range(TOP_K):
            row_copy(r, k).start(priority=k % N_DMA_QUEUES)

    h = h_ref[...]
    x = h.astype(BF16)
    hid = (_silu(jnp.dot(x, sg_ref[...], preferred_element_type=F32))
           * jnp.dot(x, su_ref[...], preferred_element_type=F32)).astype(BF16)
    shared = jnp.dot(hid, sd_ref[...], preferred_element_type=F32)

    def drain(r, carry):
        for k in range(TOP_K):
            row_copy(r, k).wait()
        return carry

    lax.fori_loop(0, tc, drain, 0)

    gates = gate_ref[...]
    routed = gates[:, 0:1] * buf_ref[0]
    for k in range(1, TOP_K):
        routed = routed + gates[:, k:k + 1] * buf_ref[k]
    o_ref[...] = _layer_norm(ALPHA * h + (routed + shared), g_ref[...], beta_ref[...])


def _combine_call(h, y, dest, gates, sg, su, sd, g, beta, tc):
    n, d = h.shape
    smem = pl.BlockSpec((tc, LANES), lambda i: (i, 0), memory_space=pltpu.SMEM)
    const = lambda shp: pl.BlockSpec(shp, lambda i: (0, 0))
    return pl.pallas_call(
        functools.partial(_combine_kernel, tc=tc),
        grid=(n // tc,),
        in_specs=[smem, pl.BlockSpec((tc, LANES), lambda i: (i, 0)),
                  pl.BlockSpec((tc, d), lambda i: (i, 0)),
                  pl.BlockSpec(memory_space=pl.ANY),
                  const((d, SHARED_DIM)), const((d, SHARED_DIM)), const((SHARED_DIM, d)),
                  const((1, d)), const((1, d))],
        out_specs=pl.BlockSpec((tc, d), lambda i: (i, 0)),
        out_shape=jax.ShapeDtypeStruct((n, d), F32),
        scratch_shapes=[pltpu.VMEM((TOP_K, tc, d), F32),
                        pltpu.SemaphoreType.DMA(())],
        compiler_params=_cparams(1),
        name="combine",
    )(dest, gates, h, y, sg.astype(BF16), su.astype(BF16), sd.astype(BF16),
      g.reshape(1, d), beta.reshape(1, d))


def _tiles(b, s):
    n = b * s
    pick = lambda pref, total: next(t for t in (pref, 512, 256, 128, 64, 32, 16, 8) if t <= pref and total % t == 0)
    return dict(
        rows=pick(512, n),
        attn_q=pick(512, s),
        attn_k=pick(256, s),
        mem_rows=pick(512, s),
        mem_kv=pick(512, b * 256),
        router=pick(1024, n),
        dispatch=pick(128, n),
        moe=pick(512, n * TOP_K),
        combine=pick(128, n),
    )


def kernel(x, mem, ln_in_g, ln_in_b, w_in, a_kv_norm, a_w_uk, a_w_uv, b_lq1, b_lk1, b_lq2, b_lk2,
           b_subln, w_o, ln1_g, ln1_b, m_wq, m_wkv, m_wo, ln2_g, ln2_b, router_w, router_bias,
           e_w_gate, e_w_up, e_w_down, s_w_gate, s_w_up, s_w_down, ln3_g, ln3_b):
    b, s, d = x.shape
    n = b * s
    m = n * TOP_K
    t = _tiles(b, s)
    mem2d = mem.reshape(-1, d)

    h = _ln_call(x.reshape(n, d), ln_in_g, ln_in_b, t["rows"])
    for l in range(DEPTH):
        lambda_init = 0.8 - 0.6 * math.exp(-0.3 * l)

        qa, ckv, ckvt, qidx, kk, widx, qb, kb, vbt = _proj_call(h, _pad_w_in(w_in[l]), a_kv_norm[l],
                                                                  t["rows"], s)
        out_a = _dsa_call(qa, qidx, widx, kk, ckv, ckvt, _block_diag_uk(a_w_uk[l]),
                          _block_diag_uv_t(a_w_uv[l]), b, s, t["attn_q"], t["attn_k"])
        lam_rows = jnp.pad(jnp.stack([b_lq1[l], b_lk1[l], b_lq2[l], b_lk2[l]]),
                           ((0, SUBLANES - 4), (0, LANES - B_QK_DIM)))
        out_b = _diff_call(qb, kb, vbt, lam_rows, b_subln[l], b, s, t["attn_q"], t["attn_k"], lambda_init)
        h = _mix_out_call(h, out_a, out_b, w_o[l], ln1_g[l], ln1_b[l], t["rows"])

        mk, mv = _mem_kv_call(mem2d, m_wkv[l], t["mem_kv"])
        h = _mem_attn_call(h, m_wq[l], mk, mv, m_wo[l], ln2_g[l], ln2_b[l], b, s, t["mem_rows"])

        top_e, gates, rank, counts = _router_call(h, router_w[l], router_bias[l], t["router"])
        gstart = jnp.cumsum(counts, axis=1) - counts
        dest = _dest_call(top_e, rank, gstart, t["router"])
        counts = counts[0, :N_EXPERTS]
        xs = _dispatch_call(h, dest, t["dispatch"])
        pairs = _group_pairs(counts, m, t["moe"])
        y = _moe_mm_call(xs, pairs, e_w_gate, e_w_up, e_w_down, l, t["moe"])
        h = _combine_call(h, y, dest, gates, s_w_gate[l], s_w_up[l], s_w_down[l],
                          ln3_g[l], ln3_b[l], t["combine"])
    return h.reshape(b, s, d)
```

```python
import functools
import math

import jax
import jax.numpy as jnp
import numpy as np
from jax import lax
from jax.experimental import pallas as pl
from jax.experimental.pallas import tpu as pltpu

F32 = jnp.float32
BF16 = jnp.bfloat16
I32 = jnp.int32
I16 = jnp.int16

D_MODEL = 1024
DEPTH = 2
CHUNK = 64
CHUNK_SHIFT = 6
A_HEADS = 8
A_HEAD_DIM = 64
A_WIDTH = A_HEADS * A_HEAD_DIM
KV_LORA = 128
IDX_HEADS = 8
IDX_DIM = 64
INDEX_TOPK_MAX = 256
B_HEADS = 4
B_QK_DIM = 64
B_V_DIM = 128
B_WIDTH = B_HEADS * B_V_DIM
MEM_HEADS = 4
MEM_HEAD_DIM = D_MODEL // MEM_HEADS
N_EXPERTS = 64
TOP_K = 8
EXPERT_DIM = 256
SHARED_DIM = 256
ROUTED_SCALE = 2.5
ALPHA = (2 * DEPTH) ** 0.25
LN_EPS = 1e-5
RMS_EPS = 1e-6

LANES = 128
SUBLANES = 8
VMEM_LIMIT = 56 * 1024 * 1024
INT_MIN = -2 ** 31
I16_MIN = -2 ** 15
V_PAD = 16
N_DMA_QUEUES = 2
NEG_BIG = -3.0e38
M_INIT = -1.0e30

NT_DIMS = (((1,), (1,)), ((), ()))


def _cparams(n_axes):
    return pltpu.CompilerParams(dimension_semantics=("arbitrary",) * n_axes,
                                vmem_limit_bytes=VMEM_LIMIT)


def _layer_norm(x, g, b):
    mu = jnp.mean(x, axis=-1, keepdims=True)
    xc = x - mu
    var = jnp.mean(xc * xc, axis=-1, keepdims=True)
    return xc * lax.rsqrt(var + LN_EPS) * g + b


def _silu(x):
    return x * (1.0 / (1.0 + jnp.exp(-x)))


def _ln_kernel(x_ref, g_ref, b_ref, o_ref):
    o_ref[...] = _layer_norm(x_ref[...], g_ref[...], b_ref[...])


def _ln_call(x, g, b, tm):
    n, d = x.shape
    return pl.pallas_call(
        _ln_kernel,
        grid=(n // tm,),
        in_specs=[pl.BlockSpec((tm, d), lambda i: (i, 0)),
                  pl.BlockSpec((1, d), lambda i: (0, 0)),
                  pl.BlockSpec((1, d), lambda i: (0, 0))],
        out_specs=pl.BlockSpec((tm, d), lambda i: (i, 0)),
        out_shape=jax.ShapeDtypeStruct((n, d), F32),
        compiler_params=_cparams(1),
        name="ln_in",
    )(x, g.reshape(1, d), b.reshape(1, d))


_SEG_QA = (0, 512)
_SEG_CKV = (512, 640)
_SEG_QIDX = (640, 1152)
_SEG_KK = (1152, 1280)
_SEG_WIDX = (1280, 1408)
_SEG_QB = (1408, 1920)
_SEG_KB = (1920, 2432)
_SEG_VB = (2432, 2944)
_PROJ_COLS = 2944


def _pos_features(pos, shape):
    lane = lax.broadcasted_iota(I32, shape, 1)
    hi = lax.shift_right_logical(pos, CHUNK_SHIFT).astype(F32)
    lo = (pos & (CHUNK - 1)).astype(F32)
    return jnp.where(lane < 2, 1.0, jnp.where(lane == 2, hi, jnp.where(lane == 3, lo, 0.0)))


def _slope_features(tqi, slope, shape):
    feat = lax.broadcasted_iota(I32, shape, 0)
    hi = lax.shift_right_logical(tqi, CHUNK_SHIFT).astype(F32)
    lo = (tqi & (CHUNK - 1)).astype(F32)
    return jnp.where(feat == 0, -slope * CHUNK * hi,
                     jnp.where(feat == 1, -slope * lo,
                               jnp.where(feat == 2, slope * CHUNK,
                                         jnp.where(feat == 3, slope, 0.0))))


def _with_ones_rows(vt):
    t = vt.shape[1]
    sub = lax.broadcasted_iota(I32, (V_PAD, t), 0)
    return jnp.concatenate([vt, jnp.where(sub == 0, 1.0, 0.0)], axis=0)


def _proj_kernel(h_ref, w_ref, kvn_ref, qa_ref, ckv_ref, ckvt_ref, qidx_ref, kk_ref, widx_ref,
                 qb_ref, kb_ref, vbt_ref, *, seq):
    tm = h_ref.shape[0]
    x = h_ref[...].astype(BF16)

    def seg(s):
        return jnp.dot(x, w_ref[:, s[0]:s[1]], preferred_element_type=F32)

    row = pl.program_id(0) * tm + lax.broadcasted_iota(I32, (tm, LANES), 0)
    feat = _pos_features(row & (seq - 1), (tm, LANES)).astype(BF16)

    qa_ref[...] = (seg(_SEG_QA) * (A_HEAD_DIM ** -0.5)).astype(BF16)
    c = seg(_SEG_CKV)
    c = c * lax.rsqrt(jnp.mean(c * c, axis=-1, keepdims=True) + RMS_EPS) * kvn_ref[...]
    ckv_ref[:, :KV_LORA] = c.astype(BF16)
    ckv_ref[:, KV_LORA:] = feat
    ckvt_ref[...] = _with_ones_rows(c.T).astype(BF16)
    qidx_ref[...] = (seg(_SEG_QIDX) * (IDX_DIM ** -0.5)).astype(BF16)
    kk_ref[...] = seg(_SEG_KK).astype(BF16)
    widx_ref[...] = seg(_SEG_WIDX) * (IDX_HEADS ** -0.5)
    qb_ref[...] = (seg(_SEG_QB) * (B_QK_DIM ** -0.5)).astype(BF16)
    kb = seg(_SEG_KB).astype(BF16)
    vt = seg(_SEG_VB).T
    for h in range(B_HEADS):
        kb_ref[:, 2 * h * LANES:(2 * h + 1) * LANES] = kb[:, h * LANES:(h + 1) * LANES]
        kb_ref[:, (2 * h + 1) * LANES:(2 * h + 2) * LANES] = feat
        vbt_ref[h * (B_V_DIM + V_PAD):(h + 1) * (B_V_DIM + V_PAD), :] = _with_ones_rows(
            vt[h * B_V_DIM:(h + 1) * B_V_DIM]).astype(BF16)


def _proj_call(h, w_pad, kvn, tm, seq):
    n, d = h.shape
    assert seq & (seq - 1) == 0 and seq <= CHUNK * 256 and seq % tm == 0
    outs = [(512, BF16, False), (2 * KV_LORA, BF16, False), (KV_LORA + V_PAD, BF16, True),
            (512, BF16, False), (128, BF16, False), (128, F32, False), (512, BF16, False),
            (2 * B_HEADS * LANES, BF16, False), (B_HEADS * (B_V_DIM + V_PAD), BF16, True)]
    spec = lambda w, t: (pl.BlockSpec((w, tm), lambda i: (0, i)) if t
                         else pl.BlockSpec((tm, w), lambda i: (i, 0)))
    shape = lambda w, dt, t: jax.ShapeDtypeStruct((w, n) if t else (n, w), dt)
    return pl.pallas_call(
        functools.partial(_proj_kernel, seq=seq),
        grid=(n // tm,),
        in_specs=[pl.BlockSpec((tm, d), lambda i: (i, 0)),
                  pl.BlockSpec((d, _PROJ_COLS), lambda i: (0, 0)),
                  pl.BlockSpec((1, KV_LORA), lambda i: (0, 0))],
        out_specs=[spec(w, t) for w, _, t in outs],
        out_shape=[shape(w, dt, t) for w, dt, t in outs],
        compiler_params=_cparams(1),
        name="proj_in",
    )(h, w_pad, kvn.reshape(1, KV_LORA))


def _pad_w_in(w_in):
    sizes = (A_WIDTH, KV_LORA, IDX_HEADS * IDX_DIM, IDX_DIM, IDX_HEADS,
             2 * B_HEADS * B_QK_DIM, 2 * B_HEADS * B_QK_DIM, B_WIDTH)
    splits = np.cumsum(sizes)[:-1].tolist()
    q_a, c_kv, q_idx, k_idx, w_idx, q_b, k_b, v_b = jnp.split(w_in, splits, axis=-1)
    w_idx = jnp.pad(w_idx, ((0, 0), (0, LANES - IDX_HEADS)))
    return jnp.concatenate([q_a, c_kv, q_idx, k_idx, k_idx, w_idx, q_b, k_b, v_b],
                           axis=-1).astype(BF16)


def _flash_probs(sl, sc, m_ref):
    m_old = m_ref[:, sl]
    m_new = jnp.maximum(m_old, jnp.max(sc, axis=0, keepdims=True))
    m_ref[:, sl] = m_new
    return jnp.exp(m_old - m_new), jnp.exp(sc - m_new).astype(BF16)


def _flash_init(m_ref, acc_ref):
    m_ref[...] = jnp.full(m_ref.shape, M_INIT, F32)
    acc_ref[...] = jnp.zeros(acc_ref.shape, F32)


def _split_halves(q_ref, n_pairs, out_ref, tq):
    q_t = q_ref[...].astype(F32).T
    feat = lax.broadcasted_iota(I32, (LANES, tq), 0)
    for j in range(n_pairs):
        qp = q_t[j * LANES:(j + 1) * LANES, :]
        out_ref[:LANES, 2 * j * tq:(2 * j + 1) * tq] = jnp.where(feat < 64, qp, 0.0).astype(BF16)
        out_ref[:LANES, (2 * j + 1) * tq:(2 * j + 2) * tq] = jnp.where(feat >= 64, qp, 0.0).astype(BF16)


def _dsa_kernel(qa_ref, qidx_ref, widx_ref, kk_ref, ckv_ref, ckvt_ref, wk_ref, wvt_ref, out_ref,
                keys_ref, khi_ref, klo_ref, qim_ref, qlat_ref, tj_ref, m_ref, acc_ref,
                *, tq, tk, seq, topk):
    i = pl.program_id(1)
    n_off = i * (tq // tk)
    n_kt = n_off + tq // tk
    idx_bits = int(math.log2(seq))

    _split_halves(qidx_ref, IDX_HEADS // 2, qim_ref, tq)
    qlat_t = lax.dot_general(wk_ref[...], qa_ref[...], NT_DIMS,
                             preferred_element_type=F32)
    t_row = i * tq + lax.broadcasted_iota(I32, (LANES, tq), 1)
    for h in range(A_HEADS):
        qlat_ref[:KV_LORA, h * tq:(h + 1) * tq] = qlat_t[h * KV_LORA:(h + 1) * KV_LORA, :].astype(BF16)
        qlat_ref[KV_LORA:, h * tq:(h + 1) * tq] = _slope_features(
            t_row, 2.0 ** -(h + 1), (LANES, tq)).astype(BF16)
    w_t = widx_ref[...].T

    row = lax.broadcasted_iota(I32, (tk, tq), 0)
    tqi = i * tq + lax.broadcasted_iota(I32, (tk, tq), 1)
    qchunk = lax.shift_right_logical(tqi, CHUNK_SHIFT)

    def idx_body(kt, carry):
        off = pl.multiple_of(kt * tk, tk)
        kkt = kk_ref[pl.ds(off, tk), :]
        s_all = jnp.dot(kkt, qim_ref[...], preferred_element_type=F32)
        acc = jnp.zeros((tk, tq), F32)
        for h in range(IDX_HEADS):
            acc = acc + w_t[h:h + 1, :] * jnp.maximum(s_all[:, h * tq:(h + 1) * tq], 0.0)
        bits = lax.bitcast_convert_type(acc + 0.0, I32)
        key = jnp.where(bits < 0, bits ^ jnp.int32(0x7FFFFFFF), bits)
        adm = lax.shift_right_logical(off + row, CHUNK_SHIFT) <= qchunk
        key = jnp.where(adm, key, jnp.int32(INT_MIN))
        keys_ref[pl.ds(off, tk), :] = key
        khi_ref[pl.ds(off, tk), :] = lax.shift_right_arithmetic(key, 16).astype(I16)
        klo_ref[pl.ds(off, tk), :] = ((key & 0xFFFF) + I16_MIN).astype(I16)
        return carry

    lax.fori_loop(0, n_kt, idx_body, 0)

    def count(pred):
        def body(kt, c):
            off = pl.multiple_of(kt * tk, tk)
            ind = jnp.where(pred(keys_ref[pl.ds(off, tk), :], off + row), 1.0, 0.0)
            return c + jnp.sum(ind, axis=0, keepdims=True)
        return lax.fori_loop(0, n_kt, body, jnp.zeros((1, tq), F32))

    def count16(ref, pred):
        pack = 2 * SUBLANES
        def body(kt, c):
            off = pl.multiple_of(kt * tk, tk)
            ind = jnp.where(pred(ref[pl.ds(off, tk), :]), jnp.int16(1), jnp.int16(0))
            parts = [ind[j * pack:(j + 1) * pack] for j in range(tk // pack)]
            while len(parts) > 1:
                parts = [parts[j] + parts[j + 1] for j in range(0, len(parts), 2)]
            return c + parts[0]
        c = lax.fori_loop(0, n_kt, body, jnp.zeros((pack, tq), I16))
        return jnp.sum(c.astype(F32), axis=0, keepdims=True)

    def search16(ref, need):
        def body(it, t):
            cand = t + lax.shift_left(jnp.int32(1), 15 - it)
            cand16 = cand.astype(I16)
            cnt = count16(ref, lambda tile: tile >= cand16)
            return jnp.where(cnt >= need, cand, t)
        return lax.fori_loop(0, 16, body, jnp.full((1, tq), I16_MIN, I32))

    t_hi = search16(khi_ref, topk)
    t_hi16 = t_hi.astype(I16)
    n_above = count16(khi_ref, lambda tile: tile > t_hi16)

    def park_body(kt, carry):
        off = pl.multiple_of(kt * tk, tk)
        klo_ref[pl.ds(off, tk), :] = jnp.where(khi_ref[pl.ds(off, tk), :] == t_hi16,
                                               klo_ref[pl.ds(off, tk), :], jnp.int16(I16_MIN))
        return carry

    lax.fori_loop(0, n_kt, park_body, 0)
    t_lo = search16(klo_ref, topk - n_above)
    thr = t_hi * 65536 + (t_lo - I16_MIN)

    live = thr > INT_MIN
    n_ge = count(lambda tile, _: tile >= thr)
    excess = jnp.logical_and(n_ge > topk, live)
    all_ties = jnp.where(live, seq, -1).astype(I32)
    tj_ref[0] = jnp.broadcast_to(thr, (SUBLANES, tq))
    tj_ref[1] = jnp.broadcast_to(all_ties, (SUBLANES, tq))

    @pl.when(jnp.max(jnp.where(excess, 1.0, 0.0)) > 0.0)
    def _():
        need = topk - count(lambda tile, _: tile > thr)

        def tie_body(it, hi):
            cand = hi + lax.shift_left(jnp.int32(1), idx_bits - 1 - it)
            cnt = count(lambda tile, pos: jnp.logical_and(tile == thr, pos < cand))
            return jnp.where(cnt < need, cand, hi)

        hi = lax.fori_loop(0, idx_bits, tie_body, jnp.zeros((1, tq), I32))
        tj_ref[1] = jnp.broadcast_to(jnp.where(excess, hi, all_ties), (SUBLANES, tq))

    thr_b = tj_ref[0][:1]
    tie_hi = tj_ref[1][:1]

    _flash_init(m_ref, acc_ref)

    def att_tile(kt, late_fix):
        off = pl.multiple_of(kt * tk, tk)
        tile = keys_ref[pl.ds(off, tk), :]
        pos = off + row
        sel = jnp.logical_or(tile > thr_b,
                             jnp.logical_and(tile == thr_b, pos <= tie_hi))
        s_all = jnp.dot(ckv_ref[pl.ds(off, tk), :], qlat_ref[...],
                        preferred_element_type=F32)
        if late_fix:
            late = jnp.maximum(pos - tqi, 0).astype(F32)
        alphas, probs = [], []
        for h in range(A_HEADS):
            sl = slice(h * tq, (h + 1) * tq)
            sc = s_all[:, sl]
            if late_fix:
                sc = sc - (2.0 * 2.0 ** -(h + 1)) * late
            alpha, p = _flash_probs(sl, jnp.where(sel, sc, NEG_BIG), m_ref)
            alphas.append(alpha)
            probs.append(p)
        pv = jnp.dot(ckvt_ref[:, pl.ds(off, tk)], jnp.concatenate(probs, axis=1),
                     preferred_element_type=F32)
        acc_ref[...] = jnp.concatenate(alphas, axis=1) * acc_ref[...] + pv

    def off_body(kt, carry):
        att_tile(kt, False)
        return carry

    def diag_body(kt, carry):
        att_tile(kt, True)
        return carry

    lax.fori_loop(0, n_off, off_body, 0)
    lax.fori_loop(n_off, n_kt, diag_body, 0)

    inv_l = 1.0 / acc_ref[KV_LORA:KV_LORA + 1, :]
    olat = jnp.concatenate([(acc_ref[:KV_LORA, h * tq:(h + 1) * tq]
                             * inv_l[:, h * tq:(h + 1) * tq]).astype(BF16)
                            for h in range(A_HEADS)], axis=0)
    out_t = jnp.dot(wvt_ref[...], olat, preferred_element_type=F32)
    out_ref[...] = out_t.T.astype(BF16)


def _dsa_call(qa, qidx, widx, kk, ckv, ckvt, wk_bd, wvt_bd, b, s, tq, tk):
    topk = min(INDEX_TOPK_MAX, s // 4)
    assert s % tq == 0 and tq % tk == 0 and tk % LANES == 0 and s & (s - 1) == 0
    blk = lambda w: pl.BlockSpec((None, tq, w), lambda bi, i: (bi, i, 0))
    full = lambda w: pl.BlockSpec((None, s, w), lambda bi, i: (bi, 0, 0))
    const = lambda a: pl.BlockSpec(a.shape, lambda bi, i: (0, 0))
    r3 = lambda a: a.reshape(b, s, a.shape[-1])
    out = pl.pallas_call(
        functools.partial(_dsa_kernel, tq=tq, tk=tk, seq=s, topk=topk),
        grid=(b, s // tq),
        in_specs=[blk(A_WIDTH), blk(IDX_HEADS * IDX_DIM), blk(LANES), full(LANES),
                  full(2 * KV_LORA),
                  pl.BlockSpec((KV_LORA + V_PAD, s), lambda bi, i: (0, bi)),
                  const(wk_bd), const(wvt_bd)],
        out_specs=blk(A_WIDTH),
        out_shape=jax.ShapeDtypeStruct((b, s, A_WIDTH), BF16),
        scratch_shapes=[pltpu.VMEM((s, tq), I32),
                        pltpu.VMEM((s, tq), I16),
                        pltpu.VMEM((s, tq), I16),
                        pltpu.VMEM((LANES, IDX_HEADS * tq), BF16),
                        pltpu.VMEM((2 * KV_LORA, A_HEADS * tq), BF16),
                        pltpu.VMEM((2, SUBLANES, tq), I32),
                        pltpu.VMEM((1, A_HEADS * tq), F32),
                        pltpu.VMEM((KV_LORA + V_PAD, A_HEADS * tq), F32)],
        compiler_params=_cparams(2),
        name="dsa",
    )(r3(qa), r3(qidx), r3(widx), r3(kk), r3(ckv), ckvt, wk_bd, wvt_bd)
    return out.reshape(b * s, A_WIDTH)


def _block_diag_uk(w_uk):
    h, c, d = w_uk.shape
    eye = jnp.eye(h, dtype=w_uk.dtype)
    return jnp.einsum('hcd,hg->hcgd', w_uk, eye).reshape(h * c, h * d).astype(BF16)


def _block_diag_uv_t(w_uv):
    h, c, d = w_uv.shape
    eye = jnp.eye(h, dtype=w_uv.dtype)
    return jnp.einsum('hcd,hg->hdgc', w_uv, eye).reshape(h * d, h * c).astype(BF16)


def _diff_kernel(qb_ref, kb_ref, vbt_ref, lam_ref, subln_ref, out_ref,
                 qm_ref, m_ref, acc_ref, *, tq, tk, lambda_init):
    i = pl.program_id(1)
    n_off = i * (tq // tk)
    dvp = B_V_DIM + V_PAD
    _split_halves(qb_ref, B_HEADS, qm_ref, tq)
    t_row = i * tq + lax.broadcasted_iota(I32, (LANES, tq), 1)
    for h in range(B_HEADS):
        feat = _slope_features(t_row, 2.0 ** (-2 * (h + 1)), (LANES, tq)).astype(BF16)
        qm_ref[LANES:, 2 * h * tq:(2 * h + 1) * tq] = feat
        qm_ref[LANES:, (2 * h + 1) * tq:(2 * h + 2) * tq] = feat

    row = lax.broadcasted_iota(I32, (tk, tq), 0)
    tqi = i * tq + lax.broadcasted_iota(I32, (tk, tq), 1)
    qchunk = lax.shift_right_logical(tqi, CHUNK_SHIFT)
    _flash_init(m_ref, acc_ref)

    def tile_step(kt, diag):
        off = pl.multiple_of(kt * tk, tk)
        s_pairs = [jnp.dot(kb_ref[pl.ds(off, tk), 2 * h * LANES:(2 * h + 2) * LANES],
                           qm_ref[:, 2 * h * tq:(2 * h + 2) * tq],
                           preferred_element_type=F32) for h in range(B_HEADS)]
        if diag:
            pos = off + row
            adm = lax.shift_right_logical(pos, CHUNK_SHIFT) <= qchunk
            late = jnp.maximum(pos - tqi, 0).astype(F32)
        alphas, probs = [], []
        for h in range(B_HEADS):
            for j in range(2):
                sl = slice((2 * h + j) * tq, (2 * h + j + 1) * tq)
                sc = s_pairs[h][:, j * tq:(j + 1) * tq]
                if diag:
                    sc = jnp.where(adm, sc - (2.0 * 2.0 ** (-2 * (h + 1))) * late, NEG_BIG)
                alpha, p = _flash_probs(sl, sc, m_ref)
                alphas.append(alpha)
                probs.append(p)
        pv = jnp.concatenate(
            [jnp.dot(vbt_ref[h * dvp:(h + 1) * dvp, pl.ds(off, tk)],
                     jnp.concatenate(probs[2 * h:2 * h + 2], axis=1), preferred_element_type=F32)
             for h in range(B_HEADS)], axis=1)
        acc_ref[...] = jnp.concatenate(alphas, axis=1) * acc_ref[...] + pv

    def off_body(kt, carry):
        tile_step(kt, False)
        return carry

    def diag_body(kt, carry):
        tile_step(kt, True)
        return carry

    lax.fori_loop(0, n_off, off_body, 0)
    lax.fori_loop(n_off, n_off + tq // tk, diag_body, 0)

    lp = lam_ref[...]
    lam = (jnp.exp(jnp.sum(lp[0:1] * lp[1:2], axis=1, keepdims=True))
           - jnp.exp(jnp.sum(lp[2:3] * lp[3:4], axis=1, keepdims=True)) + lambda_init)
    o_all = acc_ref[:B_V_DIM, :] * (1.0 / acc_ref[B_V_DIM:B_V_DIM + 1, :])
    outs = []
    for h in range(B_HEADS):
        o = (o_all[:, 2 * h * tq:(2 * h + 1) * tq]
             - lam * o_all[:, (2 * h + 1) * tq:(2 * h + 2) * tq])
        o = o * lax.rsqrt(jnp.mean(o * o, axis=0, keepdims=True) + RMS_EPS) * subln_ref[...]
        outs.append(o * (1.0 - lambda_init))
    out_ref[...] = jnp.concatenate(outs, axis=0).T.astype(BF16)


def _diff_call(qb, kb, vbt, lam_rows, subln, b, s, tq, tk, lambda_init):
    blk = pl.BlockSpec((None, tq, B_WIDTH), lambda bi, i: (bi, i, 0))
    kw = 2 * B_HEADS * LANES
    dvp = B_V_DIM + V_PAD
    out = pl.pallas_call(
        functools.partial(_diff_kernel, tq=tq, tk=tk, lambda_init=lambda_init),
        grid=(b, s // tq),
        in_specs=[blk, pl.BlockSpec((None, s, kw), lambda bi, i: (bi, 0, 0)),
                  pl.BlockSpec((B_HEADS * dvp, s), lambda bi, i: (0, bi)),
                  pl.BlockSpec((SUBLANES, LANES), lambda bi, i: (0, 0)),
                  pl.BlockSpec((B_V_DIM, 1), lambda bi, i: (0, 0))],
        out_specs=blk,
        out_shape=jax.ShapeDtypeStruct((b, s, B_WIDTH), BF16),
        scratch_shapes=[pltpu.VMEM((2 * LANES, 2 * B_HEADS * tq), BF16),
                        pltpu.VMEM((1, 2 * B_HEADS * tq), F32),
                        pltpu.VMEM((dvp, 2 * B_HEADS * tq), F32)],
        compiler_params=_cparams(2),
        name="diff",
    )(qb.reshape(b, s, B_WIDTH), kb.reshape(b, s, kw), vbt, lam_rows, subln.reshape(B_V_DIM, 1))
    return out.reshape(b * s, B_WIDTH)


def _mix_out_kernel(h_ref, a_ref, b_ref, wa_ref, wb_ref, g_ref, beta_ref, o_ref):
    y = (jnp.dot(a_ref[...], wa_ref[...], preferred_element_type=F32)
         + jnp.dot(b_ref[...], wb_ref[...], preferred_element_type=F32))
    o_ref[...] = _layer_norm(ALPHA * h_ref[...] + y, g_ref[...], beta_ref[...])


def _mix_out_call(h, out_a, out_b, w_o, g, beta, tm):
    n, d = h.shape
    w = w_o.astype(BF16)
    row = lambda wd: pl.BlockSpec((tm, wd), lambda i: (i, 0))
    const = lambda shp: pl.BlockSpec(shp, lambda i: (0, 0))
    return pl.pallas_call(
        _mix_out_kernel,
        grid=(n // tm,),
        in_specs=[row(d), row(A_WIDTH), row(B_WIDTH), const((A_WIDTH, d)), const((B_WIDTH, d)),
                  const((1, d)), const((1, d))],
        out_specs=row(d),
        out_shape=jax.ShapeDtypeStruct((n, d), F32),
        compiler_params=_cparams(1),
        name="mix_out",
    )(h, out_a, out_b, w[:A_WIDTH], w[A_WIDTH:], g.reshape(1, d), beta.reshape(1, d))


def _mem_kv_kernel(m_ref, w_ref, k_ref, v_ref):
    x = m_ref[...].astype(BF16)
    d = k_ref.shape[-1]
    k_ref[...] = jnp.dot(x, w_ref[:, :d], preferred_element_type=F32).astype(BF16)
    v_ref[...] = jnp.dot(x, w_ref[:, d:], preferred_element_type=F32).astype(BF16)


def _mem_kv_call(mem2d, wkv, tm):
    n, d = mem2d.shape
    return pl.pallas_call(
        _mem_kv_kernel,
        grid=(n // tm,),
        in_specs=[pl.BlockSpec((tm, d), lambda i: (i, 0)),
                  pl.BlockSpec((d, 2 * d), lambda i: (0, 0))],
        out_specs=[pl.BlockSpec((tm, d), lambda i: (i, 0))] * 2,
        out_shape=[jax.ShapeDtypeStruct((n, d), BF16)] * 2,
        compiler_params=_cparams(1),
        name="mem_kv",
    )(mem2d, wkv.astype(BF16))


def _mem_attn_kernel(h_ref, wq_ref, k_ref, v_ref, wo_ref, g_ref, beta_ref, o_ref):
    h = h_ref[...]
    q = jnp.dot(h.astype(BF16), wq_ref[...], preferred_element_type=F32)
    q = (q * (MEM_HEAD_DIM ** -0.5)).astype(BF16)
    outs = []
    for hd in range(MEM_HEADS):
        sl = slice(hd * MEM_HEAD_DIM, (hd + 1) * MEM_HEAD_DIM)
        sc = lax.dot_general(q[:, sl], k_ref[:, sl], NT_DIMS, preferred_element_type=F32)
        p = jnp.exp(sc - jnp.max(sc, axis=1, keepdims=True))
        den = jnp.sum(p, axis=1, keepdims=True)
        o = jnp.dot(p.astype(BF16), v_ref[:, sl], preferred_element_type=F32) / den
        outs.append(o.astype(BF16))
    y = jnp.dot(jnp.concatenate(outs, axis=1), wo_ref[...], preferred_element_type=F32)
    o_ref[...] = _layer_norm(ALPHA * h + y, g_ref[...], beta_ref[...])


def _mem_attn_call(h, wq, k, v, wo, g, beta, b, s, tm):
    n, d = h.shape
    m = k.shape[0] // b
    const = lambda shp: pl.BlockSpec(shp, lambda bi, i: (0,) * len(shp))
    row = pl.BlockSpec((None, tm, d), lambda bi, i: (bi, i, 0))
    kv = pl.BlockSpec((None, m, d), lambda bi, i: (bi, 0, 0))
    out = pl.pallas_call(
        _mem_attn_kernel,
        grid=(b, s // tm),
        in_specs=[row, const((d, d)), kv, kv, const((d, d)), const((1, d)), const((1, d))],
        out_specs=row,
        out_shape=jax.ShapeDtypeStruct((b, s, d), F32),
        compiler_params=_cparams(2),
        name="mem_attn",
    )(h.reshape(b, s, d), wq.astype(BF16), k.reshape(b, m, d), v.reshape(b, m, d),
      wo.astype(BF16), g.reshape(1, d), beta.reshape(1, d))
    return out.reshape(n, d)


def _router_kernel(h_ref, rw_ref, bias_ref, tri_ref, e_ref, g_ref, rank_ref, cnt_ref, run_ref):
    tm = h_ref.shape[0]

    @pl.when(pl.program_id(0) == 0)
    def _():
        run_ref[...] = jnp.zeros(run_ref.shape, F32)

    logits = jnp.dot(h_ref[...].astype(BF16), rw_ref[...], preferred_element_type=F32)
    scores = 1.0 / (1.0 + jnp.exp(-logits))
    lane = lax.broadcasted_iota(I32, (tm, LANES), 1)
    lanef = lane.astype(F32)
    ninf = jnp.float32(-jnp.inf)
    cur = jnp.where(lane < N_EXPERTS, scores + bias_ref[...], ninf)
    top_e = jnp.zeros((tm, LANES), F32)
    top_s = jnp.zeros((tm, LANES), F32)
    picked = jnp.zeros((tm, LANES), F32)
    firsts = []
    for k in range(TOP_K):
        mx = jnp.max(cur, axis=1, keepdims=True)
        first = jnp.min(jnp.where(cur == mx, lanef, float(LANES)), axis=1, keepdims=True)
        hit = lanef == first
        s_k = jnp.sum(jnp.where(hit, scores, 0.0), axis=1, keepdims=True)
        top_e = jnp.where(lane == k, first, top_e)
        top_s = jnp.where(lane == k, s_k, top_s)
        picked = jnp.where(hit, 1.0, picked)
        cur = jnp.where(hit, ninf, cur)
        firsts.append(first)
    e_ref[...] = top_e.astype(I32)
    g_ref[...] = top_s / jnp.sum(top_s, axis=1, keepdims=True) * ROUTED_SCALE

    slot = jnp.dot(tri_ref[...], picked.astype(BF16), preferred_element_type=F32) + run_ref[...]
    rank = jnp.zeros((tm, LANES), F32)
    for k in range(TOP_K):
        r_k = jnp.sum(jnp.where(lanef == firsts[k], slot, 0.0), axis=1, keepdims=True)
        rank = jnp.where(lane == k, r_k, rank)
    rank_ref[...] = rank.astype(I32)
    run = run_ref[...] + jnp.sum(picked, axis=0, keepdims=True)
    run_ref[...] = run
    cnt_ref[...] = run.astype(I32)


def _router_call(h, router_w, router_bias, tm):
    n, d = h.shape
    rw = jnp.pad(router_w, ((0, 0), (0, LANES - N_EXPERTS))).astype(BF16)
    bias = jnp.pad(router_bias, (0, LANES - N_EXPERTS)).reshape(1, LANES)
    tri = jnp.tri(tm, k=-1, dtype=BF16)
    row = pl.BlockSpec((tm, LANES), lambda i: (i, 0))
    const = lambda shp: pl.BlockSpec(shp, lambda i: (0, 0))
    return pl.pallas_call(
        _router_kernel,
        grid=(n // tm,),
        in_specs=[pl.BlockSpec((tm, d), lambda i: (i, 0)), const((d, LANES)), const((1, LANES)),
                  const((tm, tm))],
        out_specs=[row, row, row, const((1, LANES))],
        out_shape=[jax.ShapeDtypeStruct((n, LANES), I32),
                   jax.ShapeDtypeStruct((n, LANES), F32),
                   jax.ShapeDtypeStruct((n, LANES), I32),
                   jax.ShapeDtypeStruct((1, LANES), I32)],
        scratch_shapes=[pltpu.VMEM((1, LANES), F32)],
        compiler_params=_cparams(1),
        name="router",
    )(h, rw, bias, tri)


def _dest_kernel(e_ref, rank_ref, gs_ref, o_ref):
    tm = e_ref.shape[0]
    lane = lax.broadcasted_iota(I32, (tm, LANES), 1)
    e = e_ref[...]
    gs = gs_ref[...].astype(F32)
    start = jnp.zeros((tm, LANES), F32)
    for k in range(TOP_K):
        s_k = jnp.sum(jnp.where(lane == e[:, k:k + 1], gs, 0.0), axis=1, keepdims=True)
        start = jnp.where(lane == k, s_k, start)
    o_ref[...] = start.astype(I32) + rank_ref[...]


def _dest_call(top_e, rank, gstart, tm):
    n = top_e.shape[0]
    row = pl.BlockSpec((tm, LANES), lambda i: (i, 0))
    return pl.pallas_call(
        _dest_kernel,
        grid=(n // tm,),
        in_specs=[row, row, pl.BlockSpec((1, LANES), lambda i: (0, 0))],
        out_specs=row,
        out_shape=jax.ShapeDtypeStruct((n, LANES), I32),
        compiler_params=_cparams(1),
        name="dest",
    )(top_e, rank, gstart)


def _dispatch_kernel(dest_ref, x_ref, xs_ref, sem, *, td):
    def row_copy(r, k):
        return pltpu.make_async_copy(x_ref.at[pl.ds(r, 1), :],
                                     xs_ref.at[pl.ds(dest_ref[r, k], 1), :], sem)

    for r in range(td):
        for k in range(TOP_K):
            row_copy(r, k).start(priority=k % N_DMA_QUEUES)

    def drain(r, carry):
        for k in range(TOP_K):
            row_copy(r, k).wait()
        return carry

    lax.fori_loop(0, td, drain, 0)


def _dispatch_call(h, dest, td):
    n, d = h.shape
    m = n * TOP_K
    return pl.pallas_call(
        functools.partial(_dispatch_kernel, td=td),
        grid=(n // td,),
        in_specs=[pl.BlockSpec((td, LANES), lambda i: (i, 0), memory_space=pltpu.SMEM),
                  pl.BlockSpec((td, d), lambda i: (i, 0))],
        out_specs=pl.BlockSpec(memory_space=pl.ANY),
        out_shape=jax.ShapeDtypeStruct((m, d), F32),
        scratch_shapes=[pltpu.SemaphoreType.DMA(())],
        compiler_params=_cparams(1),
        name="dispatch",
    )(dest, h)


def _moe_mm_kernel(ptile_ref, pexp_ref, plo_ref, phi_ref, xs_ref, wg_ref, wu_ref, wd_ref, y_ref,
                   *, tmm):
    p = pl.program_id(0)
    tile = ptile_ref[p]
    lo = plo_ref[p]
    hi = phi_ref[p]
    first = jnp.logical_or(p == 0, ptile_ref[jnp.maximum(p - 1, 0)] != tile)

    def expert_out():
        x = xs_ref[...].astype(BF16)
        g = jnp.dot(x, wg_ref[...].astype(BF16), preferred_element_type=F32)
        u = jnp.dot(x, wu_ref[...].astype(BF16), preferred_element_type=F32)
        hid = (_silu(g) * u).astype(BF16)
        return jnp.dot(hid, wd_ref[...].astype(BF16), preferred_element_type=F32)

    whole = jnp.logical_and(lo <= tile * tmm, hi >= (tile + 1) * tmm)

    @pl.when(whole)
    def _():
        y_ref[...] = expert_out()

    @pl.when(jnp.logical_and(first, jnp.logical_not(whole)))
    def _():
        y_ref[...] = jnp.zeros(y_ref.shape, F32)

    @pl.when(jnp.logical_and(lo < hi, jnp.logical_not(whole)))
    def _():
        y = expert_out()
        rowg = tile * tmm + lax.broadcasted_iota(I32, (tmm, 1), 0)
        y_ref[...] += jnp.where(jnp.logical_and(rowg >= lo, rowg < hi), y, 0.0)


def _moe_mm_call(xs, pairs, wg, wu, wd, layer, tmm):
    m = xs.shape[0]
    n_pairs = pairs[0].shape[0]
    rows = pl.BlockSpec((tmm, D_MODEL), lambda p, pt, pe, plo, phi: (pt[p], 0))
    wspec = lambda shp: pl.BlockSpec((None, None) + shp,
                                     lambda p, pt, pe, plo, phi: (layer, pe[p], 0, 0))
    return pl.pallas_call(
        functools.partial(_moe_mm_kernel, tmm=tmm),
        grid_spec=pltpu.PrefetchScalarGridSpec(
            num_scalar_prefetch=4,
            grid=(n_pairs,),
            in_specs=[rows, wspec((D_MODEL, EXPERT_DIM)), wspec((D_MODEL, EXPERT_DIM)),
                      wspec((EXPERT_DIM, D_MODEL))],
            out_specs=rows),
        out_shape=jax.ShapeDtypeStruct((m, D_MODEL), F32),
        compiler_params=_cparams(1),
        name="moe_mm",
    )(*pairs, xs, wg, wu, wd)


def _group_pairs(counts, m, tmm):
    n_tiles = m // tmm
    n_pairs = n_tiles + N_EXPERTS
    gend = jnp.cumsum(counts)
    gstart = gend - counts
    t0 = jnp.arange(n_tiles, dtype=I32) * tmm
    n_le = lambda edges, v: jnp.sum(edges[None, :] <= v[:, None], axis=1).astype(I32)
    e_first = n_le(gend, t0)
    e_last = n_le(gend, t0 + (tmm - 1))
    per_tile = e_last - e_first + 1
    pend = jnp.cumsum(per_tile)
    pstart = pend - per_tile
    p = jnp.arange(n_pairs, dtype=I32)
    valid = p < pend[-1]
    tile = jnp.minimum(n_le(pend, p), n_tiles - 1)
    e = jnp.clip(e_first[tile] + (p - pstart[tile]), 0, N_EXPERTS - 1).astype(I32)
    lo = jnp.where(valid, gstart[e], 0).astype(I32)
    hi = jnp.where(valid, gend[e], 0).astype(I32)
    return tile, e, lo, hi


def _combine_kernel(dest_ref, gate_ref, h_ref, y_ref, sg_ref, su_ref, sd_ref, g_ref, beta_ref, o_ref,
                    buf_ref, sem, *, tc):
    def row_copy(r, k):
        return pltpu.make_async_copy(y_ref.at[pl.ds(dest_ref[r, k], 1), :],
                                     buf_ref.at[k, pl.ds(r, 1), :], sem)

    for r in range(tc):
        for k in range(TOP_K):
            row_copy(r, k).start(priority=k % N_DMA_QUEUES)

    h = h_ref[...]
    x = h.astype(BF16)
    hid = (_silu(jnp.dot(x, sg_ref[...], preferred_element_type=F32))
           * jnp.dot(x, su_ref[...], preferred_element_type=F32)).astype(BF16)
    shared = jnp.dot(hid, sd_ref[...], preferred_element_type=F32)

    def drain(r, carry):
        for k in range(TOP_K):
            row_copy(r, k).wait()
        return carry

    lax.fori_loop(0, tc, drain, 0)

    gates = gate_ref[...]
    routed = gates[:, 0:1] * buf_ref[0]
    for k in range(1, TOP_K):
        routed = routed + gates[:, k:k + 1] * buf_ref[k]
    o_ref[...] = _layer_norm(ALPHA * h + (routed + shared), g_ref[...], beta_ref[...])


def _combine_call(h, y, dest, gates, sg, su, sd, g, beta, tc):
    n, d = h.shape
    smem = pl.BlockSpec((tc, LANES), lambda i: (i, 0), memory_space=pltpu.SMEM)
    const = lambda shp: pl.BlockSpec(shp, lambda i: (0, 0))
    return pl.pallas_call(
        functools.partial(_combine_kernel, tc=tc),
        grid=(n // tc,),
        in_specs=[smem, pl.BlockSpec((tc, LANES), lambda i: (i, 0)),
                  pl.BlockSpec((tc, d), lambda i: (i, 0)),
                  pl.BlockSpec(memory_space=pl.ANY),
                  const((d, SHARED_DIM)), const((d, SHARED_DIM)), const((SHARED_DIM, d)),
                  const((1, d)), const((1, d))],
        out_specs=pl.BlockSpec((tc, d), lambda i: (i, 0)),
        out_shape=jax.ShapeDtypeStruct((n, d), F32),
        scratch_shapes=[pltpu.VMEM((TOP_K, tc, d), F32),
                        pltpu.SemaphoreType.DMA(())],
        compiler_params=_cparams(1),
        name="combine",
    )(dest, gates, h, y, sg.astype(BF16), su.astype(BF16), sd.astype(BF16),
      g.reshape(1, d), beta.reshape(1, d))


def _tiles(b, s):
    n = b * s
    pick = lambda pref, total: next(t for t in (pref, 512, 256, 128, 64, 32, 16, 8) if t <= pref and total % t == 0)
    return dict(
        rows=pick(512, n),
        attn_q=pick(512, s),
        attn_k=pick(256, s),
        mem_rows=pick(512, s),
        mem_kv=pick(512, b * 256),
        router=pick(1024, n),
        dispatch=pick(128, n),
        moe=pick(512, n * TOP_K),
        combine=pick(128, n),
    )


def kernel(x, mem, ln_in_g, ln_in_b, w_in, a_kv_norm, a_w_uk, a_w_uv, b_lq1, b_lk1, b_lq2, b_lk2,
           b_subln, w_o, ln1_g, ln1_b, m_wq, m_wkv, m_wo, ln2_g, ln2_b, router_w, router_bias,
           e_w_gate, e_w_up, e_w_down, s_w_gate, s_w_up, s_w_down, ln3_g, ln3_b):
    b, s, d = x.shape
    n = b * s
    m = n * TOP_K
    t = _tiles(b, s)
    mem2d = mem.reshape(-1, d)

    h = _ln_call(x.reshape(n, d), ln_in_g, ln_in_b, t["rows"])
    for l in range(DEPTH):
        lambda_init = 0.8 - 0.6 * math.exp(-0.3 * l)

        qa, ckv, ckvt, qidx, kk, widx, qb, kb, vbt = _proj_call(h, _pad_w_in(w_in[l]), a_kv_norm[l],
                                                                  t["rows"], s)
        out_a = _dsa_call(qa, qidx, widx, kk, ckv, ckvt, _block_diag_uk(a_w_uk[l]),
                          _block_diag_uv_t(a_w_uv[l]), b, s, t["attn_q"], t["attn_k"])
        lam_rows = jnp.pad(jnp.stack([b_lq1[l], b_lk1[l], b_lq2[l], b_lk2[l]]),
                           ((0, SUBLANES - 4), (0, LANES - B_QK_DIM)))
        out_b = _diff_call(qb, kb, vbt, lam_rows, b_subln[l], b, s, t["attn_q"], t["attn_k"], lambda_init)
        h = _mix_out_call(h, out_a, out_b, w_o[l], ln1_g[l], ln1_b[l], t["rows"])

        mk, mv = _mem_kv_call(mem2d, m_wkv[l], t["mem_kv"])
        h = _mem_attn_call(h, m_wq[l], mk, mv, m_wo[l], ln2_g[l], ln2_b[l], b, s, t["mem_rows"])

        top_e, gates, rank, counts = _router_call(h, router_w[l], router_bias[l], t["router"])
        gstart = jnp.cumsum(counts, axis=1) - counts
        dest = _dest_call(top_e, rank, gstart, t["router"])
        counts = counts[0, :N_EXPERTS]
        xs = _dispatch_call(h, dest, t["dispatch"])
        pairs = _group_pairs(counts, m, t["moe"])
        y = _moe_mm_call(xs, pairs, e_w_gate, e_w_up, e_w_down, l, t["moe"])
        h = _combine_call(h, y, dest, gates, s_w_gate[l], s_w_up[l], s_w_down[l],
                          ln3_g[l], ln3_b[l], t["combine"])
    return h.reshape(b, s, d)
```

```python
import functools
import math

import jax
import jax.numpy as jnp
import numpy as np
from jax import lax
from jax.experimental import pallas as pl
from jax.experimental.pallas import tpu as pltpu

F32 = jnp.float32
BF16 = jnp.bfloat16
I32 = jnp.int32
I16 = jnp.int16

D_MODEL = 1024
DEPTH = 2
CHUNK = 64
CHUNK_SHIFT = 6
A_HEADS = 8
A_HEAD_DIM = 64
A_WIDTH = A_HEADS * A_HEAD_DIM
KV_LORA = 128
IDX_HEADS = 8
IDX_DIM = 64
INDEX_TOPK_MAX = 256
B_HEADS = 4
B_QK_DIM = 64
B_V_DIM = 128
B_WIDTH = B_HEADS * B_V_DIM
MEM_HEADS = 4
MEM_HEAD_DIM = D_MODEL // MEM_HEADS
N_EXPERTS = 64
TOP_K = 8
EXPERT_DIM = 256
SHARED_DIM = 256
ROUTED_SCALE = 2.5
ALPHA = (2 * DEPTH) ** 0.25
LN_EPS = 1e-5
RMS_EPS = 1e-6

LANES = 128
SUBLANES = 8
VMEM_LIMIT = 56 * 1024 * 1024
INT_MIN = -2 ** 31
I16_MIN = -2 ** 15
V_PAD = 16
N_DMA_QUEUES = 2
NEG_BIG = -3.0e38
M_INIT = -1.0e30

NT_DIMS = (((1,), (1,)), ((), ()))


def _cparams(n_axes):
    return pltpu.CompilerParams(dimension_semantics=("arbitrary",) * n_axes,
                                vmem_limit_bytes=VMEM_LIMIT)


def _layer_norm(x, g, b):
    mu = jnp.mean(x, axis=-1, keepdims=True)
    xc = x - mu
    var = jnp.mean(xc * xc, axis=-1, keepdims=True)
    return xc * lax.rsqrt(var + LN_EPS) * g + b


def _silu(x):
    return x * (1.0 / (1.0 + jnp.exp(-x)))


def _ln_kernel(x_ref, g_ref, b_ref, o_ref):
    o_ref[...] = _layer_norm(x_ref[...], g_ref[...], b_ref[...])


def _ln_call(x, g, b, tm):
    n, d = x.shape
    return pl.pallas_call(
        _ln_kernel,
        grid=(n // tm,),
        in_specs=[pl.BlockSpec((tm, d), lambda i: (i, 0)),
                  pl.BlockSpec((1, d), lambda i: (0, 0)),
                  pl.BlockSpec((1, d), lambda i: (0, 0))],
        out_specs=pl.BlockSpec((tm, d), lambda i: (i, 0)),
        out_shape=jax.ShapeDtypeStruct((n, d), F32),
        compiler_params=_cparams(1),
        name="ln_in",
    )(x, g.reshape(1, d), b.reshape(1, d))


_SEG_QA = (0, 512)
_SEG_CKV = (512, 640)
_SEG_QIDX = (640, 1152)
_SEG_KK = (1152, 1280)
_SEG_WIDX = (1280, 1408)
_SEG_QB = (1408, 1920)
_SEG_KB = (1920, 2432)
_SEG_VB = (2432, 2944)
_PROJ_COLS = 2944


def _pos_features(pos, shape):
    lane = lax.broadcasted_iota(I32, shape, 1)
    hi = lax.shift_right_logical(pos, CHUNK_SHIFT).astype(F32)
    lo = (pos & (CHUNK - 1)).astype(F32)
    return jnp.where(lane < 2, 1.0, jnp.where(lane == 2, hi, jnp.where(lane == 3, lo, 0.0)))


def _slope_features(tqi, slope, shape):
    feat = lax.broadcasted_iota(I32, shape, 0)
    hi = lax.shift_right_logical(tqi, CHUNK_SHIFT).astype(F32)
    lo = (tqi & (CHUNK - 1)).astype(F32)
    return jnp.where(feat == 0, -slope * CHUNK * hi,
                     jnp.where(feat == 1, -slope * lo,
                               jnp.where(feat == 2, slope * CHUNK,
                                         jnp.where(feat == 3, slope, 0.0))))


def _with_ones_rows(vt):
    t = vt.shape[1]
    sub = lax.broadcasted_iota(I32, (V_PAD, t), 0)
    return jnp.concatenate([vt, jnp.where(sub == 0, 1.0, 0.0)], axis=0)


def _proj_kernel(h_ref, w_ref, kvn_ref, qa_ref, ckv_ref, ckvt_ref, qidx_ref, kk_ref, widx_ref,
                 qb_ref, kb_ref, vbt_ref, *, seq):
    tm = h_ref.shape[0]
    x = h_ref[...].astype(BF16)

    def seg(s):
        return jnp.dot(x, w_ref[:, s[0]:s[1]], preferred_element_type=F32)

    row = pl.program_id(0) * tm + lax.broadcasted_iota(I32, (tm, LANES), 0)
    feat = _pos_features(row & (seq - 1), (tm, LANES)).astype(BF16)

    qa_ref[...] = (seg(_SEG_QA) * (A_HEAD_DIM ** -0.5)).astype(BF16)
    c = seg(_SEG_CKV)
    c = c * lax.rsqrt(jnp.mean(c * c, axis=-1, keepdims=True) + RMS_EPS) * kvn_ref[...]
    ckv_ref[:, :KV_LORA] = c.astype(BF16)
    ckv_ref[:, KV_LORA:] = feat
    ckvt_ref[...] = _with_ones_rows(c.T).astype(BF16)
    qidx_ref[...] = (seg(_SEG_QIDX) * (IDX_DIM ** -0.5)).astype(BF16)
    kk_ref[...] = seg(_SEG_KK).astype(BF16)
    widx_ref[...] = seg(_SEG_WIDX) * (IDX_HEADS ** -0.5)
    qb_ref[...] = (seg(_SEG_QB) * (B_QK_DIM ** -0.5)).astype(BF16)
    kb = seg(_SEG_KB).astype(BF16)
    vt = seg(_SEG_VB).T
    for h in range(B_HEADS):
        kb_ref[:, 2 * h * LANES:(2 * h + 1) * LANES] = kb[:, h * LANES:(h + 1) * LANES]
        kb_ref[:, (2 * h + 1) * LANES:(2 * h + 2) * LANES] = feat
        vbt_ref[h * (B_V_DIM + V_PAD):(h + 1) * (B_V_DIM + V_PAD), :] = _with_ones_rows(
            vt[h * B_V_DIM:(h + 1) * B_V_DIM]).astype(BF16)


def _proj_call(h, w_pad, kvn, tm, seq):
    n, d = h.shape
    assert seq & (seq - 1) == 0 and seq <= CHUNK * 256 and seq % tm == 0
    outs = [(512, BF16, False), (2 * KV_LORA, BF16, False), (KV_LORA + V_PAD, BF16, True),
            (512, BF16, False), (128, BF16, False), (128, F32, False), (512, BF16, False),
            (2 * B_HEADS * LANES, BF16, False), (B_HEADS * (B_V_DIM + V_PAD), BF16, True)]
    spec = lambda w, t: (pl.BlockSpec((w, tm), lambda i: (0, i)) if t
                         else pl.BlockSpec((tm, w), lambda i: (i, 0)))
    shape = lambda w, dt, t: jax.ShapeDtypeStruct((w, n) if t else (n, w), dt)
    return pl.pallas_call(
        functools.partial(_proj_kernel, seq=seq),
        grid=(n // tm,),
        in_specs=[pl.BlockSpec((tm, d), lambda i: (i, 0)),
                  pl.BlockSpec((d, _PROJ_COLS), lambda i: (0, 0)),
                  pl.BlockSpec((1, KV_LORA), lambda i: (0, 0))],
        out_specs=[spec(w, t) for w, _, t in outs],
        out_shape=[shape(w, dt, t) for w, dt, t in outs],
        compiler_params=_cparams(1),
        name="proj_in",
    )(h, w_pad, kvn.reshape(1, KV_LORA))


def _pad_w_in(w_in):
    sizes = (A_WIDTH, KV_LORA, IDX_HEADS * IDX_DIM, IDX_DIM, IDX_HEADS,
             2 * B_HEADS * B_QK_DIM, 2 * B_HEADS * B_QK_DIM, B_WIDTH)
    splits = np.cumsum(sizes)[:-1].tolist()
    q_a, c_kv, q_idx, k_idx, w_idx, q_b, k_b, v_b = jnp.split(w_in, splits, axis=-1)
    w_idx = jnp.pad(w_idx, ((0, 0), (0, LANES - IDX_HEADS)))
    return jnp.concatenate([q_a, c_kv, q_idx, k_idx, k_idx, w_idx, q_b, k_b, v_b],
                           axis=-1).astype(BF16)


def _flash_probs(sl, sc, m_ref):
    m_old = m_ref[:, sl]
    m_new = jnp.maximum(m_old, jnp.max(sc, axis=0, keepdims=True))
    m_ref[:, sl] = m_new
    return jnp.exp(m_old - m_new), jnp.exp(sc - m_new).astype(BF16)


def _flash_init(m_ref, acc_ref):
    m_ref[...] = jnp.full(m_ref.shape, M_INIT, F32)
    acc_ref[...] = jnp.zeros(acc_ref.shape, F32)


def _split_halves(q_ref, n_pairs, out_ref, tq):
    q_t = q_ref[...].astype(F32).T
    feat = lax.broadcasted_iota(I32, (LANES, tq), 0)
    for j in range(n_pairs):
        qp = q_t[j * LANES:(j + 1) * LANES, :]
        out_ref[:LANES, 2 * j * tq:(2 * j + 1) * tq] = jnp.where(feat < 64, qp, 0.0).astype(BF16)
        out_ref[:LANES, (2 * j + 1) * tq:(2 * j + 2) * tq] = jnp.where(feat >= 64, qp, 0.0).astype(BF16)


def _dsa_kernel(qa_ref, qidx_ref, widx_ref, kk_ref, ckv_ref, ckvt_ref, wk_ref, wvt_ref, out_ref,
                keys_ref, khi_ref, klo_ref, qim_ref, qlat_ref, tj_ref, m_ref, acc_ref,
                *, tq, tk, ts, seq, topk):
    i = pl.program_id(1)
    n_off = i * (tq // tk)
    n_kt = n_off + tq // tk
    idx_bits = int(math.log2(seq))

    _split_halves(qidx_ref, IDX_HEADS // 2, qim_ref, tq)
    qlat_t = lax.dot_general(wk_ref[...], qa_ref[...], NT_DIMS,
                             preferred_element_type=F32)
    t_row = i * tq + lax.broadcasted_iota(I32, (LANES, tq), 1)
    for h in range(A_HEADS):
        qlat_ref[:KV_LORA, h * tq:(h + 1) * tq] = qlat_t[h * KV_LORA:(h + 1) * KV_LORA, :].astype(BF16)
        qlat_ref[KV_LORA:, h * tq:(h + 1) * tq] = _slope_features(
            t_row, 2.0 ** -(h + 1), (LANES, tq)).astype(BF16)
    w_t = widx_ref[...].T

    row = lax.broadcasted_iota(I32, (tk, tq), 0)
    tqi = i * tq + lax.broadcasted_iota(I32, (tk, tq), 1)
    qchunk = lax.shift_right_logical(tqi, CHUNK_SHIFT)

    def idx_body(kt, carry):
        off = pl.multiple_of(kt * tk, tk)
        kkt = kk_ref[pl.ds(off, tk), :]
        s_all = jnp.dot(kkt, qim_ref[...], preferred_element_type=F32)
        acc = jnp.zeros((tk, tq), F32)
        for h in range(IDX_HEADS):
            acc = acc + w_t[h:h + 1, :] * jnp.maximum(s_all[:, h * tq:(h + 1) * tq], 0.0)
        bits = lax.bitcast_convert_type(acc + 0.0, I32)
        key = jnp.where(bits < 0, bits ^ jnp.int32(0x7FFFFFFF), bits)
        adm = lax.shift_right_logical(off + row, CHUNK_SHIFT) <= qchunk
        key = jnp.where(adm, key, jnp.int32(INT_MIN))
        keys_ref[pl.ds(off, tk), :] = key
        khi_ref[pl.ds(off, tk), :] = lax.shift_right_arithmetic(key, 16).astype(I16)
        klo_ref[pl.ds(off, tk), :] = ((key & 0xFFFF) + I16_MIN).astype(I16)
        return carry

    lax.fori_loop(0, n_kt, idx_body, 0)

    n_st = n_kt * (tk // ts)
    row_s = lax.broadcasted_iota(I32, (ts, tq), 0)

    def count(pred):
        def body(st, c):
            off = pl.multiple_of(st * ts, ts)
            ind = jnp.where(pred(keys_ref[pl.ds(off, ts), :], off + row_s), 1.0, 0.0)
            return c + jnp.sum(ind, axis=0, keepdims=True)
        return lax.fori_loop(0, n_st, body, jnp.zeros((1, tq), F32))

    def count16(ref, pred):
        pack = 2 * SUBLANES
        def body(st, c):
            off = pl.multiple_of(st * ts, ts)
            ind = jnp.where(pred(ref[pl.ds(off, ts), :]), jnp.int16(1), jnp.int16(0))
            parts = [ind[j * pack:(j + 1) * pack] for j in range(ts // pack)]
            while len(parts) > 1:
                parts = [parts[j] + parts[j + 1] for j in range(0, len(parts), 2)]
            return c + parts[0]
        c = lax.fori_loop(0, n_st, body, jnp.zeros((pack, tq), I16))
        return jnp.sum(c.astype(F32), axis=0, keepdims=True)

    def search16(ref, need):
        def body(it, t):
            cand = t + lax.shift_left(jnp.int32(1), 15 - it)
            cand16 = cand.astype(I16)
            cnt = count16(ref, lambda tile: tile >= cand16)
            return jnp.where(cnt >= need, cand, t)
        return lax.fori_loop(0, 16, body, jnp.full((1, tq), I16_MIN, I32))

    t_hi = search16(khi_ref, topk)
    t_hi16 = t_hi.astype(I16)
    n_above = count16(khi_ref, lambda tile: tile > t_hi16)

    def park_body(st, carry):
        off = pl.multiple_of(st * ts, ts)
        klo_ref[pl.ds(off, ts), :] = jnp.where(khi_ref[pl.ds(off, ts), :] == t_hi16,
                                               klo_ref[pl.ds(off, ts), :], jnp.int16(I16_MIN))
        return carry

    lax.fori_loop(0, n_st, park_body, 0)
    t_lo = search16(klo_ref, topk - n_above)
    thr = t_hi * 65536 + (t_lo - I16_MIN)

    live = thr > INT_MIN
    n_ge = count(lambda tile, _: tile >= thr)
    excess = jnp.logical_and(n_ge > topk, live)
    all_ties = jnp.where(live, seq, -1).astype(I32)
    tj_ref[0] = jnp.broadcast_to(thr, (SUBLANES, tq))
    tj_ref[1] = jnp.broadcast_to(all_ties, (SUBLANES, tq))

    @pl.when(jnp.max(jnp.where(excess, 1.0, 0.0)) > 0.0)
    def _():
        need = topk - count(lambda tile, _: tile > thr)

        def tie_body(it, hi):
            cand = hi + lax.shift_left(jnp.int32(1), idx_bits - 1 - it)
            cnt = count(lambda tile, pos: jnp.logical_and(tile == thr, pos < cand))
            return jnp.where(cnt < need, cand, hi)

        hi = lax.fori_loop(0, idx_bits, tie_body, jnp.zeros((1, tq), I32))
        tj_ref[1] = jnp.broadcast_to(jnp.where(excess, hi, all_ties), (SUBLANES, tq))

    thr_b = tj_ref[0][:1]
    tie_hi = tj_ref[1][:1]

    _flash_init(m_ref, acc_ref)

    def att_tile(kt, late_fix):
        off = pl.multiple_of(kt * tk, tk)
        tile = keys_ref[pl.ds(off, tk), :]
        pos = off + row
        sel = jnp.logical_or(tile > thr_b,
                             jnp.logical_and(tile == thr_b, pos <= tie_hi))
        s_all = jnp.dot(ckv_ref[pl.ds(off, tk), :], qlat_ref[...],
                        preferred_element_type=F32)
        if late_fix:
            late = jnp.maximum(pos - tqi, 0).astype(F32)
        alphas, probs = [], []
        for h in range(A_HEADS):
            sl = slice(h * tq, (h + 1) * tq)
            sc = s_all[:, sl]
            if late_fix:
                sc = sc - (2.0 * 2.0 ** -(h + 1)) * late
            alpha, p = _flash_probs(sl, jnp.where(sel, sc, NEG_BIG), m_ref)
            alphas.append(alpha)
            probs.append(p)
        pv = jnp.dot(ckvt_ref[:, pl.ds(off, tk)], jnp.concatenate(probs, axis=1),
                     preferred_element_type=F32)
        acc_ref[...] = jnp.concatenate(alphas, axis=1) * acc_ref[...] + pv

    def off_body(kt, carry):
        att_tile(kt, False)
        return carry

    def diag_body(kt, carry):
        att_tile(kt, True)
        return carry

    lax.fori_loop(0, n_off, off_body, 0)
    lax.fori_loop(n_off, n_kt, diag_body, 0)

    inv_l = 1.0 / acc_ref[KV_LORA:KV_LORA + 1, :]
    olat = jnp.concatenate([(acc_ref[:KV_LORA, h * tq:(h + 1) * tq]
                             * inv_l[:, h * tq:(h + 1) * tq]).astype(BF16)
                            for h in range(A_HEADS)], axis=0)
    out_t = jnp.dot(wvt_ref[...], olat, preferred_element_type=F32)
    out_ref[...] = out_t.T.astype(BF16)


def _dsa_call(qa, qidx, widx, kk, ckv, ckvt, wk_bd, wvt_bd, b, s, tq, tk, ts):
    topk = min(INDEX_TOPK_MAX, s // 4)
    assert s % tq == 0 and tq % tk == 0 and tk % ts == 0 and ts % LANES == 0 and s & (s - 1) == 0
    blk = lambda w: pl.BlockSpec((None, tq, w), lambda bi, i: (bi, i, 0))
    full = lambda w: pl.BlockSpec((None, s, w), lambda bi, i: (bi, 0, 0))
    const = lambda a: pl.BlockSpec(a.shape, lambda bi, i: (0, 0))
    r3 = lambda a: a.reshape(b, s, a.shape[-1])
    out = pl.pallas_call(
        functools.partial(_dsa_kernel, tq=tq, tk=tk, ts=ts, seq=s, topk=topk),
        grid=(b, s // tq),
        in_specs=[blk(A_WIDTH), blk(IDX_HEADS * IDX_DIM), blk(LANES), full(LANES),
                  full(2 * KV_LORA),
                  pl.BlockSpec((KV_LORA + V_PAD, s), lambda bi, i: (0, bi)),
                  const(wk_bd), const(wvt_bd)],
        out_specs=blk(A_WIDTH),
        out_shape=jax.ShapeDtypeStruct((b, s, A_WIDTH), BF16),
        scratch_shapes=[pltpu.VMEM((s, tq), I32),
                        pltpu.VMEM((s, tq), I16),
                        pltpu.VMEM((s, tq), I16),
                        pltpu.VMEM((LANES, IDX_HEADS * tq), BF16),
                        pltpu.VMEM((2 * KV_LORA, A_HEADS * tq), BF16),
                        pltpu.VMEM((2, SUBLANES, tq), I32),
                        pltpu.VMEM((1, A_HEADS * tq), F32),
                        pltpu.VMEM((KV_LORA + V_PAD, A_HEADS * tq), F32)],
        compiler_params=_cparams(2),
        name="dsa",
    )(r3(qa), r3(qidx), r3(widx), r3(kk), r3(ckv), ckvt, wk_bd, wvt_bd)
    return out.reshape(b * s, A_WIDTH)


def _block_diag_uk(w_uk):
    h, c, d = w_uk.shape
    eye = jnp.eye(h, dtype=w_uk.dtype)
    return jnp.einsum('hcd,hg->hcgd', w_uk, eye).reshape(h * c, h * d).astype(BF16)


def _block_diag_uv_t(w_uv):
    h, c, d = w_uv.shape
    eye = jnp.eye(h, dtype=w_uv.dtype)
    return jnp.einsum('hcd,hg->hdgc', w_uv, eye).reshape(h * d, h * c).astype(BF16)


def _diff_kernel(qb_ref, kb_ref, vbt_ref, lam_ref, subln_ref, out_ref,
                 qm_ref, m_ref, acc_ref, *, tq, tk, lambda_init):
    i = pl.program_id(1)
    n_off = i * (tq // tk)
    dvp = B_V_DIM + V_PAD
    _split_halves(qb_ref, B_HEADS, qm_ref, tq)
    t_row = i * tq + lax.broadcasted_iota(I32, (LANES, tq), 1)
    for h in range(B_HEADS):
        feat = _slope_features(t_row, 2.0 ** (-2 * (h + 1)), (LANES, tq)).astype(BF16)
        qm_ref[LANES:, 2 * h * tq:(2 * h + 1) * tq] = feat
        qm_ref[LANES:, (2 * h + 1) * tq:(2 * h + 2) * tq] = feat

    row = lax.broadcasted_iota(I32, (tk, tq), 0)
    tqi = i * tq + lax.broadcasted_iota(I32, (tk, tq), 1)
    qchunk = lax.shift_right_logical(tqi, CHUNK_SHIFT)
    _flash_init(m_ref, acc_ref)

    def tile_step(kt, diag):
        off = pl.multiple_of(kt * tk, tk)
        s_pairs = [jnp.dot(kb_ref[pl.ds(off, tk), 2 * h * LANES:(2 * h + 2) * LANES],
                           qm_ref[:, 2 * h * tq:(2 * h + 2) * tq],
                           preferred_element_type=F32) for h in range(B_HEADS)]
        if diag:
            pos = off + row
            adm = lax.shift_right_logical(pos, CHUNK_SHIFT) <= qchunk
            late = jnp.maximum(pos - tqi, 0).astype(F32)
        alphas, probs = [], []
        for h in range(B_HEADS):
            for j in range(2):
                sl = slice((2 * h + j) * tq, (2 * h + j + 1) * tq)
                sc = s_pairs[h][:, j * tq:(j + 1) * tq]
                if diag:
                    sc = jnp.where(adm, sc - (2.0 * 2.0 ** (-2 * (h + 1))) * late, NEG_BIG)
                alpha, p = _flash_probs(sl, sc, m_ref)
                alphas.append(alpha)
                probs.append(p)
        pv = jnp.concatenate(
            [jnp.dot(vbt_ref[h * dvp:(h + 1) * dvp, pl.ds(off, tk)],
                     jnp.concatenate(probs[2 * h:2 * h + 2], axis=1), preferred_element_type=F32)
             for h in range(B_HEADS)], axis=1)
        acc_ref[...] = jnp.concatenate(alphas, axis=1) * acc_ref[...] + pv

    def off_body(kt, carry):
        tile_step(kt, False)
        return carry

    def diag_body(kt, carry):
        tile_step(kt, True)
        return carry

    lax.fori_loop(0, n_off, off_body, 0)
    lax.fori_loop(n_off, n_off + tq // tk, diag_body, 0)

    lp = lam_ref[...]
    lam = (jnp.exp(jnp.sum(lp[0:1] * lp[1:2], axis=1, keepdims=True))
           - jnp.exp(jnp.sum(lp[2:3] * lp[3:4], axis=1, keepdims=True)) + lambda_init)
    o_all = acc_ref[:B_V_DIM, :] * (1.0 / acc_ref[B_V_DIM:B_V_DIM + 1, :])
    outs = []
    for h in range(B_HEADS):
        o = (o_all[:, 2 * h * tq:(2 * h + 1) * tq]
             - lam * o_all[:, (2 * h + 1) * tq:(2 * h + 2) * tq])
        o = o * lax.rsqrt(jnp.mean(o * o, axis=0, keepdims=True) + RMS_EPS) * subln_ref[...]
        outs.append(o * (1.0 - lambda_init))
    out_ref[...] = jnp.concatenate(outs, axis=0).T.astype(BF16)


def _diff_call(qb, kb, vbt, lam_rows, subln, b, s, tq, tk, lambda_init):
    blk = pl.BlockSpec((None, tq, B_WIDTH), lambda bi, i: (bi, i, 0))
    kw = 2 * B_HEADS * LANES
    dvp = B_V_DIM + V_PAD
    out = pl.pallas_call(
        functools.partial(_diff_kernel, tq=tq, tk=tk, lambda_init=lambda_init),
        grid=(b, s // tq),
        in_specs=[blk, pl.BlockSpec((None, s, kw), lambda bi, i: (bi, 0, 0)),
                  pl.BlockSpec((B_HEADS * dvp, s), lambda bi, i: (0, bi)),
                  pl.BlockSpec((SUBLANES, LANES), lambda bi, i: (0, 0)),
                  pl.BlockSpec((B_V_DIM, 1), lambda bi, i: (0, 0))],
        out_specs=blk,
        out_shape=jax.ShapeDtypeStruct((b, s, B_WIDTH), BF16),
        scratch_shapes=[pltpu.VMEM((2 * LANES, 2 * B_HEADS * tq), BF16),
                        pltpu.VMEM((1, 2 * B_HEADS * tq), F32),
                        pltpu.VMEM((dvp, 2 * B_HEADS * tq), F32)],
        compiler_params=_cparams(2),
        name="diff",
    )(qb.reshape(b, s, B_WIDTH), kb.reshape(b, s, kw), vbt, lam_rows, subln.reshape(B_V_DIM, 1))
    return out.reshape(b * s, B_WIDTH)


def _mix_out_kernel(h_ref, a_ref, b_ref, wa_ref, wb_ref, g_ref, beta_ref, o_ref):
    y = (jnp.dot(a_ref[...], wa_ref[...], preferred_element_type=F32)
         + jnp.dot(b_ref[...], wb_ref[...], preferred_element_type=F32))
    o_ref[...] = _layer_norm(ALPHA * h_ref[...] + y, g_ref[...], beta_ref[...])


def _mix_out_call(h, out_a, out_b, w_o, g, beta, tm):
    n, d = h.shape
    w = w_o.astype(BF16)
    row = lambda wd: pl.BlockSpec((tm, wd), lambda i: (i, 0))
    const = lambda shp: pl.BlockSpec(shp, lambda i: (0, 0))
    return pl.pallas_call(
        _mix_out_kernel,
        grid=(n // tm,),
        in_specs=[row(d), row(A_WIDTH), row(B_WIDTH), const((A_WIDTH, d)), const((B_WIDTH, d)),
                  const((1, d)), const((1, d))],
        out_specs=row(d),
        out_shape=jax.ShapeDtypeStruct((n, d), F32),
        compiler_params=_cparams(1),
        name="mix_out",
    )(h, out_a, out_b, w[:A_WIDTH], w[A_WIDTH:], g.reshape(1, d), beta.reshape(1, d))


def _mem_kv_kernel(m_ref, w_ref, k_ref, v_ref):
    x = m_ref[...].astype(BF16)
    d = k_ref.shape[-1]
    k_ref[...] = jnp.dot(x, w_ref[:, :d], preferred_element_type=F32).astype(BF16)
    v_ref[...] = jnp.dot(x, w_ref[:, d:], preferred_element_type=F32).astype(BF16)


def _mem_kv_call(mem2d, wkv, tm):
    n, d = mem2d.shape
    return pl.pallas_call(
        _mem_kv_kernel,
        grid=(n // tm,),
        in_specs=[pl.BlockSpec((tm, d), lambda i: (i, 0)),
                  pl.BlockSpec((d, 2 * d), lambda i: (0, 0))],
        out_specs=[pl.BlockSpec((tm, d), lambda i: (i, 0))] * 2,
        out_shape=[jax.ShapeDtypeStruct((n, d), BF16)] * 2,
        compiler_params=_cparams(1),
        name="mem_kv",
    )(mem2d, wkv.astype(BF16))


def _mem_attn_kernel(h_ref, wq_ref, k_ref, v_ref, wo_ref, g_ref, beta_ref, o_ref):
    h = h_ref[...]
    q = jnp.dot(h.astype(BF16), wq_ref[...], preferred_element_type=F32)
    q = (q * (MEM_HEAD_DIM ** -0.5)).astype(BF16)
    outs = []
    for hd in range(MEM_HEADS):
        sl = slice(hd * MEM_HEAD_DIM, (hd + 1) * MEM_HEAD_DIM)
        sc = lax.dot_general(q[:, sl], k_ref[:, sl], NT_DIMS, preferred_element_type=F32)
        p = jnp.exp(sc - jnp.max(sc, axis=1, keepdims=True))
        den = jnp.sum(p, axis=1, keepdims=True)
        o = jnp.dot(p.astype(BF16), v_ref[:, sl], preferred_element_type=F32) / den
        outs.append(o.astype(BF16))
    y = jnp.dot(jnp.concatenate(outs, axis=1), wo_ref[...], preferred_element_type=F32)
    o_ref[...] = _layer_norm(ALPHA * h + y, g_ref[...], beta_ref[...])


def _mem_attn_call(h, wq, k, v, wo, g, beta, b, s, tm):
    n, d = h.shape
    m = k.shape[0] // b
    const = lambda shp: pl.BlockSpec(shp, lambda bi, i: (0,) * len(shp))
    row = pl.BlockSpec((None, tm, d), lambda bi, i: (bi, i, 0))
    kv = pl.BlockSpec((None, m, d), lambda bi, i: (bi, 0, 0))
    out = pl.pallas_call(
        _mem_attn_kernel,
        grid=(b, s // tm),
        in_specs=[row, const((d, d)), kv, kv, const((d, d)), const((1, d)), const((1, d))],
        out_specs=row,
        out_shape=jax.ShapeDtypeStruct((b, s, d), F32),
        compiler_params=_cparams(2),
        name="mem_attn",
    )(h.reshape(b, s, d), wq.astype(BF16), k.reshape(b, m, d), v.reshape(b, m, d),
      wo.astype(BF16), g.reshape(1, d), beta.reshape(1, d))
    return out.reshape(n, d)


def _router_kernel(h_ref, rw_ref, bias_ref, tri_ref, e_ref, g_ref, rank_ref, cnt_ref, run_ref):
    tm = h_ref.shape[0]

    @pl.when(pl.program_id(0) == 0)
    def _():
        run_ref[...] = jnp.zeros(run_ref.shape, F32)

    logits = jnp.dot(h_ref[...].astype(BF16), rw_ref[...], preferred_element_type=F32)
    scores = 1.0 / (1.0 + jnp.exp(-logits))
    lane = lax.broadcasted_iota(I32, (tm, LANES), 1)
    lanef = lane.astype(F32)
    ninf = jnp.float32(-jnp.inf)
    cur = jnp.where(lane < N_EXPERTS, scores + bias_ref[...], ninf)
    top_e = jnp.zeros((tm, LANES), F32)
    top_s = jnp.zeros((tm, LANES), F32)
    picked = jnp.zeros((tm, LANES), F32)
    firsts = []
    for k in range(TOP_K):
        mx = jnp.max(cur, axis=1, keepdims=True)
        first = jnp.min(jnp.where(cur == mx, lanef, float(LANES)), axis=1, keepdims=True)
        hit = lanef == first
        s_k = jnp.sum(jnp.where(hit, scores, 0.0), axis=1, keepdims=True)
        top_e = jnp.where(lane == k, first, top_e)
        top_s = jnp.where(lane == k, s_k, top_s)
        picked = jnp.where(hit, 1.0, picked)
        cur = jnp.where(hit, ninf, cur)
        firsts.append(first)
    e_ref[...] = top_e.astype(I32)
    g_ref[...] = top_s / jnp.sum(top_s, axis=1, keepdims=True) * ROUTED_SCALE

    slot = jnp.dot(tri_ref[...], picked.astype(BF16), preferred_element_type=F32) + run_ref[...]
    rank = jnp.zeros((tm, LANES), F32)
    for k in range(TOP_K):
        r_k = jnp.sum(jnp.where(lanef == firsts[k], slot, 0.0), axis=1, keepdims=True)
        rank = jnp.where(lane == k, r_k, rank)
    rank_ref[...] = rank.astype(I32)
    run = run_ref[...] + jnp.sum(picked, axis=0, keepdims=True)
    run_ref[...] = run
    cnt_ref[...] = run.astype(I32)


def _router_call(h, router_w, router_bias, tm):
    n, d = h.shape
    rw = jnp.pad(router_w, ((0, 0), (0, LANES - N_EXPERTS))).astype(BF16)
    bias = jnp.pad(router_bias, (0, LANES - N_EXPERTS)).reshape(1, LANES)
    tri = jnp.tri(tm, k=-1, dtype=BF16)
    row = pl.BlockSpec((tm, LANES), lambda i: (i, 0))
    const = lambda shp: pl.BlockSpec(shp, lambda i: (0, 0))
    return pl.pallas_call(
        _router_kernel,
        grid=(n // tm,),
        in_specs=[pl.BlockSpec((tm, d), lambda i: (i, 0)), const((d, LANES)), const((1, LANES)),
                  const((tm, tm))],
        out_specs=[row, row, row, const((1, LANES))],
        out_shape=[jax.ShapeDtypeStruct((n, LANES), I32),
                   jax.ShapeDtypeStruct((n, LANES), F32),
                   jax.ShapeDtypeStruct((n, LANES), I32),
                   jax.ShapeDtypeStruct((1, LANES), I32)],
        scratch_shapes=[pltpu.VMEM((1, LANES), F32)],
        compiler_params=_cparams(1),
        name="router",
    )(h, rw, bias, tri)


def _dest_kernel(e_ref, rank_ref, gs_ref, o_ref):
    tm = e_ref.shape[0]
    lane = lax.broadcasted_iota(I32, (tm, LANES), 1)
    e = e_ref[...]
    gs = gs_ref[...].astype(F32)
    start = jnp.zeros((tm, LANES), F32)
    for k in range(TOP_K):
        s_k = jnp.sum(jnp.where(lane == e[:, k:k + 1], gs, 0.0), axis=1, keepdims=True)
        start = jnp.where(lane == k, s_k, start)
    o_ref[...] = start.astype(I32) + rank_ref[...]


def _dest_call(top_e, rank, gstart, tm):
    n = top_e.shape[0]
    row = pl.BlockSpec((tm, LANES), lambda i: (i, 0))
    return pl.pallas_call(
        _dest_kernel,
        grid=(n // tm,),
        in_specs=[row, row, pl.BlockSpec((1, LANES), lambda i: (0, 0))],
        out_specs=row,
        out_shape=jax.ShapeDtypeStruct((n, LANES), I32),
        compiler_params=_cparams(1),
        name="dest",
    )(top_e, rank, gstart)


def _dispatch_kernel(dest_ref, x_ref, xs_ref, sem, *, td):
    def row_copy(r, k):
        return pltpu.make_async_copy(x_ref.at[pl.ds(r, 1), :],
                                     xs_ref.at[pl.ds(dest_ref[r, k], 1), :], sem)

    for r in range(td):
        for k in range(TOP_K):
            row_copy(r, k).start(priority=k % N_DMA_QUEUES)

    def drain(r, carry):
        for k in range(TOP_K):
            row_copy(r, k).wait()
        return carry

    lax.fori_loop(0, td, drain, 0)


def _dispatch_call(h, dest, td):
    n, d = h.shape
    m = n * TOP_K
    return pl.pallas_call(
        functools.partial(_dispatch_kernel, td=td),
        grid=(n // td,),
        in_specs=[pl.BlockSpec((td, LANES), lambda i: (i, 0), memory_space=pltpu.SMEM),
                  pl.BlockSpec((td, d), lambda i: (i, 0))],
        out_specs=pl.BlockSpec(memory_space=pl.ANY),
        out_shape=jax.ShapeDtypeStruct((m, d), F32),
        scratch_shapes=[pltpu.SemaphoreType.DMA(())],
        compiler_params=_cparams(1),
        name="dispatch",
    )(dest, h)


def _moe_mm_kernel(ptile_ref, pexp_ref, plo_ref, phi_ref, xs_ref, wg_ref, wu_ref, wd_ref, y_ref,
                   *, tmm):
    p = pl.program_id(0)
    tile = ptile_ref[p]
    lo = plo_ref[p]
    hi = phi_ref[p]
    first = jnp.logical_or(p == 0, ptile_ref[jnp.maximum(p - 1, 0)] != tile)

    def expert_out():
        x = xs_ref[...].astype(BF16)
        g = jnp.dot(x, wg_ref[...].astype(BF16), preferred_element_type=F32)
        u = jnp.dot(x, wu_ref[...].astype(BF16), preferred_element_type=F32)
        hid = (_silu(g) * u).astype(BF16)
        return jnp.dot(hid, wd_ref[...].astype(BF16), preferred_element_type=F32)

    whole = jnp.logical_and(lo <= tile * tmm, hi >= (tile + 1) * tmm)

    @pl.when(whole)
    def _():
        y_ref[...] = expert_out()

    @pl.when(jnp.logical_and(first, jnp.logical_not(whole)))
    def _():
        y_ref[...] = jnp.zeros(y_ref.shape, F32)

    @pl.when(jnp.logical_and(lo < hi, jnp.logical_not(whole)))
    def _():
        y = expert_out()
        rowg = tile * tmm + lax.broadcasted_iota(I32, (tmm, 1), 0)
        y_ref[...] += jnp.where(jnp.logical_and(rowg >= lo, rowg < hi), y, 0.0)


def _moe_mm_call(xs, pairs, wg, wu, wd, layer, tmm):
    m = xs.shape[0]
    n_pairs = pairs[0].shape[0]
    rows = pl.BlockSpec((tmm, D_MODEL), lambda p, pt, pe, plo, phi: (pt[p], 0))
    wspec = lambda shp: pl.BlockSpec((None, None) + shp,
                                     lambda p, pt, pe, plo, phi: (layer, pe[p], 0, 0))
    return pl.pallas_call(
        functools.partial(_moe_mm_kernel, tmm=tmm),
        grid_spec=pltpu.PrefetchScalarGridSpec(
            num_scalar_prefetch=4,
            grid=(n_pairs,),
            in_specs=[rows, wspec((D_MODEL, EXPERT_DIM)), wspec((D_MODEL, EXPERT_DIM)),
                      wspec((EXPERT_DIM, D_MODEL))],
            out_specs=rows),
        out_shape=jax.ShapeDtypeStruct((m, D_MODEL), F32),
        compiler_params=_cparams(1),
        name="moe_mm",
    )(*pairs, xs, wg, wu, wd)


def _group_pairs(counts, m, tmm):
    n_tiles = m // tmm
    n_pairs = n_tiles + N_EXPERTS
    gend = jnp.cumsum(counts)
    gstart = gend - counts
    t0 = jnp.arange(n_tiles, dtype=I32) * tmm
    n_le = lambda edges, v: jnp.sum(edges[None, :] <= v[:, None], axis=1).astype(I32)
    e_first = n_le(gend, t0)
    e_last = n_le(gend, t0 + (tmm - 1))
    per_tile = e_last - e_first + 1
    pend = jnp.cumsum(per_tile)
    pstart = pend - per_tile
    p = jnp.arange(n_pairs, dtype=I32)
    valid = p < pend[-1]
    tile = jnp.minimum(n_le(pend, p), n_tiles - 1)
    e = jnp.clip(e_first[tile] + (p - pstart[tile]), 0, N_EXPERTS - 1).astype(I32)
    lo = jnp.where(valid, gstart[e], 0).astype(I32)
    hi = jnp.where(valid, gend[e], 0).astype(I32)
    return tile, e, lo, hi


def _combine_kernel(dest_ref, gate_ref, h_ref, y_ref, sg_ref, su_ref, sd_ref, g_ref, beta_ref, o_ref,
                    buf_ref, sem, *, tc):
    def row_copy(r, k):
        return pltpu.make_async_copy(y_ref.at[pl.ds(dest_ref[r, k], 1), :],
                                     buf_ref.at[k, pl.ds(r, 1), :], sem)

    for r in range(tc):
        for k in range(TOP_K):
            row_copy(r, k).start(priority=k % N_DMA_QUEUES)

    h = h_ref[...]
    x = h.astype(BF16)
    hid = (_silu(jnp.dot(x, sg_ref[...], preferred_element_type=F32))
           * jnp.dot(x, su_ref[...], preferred_element_type=F32)).astype(BF16)
    shared = jnp.dot(hid, sd_ref[...], preferred_element_type=F32)

    def drain(r, carry):
        for k in range(TOP_K):
            row_copy(r, k).wait()
        return carry

    lax.fori_loop(0, tc, drain, 0)

    gates = gate_ref[...]
    routed = gates[:, 0:1] * buf_ref[0]
    for k in range(1, TOP_K):
        routed = routed + gates[:, k:k + 1] * buf_ref[k]
    o_ref[...] = _layer_norm(ALPHA * h + (routed + shared), g_ref[...], beta_ref[...])


def _combine_call(h, y, dest, gates, sg, su, sd, g, beta, tc):
    n, d = h.shape
    smem = pl.BlockSpec((tc, LANES), lambda i: (i, 0), memory_space=pltpu.SMEM)
    const = lambda shp: pl.BlockSpec(shp, lambda i: (0, 0))
    return pl.pallas_call(
        functools.partial(_combine_kernel, tc=tc),
        grid=(n // tc,),
        in_specs=[smem, pl.BlockSpec((tc, LANES), lambda i: (i, 0)),
                  pl.BlockSpec((tc, d), lambda i: (i, 0)),
                  pl.BlockSpec(memory_space=pl.ANY),
                  const((d, SHARED_DIM)), const((d, SHARED_DIM)), const((SHARED_DIM, d)),
                  const((1, d)), const((1, d))],
        out_specs=pl.BlockSpec((tc, d), lambda i: (i, 0)),
        out_shape=jax.ShapeDtypeStruct((n, d), F32),
        scratch_shapes=[pltpu.VMEM((TOP_K, tc, d), F32),
                        pltpu.SemaphoreType.DMA(())],
        compiler_params=_cparams(1),
        name="combine",
    )(dest, gates, h, y, sg.astype(BF16), su.astype(BF16), sd.astype(BF16),
      g.reshape(1, d), beta.reshape(1, d))


def _tiles(b, s):
    n = b * s
    pick = lambda pref, total: next(t for t in (pref, 512, 256, 128, 64, 32, 16, 8) if t <= pref and total % t == 0)
    return dict(
        rows=pick(512, n),
        attn_q=pick(512, s),
        attn_k=pick(512, s),
        select=pick(256, s),
        mem_rows=pick(512, s),
        mem_kv=pick(512, b * 256),
        router=pick(1024, n),
        dispatch=pick(128, n),
        moe=pick(512, n * TOP_K),
        combine=pick(128, n),
    )


def kernel(x, mem, ln_in_g, ln_in_b, w_in, a_kv_norm, a_w_uk, a_w_uv, b_lq1, b_lk1, b_lq2, b_lk2,
           b_subln, w_o, ln1_g, ln1_b, m_wq, m_wkv, m_wo, ln2_g, ln2_b, router_w, router_bias,
           e_w_gate, e_w_up, e_w_down, s_w_gate, s_w_up, s_w_down, ln3_g, ln3_b):
    b, s, d = x.shape
    n = b * s
    m = n * TOP_K
    t = _tiles(b, s)
    mem2d = mem.reshape(-1, d)

    h = _ln_call(x.reshape(n, d), ln_in_g, ln_in_b, t["rows"])
    for l in range(DEPTH):
        lambda_init = 0.8 - 0.6 * math.exp(-0.3 * l)

        qa, ckv, ckvt, qidx, kk, widx, qb, kb, vbt = _proj_call(h, _pad_w_in(w_in[l]), a_kv_norm[l],
                                                                  t["rows"], s)
        out_a = _dsa_call(qa, qidx, widx, kk, ckv, ckvt, _block_diag_uk(a_w_uk[l]),
                          _block_diag_uv_t(a_w_uv[l]), b, s, t["attn_q"], t["attn_k"], t["select"])
        lam_rows = jnp.pad(jnp.stack([b_lq1[l], b_lk1[l], b_lq2[l], b_lk2[l]]),
                           ((0, SUBLANES - 4), (0, LANES - B_QK_DIM)))
        out_b = _diff_call(qb, kb, vbt, lam_rows, b_subln[l], b, s, t["attn_q"], t["attn_k"], lambda_init)
        h = _mix_out_call(h, out_a, out_b, w_o[l], ln1_g[l], ln1_b[l], t["rows"])

        mk, mv = _mem_kv_call(mem2d, m_wkv[l], t["mem_kv"])
        h = _mem_attn_call(h, m_wq[l], mk, mv, m_wo[l], ln2_g[l], ln2_b[l], b, s, t["mem_rows"])

        top_e, gates, rank, counts = _router_call(h, router_w[l], router_bias[l], t["router"])
        gstart = jnp.cumsum(counts, axis=1) - counts
        dest = _dest_call(top_e, rank, gstart, t["router"])
        counts = counts[0, :N_EXPERTS]
        xs = _dispatch_call(h, dest, t["dispatch"])
        pairs = _group_pairs(counts, m, t["moe"])
        y = _moe_mm_call(xs, pairs, e_w_gate, e_w_up, e_w_down, l, t["moe"])
        h = _combine_call(h, y, dest, gates, s_w_gate[l], s_w_up[l], s_w_down[l],
                          ln3_g[l], ln3_b[l], t["combine"])
    return h.reshape(b, s, d)
```

```python
import functools
import math

import jax
import jax.numpy as jnp
import numpy as np
from jax import lax
from jax.experimental import pallas as pl
from jax.experimental.pallas import tpu as pltpu

F32 = jnp.float32
BF16 = jnp.bfloat16
I32 = jnp.int32
I16 = jnp.int16

D_MODEL = 1024
DEPTH = 2
CHUNK = 64
CHUNK_SHIFT = 6
A_HEADS = 8
A_HEAD_DIM = 64
A_WIDTH = A_HEADS * A_HEAD_DIM
KV_LORA = 128
IDX_HEADS = 8
IDX_DIM = 64
INDEX_TOPK_MAX = 256
B_HEADS = 4
B_QK_DIM = 64
B_V_DIM = 128
B_WIDTH = B_HEADS * B_V_DIM
MEM_HEADS = 4
MEM_HEAD_DIM = D_MODEL // MEM_HEADS
N_EXPERTS = 64
TOP_K = 8
EXPERT_DIM = 256
SHARED_DIM = 256
ROUTED_SCALE = 2.5
ALPHA = (2 * DEPTH) ** 0.25
LN_EPS = 1e-5
RMS_EPS = 1e-6

LANES = 128
SUBLANES = 8
VMEM_LIMIT = 56 * 1024 * 1024
INT_MIN = -2 ** 31
I16_MIN = -2 ** 15
V_PAD = 16
N_DMA_QUEUES = 2
NEG_BIG = -3.0e38
M_INIT = -1.0e30

NT_DIMS = (((1,), (1,)), ((), ()))


def _cparams(n_axes):
    return pltpu.CompilerParams(dimension_semantics=("arbitrary",) * n_axes,
                                vmem_limit_bytes=VMEM_LIMIT)


def _layer_norm(x, g, b):
    mu = jnp.mean(x, axis=-1, keepdims=True)
    xc = x - mu
    var = jnp.mean(xc * xc, axis=-1, keepdims=True)
    return xc * lax.rsqrt(var + LN_EPS) * g + b


def _silu(x):
    return x * (1.0 / (1.0 + jnp.exp(-x)))


def _ln_kernel(x_ref, g_ref, b_ref, o_ref):
    o_ref[...] = _layer_norm(x_ref[...], g_ref[...], b_ref[...])


def _ln_call(x, g, b, tm):
    n, d = x.shape
    return pl.pallas_call(
        _ln_kernel,
        grid=(n // tm,),
        in_specs=[pl.BlockSpec((tm, d), lambda i: (i, 0)),
                  pl.BlockSpec((1, d), lambda i: (0, 0)),
                  pl.BlockSpec((1, d), lambda i: (0, 0))],
        out_specs=pl.BlockSpec((tm, d), lambda i: (i, 0)),
        out_shape=jax.ShapeDtypeStruct((n, d), F32),
        compiler_params=_cparams(1),
        name="ln_in",
    )(x, g.reshape(1, d), b.reshape(1, d))


_SEG_QA = (0, 512)
_SEG_CKV = (512, 640)
_SEG_QIDX = (640, 1152)
_SEG_KK = (1152, 1280)
_SEG_WIDX = (1280, 1408)
_SEG_QB = (1408, 1920)
_SEG_KB = (1920, 2432)
_SEG_VB = (2432, 2944)
_PROJ_COLS = 2944


def _pos_features(pos, shape):
    lane = lax.broadcasted_iota(I32, shape, 1)
    hi = lax.shift_right_logical(pos, CHUNK_SHIFT).astype(F32)
    lo = (pos & (CHUNK - 1)).astype(F32)
    return jnp.where(lane < 2, 1.0, jnp.where(lane == 2, hi, jnp.where(lane == 3, lo, 0.0)))


def _slope_features(tqi, slope, shape):
    feat = lax.broadcasted_iota(I32, shape, 0)
    hi = lax.shift_right_logical(tqi, CHUNK_SHIFT).astype(F32)
    lo = (tqi & (CHUNK - 1)).astype(F32)
    return jnp.where(feat == 0, -slope * CHUNK * hi,
                     jnp.where(feat == 1, -slope * lo,
                               jnp.where(feat == 2, slope * CHUNK,
                                         jnp.where(feat == 3, slope, 0.0))))


def _with_ones_rows(vt):
    t = vt.shape[1]
    sub = lax.broadcasted_iota(I32, (V_PAD, t), 0)
    return jnp.concatenate([vt, jnp.where(sub == 0, 1.0, 0.0)], axis=0)


def _proj_kernel(h_ref, w_ref, kvn_ref, qa_ref, ckv_ref, ckvt_ref, qidx_ref, kk_ref, widx_ref,
                 qb_ref, kb_ref, vbt_ref, *, seq):
    tm = h_ref.shape[0]
    x = h_ref[...].astype(BF16)

    def seg(s):
        return jnp.dot(x, w_ref[:, s[0]:s[1]], preferred_element_type=F32)

    row = pl.program_id(0) * tm + lax.broadcasted_iota(I32, (tm, LANES), 0)
    feat = _pos_features(row & (seq - 1), (tm, LANES)).astype(BF16)

    qa_ref[...] = (seg(_SEG_QA) * (A_HEAD_DIM ** -0.5)).astype(BF16)
    c = seg(_SEG_CKV)
    c = c * lax.rsqrt(jnp.mean(c * c, axis=-1, keepdims=True) + RMS_EPS) * kvn_ref[...]
    ckv_ref[:, :KV_LORA] = c.astype(BF16)
    ckv_ref[:, KV_LORA:] = feat
    ckvt_ref[...] = _with_ones_rows(c.T).astype(BF16)
    qidx_ref[...] = (seg(_SEG_QIDX) * (IDX_DIM ** -0.5)).astype(BF16)
    kk_ref[...] = seg(_SEG_KK).astype(BF16)
    widx_ref[...] = seg(_SEG_WIDX) * (IDX_HEADS ** -0.5)
    qb_ref[...] = (seg(_SEG_QB) * (B_QK_DIM ** -0.5)).astype(BF16)
    kb = seg(_SEG_KB).astype(BF16)
    vt = seg(_SEG_VB).T
    for h in range(B_HEADS):
        kb_ref[:, 2 * h * LANES:(2 * h + 1) * LANES] = kb[:, h * LANES:(h + 1) * LANES]
        kb_ref[:, (2 * h + 1) * LANES:(2 * h + 2) * LANES] = feat
        vbt_ref[h * (B_V_DIM + V_PAD):(h + 1) * (B_V_DIM + V_PAD), :] = _with_ones_rows(
            vt[h * B_V_DIM:(h + 1) * B_V_DIM]).astype(BF16)


def _proj_call(h, w_pad, kvn, tm, seq):
    n, d = h.shape
    assert seq & (seq - 1) == 0 and seq <= CHUNK * 256 and seq % tm == 0
    outs = [(512, BF16, False), (2 * KV_LORA, BF16, False), (KV_LORA + V_PAD, BF16, True),
            (512, BF16, False), (128, BF16, False), (128, F32, False), (512, BF16, False),
            (2 * B_HEADS * LANES, BF16, False), (B_HEADS * (B_V_DIM + V_PAD), BF16, True)]
    spec = lambda w, t: (pl.BlockSpec((w, tm), lambda i: (0, i)) if t
                         else pl.BlockSpec((tm, w), lambda i: (i, 0)))
    shape = lambda w, dt, t: jax.ShapeDtypeStruct((w, n) if t else (n, w), dt)
    return pl.pallas_call(
        functools.partial(_proj_kernel, seq=seq),
        grid=(n // tm,),
        in_specs=[pl.BlockSpec((tm, d), lambda i: (i, 0)),
                  pl.BlockSpec((d, _PROJ_COLS), lambda i: (0, 0)),
                  pl.BlockSpec((1, KV_LORA), lambda i: (0, 0))],
        out_specs=[spec(w, t) for w, _, t in outs],
        out_shape=[shape(w, dt, t) for w, dt, t in outs],
        compiler_params=_cparams(1),
        name="proj_in",
    )(h, w_pad, kvn.reshape(1, KV_LORA))


def _pad_w_in(w_in):
    sizes = (A_WIDTH, KV_LORA, IDX_HEADS * IDX_DIM, IDX_DIM, IDX_HEADS,
             2 * B_HEADS * B_QK_DIM, 2 * B_HEADS * B_QK_DIM, B_WIDTH)
    splits = np.cumsum(sizes)[:-1].tolist()
    q_a, c_kv, q_idx, k_idx, w_idx, q_b, k_b, v_b = jnp.split(w_in, splits, axis=-1)
    w_idx = jnp.pad(w_idx, ((0, 0), (0, LANES - IDX_HEADS)))
    return jnp.concatenate([q_a, c_kv, q_idx, k_idx, k_idx, w_idx, q_b, k_b, v_b],
                           axis=-1).astype(BF16)


def _flash_probs(sl, sc, m_ref):
    m_old = m_ref[:, sl]
    m_new = jnp.maximum(m_old, jnp.max(sc, axis=0, keepdims=True))
    m_ref[:, sl] = m_new
    return jnp.exp(m_old - m_new), jnp.exp(sc - m_new).astype(BF16)


def _flash_init(m_ref, acc_ref):
    m_ref[...] = jnp.full(m_ref.shape, M_INIT, F32)
    acc_ref[...] = jnp.zeros(acc_ref.shape, F32)


def _split_halves(q_ref, n_pairs, out_ref, tq):
    q_t = q_ref[...].astype(F32).T
    feat = lax.broadcasted_iota(I32, (LANES, tq), 0)
    for j in range(n_pairs):
        qp = q_t[j * LANES:(j + 1) * LANES, :]
        out_ref[:LANES, 2 * j * tq:(2 * j + 1) * tq] = jnp.where(feat < 64, qp, 0.0).astype(BF16)
        out_ref[:LANES, (2 * j + 1) * tq:(2 * j + 2) * tq] = jnp.where(feat >= 64, qp, 0.0).astype(BF16)


def _dsa_kernel(qa_ref, qidx_ref, widx_ref, kk_ref, ckv_ref, ckvt_ref, wk_ref, wvt_ref, out_ref,
                keys_ref, khi_ref, klo_ref, qim_ref, qlat_ref, tj_ref, m_ref, acc_ref,
                *, tq, tk, ts, seq, topk):
    i = pl.program_id(1)
    n_off = i * (tq // tk)
    n_kt = n_off + tq // tk
    idx_bits = int(math.log2(seq))

    _split_halves(qidx_ref, IDX_HEADS // 2, qim_ref, tq)
    qlat_t = lax.dot_general(wk_ref[...], qa_ref[...], NT_DIMS,
                             preferred_element_type=F32)
    t_row = i * tq + lax.broadcasted_iota(I32, (LANES, tq), 1)
    for h in range(A_HEADS):
        qlat_ref[:KV_LORA, h * tq:(h + 1) * tq] = qlat_t[h * KV_LORA:(h + 1) * KV_LORA, :].astype(BF16)
        qlat_ref[KV_LORA:, h * tq:(h + 1) * tq] = _slope_features(
            t_row, 2.0 ** -(h + 1), (LANES, tq)).astype(BF16)
    w_t = widx_ref[...].T

    row = lax.broadcasted_iota(I32, (tk, tq), 0)
    tqi = i * tq + lax.broadcasted_iota(I32, (tk, tq), 1)
    qchunk = lax.shift_right_logical(tqi, CHUNK_SHIFT)

    def idx_body(kt, carry):
        off = pl.multiple_of(kt * tk, tk)
        kkt = kk_ref[pl.ds(off, tk), :]
        s_all = jnp.dot(kkt, qim_ref[...], preferred_element_type=F32)
        acc = jnp.zeros((tk, tq), F32)
        for h in range(IDX_HEADS):
            acc = acc + w_t[h:h + 1, :] * jnp.maximum(s_all[:, h * tq:(h + 1) * tq], 0.0)
        bits = lax.bitcast_convert_type(acc + 0.0, I32)
        key = jnp.where(bits < 0, bits ^ jnp.int32(0x7FFFFFFF), bits)
        adm = lax.shift_right_logical(off + row, CHUNK_SHIFT) <= qchunk
        key = jnp.where(adm, key, jnp.int32(INT_MIN))
        keys_ref[pl.ds(off, tk), :] = key
        khi_ref[pl.ds(off, tk), :] = lax.shift_right_arithmetic(key, 16).astype(I16)
        klo_ref[pl.ds(off, tk), :] = ((key & 0xFFFF) + I16_MIN).astype(I16)
        return carry

    lax.fori_loop(0, n_kt, idx_body, 0)

    n_st = n_kt * (tk // ts)
    row_s = lax.broadcasted_iota(I32, (ts, tq), 0)

    def count(pred):
        def body(st, c):
            off = pl.multiple_of(st * ts, ts)
            ind = jnp.where(pred(keys_ref[pl.ds(off, ts), :], off + row_s), 1.0, 0.0)
            return c + jnp.sum(ind, axis=0, keepdims=True)
        return lax.fori_loop(0, n_st, body, jnp.zeros((1, tq), F32))

    def count16(ref, pred):
        pack = 2 * SUBLANES
        def body(st, c):
            off = pl.multiple_of(st * ts, ts)
            ind = jnp.where(pred(ref[pl.ds(off, ts), :]), jnp.int16(1), jnp.int16(0))
            parts = [ind[j * pack:(j + 1) * pack] for j in range(ts // pack)]
            while len(parts) > 1:
                parts = [parts[j] + parts[j + 1] for j in range(0, len(parts), 2)]
            return c + parts[0]
        c = lax.fori_loop(0, n_st, body, jnp.zeros((pack, tq), I16))
        return jnp.sum(c.astype(F32), axis=0, keepdims=True)

    def search16(ref, need):
        def body(it, t):
            cand = t + lax.shift_left(jnp.int32(1), 15 - it)
            cand16 = cand.astype(I16)
            cnt = count16(ref, lambda tile: tile >= cand16)
            return jnp.where(cnt >= need, cand, t)
        return lax.fori_loop(0, 16, body, jnp.full((1, tq), I16_MIN, I32))

    t_hi = search16(khi_ref, topk)
    t_hi16 = t_hi.astype(I16)
    n_above = count16(khi_ref, lambda tile: tile > t_hi16)

    def park_body(st, carry):
        off = pl.multiple_of(st * ts, ts)
        klo_ref[pl.ds(off, ts), :] = jnp.where(khi_ref[pl.ds(off, ts), :] == t_hi16,
                                               klo_ref[pl.ds(off, ts), :], jnp.int16(I16_MIN))
        return carry

    lax.fori_loop(0, n_st, park_body, 0)
    t_lo = search16(klo_ref, topk - n_above)
    thr = t_hi * 65536 + (t_lo - I16_MIN)

    live = thr > INT_MIN
    t_lo16 = t_lo.astype(I16)
    n_ge = n_above + count16(klo_ref, lambda tile: tile >= t_lo16)
    excess = jnp.logical_and(n_ge > topk, live)
    all_ties = jnp.where(live, seq, -1).astype(I32)
    tj_ref[0] = jnp.broadcast_to(thr, (SUBLANES, tq))
    tj_ref[1] = jnp.broadcast_to(all_ties, (SUBLANES, tq))

    @pl.when(jnp.max(jnp.where(excess, 1.0, 0.0)) > 0.0)
    def _():
        need = topk - count(lambda tile, _: tile > thr)

        def tie_body(it, hi):
            cand = hi + lax.shift_left(jnp.int32(1), idx_bits - 1 - it)
            cnt = count(lambda tile, pos: jnp.logical_and(tile == thr, pos < cand))
            return jnp.where(cnt < need, cand, hi)

        hi = lax.fori_loop(0, idx_bits, tie_body, jnp.zeros((1, tq), I32))
        tj_ref[1] = jnp.broadcast_to(jnp.where(excess, hi, all_ties), (SUBLANES, tq))

    thr_b = tj_ref[0][:1]
    tie_hi = tj_ref[1][:1]

    _flash_init(m_ref, acc_ref)

    def att_tile(kt, late_fix):
        off = pl.multiple_of(kt * tk, tk)
        tile = keys_ref[pl.ds(off, tk), :]
        pos = off + row
        sel = jnp.logical_or(tile > thr_b,
                             jnp.logical_and(tile == thr_b, pos <= tie_hi))
        s_all = jnp.dot(ckv_ref[pl.ds(off, tk), :], qlat_ref[...],
                        preferred_element_type=F32)
        if late_fix:
            late = jnp.maximum(pos - tqi, 0).astype(F32)
        alphas, probs = [], []
        for h in range(A_HEADS):
            sl = slice(h * tq, (h + 1) * tq)
            sc = s_all[:, sl]
            if late_fix:
                sc = sc - (2.0 * 2.0 ** -(h + 1)) * late
            alpha, p = _flash_probs(sl, jnp.where(sel, sc, NEG_BIG), m_ref)
            alphas.append(alpha)
            probs.append(p)
        pv = jnp.dot(ckvt_ref[:, pl.ds(off, tk)], jnp.concatenate(probs, axis=1),
                     preferred_element_type=F32)
        acc_ref[...] = jnp.concatenate(alphas, axis=1) * acc_ref[...] + pv

    def off_body(kt, carry):
        att_tile(kt, False)
        return carry

    def diag_body(kt, carry):
        att_tile(kt, True)
        return carry

    lax.fori_loop(0, n_off, off_body, 0)
    lax.fori_loop(n_off, n_kt, diag_body, 0)

    inv_l = 1.0 / acc_ref[KV_LORA:KV_LORA + 1, :]
    olat = jnp.concatenate([(acc_ref[:KV_LORA, h * tq:(h + 1) * tq]
                             * inv_l[:, h * tq:(h + 1) * tq]).astype(BF16)
                            for h in range(A_HEADS)], axis=0)
    out_t = jnp.dot(wvt_ref[...], olat, preferred_element_type=F32)
    out_ref[...] = out_t.T.astype(BF16)


def _dsa_call(qa, qidx, widx, kk, ckv, ckvt, wk_bd, wvt_bd, b, s, tq, tk, ts):
    topk = min(INDEX_TOPK_MAX, s // 4)
    assert s % tq == 0 and tq % tk == 0 and tk % ts == 0 and ts % LANES == 0 and s & (s - 1) == 0
    blk = lambda w: pl.BlockSpec((None, tq, w), lambda bi, i: (bi, i, 0))
    full = lambda w: pl.BlockSpec((None, s, w), lambda bi, i: (bi, 0, 0))
    const = lambda a: pl.BlockSpec(a.shape, lambda bi, i: (0, 0))
    r3 = lambda a: a.reshape(b, s, a.shape[-1])
    out = pl.pallas_call(
        functools.partial(_dsa_kernel, tq=tq, tk=tk, ts=ts, seq=s, topk=topk),
        grid=(b, s // tq),
        in_specs=[blk(A_WIDTH), blk(IDX_HEADS * IDX_DIM), blk(LANES), full(LANES),
                  full(2 * KV_LORA),
                  pl.BlockSpec((KV_LORA + V_PAD, s), lambda bi, i: (0, bi)),
                  const(wk_bd), const(wvt_bd)],
        out_specs=blk(A_WIDTH),
        out_shape=jax.ShapeDtypeStruct((b, s, A_WIDTH), BF16),
        scratch_shapes=[pltpu.VMEM((s, tq), I32),
                        pltpu.VMEM((s, tq), I16),
                        pltpu.VMEM((s, tq), I16),
                        pltpu.VMEM((LANES, IDX_HEADS * tq), BF16),
                        pltpu.VMEM((2 * KV_LORA, A_HEADS * tq), BF16),
                        pltpu.VMEM((2, SUBLANES, tq), I32),
                        pltpu.VMEM((1, A_HEADS * tq), F32),
                        pltpu.VMEM((KV_LORA + V_PAD, A_HEADS * tq), F32)],
        compiler_params=_cparams(2),
        name="dsa",
    )(r3(qa), r3(qidx), r3(widx), r3(kk), r3(ckv), ckvt, wk_bd, wvt_bd)
    return out.reshape(b * s, A_WIDTH)


def _block_diag_uk(w_uk):
    h, c, d = w_uk.shape
    eye = jnp.eye(h, dtype=w_uk.dtype)
    return jnp.einsum('hcd,hg->hcgd', w_uk, eye).reshape(h * c, h * d).astype(BF16)


def _block_diag_uv_t(w_uv):
    h, c, d = w_uv.shape
    eye = jnp.eye(h, dtype=w_uv.dtype)
    return jnp.einsum('hcd,hg->hdgc', w_uv, eye).reshape(h * d, h * c).astype(BF16)


def _diff_kernel(qb_ref, kb_ref, vbt_ref, lam_ref, subln_ref, out_ref,
                 qm_ref, m_ref, acc_ref, *, tq, tk, lambda_init):
    i = pl.program_id(1)
    n_off = i * (tq // tk)
    dvp = B_V_DIM + V_PAD
    _split_halves(qb_ref, B_HEADS, qm_ref, tq)
    t_row = i * tq + lax.broadcasted_iota(I32, (LANES, tq), 1)
    for h in range(B_HEADS):
        feat = _slope_features(t_row, 2.0 ** (-2 * (h + 1)), (LANES, tq)).astype(BF16)
        qm_ref[LANES:, 2 * h * tq:(2 * h + 1) * tq] = feat
        qm_ref[LANES:, (2 * h + 1) * tq:(2 * h + 2) * tq] = feat

    row = lax.broadcasted_iota(I32, (tk, tq), 0)
    tqi = i * tq + lax.broadcasted_iota(I32, (tk, tq), 1)
    qchunk = lax.shift_right_logical(tqi, CHUNK_SHIFT)
    _flash_init(m_ref, acc_ref)

    def tile_step(kt, diag):
        off = pl.multiple_of(kt * tk, tk)
        s_pairs = [jnp.dot(kb_ref[pl.ds(off, tk), 2 * h * LANES:(2 * h + 2) * LANES],
                           qm_ref[:, 2 * h * tq:(2 * h + 2) * tq],
                           preferred_element_type=F32) for h in range(B_HEADS)]
        if diag:
            pos = off + row
            adm = lax.shift_right_logical(pos, CHUNK_SHIFT) <= qchunk
            late = jnp.maximum(pos - tqi, 0).astype(F32)
        alphas, probs = [], []
        for h in range(B_HEADS):
            for j in range(2):
                sl = slice((2 * h + j) * tq, (2 * h + j + 1) * tq)
                sc = s_pairs[h][:, j * tq:(j + 1) * tq]
                if diag:
                    sc = jnp.where(adm, sc - (2.0 * 2.0 ** (-2 * (h + 1))) * late, NEG_BIG)
                alpha, p = _flash_probs(sl, sc, m_ref)
                alphas.append(alpha)
                probs.append(p)
        pv = jnp.concatenate(
            [jnp.dot(vbt_ref[h * dvp:(h + 1) * dvp, pl.ds(off, tk)],
                     jnp.concatenate(probs[2 * h:2 * h + 2], axis=1), preferred_element_type=F32)
             for h in range(B_HEADS)], axis=1)
        acc_ref[...] = jnp.concatenate(alphas, axis=1) * acc_ref[...] + pv

    def off_body(kt, carry):
        tile_step(kt, False)
        return carry

    def diag_body(kt, carry):
        tile_step(kt, True)
        return carry

    lax.fori_loop(0, n_off, off_body, 0)
    lax.fori_loop(n_off, n_off + tq // tk, diag_body, 0)

    lp = lam_ref[...]
    lam = (jnp.exp(jnp.sum(lp[0:1] * lp[1:2], axis=1, keepdims=True))
           - jnp.exp(jnp.sum(lp[2:3] * lp[3:4], axis=1, keepdims=True)) + lambda_init)
    o_all = acc_ref[:B_V_DIM, :] * (1.0 / acc_ref[B_V_DIM:B_V_DIM + 1, :])
    outs = []
    for h in range(B_HEADS):
        o = (o_all[:, 2 * h * tq:(2 * h + 1) * tq]
             - lam * o_all[:, (2 * h + 1) * tq:(2 * h + 2) * tq])
        o = o * lax.rsqrt(jnp.mean(o * o, axis=0, keepdims=True) + RMS_EPS) * subln_ref[...]
        outs.append(o * (1.0 - lambda_init))
    out_ref[...] = jnp.concatenate(outs, axis=0).T.astype(BF16)


def _diff_call(qb, kb, vbt, lam_rows, subln, b, s, tq, tk, lambda_init):
    blk = pl.BlockSpec((None, tq, B_WIDTH), lambda bi, i: (bi, i, 0))
    kw = 2 * B_HEADS * LANES
    dvp = B_V_DIM + V_PAD
    out = pl.pallas_call(
        functools.partial(_diff_kernel, tq=tq, tk=tk, lambda_init=lambda_init),
        grid=(b, s // tq),
        in_specs=[blk, pl.BlockSpec((None, s, kw), lambda bi, i: (bi, 0, 0)),
                  pl.BlockSpec((B_HEADS * dvp, s), lambda bi, i: (0, bi)),
                  pl.BlockSpec((SUBLANES, LANES), lambda bi, i: (0, 0)),
                  pl.BlockSpec((B_V_DIM, 1), lambda bi, i: (0, 0))],
        out_specs=blk,
        out_shape=jax.ShapeDtypeStruct((b, s, B_WIDTH), BF16),
        scratch_shapes=[pltpu.VMEM((2 * LANES, 2 * B_HEADS * tq), BF16),
                        pltpu.VMEM((1, 2 * B_HEADS * tq), F32),
                        pltpu.VMEM((dvp, 2 * B_HEADS * tq), F32)],
        compiler_params=_cparams(2),
        name="diff",
    )(qb.reshape(b, s, B_WIDTH), kb.reshape(b, s, kw), vbt, lam_rows, subln.reshape(B_V_DIM, 1))
    return out.reshape(b * s, B_WIDTH)


def _mix_out_kernel(h_ref, a_ref, b_ref, wa_ref, wb_ref, g_ref, beta_ref, o_ref):
    y = (jnp.dot(a_ref[...], wa_ref[...], preferred_element_type=F32)
         + jnp.dot(b_ref[...], wb_ref[...], preferred_element_type=F32))
    o_ref[...] = _layer_norm(ALPHA * h_ref[...] + y, g_ref[...], beta_ref[...])


def _mix_out_call(h, out_a, out_b, w_o, g, beta, tm):
    n, d = h.shape
    w = w_o.astype(BF16)
    row = lambda wd: pl.BlockSpec((tm, wd), lambda i: (i, 0))
    const = lambda shp: pl.BlockSpec(shp, lambda i: (0, 0))
    return pl.pallas_call(
        _mix_out_kernel,
        grid=(n // tm,),
        in_specs=[row(d), row(A_WIDTH), row(B_WIDTH), const((A_WIDTH, d)), const((B_WIDTH, d)),
                  const((1, d)), const((1, d))],
        out_specs=row(d),
        out_shape=jax.ShapeDtypeStruct((n, d), F32),
        compiler_params=_cparams(1),
        name="mix_out",
    )(h, out_a, out_b, w[:A_WIDTH], w[A_WIDTH:], g.reshape(1, d), beta.reshape(1, d))


def _mem_kv_kernel(m_ref, w_ref, k_ref, v_ref):
    x = m_ref[...].astype(BF16)
    d = k_ref.shape[-1]
    k_ref[...] = jnp.dot(x, w_ref[:, :d], preferred_element_type=F32).astype(BF16)
    v_ref[...] = jnp.dot(x, w_ref[:, d:], preferred_element_type=F32).astype(BF16)


def _mem_kv_call(mem2d, wkv, tm):
    n, d = mem2d.shape
    return pl.pallas_call(
        _mem_kv_kernel,
        grid=(n // tm,),
        in_specs=[pl.BlockSpec((tm, d), lambda i: (i, 0)),
                  pl.BlockSpec((d, 2 * d), lambda i: (0, 0))],
        out_specs=[pl.BlockSpec((tm, d), lambda i: (i, 0))] * 2,
        out_shape=[jax.ShapeDtypeStruct((n, d), BF16)] * 2,
        compiler_params=_cparams(1),
        name="mem_kv",
    )(mem2d, wkv.astype(BF16))


def _mem_attn_kernel(h_ref, wq_ref, k_ref, v_ref, wo_ref, g_ref, beta_ref, o_ref):
    h = h_ref[...]
    q = jnp.dot(h.astype(BF16), wq_ref[...], preferred_element_type=F32)
    q = (q * (MEM_HEAD_DIM ** -0.5)).astype(BF16)
    outs = []
    for hd in range(MEM_HEADS):
        sl = slice(hd * MEM_HEAD_DIM, (hd + 1) * MEM_HEAD_DIM)
        sc = lax.dot_general(q[:, sl], k_ref[:, sl], NT_DIMS, preferred_element_type=F32)
        p = jnp.exp(sc - jnp.max(sc, axis=1, keepdims=True))
        den = jnp.sum(p, axis=1, keepdims=True)
        o = jnp.dot(p.astype(BF16), v_ref[:, sl], preferred_element_type=F32) / den
        outs.append(o.astype(BF16))
    y = jnp.dot(jnp.concatenate(outs, axis=1), wo_ref[...], preferred_element_type=F32)
    o_ref[...] = _layer_norm(ALPHA * h + y, g_ref[...], beta_ref[...])


def _mem_attn_call(h, wq, k, v, wo, g, beta, b, s, tm):
    n, d = h.shape
    m = k.shape[0] // b
    const = lambda shp: pl.BlockSpec(shp, lambda bi, i: (0,) * len(shp))
    row = pl.BlockSpec((None, tm, d), lambda bi, i: (bi, i, 0))
    kv = pl.BlockSpec((None, m, d), lambda bi, i: (bi, 0, 0))
    out = pl.pallas_call(
        _mem_attn_kernel,
        grid=(b, s // tm),
        in_specs=[row, const((d, d)), kv, kv, const((d, d)), const((1, d)), const((1, d))],
        out_specs=row,
        out_shape=jax.ShapeDtypeStruct((b, s, d), F32),
        compiler_params=_cparams(2),
        name="mem_attn",
    )(h.reshape(b, s, d), wq.astype(BF16), k.reshape(b, m, d), v.reshape(b, m, d),
      wo.astype(BF16), g.reshape(1, d), beta.reshape(1, d))
    return out.reshape(n, d)


def _router_kernel(h_ref, rw_ref, bias_ref, tri_ref, e_ref, g_ref, rank_ref, cnt_ref, run_ref):
    tm = h_ref.shape[0]

    @pl.when(pl.program_id(0) == 0)
    def _():
        run_ref[...] = jnp.zeros(run_ref.shape, F32)

    logits = jnp.dot(h_ref[...].astype(BF16), rw_ref[...], preferred_element_type=F32)
    scores = 1.0 / (1.0 + jnp.exp(-logits))
    lane = lax.broadcasted_iota(I32, (tm, LANES), 1)
    lanef = lane.astype(F32)
    ninf = jnp.float32(-jnp.inf)
    cur = jnp.where(lane < N_EXPERTS, scores + bias_ref[...], ninf)
    top_e = jnp.zeros((tm, LANES), F32)
    top_s = jnp.zeros((tm, LANES), F32)
    picked = jnp.zeros((tm, LANES), F32)
    for k in range(TOP_K):
        mx = jnp.max(cur, axis=1, keepdims=True)
        first = jnp.min(jnp.where(cur == mx, lanef, float(LANES)), axis=1, keepdims=True)
        hit = lanef == first
        s_k = jnp.sum(jnp.where(hit, scores, 0.0), axis=1, keepdims=True)
        top_e = jnp.where(lane == k, first, top_e)
        top_s = jnp.where(lane == k, s_k, top_s)
        picked = jnp.where(hit, 1.0, picked)
        cur = jnp.where(hit, ninf, cur)
    e_ref[...] = top_e.astype(I32)
    g_ref[...] = top_s / jnp.sum(top_s, axis=1, keepdims=True) * ROUTED_SCALE

    slot = jnp.dot(tri_ref[...], picked.astype(BF16), preferred_element_type=F32) + run_ref[...]
    rank_ref[...] = slot.astype(I32)
    run = run_ref[...] + jnp.sum(picked, axis=0, keepdims=True)
    run_ref[...] = run
    cnt_ref[...] = run.astype(I32)


def _router_call(h, router_w, router_bias, tm):
    n, d = h.shape
    rw = jnp.pad(router_w, ((0, 0), (0, LANES - N_EXPERTS))).astype(BF16)
    bias = jnp.pad(router_bias, (0, LANES - N_EXPERTS)).reshape(1, LANES)
    tri = jnp.tri(tm, k=-1, dtype=BF16)
    row = pl.BlockSpec((tm, LANES), lambda i: (i, 0))
    const = lambda shp: pl.BlockSpec(shp, lambda i: (0, 0))
    return pl.pallas_call(
        _router_kernel,
        grid=(n // tm,),
        in_specs=[pl.BlockSpec((tm, d), lambda i: (i, 0)), const((d, LANES)), const((1, LANES)),
                  const((tm, tm))],
        out_specs=[row, row, row, const((1, LANES))],
        out_shape=[jax.ShapeDtypeStruct((n, LANES), I32),
                   jax.ShapeDtypeStruct((n, LANES), F32),
                   jax.ShapeDtypeStruct((n, LANES), I32),
                   jax.ShapeDtypeStruct((1, LANES), I32)],
        scratch_shapes=[pltpu.VMEM((1, LANES), F32)],
        compiler_params=_cparams(1),
        name="router",
    )(h, rw, bias, tri)


def _dest_kernel(e_ref, rank_ref, gs_ref, o_ref):
    tm = e_ref.shape[0]
    lane = lax.broadcasted_iota(I32, (tm, LANES), 1)
    e = e_ref[...]
    row_of = (gs_ref[...] + rank_ref[...]).astype(F32)
    dest = jnp.zeros((tm, LANES), F32)
    for k in range(TOP_K):
        d_k = jnp.sum(jnp.where(lane == e[:, k:k + 1], row_of, 0.0), axis=1, keepdims=True)
        dest = jnp.where(lane == k, d_k, dest)
    o_ref[...] = dest.astype(I32)


def _dest_call(top_e, rank, gstart, tm):
    n = top_e.shape[0]
    row = pl.BlockSpec((tm, LANES), lambda i: (i, 0))
    return pl.pallas_call(
        _dest_kernel,
        grid=(n // tm,),
        in_specs=[row, row, pl.BlockSpec((1, LANES), lambda i: (0, 0))],
        out_specs=row,
        out_shape=jax.ShapeDtypeStruct((n, LANES), I32),
        compiler_params=_cparams(1),
        name="dest",
    )(top_e, rank, gstart)


def _dispatch_kernel(dest_ref, x_ref, xs_ref, sem, *, td):
    def row_copy(r, k):
        return pltpu.make_async_copy(x_ref.at[pl.ds(r, 1), :],
                                     xs_ref.at[pl.ds(dest_ref[r, k], 1), :], sem)

    for r in range(td):
        for k in range(TOP_K):
            row_copy(r, k).start(priority=k % N_DMA_QUEUES)

    def drain(r, carry):
        for k in range(TOP_K):
            row_copy(r, k).wait()
        return carry

    lax.fori_loop(0, td, drain, 0)


def _dispatch_call(h, dest, td):
    n, d = h.shape
    m = n * TOP_K
    return pl.pallas_call(
        functools.partial(_dispatch_kernel, td=td),
        grid=(n // td,),
        in_specs=[pl.BlockSpec((td, LANES), lambda i: (i, 0), memory_space=pltpu.SMEM),
                  pl.BlockSpec((td, d), lambda i: (i, 0))],
        out_specs=pl.BlockSpec(memory_space=pl.ANY),
        out_shape=jax.ShapeDtypeStruct((m, d), F32),
        scratch_shapes=[pltpu.SemaphoreType.DMA(())],
        compiler_params=_cparams(1),
        name="dispatch",
    )(dest, h)


def _moe_mm_kernel(ptile_ref, pexp_ref, plo_ref, phi_ref, xs_ref, wg_ref, wu_ref, wd_ref, y_ref,
                   *, tmm):
    p = pl.program_id(0)
    tile = ptile_ref[p]
    lo = plo_ref[p]
    hi = phi_ref[p]
    first = jnp.logical_or(p == 0, ptile_ref[jnp.maximum(p - 1, 0)] != tile)

    def expert_out():
        x = xs_ref[...].astype(BF16)
        g = jnp.dot(x, wg_ref[...].astype(BF16), preferred_element_type=F32)
        u = jnp.dot(x, wu_ref[...].astype(BF16), preferred_element_type=F32)
        hid = (_silu(g) * u).astype(BF16)
        return jnp.dot(hid, wd_ref[...].astype(BF16), preferred_element_type=F32)

    whole = jnp.logical_and(lo <= tile * tmm, hi >= (tile + 1) * tmm)

    @pl.when(whole)
    def _():
        y_ref[...] = expert_out()

    @pl.when(jnp.logical_and(first, jnp.logical_not(whole)))
    def _():
        y_ref[...] = jnp.zeros(y_ref.shape, F32)

    @pl.when(jnp.logical_and(lo < hi, jnp.logical_not(whole)))
    def _():
        y = expert_out()
        rowg = tile * tmm + lax.broadcasted_iota(I32, (tmm, 1), 0)
        y_ref[...] += jnp.where(jnp.logical_and(rowg >= lo, rowg < hi), y, 0.0)


def _moe_mm_call(xs, pairs, wg, wu, wd, layer, tmm):
    m = xs.shape[0]
    n_pairs = pairs[0].shape[0]
    rows = pl.BlockSpec((tmm, D_MODEL), lambda p, pt, pe, plo, phi: (pt[p], 0))
    wspec = lambda shp: pl.BlockSpec((None, None) + shp,
                                     lambda p, pt, pe, plo, phi: (layer, pe[p], 0, 0))
    return pl.pallas_call(
        functools.partial(_moe_mm_kernel, tmm=tmm),
        grid_spec=pltpu.PrefetchScalarGridSpec(
            num_scalar_prefetch=4,
            grid=(n_pairs,),
            in_specs=[rows, wspec((D_MODEL, EXPERT_DIM)), wspec((D_MODEL, EXPERT_DIM)),
                      wspec((EXPERT_DIM, D_MODEL))],
            out_specs=rows),
        out_shape=jax.ShapeDtypeStruct((m, D_MODEL), F32),
        compiler_params=_cparams(1),
        name="moe_mm",
    )(*pairs, xs, wg, wu, wd)


def _group_pairs(counts, m, tmm):
    n_tiles = m // tmm
    n_pairs = n_tiles + N_EXPERTS
    gend = jnp.cumsum(counts)
    gstart = gend - counts
    t0 = jnp.arange(n_tiles, dtype=I32) * tmm
    n_le = lambda edges, v: jnp.sum(edges[None, :] <= v[:, None], axis=1).astype(I32)
    e_first = n_le(gend, t0)
    e_last = n_le(gend, t0 + (tmm - 1))
    per_tile = e_last - e_first + 1
    pend = jnp.cumsum(per_tile)
    pstart = pend - per_tile
    p = jnp.arange(n_pairs, dtype=I32)
    valid = p < pend[-1]
    tile = jnp.minimum(n_le(pend, p), n_tiles - 1)
    e = jnp.clip(e_first[tile] + (p - pstart[tile]), 0, N_EXPERTS - 1).astype(I32)
    lo = jnp.where(valid, gstart[e], 0).astype(I32)
    hi = jnp.where(valid, gend[e], 0).astype(I32)
    return tile, e, lo, hi


def _combine_kernel(dest_ref, gate_ref, h_ref, y_ref, sg_ref, su_ref, sd_ref, g_ref, beta_ref, o_ref,
                    buf_ref, sem, *, tc):
    def row_copy(r, k):
        return pltpu.make_async_copy(y_ref.at[pl.ds(dest_ref[r, k], 1), :],
                                     buf_ref.at[k, pl.ds(r, 1), :], sem)

    for r in range(tc):
        for k in range(TOP_K):
            row_copy(r, k).start(priority=k % N_DMA_QUEUES)

    h = h_ref[...]
    x = h.astype(BF16)
    hid = (_silu(jnp.dot(x, sg_ref[...], preferred_element_type=F32))
           * jnp.dot(x, su_ref[...], preferred_element_type=F32)).astype(BF16)
    shared = jnp.dot(hid, sd_ref[...], preferred_element_type=F32)

    def drain(r, carry):
        for k in range(TOP_K):
            row_copy(r, k).wait()
        return carry

    lax.fori_loop(0, tc, drain, 0)

    gates = gate_ref[...]
    routed = gates[:, 0:1] * buf_ref[0]
    for k in range(1, TOP_K):
        routed = routed + gates[:, k:k + 1] * buf_ref[k]
    o_ref[...] = _layer_norm(ALPHA * h + (routed + shared), g_ref[...], beta_ref[...])


def _combine_call(h, y, dest, gates, sg, su, sd, g, beta, tc):
    n, d = h.shape
    smem = pl.BlockSpec((tc, LANES), lambda i: (i, 0), memory_space=pltpu.SMEM)
    const = lambda shp: pl.BlockSpec(shp, lambda i: (0, 0))
    return pl.pallas_call(
        functools.partial(_combine_kernel, tc=tc),
        grid=(n // tc,),
        in_specs=[smem, pl.BlockSpec((tc, LANES), lambda i: (i, 0)),
                  pl.BlockSpec((tc, d), lambda i: (i, 0)),
                  pl.BlockSpec(memory_space=pl.ANY),
                  const((d, SHARED_DIM)), const((d, SHARED_DIM)), const((SHARED_DIM, d)),
                  const((1, d)), const((1, d))],
        out_specs=pl.BlockSpec((tc, d), lambda i: (i, 0)),
        out_shape=jax.ShapeDtypeStruct((n, d), F32),
        scratch_shapes=[pltpu.VMEM((TOP_K, tc, d), F32),
                        pltpu.SemaphoreType.DMA(())],
        compiler_params=_cparams(1),
        name="combine",
    )(dest, gates, h, y, sg.astype(BF16), su.astype(BF16), sd.astype(BF16),
      g.reshape(1, d), beta.reshape(1, d))


def _tiles(b, s):
    n = b * s
    pick = lambda pref, total: next(t for t in (pref, 512, 256, 128, 64, 32, 16, 8) if t <= pref and total % t == 0)
    return dict(
        rows=pick(512, n),
        attn_q=pick(512, s),
        attn_k=pick(512, s),
        select=pick(256, s),
        mem_rows=pick(512, s),
        mem_kv=pick(512, b * 256),
        router=pick(1024, n),
        dispatch=pick(128, n),
        moe=pick(512, n * TOP_K),
        combine=pick(128, n),
    )


def kernel(x, mem, ln_in_g, ln_in_b, w_in, a_kv_norm, a_w_uk, a_w_uv, b_lq1, b_lk1, b_lq2, b_lk2,
           b_subln, w_o, ln1_g, ln1_b, m_wq, m_wkv, m_wo, ln2_g, ln2_b, router_w, router_bias,
           e_w_gate, e_w_up, e_w_down, s_w_gate, s_w_up, s_w_down, ln3_g, ln3_b):
    b, s, d = x.shape
    n = b * s
    m = n * TOP_K
    t = _tiles(b, s)
    mem2d = mem.reshape(-1, d)

    h = _ln_call(x.reshape(n, d), ln_in_g, ln_in_b, t["rows"])
    for l in range(DEPTH):
        lambda_init = 0.8 - 0.6 * math.exp(-0.3 * l)

        qa, ckv, ckvt, qidx, kk, widx, qb, kb, vbt = _proj_call(h, _pad_w_in(w_in[l]), a_kv_norm[l],
                                                                  t["rows"], s)
        out_a = _dsa_call(qa, qidx, widx, kk, ckv, ckvt, _block_diag_uk(a_w_uk[l]),
                          _block_diag_uv_t(a_w_uv[l]), b, s, t["attn_q"], t["attn_k"], t["select"])
        lam_rows = jnp.pad(jnp.stack([b_lq1[l], b_lk1[l], b_lq2[l], b_lk2[l]]),
                           ((0, SUBLANES - 4), (0, LANES - B_QK_DIM)))
        out_b = _diff_call(qb, kb, vbt, lam_rows, b_subln[l], b, s, t["attn_q"], t["attn_k"], lambda_init)
        h = _mix_out_call(h, out_a, out_b, w_o[l], ln1_g[l], ln1_b[l], t["rows"])

        mk, mv = _mem_kv_call(mem2d, m_wkv[l], t["mem_kv"])
        h = _mem_attn_call(h, m_wq[l], mk, mv, m_wo[l], ln2_g[l], ln2_b[l], b, s, t["mem_rows"])

        top_e, gates, rank, counts = _router_call(h, router_w[l], router_bias[l], t["router"])
        gstart = jnp.cumsum(counts, axis=1) - counts
        dest = _dest_call(top_e, rank, gstart, t["router"])
        counts = counts[0, :N_EXPERTS]
        xs = _dispatch_call(h, dest, t["dispatch"])
        pairs = _group_pairs(counts, m, t["moe"])
        y = _moe_mm_call(xs, pairs, e_w_gate, e_w_up, e_w_down, l, t["moe"])
        h = _combine_call(h, y, dest, gates, s_w_gate[l], s_w_up[l], s_w_down[l],
                          ln3_g[l], ln3_b[l], t["combine"])
    return h.reshape(b, s, d)
```

```python
import functools
import math

import jax
import jax.numpy as jnp
import numpy as np
from jax import lax
from jax.experimental import pallas as pl
from jax.experimental.pallas import tpu as pltpu

F32 = jnp.float32
BF16 = jnp.bfloat16
I32 = jnp.int32
I16 = jnp.int16

D_MODEL = 1024
DEPTH = 2
CHUNK = 64
CHUNK_SHIFT = 6
A_HEADS = 8
A_HEAD_DIM = 64
A_WIDTH = A_HEADS * A_HEAD_DIM
KV_LORA = 128
IDX_HEADS = 8
IDX_DIM = 64
INDEX_TOPK_MAX = 256
B_HEADS = 4
B_QK_DIM = 64
B_V_DIM = 128
B_WIDTH = B_HEADS * B_V_DIM
MEM_HEADS = 4
MEM_HEAD_DIM = D_MODEL // MEM_HEADS
N_EXPERTS = 64
TOP_K = 8
EXPERT_DIM = 256
SHARED_DIM = 256
ROUTED_SCALE = 2.5
ALPHA = (2 * DEPTH) ** 0.25
LN_EPS = 1e-5
RMS_EPS = 1e-6

LANES = 128
SUBLANES = 8
VMEM_LIMIT = 56 * 1024 * 1024
INT_MIN = -2 ** 31
I16_MIN = -2 ** 15
V_PAD = 16
N_DMA_QUEUES = 2
NEG_BIG = -3.0e38
M_INIT = -1.0e30

NT_DIMS = (((1,), (1,)), ((), ()))


def _cparams(n_axes):
    return pltpu.CompilerParams(dimension_semantics=("arbitrary",) * n_axes,
                                vmem_limit_bytes=VMEM_LIMIT)


def _layer_norm(x, g, b):
    mu = jnp.mean(x, axis=-1, keepdims=True)
    xc = x - mu
    var = jnp.mean(xc * xc, axis=-1, keepdims=True)
    return xc * lax.rsqrt(var + LN_EPS) * g + b


def _silu(x):
    return x * (1.0 / (1.0 + jnp.exp(-x)))


def _ln_kernel(x_ref, g_ref, b_ref, o_ref):
    o_ref[...] = _layer_norm(x_ref[...], g_ref[...], b_ref[...])


def _ln_call(x, g, b, tm):
    n, d = x.shape
    return pl.pallas_call(
        _ln_kernel,
        grid=(n // tm,),
        in_specs=[pl.BlockSpec((tm, d), lambda i: (i, 0)),
                  pl.BlockSpec((1, d), lambda i: (0, 0)),
                  pl.BlockSpec((1, d), lambda i: (0, 0))],
        out_specs=pl.BlockSpec((tm, d), lambda i: (i, 0)),
        out_shape=jax.ShapeDtypeStruct((n, d), F32),
        compiler_params=_cparams(1),
        name="ln_in",
    )(x, g.reshape(1, d), b.reshape(1, d))


_SEG_QA = (0, 512)
_SEG_CKV = (512, 640)
_SEG_QIDX = (640, 1152)
_SEG_KK = (1152, 1280)
_SEG_WIDX = (1280, 1408)
_SEG_QB = (1408, 1920)
_SEG_KB = (1920, 2432)
_SEG_VB = (2432, 2944)
_PROJ_COLS = 2944


def _pos_features(pos, shape):
    lane = lax.broadcasted_iota(I32, shape, 1)
    hi = lax.shift_right_logical(pos, CHUNK_SHIFT).astype(F32)
    lo = (pos & (CHUNK - 1)).astype(F32)
    return jnp.where(lane < 2, 1.0, jnp.where(lane == 2, hi, jnp.where(lane == 3, lo, 0.0)))


def _slope_features(tqi, slope, shape):
    feat = lax.broadcasted_iota(I32, shape, 0)
    hi = lax.shift_right_logical(tqi, CHUNK_SHIFT).astype(F32)
    lo = (tqi & (CHUNK - 1)).astype(F32)
    return jnp.where(feat == 0, -slope * CHUNK * hi,
                     jnp.where(feat == 1, -slope * lo,
                               jnp.where(feat == 2, slope * CHUNK,
                                         jnp.where(feat == 3, slope, 0.0))))


def _with_ones_rows(vt):
    t = vt.shape[1]
    sub = lax.broadcasted_iota(I32, (V_PAD, t), 0)
    return jnp.concatenate([vt, jnp.where(sub == 0, 1.0, 0.0)], axis=0)


def _proj_kernel(h_ref, w_ref, kvn_ref, qa_ref, ckv_ref, ckvt_ref, qidx_ref, kk_ref, widx_ref,
                 qb_ref, kb_ref, vbt_ref, *, seq):
    tm = h_ref.shape[0]
    x = h_ref[...].astype(BF16)

    def seg(s):
        return jnp.dot(x, w_ref[:, s[0]:s[1]], preferred_element_type=F32)

    row = pl.program_id(0) * tm + lax.broadcasted_iota(I32, (tm, LANES), 0)
    feat = _pos_features(row & (seq - 1), (tm, LANES)).astype(BF16)

    qa_ref[...] = (seg(_SEG_QA) * (A_HEAD_DIM ** -0.5)).astype(BF16)
    c = seg(_SEG_CKV)
    c = c * lax.rsqrt(jnp.mean(c * c, axis=-1, keepdims=True) + RMS_EPS) * kvn_ref[...]
    ckv_ref[:, :KV_LORA] = c.astype(BF16)
    ckv_ref[:, KV_LORA:] = feat
    ckvt_ref[...] = _with_ones_rows(c.T).astype(BF16)
    qidx_ref[...] = (seg(_SEG_QIDX) * (IDX_DIM ** -0.5)).astype(BF16)
    kk_ref[...] = seg(_SEG_KK).astype(BF16)
    widx_ref[...] = seg(_SEG_WIDX) * (IDX_HEADS ** -0.5)
    qb_ref[...] = (seg(_SEG_QB) * (B_QK_DIM ** -0.5)).astype(BF16)
    kb = seg(_SEG_KB).astype(BF16)
    vt = seg(_SEG_VB).T
    for h in range(B_HEADS):
        kb_ref[:, 2 * h * LANES:(2 * h + 1) * LANES] = kb[:, h * LANES:(h + 1) * LANES]
        kb_ref[:, (2 * h + 1) * LANES:(2 * h + 2) * LANES] = feat
        vbt_ref[h * (B_V_DIM + V_PAD):(h + 1) * (B_V_DIM + V_PAD), :] = _with_ones_rows(
            vt[h * B_V_DIM:(h + 1) * B_V_DIM]).astype(BF16)


def _proj_call(h, w_pad, kvn, tm, seq):
    n, d = h.shape
    assert seq & (seq - 1) == 0 and seq <= CHUNK * 256 and seq % tm == 0
    outs = [(512, BF16, False), (2 * KV_LORA, BF16, False), (KV_LORA + V_PAD, BF16, True),
            (512, BF16, False), (128, BF16, False), (128, F32, False), (512, BF16, False),
            (2 * B_HEADS * LANES, BF16, False), (B_HEADS * (B_V_DIM + V_PAD), BF16, True)]
    spec = lambda w, t: (pl.BlockSpec((w, tm), lambda i: (0, i)) if t
                         else pl.BlockSpec((tm, w), lambda i: (i, 0)))
    shape = lambda w, dt, t: jax.ShapeDtypeStruct((w, n) if t else (n, w), dt)
    return pl.pallas_call(
        functools.partial(_proj_kernel, seq=seq),
        grid=(n // tm,),
        in_specs=[pl.BlockSpec((tm, d), lambda i: (i, 0)),
                  pl.BlockSpec((d, _PROJ_COLS), lambda i: (0, 0)),
                  pl.BlockSpec((1, KV_LORA), lambda i: (0, 0))],
        out_specs=[spec(w, t) for w, _, t in outs],
        out_shape=[shape(w, dt, t) for w, dt, t in outs],
        compiler_params=_cparams(1),
        name="proj_in",
    )(h, w_pad, kvn.reshape(1, KV_LORA))


def _pad_w_in(w_in):
    sizes = (A_WIDTH, KV_LORA, IDX_HEADS * IDX_DIM, IDX_DIM, IDX_HEADS,
             2 * B_HEADS * B_QK_DIM, 2 * B_HEADS * B_QK_DIM, B_WIDTH)
    splits = np.cumsum(sizes)[:-1].tolist()
    q_a, c_kv, q_idx, k_idx, w_idx, q_b, k_b, v_b = jnp.split(w_in, splits, axis=-1)
    w_idx = jnp.pad(w_idx, ((0, 0), (0, LANES - IDX_HEADS)))
    return jnp.concatenate([q_a, c_kv, q_idx, k_idx, k_idx, w_idx, q_b, k_b, v_b],
                           axis=-1).astype(BF16)


def _flash_probs(sl, sc, m_ref):
    m_old = m_ref[:, sl]
    m_new = jnp.maximum(m_old, jnp.max(sc, axis=0, keepdims=True))
    m_ref[:, sl] = m_new
    return jnp.exp(m_old - m_new), jnp.exp(sc - m_new).astype(BF16)


def _flash_init(m_ref, acc_ref):
    m_ref[...] = jnp.full(m_ref.shape, M_INIT, F32)
    acc_ref[...] = jnp.zeros(acc_ref.shape, F32)


def _split_halves(q_ref, n_pairs, out_ref, tq):
    q_t = q_ref[...].astype(F32).T
    feat = lax.broadcasted_iota(I32, (LANES, tq), 0)
    for j in range(n_pairs):
        qp = q_t[j * LANES:(j + 1) * LANES, :]
        out_ref[:LANES, 2 * j * tq:(2 * j + 1) * tq] = jnp.where(feat < 64, qp, 0.0).astype(BF16)
        out_ref[:LANES, (2 * j + 1) * tq:(2 * j + 2) * tq] = jnp.where(feat >= 64, qp, 0.0).astype(BF16)


def _dsa_kernel(qa_ref, qidx_ref, widx_ref, kk_ref, ckv_ref, ckvt_ref, wk_ref, wvt_ref, out_ref,
                keys_ref, khi_ref, klo_ref, qim_ref, qlat_ref, tj_ref, m_ref, acc_ref,
                *, tq, tk, ts, seq, topk):
    i = pl.program_id(1)
    n_off = i * (tq // tk)
    n_kt = n_off + tq // tk
    idx_bits = int(math.log2(seq))

    _split_halves(qidx_ref, IDX_HEADS // 2, qim_ref, tq)
    qlat_t = lax.dot_general(wk_ref[...], qa_ref[...], NT_DIMS,
                             preferred_element_type=F32)
    t_row = i * tq + lax.broadcasted_iota(I32, (LANES, tq), 1)
    for h in range(A_HEADS):
        qlat_ref[:KV_LORA, h * tq:(h + 1) * tq] = qlat_t[h * KV_LORA:(h + 1) * KV_LORA, :].astype(BF16)
        qlat_ref[KV_LORA:, h * tq:(h + 1) * tq] = _slope_features(
            t_row, 2.0 ** -(h + 1), (LANES, tq)).astype(BF16)
    w_t = widx_ref[...].T

    row = lax.broadcasted_iota(I32, (tk, tq), 0)
    tqi = i * tq + lax.broadcasted_iota(I32, (tk, tq), 1)
    qchunk = lax.shift_right_logical(tqi, CHUNK_SHIFT)

    def idx_body(kt, carry):
        off = pl.multiple_of(kt * tk, tk)
        kkt = kk_ref[pl.ds(off, tk), :]
        s_all = jnp.dot(kkt, qim_ref[...], preferred_element_type=F32)
        acc = jnp.zeros((tk, tq), F32)
        for h in range(IDX_HEADS):
            acc = acc + w_t[h:h + 1, :] * jnp.maximum(s_all[:, h * tq:(h + 1) * tq], 0.0)
        bits = lax.bitcast_convert_type(acc + 0.0, I32)
        key = jnp.where(bits < 0, bits ^ jnp.int32(0x7FFFFFFF), bits)
        adm = lax.shift_right_logical(off + row, CHUNK_SHIFT) <= qchunk
        key = jnp.where(adm, key, jnp.int32(INT_MIN))
        keys_ref[pl.ds(off, tk), :] = key
        khi_ref[pl.ds(off, tk), :] = lax.shift_right_arithmetic(key, 16).astype(I16)
        klo_ref[pl.ds(off, tk), :] = ((key & 0xFFFF) + I16_MIN).astype(I16)
        return carry

    lax.fori_loop(0, n_kt, idx_body, 0)

    n_st = n_kt * (tk // ts)
    row_s = lax.broadcasted_iota(I32, (ts, tq), 0)

    def count(pred):
        def body(st, c):
            off = pl.multiple_of(st * ts, ts)
            ind = jnp.where(pred(keys_ref[pl.ds(off, ts), :], off + row_s), 1.0, 0.0)
            return c + jnp.sum(ind, axis=0, keepdims=True)
        return lax.fori_loop(0, n_st, body, jnp.zeros((1, tq), F32))

    def count16(ref, pred):
        pack = 2 * SUBLANES
        def body(st, c):
            off = pl.multiple_of(st * ts, ts)
            ind = jnp.where(pred(ref[pl.ds(off, ts), :]), jnp.int16(1), jnp.int16(0))
            parts = [ind[j * pack:(j + 1) * pack] for j in range(ts // pack)]
            while len(parts) > 1:
                parts = [parts[j] + parts[j + 1] for j in range(0, len(parts), 2)]
            return c + parts[0]
        c = lax.fori_loop(0, n_st, body, jnp.zeros((pack, tq), I16))
        return jnp.sum(c.astype(F32), axis=0, keepdims=True)

    def search16(ref, need):
        def body(it, t):
            cand = t + lax.shift_left(jnp.int32(1), 15 - it)
            cand16 = cand.astype(I16)
            cnt = count16(ref, lambda tile: tile >= cand16)
            return jnp.where(cnt >= need, cand, t)
        return lax.fori_loop(0, 16, body, jnp.full((1, tq), I16_MIN, I32))

    t_hi = search16(khi_ref, topk)
    t_hi16 = t_hi.astype(I16)
    n_above = count16(khi_ref, lambda tile: tile > t_hi16)

    def park_body(st, carry):
        off = pl.multiple_of(st * ts, ts)
        klo_ref[pl.ds(off, ts), :] = jnp.where(khi_ref[pl.ds(off, ts), :] == t_hi16,
                                               klo_ref[pl.ds(off, ts), :], jnp.int16(I16_MIN))
        return carry

    lax.fori_loop(0, n_st, park_body, 0)
    t_lo = search16(klo_ref, topk - n_above)
    thr = t_hi * 65536 + (t_lo - I16_MIN)

    live = thr > INT_MIN
    t_lo16 = t_lo.astype(I16)
    n_ge = n_above + count16(klo_ref, lambda tile: tile >= t_lo16)
    excess = jnp.logical_and(n_ge > topk, live)
    all_ties = jnp.where(live, seq, -1).astype(I32)
    tj_ref[0] = jnp.broadcast_to(thr, (SUBLANES, tq))
    tj_ref[1] = jnp.broadcast_to(all_ties, (SUBLANES, tq))

    @pl.when(jnp.max(jnp.where(excess, 1.0, 0.0)) > 0.0)
    def _():
        need = topk - count(lambda tile, _: tile > thr)

        def tie_body(it, hi):
            cand = hi + lax.shift_left(jnp.int32(1), idx_bits - 1 - it)
            cnt = count(lambda tile, pos: jnp.logical_and(tile == thr, pos < cand))
            return jnp.where(cnt < need, cand, hi)

        hi = lax.fori_loop(0, idx_bits, tie_body, jnp.zeros((1, tq), I32))
        tj_ref[1] = jnp.broadcast_to(jnp.where(excess, hi, all_ties), (SUBLANES, tq))

    thr_b = tj_ref[0][:1]
    tie_hi = tj_ref[1][:1]

    _flash_init(m_ref, acc_ref)

    def att_tile(kt, late_fix):
        off = pl.multiple_of(kt * tk, tk)
        tile = keys_ref[pl.ds(off, tk), :]
        pos = off + row
        sel = jnp.logical_or(tile > thr_b,
                             jnp.logical_and(tile == thr_b, pos <= tie_hi))
        s_all = jnp.dot(ckv_ref[pl.ds(off, tk), :], qlat_ref[...],
                        preferred_element_type=F32)
        if late_fix:
            late = jnp.maximum(pos - tqi, 0).astype(F32)
        alphas, probs = [], []
        for h in range(A_HEADS):
            sl = slice(h * tq, (h + 1) * tq)
            sc = s_all[:, sl]
            if late_fix:
                sc = sc - (2.0 * 2.0 ** -(h + 1)) * late
            alpha, p = _flash_probs(sl, jnp.where(sel, sc, NEG_BIG), m_ref)
            alphas.append(alpha)
            probs.append(p)
        pv = jnp.dot(ckvt_ref[:, pl.ds(off, tk)], jnp.concatenate(probs, axis=1),
                     preferred_element_type=F32)
        acc_ref[...] = jnp.concatenate(alphas, axis=1) * acc_ref[...] + pv

    def off_body(kt, carry):
        att_tile(kt, False)
        return carry

    def diag_body(kt, carry):
        att_tile(kt, True)
        return carry

    lax.fori_loop(0, n_off, off_body, 0)
    lax.fori_loop(n_off, n_kt, diag_body, 0)

    inv_l = 1.0 / acc_ref[KV_LORA:KV_LORA + 1, :]
    olat = jnp.concatenate([(acc_ref[:KV_LORA, h * tq:(h + 1) * tq]
                             * inv_l[:, h * tq:(h + 1) * tq]).astype(BF16)
                            for h in range(A_HEADS)], axis=0)
    out_t = jnp.dot(wvt_ref[...], olat, preferred_element_type=F32)
    out_ref[...] = out_t.T.astype(BF16)


def _dsa_call(qa, qidx, widx, kk, ckv, ckvt, wk_bd, wvt_bd, b, s, tq, tk, ts):
    topk = min(INDEX_TOPK_MAX, s // 4)
    assert s % tq == 0 and tq % tk == 0 and tk % ts == 0 and ts % LANES == 0 and s & (s - 1) == 0
    blk = lambda w: pl.BlockSpec((None, tq, w), lambda bi, i: (bi, i, 0))
    full = lambda w: pl.BlockSpec((None, s, w), lambda bi, i: (bi, 0, 0))
    const = lambda a: pl.BlockSpec(a.shape, lambda bi, i: (0, 0))
    r3 = lambda a: a.reshape(b, s, a.shape[-1])
    out = pl.pallas_call(
        functools.partial(_dsa_kernel, tq=tq, tk=tk, ts=ts, seq=s, topk=topk),
        grid=(b, s // tq),
        in_specs=[blk(A_WIDTH), blk(IDX_HEADS * IDX_DIM), blk(LANES), full(LANES),
                  full(2 * KV_LORA),
                  pl.BlockSpec((KV_LORA + V_PAD, s), lambda bi, i: (0, bi)),
                  const(wk_bd), const(wvt_bd)],
        out_specs=blk(A_WIDTH),
        out_shape=jax.ShapeDtypeStruct((b, s, A_WIDTH), BF16),
        scratch_shapes=[pltpu.VMEM((s, tq), I32),
                        pltpu.VMEM((s, tq), I16),
                        pltpu.VMEM((s, tq), I16),
                        pltpu.VMEM((LANES, IDX_HEADS * tq), BF16),
                        pltpu.VMEM((2 * KV_LORA, A_HEADS * tq), BF16),
                        pltpu.VMEM((2, SUBLANES, tq), I32),
                        pltpu.VMEM((1, A_HEADS * tq), F32),
                        pltpu.VMEM((KV_LORA + V_PAD, A_HEADS * tq), F32)],
        compiler_params=_cparams(2),
        name="dsa",
    )(r3(qa), r3(qidx), r3(widx), r3(kk), r3(ckv), ckvt, wk_bd, wvt_bd)
    return out.reshape(b * s, A_WIDTH)


def _block_diag_uk(w_uk):
    h, c, d = w_uk.shape
    eye = jnp.eye(h, dtype=w_uk.dtype)
    return jnp.einsum('hcd,hg->hcgd', w_uk, eye).reshape(h * c, h * d).astype(BF16)


def _block_diag_uv_t(w_uv):
    h, c, d = w_uv.shape
    eye = jnp.eye(h, dtype=w_uv.dtype)
    return jnp.einsum('hcd,hg->hdgc', w_uv, eye).reshape(h * d, h * c).astype(BF16)


def _diff_kernel(qb_ref, kb_ref, vbt_ref, lam_ref, subln_ref, out_ref,
                 qm_ref, m_ref, acc_ref, *, tq, tk, lambda_init):
    i = pl.program_id(1)
    n_off = i * (tq // tk)
    dvp = B_V_DIM + V_PAD
    _split_halves(qb_ref, B_HEADS, qm_ref, tq)
    t_row = i * tq + lax.broadcasted_iota(I32, (LANES, tq), 1)
    for h in range(B_HEADS):
        feat = _slope_features(t_row, 2.0 ** (-2 * (h + 1)), (LANES, tq)).astype(BF16)
        qm_ref[LANES:, 2 * h * tq:(2 * h + 1) * tq] = feat
        qm_ref[LANES:, (2 * h + 1) * tq:(2 * h + 2) * tq] = feat

    row = lax.broadcasted_iota(I32, (tk, tq), 0)
    tqi = i * tq + lax.broadcasted_iota(I32, (tk, tq), 1)
    qchunk = lax.shift_right_logical(tqi, CHUNK_SHIFT)
    _flash_init(m_ref, acc_ref)

    def tile_step(kt, diag):
        off = pl.multiple_of(kt * tk, tk)
        s_pairs = [jnp.dot(kb_ref[pl.ds(off, tk), 2 * h * LANES:(2 * h + 2) * LANES],
                           qm_ref[:, 2 * h * tq:(2 * h + 2) * tq],
                           preferred_element_type=F32) for h in range(B_HEADS)]
        if diag:
            pos = off + row
            adm = lax.shift_right_logical(pos, CHUNK_SHIFT) <= qchunk
            late = jnp.maximum(pos - tqi, 0).astype(F32)
        alphas, probs = [], []
        for h in range(B_HEADS):
            for j in range(2):
                sl = slice((2 * h + j) * tq, (2 * h + j + 1) * tq)
                sc = s_pairs[h][:, j * tq:(j + 1) * tq]
                if diag:
                    sc = jnp.where(adm, sc - (2.0 * 2.0 ** (-2 * (h + 1))) * late, NEG_BIG)
                alpha, p = _flash_probs(sl, sc, m_ref)
                alphas.append(alpha)
                probs.append(p)
        pv = jnp.concatenate(
            [jnp.dot(vbt_ref[h * dvp:(h + 1) * dvp, pl.ds(off, tk)],
                     jnp.concatenate(probs[2 * h:2 * h + 2], axis=1), preferred_element_type=F32)
             for h in range(B_HEADS)], axis=1)
        acc_ref[...] = jnp.concatenate(alphas, axis=1) * acc_ref[...] + pv

    def off_body(kt, carry):
        tile_step(kt, False)
        return carry

    def diag_body(kt, carry):
        tile_step(kt, True)
        return carry

    lax.fori_loop(0, n_off, off_body, 0)
    lax.fori_loop(n_off, n_off + tq // tk, diag_body, 0)

    lp = lam_ref[...]
    lam = (jnp.exp(jnp.sum(lp[0:1] * lp[1:2], axis=1, keepdims=True))
           - jnp.exp(jnp.sum(lp[2:3] * lp[3:4], axis=1, keepdims=True)) + lambda_init)
    o_all = acc_ref[:B_V_DIM, :] * (1.0 / acc_ref[B_V_DIM:B_V_DIM + 1, :])
    outs = []
    for h in range(B_HEADS):
        o = (o_all[:, 2 * h * tq:(2 * h + 1) * tq]
             - lam * o_all[:, (2 * h + 1) * tq:(2 * h + 2) * tq])
        o = o * lax.rsqrt(jnp.mean(o * o, axis=0, keepdims=True) + RMS_EPS) * subln_ref[...]
        outs.append(o * (1.0 - lambda_init))
    out_ref[...] = jnp.concatenate(outs, axis=0).T.astype(BF16)


def _diff_call(qb, kb, vbt, lam_rows, subln, b, s, tq, tk, lambda_init):
    blk = pl.BlockSpec((None, tq, B_WIDTH), lambda bi, i: (bi, i, 0))
    kw = 2 * B_HEADS * LANES
    dvp = B_V_DIM + V_PAD
    out = pl.pallas_call(
        functools.partial(_diff_kernel, tq=tq, tk=tk, lambda_init=lambda_init),
        grid=(b, s // tq),
        in_specs=[blk, pl.BlockSpec((None, s, kw), lambda bi, i: (bi, 0, 0)),
                  pl.BlockSpec((B_HEADS * dvp, s), lambda bi, i: (0, bi)),
                  pl.BlockSpec((SUBLANES, LANES), lambda bi, i: (0, 0)),
                  pl.BlockSpec((B_V_DIM, 1), lambda bi, i: (0, 0))],
        out_specs=blk,
        out_shape=jax.ShapeDtypeStruct((b, s, B_WIDTH), BF16),
        scratch_shapes=[pltpu.VMEM((2 * LANES, 2 * B_HEADS * tq), BF16),
                        pltpu.VMEM((1, 2 * B_HEADS * tq), F32),
                        pltpu.VMEM((dvp, 2 * B_HEADS * tq), F32)],
        compiler_params=_cparams(2),
        name="diff",
    )(qb.reshape(b, s, B_WIDTH), kb.reshape(b, s, kw), vbt, lam_rows, subln.reshape(B_V_DIM, 1))
    return out.reshape(b * s, B_WIDTH)


def _mix_out_kernel(h_ref, a_ref, b_ref, wa_ref, wb_ref, g_ref, beta_ref, o_ref):
    y = (jnp.dot(a_ref[...], wa_ref[...], preferred_element_type=F32)
         + jnp.dot(b_ref[...], wb_ref[...], preferred_element_type=F32))
    o_ref[...] = _layer_norm(ALPHA * h_ref[...] + y, g_ref[...], beta_ref[...])


def _mix_out_call(h, out_a, out_b, w_o, g, beta, tm):
    n, d = h.shape
    w = w_o.astype(BF16)
    row = lambda wd: pl.BlockSpec((tm, wd), lambda i: (i, 0))
    const = lambda shp: pl.BlockSpec(shp, lambda i: (0, 0))
    return pl.pallas_call(
        _mix_out_kernel,
        grid=(n // tm,),
        in_specs=[row(d), row(A_WIDTH), row(B_WIDTH), const((A_WIDTH, d)), const((B_WIDTH, d)),
                  const((1, d)), const((1, d))],
        out_specs=row(d),
        out_shape=jax.ShapeDtypeStruct((n, d), F32),
        compiler_params=_cparams(1),
        name="mix_out",
    )(h, out_a, out_b, w[:A_WIDTH], w[A_WIDTH:], g.reshape(1, d), beta.reshape(1, d))


def _mem_kv_kernel(m_ref, w_ref, k_ref, v_ref):
    x = m_ref[...].astype(BF16)
    d = k_ref.shape[-1]
    k_ref[...] = jnp.dot(x, w_ref[:, :d], preferred_element_type=F32).astype(BF16)
    v_ref[...] = jnp.dot(x, w_ref[:, d:], preferred_element_type=F32).astype(BF16)


def _mem_kv_call(mem2d, wkv, tm):
    n, d = mem2d.shape
    return pl.pallas_call(
        _mem_kv_kernel,
        grid=(n // tm,),
        in_specs=[pl.BlockSpec((tm, d), lambda i: (i, 0)),
                  pl.BlockSpec((d, 2 * d), lambda i: (0, 0))],
        out_specs=[pl.BlockSpec((tm, d), lambda i: (i, 0))] * 2,
        out_shape=[jax.ShapeDtypeStruct((n, d), BF16)] * 2,
        compiler_params=_cparams(1),
        name="mem_kv",
    )(mem2d, wkv.astype(BF16))


def _mem_attn_kernel(h_ref, wq_ref, k_ref, v_ref, wo_ref, g_ref, beta_ref, o_ref):
    h = h_ref[...]
    q = jnp.dot(h.astype(BF16), wq_ref[...], preferred_element_type=F32)
    q = (q * (MEM_HEAD_DIM ** -0.5)).astype(BF16)
    outs = []
    for hd in range(MEM_HEADS):
        sl = slice(hd * MEM_HEAD_DIM, (hd + 1) * MEM_HEAD_DIM)
        sc = lax.dot_general(q[:, sl], k_ref[:, sl], NT_DIMS, preferred_element_type=F32)
        p = jnp.exp(sc - jnp.max(sc, axis=1, keepdims=True))
        den = jnp.sum(p, axis=1, keepdims=True)
        o = jnp.dot(p.astype(BF16), v_ref[:, sl], preferred_element_type=F32) / den
        outs.append(o.astype(BF16))
    y = jnp.dot(jnp.concatenate(outs, axis=1), wo_ref[...], preferred_element_type=F32)
    o_ref[...] = _layer_norm(ALPHA * h + y, g_ref[...], beta_ref[...])


def _mem_attn_call(h, wq, k, v, wo, g, beta, b, s, tm):
    n, d = h.shape
    m = k.shape[0] // b
    const = lambda shp: pl.BlockSpec(shp, lambda bi, i: (0,) * len(shp))
    row = pl.BlockSpec((None, tm, d), lambda bi, i: (bi, i, 0))
    kv = pl.BlockSpec((None, m, d), lambda bi, i: (bi, 0, 0))
    out = pl.pallas_call(
        _mem_attn_kernel,
        grid=(b, s // tm),
        in_specs=[row, const((d, d)), kv, kv, const((d, d)), const((1, d)), const((1, d))],
        out_specs=row,
        out_shape=jax.ShapeDtypeStruct((b, s, d), F32),
        compiler_params=_cparams(2),
        name="mem_attn",
    )(h.reshape(b, s, d), wq.astype(BF16), k.reshape(b, m, d), v.reshape(b, m, d),
      wo.astype(BF16), g.reshape(1, d), beta.reshape(1, d))
    return out.reshape(n, d)


def _router_kernel(h_ref, rw_ref, bias_ref, tri_ref, e_ref, g_ref, rank_ref, cnt_ref, run_ref):
    tm = h_ref.shape[0]

    @pl.when(pl.program_id(0) == 0)
    def _():
        run_ref[...] = jnp.zeros(run_ref.shape, F32)

    logits = jnp.dot(h_ref[...].astype(BF16), rw_ref[...], preferred_element_type=F32)
    scores = 1.0 / (1.0 + jnp.exp(-logits))
    lane = lax.broadcasted_iota(I32, (tm, LANES), 1)
    lanef = lane.astype(F32)
    ninf = jnp.float32(-jnp.inf)
    cur = jnp.where(lane < N_EXPERTS, scores + bias_ref[...], ninf)
    top_e = jnp.zeros((tm, LANES), F32)
    top_s = jnp.zeros((tm, LANES), F32)
    picked = jnp.zeros((tm, LANES), F32)
    for k in range(TOP_K):
        mx = jnp.max(cur, axis=1, keepdims=True)
        first = jnp.min(jnp.where(cur == mx, lanef, float(LANES)), axis=1, keepdims=True)
        hit = lanef == first
        s_k = jnp.sum(jnp.where(hit, scores, 0.0), axis=1, keepdims=True)
        top_e = jnp.where(lane == k, first, top_e)
        top_s = jnp.where(lane == k, s_k, top_s)
        picked = jnp.where(hit, 1.0, picked)
        cur = jnp.where(hit, ninf, cur)
    e_ref[...] = top_e.astype(I32)
    g_ref[...] = top_s / jnp.sum(top_s, axis=1, keepdims=True) * ROUTED_SCALE

    slot = jnp.dot(tri_ref[...], picked.astype(BF16), preferred_element_type=F32) + run_ref[...]
    rank_ref[...] = slot.astype(I32)
    run = run_ref[...] + jnp.sum(picked, axis=0, keepdims=True)
    run_ref[...] = run
    cnt_ref[...] = run.astype(I32)


def _router_call(h, router_w, router_bias, tm):
    n, d = h.shape
    rw = jnp.pad(router_w, ((0, 0), (0, LANES - N_EXPERTS))).astype(BF16)
    bias = jnp.pad(router_bias, (0, LANES - N_EXPERTS)).reshape(1, LANES)
    tri = jnp.tri(tm, k=-1, dtype=BF16)
    row = pl.BlockSpec((tm, LANES), lambda i: (i, 0))
    const = lambda shp: pl.BlockSpec(shp, lambda i: (0, 0))
    return pl.pallas_call(
        _router_kernel,
        grid=(n // tm,),
        in_specs=[pl.BlockSpec((tm, d), lambda i: (i, 0)), const((d, LANES)), const((1, LANES)),
                  const((tm, tm))],
        out_specs=[row, row, row, const((1, LANES))],
        out_shape=[jax.ShapeDtypeStruct((n, LANES), I32),
                   jax.ShapeDtypeStruct((n, LANES), F32),
                   jax.ShapeDtypeStruct((n, LANES), I32),
                   jax.ShapeDtypeStruct((1, LANES), I32)],
        scratch_shapes=[pltpu.VMEM((1, LANES), F32)],
        compiler_params=_cparams(1),
        name="router",
    )(h, rw, bias, tri)


def _dest_kernel(e_ref, rank_ref, gs_ref, o_ref):
    tm = e_ref.shape[0]
    lane = lax.broadcasted_iota(I32, (tm, LANES), 1)
    e = e_ref[...]
    row_of = (gs_ref[...] + rank_ref[...]).astype(F32)
    dest = jnp.zeros((tm, LANES), F32)
    for k in range(TOP_K):
        d_k = jnp.sum(jnp.where(lane == e[:, k:k + 1], row_of, 0.0), axis=1, keepdims=True)
        dest = jnp.where(lane == k, d_k, dest)
    o_ref[...] = dest.astype(I32)


def _dest_call(top_e, rank, gstart, tm):
    n = top_e.shape[0]
    row = pl.BlockSpec((tm, LANES), lambda i: (i, 0))
    return pl.pallas_call(
        _dest_kernel,
        grid=(n // tm,),
        in_specs=[row, row, pl.BlockSpec((1, LANES), lambda i: (0, 0))],
        out_specs=row,
        out_shape=jax.ShapeDtypeStruct((n, LANES), I32),
        compiler_params=_cparams(1),
        name="dest",
    )(top_e, rank, gstart)


def _dispatch_kernel(dest_ref, x_ref, xs_ref, xp_ref, sem, *, td):
    half = x_ref.shape[1] // 2
    xp_ref[...] = pltpu.pack_elementwise([x_ref[:, :half], x_ref[:, half:]], packed_dtype=BF16)

    def row_copy(r, k):
        return pltpu.make_async_copy(xp_ref.at[pl.ds(r, 1), :],
                                     xs_ref.at[pl.ds(dest_ref[r, k], 1), :], sem)

    for r in range(td):
        for k in range(TOP_K):
            row_copy(r, k).start(priority=k % N_DMA_QUEUES)

    def drain(r, carry):
        for k in range(TOP_K):
            row_copy(r, k).wait()
        return carry

    lax.fori_loop(0, td, drain, 0)


def _dispatch_call(h, dest, td):
    n, d = h.shape
    m = n * TOP_K
    return pl.pallas_call(
        functools.partial(_dispatch_kernel, td=td),
        grid=(n // td,),
        in_specs=[pl.BlockSpec((td, LANES), lambda i: (i, 0), memory_space=pltpu.SMEM),
                  pl.BlockSpec((td, d), lambda i: (i, 0))],
        out_specs=pl.BlockSpec(memory_space=pl.ANY),
        out_shape=jax.ShapeDtypeStruct((m, d // 2), I32),
        scratch_shapes=[pltpu.VMEM((td, d // 2), I32), pltpu.SemaphoreType.DMA(())],
        compiler_params=_cparams(1),
        name="dispatch",
    )(dest, h)


def _moe_mm_kernel(ptile_ref, pexp_ref, plo_ref, phi_ref, xs_ref, wg_ref, wu_ref, wd_ref, y_ref,
                   *, tmm):
    p = pl.program_id(0)
    tile = ptile_ref[p]
    lo = plo_ref[p]
    hi = phi_ref[p]
    first = jnp.logical_or(p == 0, ptile_ref[jnp.maximum(p - 1, 0)] != tile)

    def expert_out():
        xp = xs_ref[...]
        x = jnp.concatenate(
            [pltpu.unpack_elementwise(xp, index=j, packed_dtype=BF16, unpacked_dtype=F32)
             for j in range(2)], axis=1).astype(BF16)
        g = jnp.dot(x, wg_ref[...].astype(BF16), preferred_element_type=F32)
        u = jnp.dot(x, wu_ref[...].astype(BF16), preferred_element_type=F32)
        hid = (_silu(g) * u).astype(BF16)
        return jnp.dot(hid, wd_ref[...].astype(BF16), preferred_element_type=F32)

    whole = jnp.logical_and(lo <= tile * tmm, hi >= (tile + 1) * tmm)

    @pl.when(whole)
    def _():
        y_ref[...] = expert_out()

    @pl.when(jnp.logical_and(first, jnp.logical_not(whole)))
    def _():
        y_ref[...] = jnp.zeros(y_ref.shape, F32)

    @pl.when(jnp.logical_and(lo < hi, jnp.logical_not(whole)))
    def _():
        y = expert_out()
        rowg = tile * tmm + lax.broadcasted_iota(I32, (tmm, 1), 0)
        y_ref[...] += jnp.where(jnp.logical_and(rowg >= lo, rowg < hi), y, 0.0)


def _moe_mm_call(xs, pairs, wg, wu, wd, layer, tmm):
    m = xs.shape[0]
    n_pairs = pairs[0].shape[0]
    rows = lambda w: pl.BlockSpec((tmm, w), lambda p, pt, pe, plo, phi: (pt[p], 0))
    wspec = lambda shp: pl.BlockSpec((None, None) + shp,
                                     lambda p, pt, pe, plo, phi: (layer, pe[p], 0, 0))
    return pl.pallas_call(
        functools.partial(_moe_mm_kernel, tmm=tmm),
        grid_spec=pltpu.PrefetchScalarGridSpec(
            num_scalar_prefetch=4,
            grid=(n_pairs,),
            in_specs=[rows(xs.shape[1]), wspec((D_MODEL, EXPERT_DIM)), wspec((D_MODEL, EXPERT_DIM)),
                      wspec((EXPERT_DIM, D_MODEL))],
            out_specs=rows(D_MODEL)),
        out_shape=jax.ShapeDtypeStruct((m, D_MODEL), F32),
        compiler_params=_cparams(1),
        name="moe_mm",
    )(*pairs, xs, wg, wu, wd)


def _group_pairs(counts, m, tmm):
    n_tiles = m // tmm
    n_pairs = n_tiles + N_EXPERTS
    gend = jnp.cumsum(counts)
    gstart = gend - counts
    t0 = jnp.arange(n_tiles, dtype=I32) * tmm
    n_le = lambda edges, v: jnp.sum(edges[None, :] <= v[:, None], axis=1).astype(I32)
    e_first = n_le(gend, t0)
    e_last = n_le(gend, t0 + (tmm - 1))
    per_tile = e_last - e_first + 1
    pend = jnp.cumsum(per_tile)
    pstart = pend - per_tile
    p = jnp.arange(n_pairs, dtype=I32)
    valid = p < pend[-1]
    tile = jnp.minimum(n_le(pend, p), n_tiles - 1)
    e = jnp.clip(e_first[tile] + (p - pstart[tile]), 0, N_EXPERTS - 1).astype(I32)
    lo = jnp.where(valid, gstart[e], 0).astype(I32)
    hi = jnp.where(valid, gend[e], 0).astype(I32)
    return tile, e, lo, hi


def _combine_kernel(dest_ref, gate_ref, h_ref, y_ref, sg_ref, su_ref, sd_ref, g_ref, beta_ref, o_ref,
                    buf_ref, sem, *, tc):
    def row_copy(r, k):
        return pltpu.make_async_copy(y_ref.at[pl.ds(dest_ref[r, k], 1), :],
                                     buf_ref.at[k, pl.ds(r, 1), :], sem)

    for r in range(tc):
        for k in range(TOP_K):
            row_copy(r, k).start(priority=k % N_DMA_QUEUES)

    h = h_ref[...]
    x = h.astype(BF16)
    hid = (_silu(jnp.dot(x, sg_ref[...], preferred_element_type=F32))
           * jnp.dot(x, su_ref[...], preferred_element_type=F32)).astype(BF16)
    shared = jnp.dot(hid, sd_ref[...], preferred_element_type=F32)

    def drain(r, carry):
        for k in range(TOP_K):
            row_copy(r, k).wait()
        return carry

    lax.fori_loop(0, tc, drain, 0)

    gates = gate_ref[...]
    routed = gates[:, 0:1] * buf_ref[0]
    for k in range(1, TOP_K):
        routed = routed + gates[:, k:k + 1] * buf_ref[k]
    o_ref[...] = _layer_norm(ALPHA * h + (routed + shared), g_ref[...], beta_ref[...])


def _combine_call(h, y, dest, gates, sg, su, sd, g, beta, tc):
    n, d = h.shape
    smem = pl.BlockSpec((tc, LANES), lambda i: (i, 0), memory_space=pltpu.SMEM)
    const = lambda shp: pl.BlockSpec(shp, lambda i: (0, 0))
    return pl.pallas_call(
        functools.partial(_combine_kernel, tc=tc),
        grid=(n // tc,),
        in_specs=[smem, pl.BlockSpec((tc, LANES), lambda i: (i, 0)),
                  pl.BlockSpec((tc, d), lambda i: (i, 0)),
                  pl.BlockSpec(memory_space=pl.ANY),
                  const((d, SHARED_DIM)), const((d, SHARED_DIM)), const((SHARED_DIM, d)),
                  const((1, d)), const((1, d))],
        out_specs=pl.BlockSpec((tc, d), lambda i: (i, 0)),
        out_shape=jax.ShapeDtypeStruct((n, d), F32),
        scratch_shapes=[pltpu.VMEM((TOP_K, tc, d), F32),
                        pltpu.SemaphoreType.DMA(())],
        compiler_params=_cparams(1),
        name="combine",
    )(dest, gates, h, y, sg.astype(BF16), su.astype(BF16), sd.astype(BF16),
      g.reshape(1, d), beta.reshape(1, d))


def _tiles(b, s):
    n = b * s
    pick = lambda pref, total: next(t for t in (pref, 512, 256, 128, 64, 32, 16, 8) if t <= pref and total % t == 0)
    return dict(
        rows=pick(512, n),
        attn_q=pick(512, s),
        attn_k=pick(512, s),
        select=pick(256, s),
        mem_rows=pick(512, s),
        mem_kv=pick(512, b * 256),
        router=pick(1024, n),
        dispatch=pick(128, n),
        moe=pick(512, n * TOP_K),
        combine=pick(128, n),
    )


def kernel(x, mem, ln_in_g, ln_in_b, w_in, a_kv_norm, a_w_uk, a_w_uv, b_lq1, b_lk1, b_lq2, b_lk2,
           b_subln, w_o, ln1_g, ln1_b, m_wq, m_wkv, m_wo, ln2_g, ln2_b, router_w, router_bias,
           e_w_gate, e_w_up, e_w_down, s_w_gate, s_w_up, s_w_down, ln3_g, ln3_b):
    b, s, d = x.shape
    n = b * s
    m = n * TOP_K
    t = _tiles(b, s)
    mem2d = mem.reshape(-1, d)

    h = _ln_call(x.reshape(n, d), ln_in_g, ln_in_b, t["rows"])
    for l in range(DEPTH):
        lambda_init = 0.8 - 0.6 * math.exp(-0.3 * l)

        qa, ckv, ckvt, qidx, kk, widx, qb, kb, vbt = _proj_call(h, _pad_w_in(w_in[l]), a_kv_norm[l],
                                                                  t["rows"], s)
        out_a = _dsa_call(qa, qidx, widx, kk, ckv, ckvt, _block_diag_uk(a_w_uk[l]),
                          _block_diag_uv_t(a_w_uv[l]), b, s, t["attn_q"], t["attn_k"], t["select"])
        lam_rows = jnp.pad(jnp.stack([b_lq1[l], b_lk1[l], b_lq2[l], b_lk2[l]]),
                           ((0, SUBLANES - 4), (0, LANES - B_QK_DIM)))
        out_b = _diff_call(qb, kb, vbt, lam_rows, b_subln[l], b, s, t["attn_q"], t["attn_k"], lambda_init)
        h = _mix_out_call(h, out_a, out_b, w_o[l], ln1_g[l], ln1_b[l], t["rows"])

        mk, mv = _mem_kv_call(mem2d, m_wkv[l], t["mem_kv"])
        h = _mem_attn_call(h, m_wq[l], mk, mv, m_wo[l], ln2_g[l], ln2_b[l], b, s, t["mem_rows"])

        top_e, gates, rank, counts = _router_call(h, router_w[l], router_bias[l], t["router"])
        gstart = jnp.cumsum(counts, axis=1) - counts
        dest = _dest_call(top_e, rank, gstart, t["router"])
        counts = counts[0, :N_EXPERTS]
        xs = _dispatch_call(h, dest, t["dispatch"])
        pairs = _group_pairs(counts, m, t["moe"])
        y = _moe_mm_call(xs, pairs, e_w_gate, e_w_up, e_w_down, l, t["moe"])
        h = _combine_call(h, y, dest, gates, s_w_gate[l], s_w_up[l], s_w_down[l],
                          ln3_g[l], ln3_b[l], t["combine"])
    return h.reshape(b, s, d)
```

```python
import functools
import math

import jax
import jax.numpy as jnp
import numpy as np
from jax import lax
from jax.experimental import pallas as pl
from jax.experimental.pallas import tpu as pltpu

F32 = jnp.float32
BF16 = jnp.bfloat16
I32 = jnp.int32
I16 = jnp.int16

D_MODEL = 1024
DEPTH = 2
CHUNK = 64
CHUNK_SHIFT = 6
A_HEADS = 8
A_HEAD_DIM = 64
A_WIDTH = A_HEADS * A_HEAD_DIM
KV_LORA = 128
IDX_HEADS = 8
IDX_DIM = 64
INDEX_TOPK_MAX = 256
B_HEADS = 4
B_QK_DIM = 64
B_V_DIM = 128
B_WIDTH = B_HEADS * B_V_DIM
MEM_HEADS = 4
MEM_HEAD_DIM = D_MODEL // MEM_HEADS
N_EXPERTS = 64
TOP_K = 8
EXPERT_DIM = 256
SHARED_DIM = 256
ROUTED_SCALE = 2.5
ALPHA = (2 * DEPTH) ** 0.25
LN_EPS = 1e-5
RMS_EPS = 1e-6

LANES = 128
SUBLANES = 8
VMEM_LIMIT = 56 * 1024 * 1024
INT_MIN = -2 ** 31
I16_MIN = -2 ** 15
V_PAD = 16
N_DMA_QUEUES = 2
NEG_BIG = -3.0e38
M_INIT = -1.0e30

NT_DIMS = (((1,), (1,)), ((), ()))


def _cparams(n_axes):
    return pltpu.CompilerParams(dimension_semantics=("arbitrary",) * n_axes,
                                vmem_limit_bytes=VMEM_LIMIT)


def _layer_norm(x, g, b):
    mu = jnp.mean(x, axis=-1, keepdims=True)
    xc = x - mu
    var = jnp.mean(xc * xc, axis=-1, keepdims=True)
    return xc * lax.rsqrt(var + LN_EPS) * g + b


def _silu(x):
    return x * (1.0 / (1.0 + jnp.exp(-x)))


def _ln_kernel(x_ref, g_ref, b_ref, o_ref):
    o_ref[...] = _layer_norm(x_ref[...], g_ref[...], b_ref[...])


def _ln_call(x, g, b, tm):
    n, d = x.shape
    return pl.pallas_call(
        _ln_kernel,
        grid=(n // tm,),
        in_specs=[pl.BlockSpec((tm, d), lambda i: (i, 0)),
                  pl.BlockSpec((1, d), lambda i: (0, 0)),
                  pl.BlockSpec((1, d), lambda i: (0, 0))],
        out_specs=pl.BlockSpec((tm, d), lambda i: (i, 0)),
        out_shape=jax.ShapeDtypeStruct((n, d), F32),
        compiler_params=_cparams(1),
        name="ln_in",
    )(x, g.reshape(1, d), b.reshape(1, d))


_SEG_QA = (0, 512)
_SEG_CKV = (512, 640)
_SEG_QIDX = (640, 1152)
_SEG_KK = (1152, 1280)
_SEG_WIDX = (1280, 1408)
_SEG_QB = (1408, 1920)
_SEG_KB = (1920, 2432)
_SEG_VB = (2432, 2944)
_PROJ_COLS = 2944


def _pos_features(pos, shape):
    lane = lax.broadcasted_iota(I32, shape, 1)
    hi = lax.shift_right_logical(pos, CHUNK_SHIFT).astype(F32)
    lo = (pos & (CHUNK - 1)).astype(F32)
    return jnp.where(lane < 2, 1.0, jnp.where(lane == 2, hi, jnp.where(lane == 3, lo, 0.0)))


def _slope_features(tqi, slope, shape):
    feat = lax.broadcasted_iota(I32, shape, 0)
    hi = lax.shift_right_logical(tqi, CHUNK_SHIFT).astype(F32)
    lo = (tqi & (CHUNK - 1)).astype(F32)
    return jnp.where(feat == 0, -slope * CHUNK * hi,
                     jnp.where(feat == 1, -slope * lo,
                               jnp.where(feat == 2, slope * CHUNK,
                                         jnp.where(feat == 3, slope, 0.0))))


def _with_ones_rows(vt):
    t = vt.shape[1]
    sub = lax.broadcasted_iota(I32, (V_PAD, t), 0)
    return jnp.concatenate([vt, jnp.where(sub == 0, 1.0, 0.0)], axis=0)


def _proj_kernel(h_ref, w_ref, kvn_ref, qa_ref, ckv_ref, ckvt_ref, qidx_ref, kk_ref, widx_ref,
                 qb_ref, kb_ref, vbt_ref, *, seq):
    tm = h_ref.shape[0]
    x = h_ref[...].astype(BF16)

    def seg(s):
        return jnp.dot(x, w_ref[:, s[0]:s[1]], preferred_element_type=F32)

    row = pl.program_id(0) * tm + lax.broadcasted_iota(I32, (tm, LANES), 0)
    feat = _pos_features(row & (seq - 1), (tm, LANES)).astype(BF16)

    qa_ref[...] = (seg(_SEG_QA) * (A_HEAD_DIM ** -0.5)).astype(BF16)
    c = seg(_SEG_CKV)
    c = c * lax.rsqrt(jnp.mean(c * c, axis=-1, keepdims=True) + RMS_EPS) * kvn_ref[...]
    ckv_ref[:, :KV_LORA] = c.astype(BF16)
    ckv_ref[:, KV_LORA:] = feat
    ckvt_ref[...] = _with_ones_rows(c.T).astype(BF16)
    qidx_ref[...] = (seg(_SEG_QIDX) * (IDX_DIM ** -0.5)).astype(BF16)
    kk_ref[...] = seg(_SEG_KK).astype(BF16)
    widx_ref[...] = seg(_SEG_WIDX) * (IDX_HEADS ** -0.5)
    qb_ref[...] = (seg(_SEG_QB) * (B_QK_DIM ** -0.5)).astype(BF16)
    kb = seg(_SEG_KB).astype(BF16)
    vt = seg(_SEG_VB).T
    for h in range(B_HEADS):
        kb_ref[:, 2 * h * LANES:(2 * h + 1) * LANES] = kb[:, h * LANES:(h + 1) * LANES]
        kb_ref[:, (2 * h + 1) * LANES:(2 * h + 2) * LANES] = feat
        vbt_ref[h * (B_V_DIM + V_PAD):(h + 1) * (B_V_DIM + V_PAD), :] = _with_ones_rows(
            vt[h * B_V_DIM:(h + 1) * B_V_DIM]).astype(BF16)


def _proj_call(h, w_pad, kvn, tm, seq):
    n, d = h.shape
    assert seq & (seq - 1) == 0 and seq <= CHUNK * 256 and seq % tm == 0
    outs = [(512, BF16, False), (2 * KV_LORA, BF16, False), (KV_LORA + V_PAD, BF16, True),
            (512, BF16, False), (128, BF16, False), (128, F32, False), (512, BF16, False),
            (2 * B_HEADS * LANES, BF16, False), (B_HEADS * (B_V_DIM + V_PAD), BF16, True)]
    spec = lambda w, t: (pl.BlockSpec((w, tm), lambda i: (0, i)) if t
                         else pl.BlockSpec((tm, w), lambda i: (i, 0)))
    shape = lambda w, dt, t: jax.ShapeDtypeStruct((w, n) if t else (n, w), dt)
    return pl.pallas_call(
        functools.partial(_proj_kernel, seq=seq),
        grid=(n // tm,),
        in_specs=[pl.BlockSpec((tm, d), lambda i: (i, 0)),
                  pl.BlockSpec((d, _PROJ_COLS), lambda i: (0, 0)),
                  pl.BlockSpec((1, KV_LORA), lambda i: (0, 0))],
        out_specs=[spec(w, t) for w, _, t in outs],
        out_shape=[shape(w, dt, t) for w, dt, t in outs],
        compiler_params=_cparams(1),
        name="proj_in",
    )(h, w_pad, kvn.reshape(1, KV_LORA))


def _pad_w_in(w_in):
    sizes = (A_WIDTH, KV_LORA, IDX_HEADS * IDX_DIM, IDX_DIM, IDX_HEADS,
             2 * B_HEADS * B_QK_DIM, 2 * B_HEADS * B_QK_DIM, B_WIDTH)
    splits = np.cumsum(sizes)[:-1].tolist()
    q_a, c_kv, q_idx, k_idx, w_idx, q_b, k_b, v_b = jnp.split(w_in, splits, axis=-1)
    w_idx = jnp.pad(w_idx, ((0, 0), (0, LANES - IDX_HEADS)))
    return jnp.concatenate([q_a, c_kv, q_idx, k_idx, k_idx, w_idx, q_b, k_b, v_b],
                           axis=-1).astype(BF16)


def _flash_probs(sl, sc, m_ref):
    m_old = m_ref[:, sl]
    m_new = jnp.maximum(m_old, jnp.max(sc, axis=0, keepdims=True))
    m_ref[:, sl] = m_new
    return jnp.exp(m_old - m_new), jnp.exp(sc - m_new).astype(BF16)


def _flash_init(m_ref, acc_ref):
    m_ref[...] = jnp.full(m_ref.shape, M_INIT, F32)
    acc_ref[...] = jnp.zeros(acc_ref.shape, F32)


def _split_halves(q_ref, n_pairs, out_ref, tq):
    q_t = q_ref[...].astype(F32).T
    feat = lax.broadcasted_iota(I32, (LANES, tq), 0)
    for j in range(n_pairs):
        qp = q_t[j * LANES:(j + 1) * LANES, :]
        out_ref[:LANES, 2 * j * tq:(2 * j + 1) * tq] = jnp.where(feat < 64, qp, 0.0).astype(BF16)
        out_ref[:LANES, (2 * j + 1) * tq:(2 * j + 2) * tq] = jnp.where(feat >= 64, qp, 0.0).astype(BF16)


def _dsa_kernel(qa_ref, qidx_ref, widx_ref, kk_ref, ckv_ref, ckvt_ref, wk_ref, wvt_ref, out_ref,
                keys_ref, khi_ref, klo_ref, qim_ref, qlat_ref, tj_ref, m_ref, acc_ref,
                *, tq, tk, ts, seq, topk):
    i = pl.program_id(1)
    n_off = i * (tq // tk)
    n_kt = n_off + tq // tk
    idx_bits = int(math.log2(seq))

    _split_halves(qidx_ref, IDX_HEADS // 2, qim_ref, tq)
    qlat_t = lax.dot_general(wk_ref[...], qa_ref[...], NT_DIMS,
                             preferred_element_type=F32)
    t_row = i * tq + lax.broadcasted_iota(I32, (LANES, tq), 1)
    for h in range(A_HEADS):
        qlat_ref[:KV_LORA, h * tq:(h + 1) * tq] = qlat_t[h * KV_LORA:(h + 1) * KV_LORA, :].astype(BF16)
        qlat_ref[KV_LORA:, h * tq:(h + 1) * tq] = _slope_features(
            t_row, 2.0 ** -(h + 1), (LANES, tq)).astype(BF16)
    w_t = widx_ref[...].T

    row = lax.broadcasted_iota(I32, (tk, tq), 0)
    tqi = i * tq + lax.broadcasted_iota(I32, (tk, tq), 1)
    qchunk = lax.shift_right_logical(tqi, CHUNK_SHIFT)

    def idx_tile(kt, overlaps_block):
        off = pl.multiple_of(kt * tk, tk)
        kkt = kk_ref[pl.ds(off, tk), :]
        s_all = jnp.dot(kkt, qim_ref[...], preferred_element_type=F32)
        acc = jnp.zeros((tk, tq), F32)
        for h in range(IDX_HEADS):
            acc = acc + w_t[h:h + 1, :] * jnp.maximum(s_all[:, h * tq:(h + 1) * tq], 0.0)
        bits = lax.bitcast_convert_type(acc + 0.0, I32)
        key = jnp.where(bits < 0, bits ^ jnp.int32(0x7FFFFFFF), bits)
        if overlaps_block:
            adm = lax.shift_right_logical(off + row, CHUNK_SHIFT) <= qchunk
            key = jnp.where(adm, key, jnp.int32(INT_MIN))
        keys_ref[pl.ds(off, tk), :] = key
        khi_ref[pl.ds(off, tk), :] = lax.shift_right_arithmetic(key, 16).astype(I16)
        klo_ref[pl.ds(off, tk), :] = ((key & 0xFFFF) + I16_MIN).astype(I16)

    def idx_off_body(kt, carry):
        idx_tile(kt, False)
        return carry

    def idx_diag_body(kt, carry):
        idx_tile(kt, True)
        return carry

    lax.fori_loop(0, n_off, idx_off_body, 0)
    lax.fori_loop(n_off, n_kt, idx_diag_body, 0)

    n_st = n_kt * (tk // ts)
    row_s = lax.broadcasted_iota(I32, (ts, tq), 0)

    def count(pred):
        def body(st, c):
            off = pl.multiple_of(st * ts, ts)
            ind = jnp.where(pred(keys_ref[pl.ds(off, ts), :], off + row_s), 1.0, 0.0)
            return c + jnp.sum(ind, axis=0, keepdims=True)
        return lax.fori_loop(0, n_st, body, jnp.zeros((1, tq), F32))

    def count16(ref, pred):
        pack = 2 * SUBLANES
        def body(st, c):
            off = pl.multiple_of(st * ts, ts)
            ind = jnp.where(pred(ref[pl.ds(off, ts), :]), jnp.int16(1), jnp.int16(0))
            parts = [ind[j * pack:(j + 1) * pack] for j in range(ts // pack)]
            while len(parts) > 1:
                parts = [parts[j] + parts[j + 1] for j in range(0, len(parts), 2)]
            return c + parts[0]
        c = lax.fori_loop(0, n_st, body, jnp.zeros((pack, tq), I16))
        return jnp.sum(c.astype(F32), axis=0, keepdims=True)

    def search16(ref, need):
        def body(it, t):
            cand = t + lax.shift_left(jnp.int32(1), 15 - it)
            cand16 = cand.astype(I16)
            cnt = count16(ref, lambda tile: tile >= cand16)
            return jnp.where(cnt >= need, cand, t)
        return lax.fori_loop(0, 16, body, jnp.full((1, tq), I16_MIN, I32))

    t_hi = search16(khi_ref, topk)
    t_hi16 = t_hi.astype(I16)
    n_above = count16(khi_ref, lambda tile: tile > t_hi16)

    def park_body(st, carry):
        off = pl.multiple_of(st * ts, ts)
        klo_ref[pl.ds(off, ts), :] = jnp.where(khi_ref[pl.ds(off, ts), :] == t_hi16,
                                               klo_ref[pl.ds(off, ts), :], jnp.int16(I16_MIN))
        return carry

    lax.fori_loop(0, n_st, park_body, 0)
    t_lo = search16(klo_ref, topk - n_above)
    thr = t_hi * 65536 + (t_lo - I16_MIN)

    live = thr > INT_MIN
    t_lo16 = t_lo.astype(I16)
    n_ge = n_above + count16(klo_ref, lambda tile: tile >= t_lo16)
    excess = jnp.logical_and(n_ge > topk, live)
    all_ties = jnp.where(live, seq, -1).astype(I32)
    tj_ref[0] = jnp.broadcast_to(thr, (SUBLANES, tq))
    tj_ref[1] = jnp.broadcast_to(all_ties, (SUBLANES, tq))

    @pl.when(jnp.max(jnp.where(excess, 1.0, 0.0)) > 0.0)
    def _():
        need = topk - count(lambda tile, _: tile > thr)

        def tie_body(it, hi):
            cand = hi + lax.shift_left(jnp.int32(1), idx_bits - 1 - it)
            cnt = count(lambda tile, pos: jnp.logical_and(tile == thr, pos < cand))
            return jnp.where(cnt < need, cand, hi)

        hi = lax.fori_loop(0, idx_bits, tie_body, jnp.zeros((1, tq), I32))
        tj_ref[1] = jnp.broadcast_to(jnp.where(excess, hi, all_ties), (SUBLANES, tq))

    thr_b = tj_ref[0][:1]
    tie_hi = tj_ref[1][:1]

    _flash_init(m_ref, acc_ref)

    def att_tile(kt, late_fix):
        off = pl.multiple_of(kt * tk, tk)
        tile = keys_ref[pl.ds(off, tk), :]
        pos = off + row
        sel = jnp.logical_or(tile > thr_b,
                             jnp.logical_and(tile == thr_b, pos <= tie_hi))
        s_all = jnp.dot(ckv_ref[pl.ds(off, tk), :], qlat_ref[...],
                        preferred_element_type=F32)
        if late_fix:
            late = jnp.maximum(pos - tqi, 0).astype(F32)
        alphas, probs = [], []
        for h in range(A_HEADS):
            sl = slice(h * tq, (h + 1) * tq)
            sc = s_all[:, sl]
            if late_fix:
                sc = sc - (2.0 * 2.0 ** -(h + 1)) * late
            alpha, p = _flash_probs(sl, jnp.where(sel, sc, NEG_BIG), m_ref)
            alphas.append(alpha)
            probs.append(p)
        pv = jnp.dot(ckvt_ref[:, pl.ds(off, tk)], jnp.concatenate(probs, axis=1),
                     preferred_element_type=F32)
        acc_ref[...] = jnp.concatenate(alphas, axis=1) * acc_ref[...] + pv

    def off_body(kt, carry):
        att_tile(kt, False)
        return carry

    def diag_body(kt, carry):
        att_tile(kt, True)
        return carry

    lax.fori_loop(0, n_off, off_body, 0)
    lax.fori_loop(n_off, n_kt, diag_body, 0)

    inv_l = 1.0 / acc_ref[KV_LORA:KV_LORA + 1, :]
    olat = jnp.concatenate([(acc_ref[:KV_LORA, h * tq:(h + 1) * tq]
                             * inv_l[:, h * tq:(h + 1) * tq]).astype(BF16)
                            for h in range(A_HEADS)], axis=0)
    out_t = jnp.dot(wvt_ref[...], olat, preferred_element_type=F32)
    out_ref[...] = out_t.T.astype(BF16)


def _dsa_call(qa, qidx, widx, kk, ckv, ckvt, wk_bd, wvt_bd, b, s, tq, tk, ts):
    topk = min(INDEX_TOPK_MAX, s // 4)
    assert s % tq == 0 and tq % tk == 0 and tk % ts == 0 and ts % LANES == 0 and s & (s - 1) == 0
    blk = lambda w: pl.BlockSpec((None, tq, w), lambda bi, i: (bi, i, 0))
    full = lambda w: pl.BlockSpec((None, s, w), lambda bi, i: (bi, 0, 0))
    const = lambda a: pl.BlockSpec(a.shape, lambda bi, i: (0, 0))
    r3 = lambda a: a.reshape(b, s, a.shape[-1])
    out = pl.pallas_call(
        functools.partial(_dsa_kernel, tq=tq, tk=tk, ts=ts, seq=s, topk=topk),
        grid=(b, s // tq),
        in_specs=[blk(A_WIDTH), blk(IDX_HEADS * IDX_DIM), blk(LANES), full(LANES),
                  full(2 * KV_LORA),
                  pl.BlockSpec((KV_LORA + V_PAD, s), lambda bi, i: (0, bi)),
                  const(wk_bd), const(wvt_bd)],
        out_specs=blk(A_WIDTH),
        out_shape=jax.ShapeDtypeStruct((b, s, A_WIDTH), BF16),
        scratch_shapes=[pltpu.VMEM((s, tq), I32),
                        pltpu.VMEM((s, tq), I16),
                        pltpu.VMEM((s, tq), I16),
                        pltpu.VMEM((LANES, IDX_HEADS * tq), BF16),
                        pltpu.VMEM((2 * KV_LORA, A_HEADS * tq), BF16),
                        pltpu.VMEM((2, SUBLANES, tq), I32),
                        pltpu.VMEM((1, A_HEADS * tq), F32),
                        pltpu.VMEM((KV_LORA + V_PAD, A_HEADS * tq), F32)],
        compiler_params=_cparams(2),
        name="dsa",
    )(r3(qa), r3(qidx), r3(widx), r3(kk), r3(ckv), ckvt, wk_bd, wvt_bd)
    return out.reshape(b * s, A_WIDTH)


def _block_diag_uk(w_uk):
    h, c, d = w_uk.shape
    eye = jnp.eye(h, dtype=w_uk.dtype)
    return jnp.einsum('hcd,hg->hcgd', w_uk, eye).reshape(h * c, h * d).astype(BF16)


def _block_diag_uv_t(w_uv):
    h, c, d = w_uv.shape
    eye = jnp.eye(h, dtype=w_uv.dtype)
    return jnp.einsum('hcd,hg->hdgc', w_uv, eye).reshape(h * d, h * c).astype(BF16)


def _diff_kernel(qb_ref, kb_ref, vbt_ref, lam_ref, subln_ref, out_ref,
                 qm_ref, m_ref, acc_ref, *, tq, tk, lambda_init):
    i = pl.program_id(1)
    n_off = i * (tq // tk)
    dvp = B_V_DIM + V_PAD
    _split_halves(qb_ref, B_HEADS, qm_ref, tq)
    t_row = i * tq + lax.broadcasted_iota(I32, (LANES, tq), 1)
    for h in range(B_HEADS):
        feat = _slope_features(t_row, 2.0 ** (-2 * (h + 1)), (LANES, tq)).astype(BF16)
        qm_ref[LANES:, 2 * h * tq:(2 * h + 1) * tq] = feat
        qm_ref[LANES:, (2 * h + 1) * tq:(2 * h + 2) * tq] = feat

    row = lax.broadcasted_iota(I32, (tk, tq), 0)
    tqi = i * tq + lax.broadcasted_iota(I32, (tk, tq), 1)
    qchunk = lax.shift_right_logical(tqi, CHUNK_SHIFT)
    _flash_init(m_ref, acc_ref)

    def tile_step(kt, diag):
        off = pl.multiple_of(kt * tk, tk)
        s_pairs = [jnp.dot(kb_ref[pl.ds(off, tk), 2 * h * LANES:(2 * h + 2) * LANES],
                           qm_ref[:, 2 * h * tq:(2 * h + 2) * tq],
                           preferred_element_type=F32) for h in range(B_HEADS)]
        if diag:
            pos = off + row
            adm = lax.shift_right_logical(pos, CHUNK_SHIFT) <= qchunk
            late = jnp.maximum(pos - tqi, 0).astype(F32)
        alphas, probs = [], []
        for h in range(B_HEADS):
            for j in range(2):
                sl = slice((2 * h + j) * tq, (2 * h + j + 1) * tq)
                sc = s_pairs[h][:, j * tq:(j + 1) * tq]
                if diag:
                    sc = jnp.where(adm, sc - (2.0 * 2.0 ** (-2 * (h + 1))) * late, NEG_BIG)
                alpha, p = _flash_probs(sl, sc, m_ref)
                alphas.append(alpha)
                probs.append(p)
        pv = jnp.concatenate(
            [jnp.dot(vbt_ref[h * dvp:(h + 1) * dvp, pl.ds(off, tk)],
                     jnp.concatenate(probs[2 * h:2 * h + 2], axis=1), preferred_element_type=F32)
             for h in range(B_HEADS)], axis=1)
        acc_ref[...] = jnp.concatenate(alphas, axis=1) * acc_ref[...] + pv

    def off_body(kt, carry):
        tile_step(kt, False)
        return carry

    def diag_body(kt, carry):
        tile_step(kt, True)
        return carry

    lax.fori_loop(0, n_off, off_body, 0)
    lax.fori_loop(n_off, n_off + tq // tk, diag_body, 0)

    lp = lam_ref[...]
    lam = (jnp.exp(jnp.sum(lp[0:1] * lp[1:2], axis=1, keepdims=True))
           - jnp.exp(jnp.sum(lp[2:3] * lp[3:4], axis=1, keepdims=True)) + lambda_init)
    o_all = acc_ref[:B_V_DIM, :] * (1.0 / acc_ref[B_V_DIM:B_V_DIM + 1, :])
    outs = []
    for h in range(B_HEADS):
        o = (o_all[:, 2 * h * tq:(2 * h + 1) * tq]
             - lam * o_all[:, (2 * h + 1) * tq:(2 * h + 2) * tq])
        o = o * lax.rsqrt(jnp.mean(o * o, axis=0, keepdims=True) + RMS_EPS) * subln_ref[...]
        outs.append(o * (1.0 - lambda_init))
    out_ref[...] = jnp.concatenate(outs, axis=0).T.astype(BF16)


def _diff_call(qb, kb, vbt, lam_rows, subln, b, s, tq, tk, lambda_init):
    blk = pl.BlockSpec((None, tq, B_WIDTH), lambda bi, i: (bi, i, 0))
    kw = 2 * B_HEADS * LANES
    dvp = B_V_DIM + V_PAD
    out = pl.pallas_call(
        functools.partial(_diff_kernel, tq=tq, tk=tk, lambda_init=lambda_init),
        grid=(b, s // tq),
        in_specs=[blk, pl.BlockSpec((None, s, kw), lambda bi, i: (bi, 0, 0)),
                  pl.BlockSpec((B_HEADS * dvp, s), lambda bi, i: (0, bi)),
                  pl.BlockSpec((SUBLANES, LANES), lambda bi, i: (0, 0)),
                  pl.BlockSpec((B_V_DIM, 1), lambda bi, i: (0, 0))],
        out_specs=blk,
        out_shape=jax.ShapeDtypeStruct((b, s, B_WIDTH), BF16),
        scratch_shapes=[pltpu.VMEM((2 * LANES, 2 * B_HEADS * tq), BF16),
                        pltpu.VMEM((1, 2 * B_HEADS * tq), F32),
                        pltpu.VMEM((dvp, 2 * B_HEADS * tq), F32)],
        compiler_params=_cparams(2),
        name="diff",
    )(qb.reshape(b, s, B_WIDTH), kb.reshape(b, s, kw), vbt, lam_rows, subln.reshape(B_V_DIM, 1))
    return out.reshape(b * s, B_WIDTH)


def _mix_out_kernel(h_ref, a_ref, b_ref, wa_ref, wb_ref, g_ref, beta_ref, o_ref):
    y = (jnp.dot(a_ref[...], wa_ref[...], preferred_element_type=F32)
         + jnp.dot(b_ref[...], wb_ref[...], preferred_element_type=F32))
    o_ref[...] = _layer_norm(ALPHA * h_ref[...] + y, g_ref[...], beta_ref[...])


def _mix_out_call(h, out_a, out_b, w_o, g, beta, tm):
    n, d = h.shape
    w = w_o.astype(BF16)
    row = lambda wd: pl.BlockSpec((tm, wd), lambda i: (i, 0))
    const = lambda shp: pl.BlockSpec(shp, lambda i: (0, 0))
    return pl.pallas_call(
        _mix_out_kernel,
        grid=(n // tm,),
        in_specs=[row(d), row(A_WIDTH), row(B_WIDTH), const((A_WIDTH, d)), const((B_WIDTH, d)),
                  const((1, d)), const((1, d))],
        out_specs=row(d),
        out_shape=jax.ShapeDtypeStruct((n, d), F32),
        compiler_params=_cparams(1),
        name="mix_out",
    )(h, out_a, out_b, w[:A_WIDTH], w[A_WIDTH:], g.reshape(1, d), beta.reshape(1, d))


def _mem_kv_kernel(m_ref, w_ref, k_ref, v_ref):
    x = m_ref[...].astype(BF16)
    d = k_ref.shape[-1]
    k_ref[...] = jnp.dot(x, w_ref[:, :d], preferred_element_type=F32).astype(BF16)
    v_ref[...] = jnp.dot(x, w_ref[:, d:], preferred_element_type=F32).astype(BF16)


def _mem_kv_call(mem2d, wkv, tm):
    n, d = mem2d.shape
    return pl.pallas_call(
        _mem_kv_kernel,
        grid=(n // tm,),
        in_specs=[pl.BlockSpec((tm, d), lambda i: (i, 0)),
                  pl.BlockSpec((d, 2 * d), lambda i: (0, 0))],
        out_specs=[pl.BlockSpec((tm, d), lambda i: (i, 0))] * 2,
        out_shape=[jax.ShapeDtypeStruct((n, d), BF16)] * 2,
        compiler_params=_cparams(1),
        name="mem_kv",
    )(mem2d, wkv.astype(BF16))


def _mem_attn_kernel(h_ref, wq_ref, k_ref, v_ref, wo_ref, g_ref, beta_ref, o_ref):
    h = h_ref[...]
    q = jnp.dot(h.astype(BF16), wq_ref[...], preferred_element_type=F32)
    q = (q * (MEM_HEAD_DIM ** -0.5)).astype(BF16)
    outs = []
    for hd in range(MEM_HEADS):
        sl = slice(hd * MEM_HEAD_DIM, (hd + 1) * MEM_HEAD_DIM)
        sc = lax.dot_general(q[:, sl], k_ref[:, sl], NT_DIMS, preferred_element_type=F32)
        p = jnp.exp(sc - jnp.max(sc, axis=1, keepdims=True))
        den = jnp.sum(p, axis=1, keepdims=True)
        o = jnp.dot(p.astype(BF16), v_ref[:, sl], preferred_element_type=F32) / den
        outs.append(o.astype(BF16))
    y = jnp.dot(jnp.concatenate(outs, axis=1), wo_ref[...], preferred_element_type=F32)
    o_ref[...] = _layer_norm(ALPHA * h + y, g_ref[...], beta_ref[...])


def _mem_attn_call(h, wq, k, v, wo, g, beta, b, s, tm):
    n, d = h.shape
    m = k.shape[0] // b
    const = lambda shp: pl.BlockSpec(shp, lambda bi, i: (0,) * len(shp))
    row = pl.BlockSpec((None, tm, d), lambda bi, i: (bi, i, 0))
    kv = pl.BlockSpec((None, m, d), lambda bi, i: (bi, 0, 0))
    out = pl.pallas_call(
        _mem_attn_kernel,
        grid=(b, s // tm),
        in_specs=[row, const((d, d)), kv, kv, const((d, d)), const((1, d)), const((1, d))],
        out_specs=row,
        out_shape=jax.ShapeDtypeStruct((b, s, d), F32),
        compiler_params=_cparams(2),
        name="mem_attn",
    )(h.reshape(b, s, d), wq.astype(BF16), k.reshape(b, m, d), v.reshape(b, m, d),
      wo.astype(BF16), g.reshape(1, d), beta.reshape(1, d))
    return out.reshape(n, d)


def _router_kernel(h_ref, rw_ref, bias_ref, tri_ref, e_ref, g_ref, rank_ref, cnt_ref, run_ref):
    tm = h_ref.shape[0]

    @pl.when(pl.program_id(0) == 0)
    def _():
        run_ref[...] = jnp.zeros(run_ref.shape, F32)

    logits = jnp.dot(h_ref[...].astype(BF16), rw_ref[...], preferred_element_type=F32)
    scores = 1.0 / (1.0 + jnp.exp(-logits))
    lane = lax.broadcasted_iota(I32, (tm, LANES), 1)
    lanef = lane.astype(F32)
    ninf = jnp.float32(-jnp.inf)
    cur = jnp.where(lane < N_EXPERTS, scores + bias_ref[...], ninf)
    top_e = jnp.zeros((tm, LANES), F32)
    top_s = jnp.zeros((tm, LANES), F32)
    picked = jnp.zeros((tm, LANES), F32)
    for k in range(TOP_K):
        mx = jnp.max(cur, axis=1, keepdims=True)
        first = jnp.min(jnp.where(cur == mx, lanef, float(LANES)), axis=1, keepdims=True)
        hit = lanef == first
        s_k = jnp.sum(jnp.where(hit, scores, 0.0), axis=1, keepdims=True)
        top_e = jnp.where(lane == k, first, top_e)
        top_s = jnp.where(lane == k, s_k, top_s)
        picked = jnp.where(hit, 1.0, picked)
        cur = jnp.where(hit, ninf, cur)
    e_ref[...] = top_e.astype(I32)
    g_ref[...] = top_s / jnp.sum(top_s, axis=1, keepdims=True) * ROUTED_SCALE

    slot = jnp.dot(tri_ref[...], picked.astype(BF16), preferred_element_type=F32) + run_ref[...]
    rank_ref[...] = slot.astype(I32)
    run = run_ref[...] + jnp.sum(picked, axis=0, keepdims=True)
    run_ref[...] = run
    cnt_ref[...] = run.astype(I32)


def _router_call(h, router_w, router_bias, tm):
    n, d = h.shape
    rw = jnp.pad(router_w, ((0, 0), (0, LANES - N_EXPERTS))).astype(BF16)
    bias = jnp.pad(router_bias, (0, LANES - N_EXPERTS)).reshape(1, LANES)
    tri = jnp.tri(tm, k=-1, dtype=BF16)
    row = pl.BlockSpec((tm, LANES), lambda i: (i, 0))
    const = lambda shp: pl.BlockSpec(shp, lambda i: (0, 0))
    return pl.pallas_call(
        _router_kernel,
        grid=(n // tm,),
        in_specs=[pl.BlockSpec((tm, d), lambda i: (i, 0)), const((d, LANES)), const((1, LANES)),
                  const((tm, tm))],
        out_specs=[row, row, row, const((1, LANES))],
        out_shape=[jax.ShapeDtypeStruct((n, LANES), I32),
                   jax.ShapeDtypeStruct((n, LANES), F32),
                   jax.ShapeDtypeStruct((n, LANES), I32),
                   jax.ShapeDtypeStruct((1, LANES), I32)],
        scratch_shapes=[pltpu.VMEM((1, LANES), F32)],
        compiler_params=_cparams(1),
        name="router",
    )(h, rw, bias, tri)


def _dest_kernel(e_ref, rank_ref, gs_ref, o_ref):
    tm = e_ref.shape[0]
    lane = lax.broadcasted_iota(I32, (tm, LANES), 1)
    e = e_ref[...]
    row_of = (gs_ref[...] + rank_ref[...]).astype(F32)
    dest = jnp.zeros((tm, LANES), F32)
    for k in range(TOP_K):
        d_k = jnp.sum(jnp.where(lane == e[:, k:k + 1], row_of, 0.0), axis=1, keepdims=True)
        dest = jnp.where(lane == k, d_k, dest)
    o_ref[...] = dest.astype(I32)


def _dest_call(top_e, rank, gstart, tm):
    n = top_e.shape[0]
    row = pl.BlockSpec((tm, LANES), lambda i: (i, 0))
    return pl.pallas_call(
        _dest_kernel,
        grid=(n // tm,),
        in_specs=[row, row, pl.BlockSpec((1, LANES), lambda i: (0, 0))],
        out_specs=row,
        out_shape=jax.ShapeDtypeStruct((n, LANES), I32),
        compiler_params=_cparams(1),
        name="dest",
    )(top_e, rank, gstart)


def _dispatch_kernel(dest_ref, x_ref, xs_ref, xp_ref, sem, *, td):
    half = x_ref.shape[1] // 2
    xp_ref[...] = pltpu.pack_elementwise([x_ref[:, :half], x_ref[:, half:]], packed_dtype=BF16)

    def row_copy(r, k):
        return pltpu.make_async_copy(xp_ref.at[pl.ds(r, 1), :],
                                     xs_ref.at[pl.ds(dest_ref[r, k], 1), :], sem)

    for r in range(td):
        for k in range(TOP_K):
            row_copy(r, k).start(priority=k % N_DMA_QUEUES)

    def drain(r, carry):
        for k in range(TOP_K):
            row_copy(r, k).wait()
        return carry

    lax.fori_loop(0, td, drain, 0)


def _dispatch_call(h, dest, td):
    n, d = h.shape
    m = n * TOP_K
    return pl.pallas_call(
        functools.partial(_dispatch_kernel, td=td),
        grid=(n // td,),
        in_specs=[pl.BlockSpec((td, LANES), lambda i: (i, 0), memory_space=pltpu.SMEM),
                  pl.BlockSpec((td, d), lambda i: (i, 0))],
        out_specs=pl.BlockSpec(memory_space=pl.ANY),
        out_shape=jax.ShapeDtypeStruct((m, d // 2), I32),
        scratch_shapes=[pltpu.VMEM((td, d // 2), I32), pltpu.SemaphoreType.DMA(())],
        compiler_params=_cparams(1),
        name="dispatch",
    )(dest, h)


def _moe_mm_kernel(ptile_ref, pexp_ref, plo_ref, phi_ref, xs_ref, wg_ref, wu_ref, wd_ref, y_ref,
                   *, tmm):
    p = pl.program_id(0)
    tile = ptile_ref[p]
    lo = plo_ref[p]
    hi = phi_ref[p]
    first = jnp.logical_or(p == 0, ptile_ref[jnp.maximum(p - 1, 0)] != tile)

    def expert_out():
        xp = xs_ref[...]
        x = jnp.concatenate(
            [pltpu.unpack_elementwise(xp, index=j, packed_dtype=BF16, unpacked_dtype=F32)
             for j in range(2)], axis=1).astype(BF16)
        g = jnp.dot(x, wg_ref[...].astype(BF16), preferred_element_type=F32)
        u = jnp.dot(x, wu_ref[...].astype(BF16), preferred_element_type=F32)
        hid = (_silu(g) * u).astype(BF16)
        return jnp.dot(hid, wd_ref[...].astype(BF16), preferred_element_type=F32)

    whole = jnp.logical_and(lo <= tile * tmm, hi >= (tile + 1) * tmm)

    @pl.when(whole)
    def _():
        y_ref[...] = expert_out()

    @pl.when(jnp.logical_and(first, jnp.logical_not(whole)))
    def _():
        y_ref[...] = jnp.zeros(y_ref.shape, F32)

    @pl.when(jnp.logical_and(lo < hi, jnp.logical_not(whole)))
    def _():
        y = expert_out()
        rowg = tile * tmm + lax.broadcasted_iota(I32, (tmm, 1), 0)
        y_ref[...] += jnp.where(jnp.logical_and(rowg >= lo, rowg < hi), y, 0.0)


def _moe_mm_call(xs, pairs, wg, wu, wd, layer, tmm):
    m = xs.shape[0]
    n_pairs = pairs[0].shape[0]
    rows = lambda w: pl.BlockSpec((tmm, w), lambda p, pt, pe, plo, phi: (pt[p], 0))
    wspec = lambda shp: pl.BlockSpec((None, None) + shp,
                                     lambda p, pt, pe, plo, phi: (layer, pe[p], 0, 0))
    return pl.pallas_call(
        functools.partial(_moe_mm_kernel, tmm=tmm),
        grid_spec=pltpu.PrefetchScalarGridSpec(
            num_scalar_prefetch=4,
            grid=(n_pairs,),
            in_specs=[rows(xs.shape[1]), wspec((D_MODEL, EXPERT_DIM)), wspec((D_MODEL, EXPERT_DIM)),
                      wspec((EXPERT_DIM, D_MODEL))],
            out_specs=rows(D_MODEL)),
        out_shape=jax.ShapeDtypeStruct((m, D_MODEL), F32),
        compiler_params=_cparams(1),
        name="moe_mm",
    )(*pairs, xs, wg, wu, wd)


def _group_pairs(counts, m, tmm):
    n_tiles = m // tmm
    n_pairs = n_tiles + N_EXPERTS
    gend = jnp.cumsum(counts)
    gstart = gend - counts
    t0 = jnp.arange(n_tiles, dtype=I32) * tmm
    n_le = lambda edges, v: jnp.sum(edges[None, :] <= v[:, None], axis=1).astype(I32)
    e_first = n_le(gend, t0)
    e_last = n_le(gend, t0 + (tmm - 1))
    per_tile = e_last - e_first + 1
    pend = jnp.cumsum(per_tile)
    pstart = pend - per_tile
    p = jnp.arange(n_pairs, dtype=I32)
    valid = p < pend[-1]
    tile = jnp.minimum(n_le(pend, p), n_tiles - 1)
    e = jnp.clip(e_first[tile] + (p - pstart[tile]), 0, N_EXPERTS - 1).astype(I32)
    lo = jnp.where(valid, gstart[e], 0).astype(I32)
    hi = jnp.where(valid, gend[e], 0).astype(I32)
    return tile, e, lo, hi


def _combine_kernel(dest_ref, gate_ref, h_ref, y_ref, sg_ref, su_ref, sd_ref, g_ref, beta_ref, o_ref,
                    buf_ref, sem, *, tc):
    def row_copy(r, k):
        return pltpu.make_async_copy(y_ref.at[pl.ds(dest_ref[r, k], 1), :],
                                     buf_ref.at[k, pl.ds(r, 1), :], sem)

    for r in range(tc):
        for k in range(TOP_K):
            row_copy(r, k).start(priority=k % N_DMA_QUEUES)

    h = h_ref[...]
    x = h.astype(BF16)
    hid = (_silu(jnp.dot(x, sg_ref[...], preferred_element_type=F32))
           * jnp.dot(x, su_ref[...], preferred_element_type=F32)).astype(BF16)
    shared = jnp.dot(hid, sd_ref[...], preferred_element_type=F32)

    def drain(r, carry):
        for k in range(TOP_K):
            row_copy(r, k).wait()
        return carry

    lax.fori_loop(0, tc, drain, 0)

    gates = gate_ref[...]
    routed = gates[:, 0:1] * buf_ref[0]
    for k in range(1, TOP_K):
        routed = routed + gates[:, k:k + 1] * buf_ref[k]
    o_ref[...] = _layer_norm(ALPHA * h + (routed + shared), g_ref[...], beta_ref[...])


def _combine_call(h, y, dest, gates, sg, su, sd, g, beta, tc):
    n, d = h.shape
    smem = pl.BlockSpec((tc, LANES), lambda i: (i, 0), memory_space=pltpu.SMEM)
    const = lambda shp: pl.BlockSpec(shp, lambda i: (0, 0))
    return pl.pallas_call(
        functools.partial(_combine_kernel, tc=tc),
        grid=(n // tc,),
        in_specs=[smem, pl.BlockSpec((tc, LANES), lambda i: (i, 0)),
                  pl.BlockSpec((tc, d), lambda i: (i, 0)),
                  pl.BlockSpec(memory_space=pl.ANY),
                  const((d, SHARED_DIM)), const((d, SHARED_DIM)), const((SHARED_DIM, d)),
                  const((1, d)), const((1, d))],
        out_specs=pl.BlockSpec((tc, d), lambda i: (i, 0)),
        out_shape=jax.ShapeDtypeStruct((n, d), F32),
        scratch_shapes=[pltpu.VMEM((TOP_K, tc, d), F32),
                        pltpu.SemaphoreType.DMA(())],
        compiler_params=_cparams(1),
        name="combine",
    )(dest, gates, h, y, sg.astype(BF16), su.astype(BF16), sd.astype(BF16),
      g.reshape(1, d), beta.reshape(1, d))


def _tiles(b, s):
    n = b * s
    pick = lambda pref, total: next(t for t in (pref, 512, 256, 128, 64, 32, 16, 8) if t <= pref and total % t == 0)
    return dict(
        rows=pick(512, n),
        attn_q=pick(512, s),
        attn_k=pick(512, s),
        select=pick(256, s),
        mem_rows=pick(512, s),
        mem_kv=pick(512, b * 256),
        router=pick(1024, n),
        dispatch=pick(128, n),
        moe=pick(512, n * TOP_K),
        combine=pick(128, n),
    )


def kernel(x, mem, ln_in_g, ln_in_b, w_in, a_kv_norm, a_w_uk, a_w_uv, b_lq1, b_lk1, b_lq2, b_lk2,
           b_subln, w_o, ln1_g, ln1_b, m_wq, m_wkv, m_wo, ln2_g, ln2_b, router_w, router_bias,
           e_w_gate, e_w_up, e_w_down, s_w_gate, s_w_up, s_w_down, ln3_g, ln3_b):
    b, s, d = x.shape
    n = b * s
    m = n * TOP_K
    t = _tiles(b, s)
    mem2d = mem.reshape(-1, d)

    h = _ln_call(x.reshape(n, d), ln_in_g, ln_in_b, t["rows"])
    for l in range(DEPTH):
        lambda_init = 0.8 - 0.6 * math.exp(-0.3 * l)

        qa, ckv, ckvt, qidx, kk, widx, qb, kb, vbt = _proj_call(h, _pad_w_in(w_in[l]), a_kv_norm[l],
                                                                  t["rows"], s)
        out_a = _dsa_call(qa, qidx, widx, kk, ckv, ckvt, _block_diag_uk(a_w_uk[l]),
                          _block_diag_uv_t(a_w_uv[l]), b, s, t["attn_q"], t["attn_k"], t["select"])
        lam_rows = jnp.pad(jnp.stack([b_lq1[l], b_lk1[l], b_lq2[l], b_lk2[l]]),
                           ((0, SUBLANES - 4), (0, LANES - B_QK_DIM)))
        out_b = _diff_call(qb, kb, vbt, lam_rows, b_subln[l], b, s, t["attn_q"], t["attn_k"], lambda_init)
        h = _mix_out_call(h, out_a, out_b, w_o[l], ln1_g[l], ln1_b[l], t["rows"])

        mk, mv = _mem_kv_call(mem2d, m_wkv[l], t["mem_kv"])
        h = _mem_attn_call(h, m_wq[l], mk, mv, m_wo[l], ln2_g[l], ln2_b[l], b, s, t["mem_rows"])

        top_e, gates, rank, counts = _router_call(h, router_w[l], router_bias[l], t["router"])
        gstart = jnp.cumsum(counts, axis=1) - counts
        dest = _dest_call(top_e, rank, gstart, t["router"])
        counts = counts[0, :N_EXPERTS]
        xs = _dispatch_call(h, dest, t["dispatch"])
        pairs = _group_pairs(counts, m, t["moe"])
        y = _moe_mm_call(xs, pairs, e_w_gate, e_w_up, e_w_down, l, t["moe"])
        h = _combine_call(h, y, dest, gates, s_w_gate[l], s_w_up[l], s_w_down[l],
                          ln3_g[l], ln3_b[l], t["combine"])
    return h.reshape(b, s, d)
```

```python
import functools
import math

import jax
import jax.numpy as jnp
import numpy as np
from jax import lax
from jax.experimental import pallas as pl
from jax.experimental.pallas import tpu as pltpu

F32 = jnp.float32
BF16 = jnp.bfloat16
I32 = jnp.int32
I16 = jnp.int16

D_MODEL = 1024
DEPTH = 2
CHUNK = 64
CHUNK_SHIFT = 6
A_HEADS = 8
A_HEAD_DIM = 64
A_WIDTH = A_HEADS * A_HEAD_DIM
KV_LORA = 128
IDX_HEADS = 8
IDX_DIM = 64
INDEX_TOPK_MAX = 256
B_HEADS = 4
B_QK_DIM = 64
B_V_DIM = 128
B_WIDTH = B_HEADS * B_V_DIM
MEM_HEADS = 4
MEM_HEAD_DIM = D_MODEL // MEM_HEADS
N_EXPERTS = 64
TOP_K = 8
EXPERT_DIM = 256
SHARED_DIM = 256
ROUTED_SCALE = 2.5
ALPHA = (2 * DEPTH) ** 0.25
LN_EPS = 1e-5
RMS_EPS = 1e-6

LANES = 128
SUBLANES = 8
VMEM_LIMIT = 56 * 1024 * 1024
INT_MIN = -2 ** 31
I16_MIN = -2 ** 15
V_PAD = 16
N_DMA_QUEUES = 2
NEG_BIG = -3.0e38
M_INIT = -1.0e30

NT_DIMS = (((1,), (1,)), ((), ()))


def _cparams(n_axes):
    return pltpu.CompilerParams(dimension_semantics=("arbitrary",) * n_axes,
                                vmem_limit_bytes=VMEM_LIMIT)


def _layer_norm(x, g, b):
    mu = jnp.mean(x, axis=-1, keepdims=True)
    xc = x - mu
    var = jnp.mean(xc * xc, axis=-1, keepdims=True)
    return xc * lax.rsqrt(var + LN_EPS) * g + b


def _silu(x):
    return x * (1.0 / (1.0 + jnp.exp(-x)))


def _ln_kernel(x_ref, g_ref, b_ref, o_ref):
    o_ref[...] = _layer_norm(x_ref[...], g_ref[...], b_ref[...])


def _ln_call(x, g, b, tm):
    n, d = x.shape
    return pl.pallas_call(
        _ln_kernel,
        grid=(n // tm,),
        in_specs=[pl.BlockSpec((tm, d), lambda i: (i, 0)),
                  pl.BlockSpec((1, d), lambda i: (0, 0)),
                  pl.BlockSpec((1, d), lambda i: (0, 0))],
        out_specs=pl.BlockSpec((tm, d), lambda i: (i, 0)),
        out_shape=jax.ShapeDtypeStruct((n, d), F32),
        compiler_params=_cparams(1),
        name="ln_in",
    )(x, g.reshape(1, d), b.reshape(1, d))


_SEG_QA = (0, 512)
_SEG_CKV = (512, 640)
_SEG_QIDX = (640, 1152)
_SEG_KK = (1152, 1280)
_SEG_WIDX = (1280, 1408)
_SEG_QB = (1408, 1920)
_SEG_KB = (1920, 2432)
_SEG_VB = (2432, 2944)
_PROJ_COLS = 2944


def _pos_features(pos, shape):
    lane = lax.broadcasted_iota(I32, shape, 1)
    hi = lax.shift_right_logical(pos, CHUNK_SHIFT).astype(F32)
    lo = (pos & (CHUNK - 1)).astype(F32)
    return jnp.where(lane < 2, 1.0, jnp.where(lane == 2, hi, jnp.where(lane == 3, lo, 0.0)))


def _slope_features(tqi, slope, shape):
    feat = lax.broadcasted_iota(I32, shape, 0)
    hi = lax.shift_right_logical(tqi, CHUNK_SHIFT).astype(F32)
    lo = (tqi & (CHUNK - 1)).astype(F32)
    return jnp.where(feat == 0, -slope * CHUNK * hi,
                     jnp.where(feat == 1, -slope * lo,
                               jnp.where(feat == 2, slope * CHUNK,
                                         jnp.where(feat == 3, slope, 0.0))))


def _with_ones_rows(vt):
    t = vt.shape[1]
    sub = lax.broadcasted_iota(I32, (V_PAD, t), 0)
    return jnp.concatenate([vt, jnp.where(sub == 0, 1.0, 0.0)], axis=0)


def _proj_kernel(h_ref, w_ref, kvn_ref, qa_ref, ckv_ref, ckvt_ref, qidx_ref, kk_ref, widx_ref,
                 qb_ref, kb_ref, vbt_ref, *, seq):
    tm = h_ref.shape[0]
    x = h_ref[...].astype(BF16)

    def seg(s):
        return jnp.dot(x, w_ref[:, s[0]:s[1]], preferred_element_type=F32)

    row = pl.program_id(0) * tm + lax.broadcasted_iota(I32, (tm, LANES), 0)
    feat = _pos_features(row & (seq - 1), (tm, LANES)).astype(BF16)

    qa_ref[...] = (seg(_SEG_QA) * (A_HEAD_DIM ** -0.5)).astype(BF16)
    c = seg(_SEG_CKV)
    c = c * lax.rsqrt(jnp.mean(c * c, axis=-1, keepdims=True) + RMS_EPS) * kvn_ref[...]
    ckv_ref[:, :KV_LORA] = c.astype(BF16)
    ckv_ref[:, KV_LORA:] = feat
    ckvt_ref[...] = _with_ones_rows(c.T).astype(BF16)
    qidx_ref[...] = (seg(_SEG_QIDX) * (IDX_DIM ** -0.5)).astype(BF16)
    kk_ref[...] = seg(_SEG_KK).astype(BF16)
    widx_ref[...] = seg(_SEG_WIDX) * (IDX_HEADS ** -0.5)
    qb_ref[...] = (seg(_SEG_QB) * (B_QK_DIM ** -0.5)).astype(BF16)
    kb = seg(_SEG_KB).astype(BF16)
    vt = seg(_SEG_VB).T
    for h in range(B_HEADS):
        kb_ref[:, 2 * h * LANES:(2 * h + 1) * LANES] = kb[:, h * LANES:(h + 1) * LANES]
        kb_ref[:, (2 * h + 1) * LANES:(2 * h + 2) * LANES] = feat
        vbt_ref[h * (B_V_DIM + V_PAD):(h + 1) * (B_V_DIM + V_PAD), :] = _with_ones_rows(
            vt[h * B_V_DIM:(h + 1) * B_V_DIM]).astype(BF16)


def _proj_call(h, w_pad, kvn, tm, seq):
    n, d = h.shape
    assert seq & (seq - 1) == 0 and seq <= CHUNK * 256 and seq % tm == 0
    outs = [(512, BF16, False), (2 * KV_LORA, BF16, False), (KV_LORA + V_PAD, BF16, True),
            (512, BF16, False), (128, BF16, False), (128, F32, False), (512, BF16, False),
            (2 * B_HEADS * LANES, BF16, False), (B_HEADS * (B_V_DIM + V_PAD), BF16, True)]
    spec = lambda w, t: (pl.BlockSpec((w, tm), lambda i: (0, i)) if t
                         else pl.BlockSpec((tm, w), lambda i: (i, 0)))
    shape = lambda w, dt, t: jax.ShapeDtypeStruct((w, n) if t else (n, w), dt)
    return pl.pallas_call(
        functools.partial(_proj_kernel, seq=seq),
        grid=(n // tm,),
        in_specs=[pl.BlockSpec((tm, d), lambda i: (i, 0)),
                  pl.BlockSpec((d, _PROJ_COLS), lambda i: (0, 0)),
                  pl.BlockSpec((1, KV_LORA), lambda i: (0, 0))],
        out_specs=[spec(w, t) for w, _, t in outs],
        out_shape=[shape(w, dt, t) for w, dt, t in outs],
        compiler_params=_cparams(1),
        name="proj_in",
    )(h, w_pad, kvn.reshape(1, KV_LORA))


def _pad_w_in(w_in):
    sizes = (A_WIDTH, KV_LORA, IDX_HEADS * IDX_DIM, IDX_DIM, IDX_HEADS,
             2 * B_HEADS * B_QK_DIM, 2 * B_HEADS * B_QK_DIM, B_WIDTH)
    splits = np.cumsum(sizes)[:-1].tolist()
    q_a, c_kv, q_idx, k_idx, w_idx, q_b, k_b, v_b = jnp.split(w_in, splits, axis=-1)
    w_idx = jnp.pad(w_idx, ((0, 0), (0, LANES - IDX_HEADS)))
    return jnp.concatenate([q_a, c_kv, q_idx, k_idx, k_idx, w_idx, q_b, k_b, v_b],
                           axis=-1).astype(BF16)


def _flash_probs(sl, sc, m_ref):
    m_old = m_ref[:, sl]
    m_new = jnp.maximum(m_old, jnp.max(sc, axis=0, keepdims=True))
    m_ref[:, sl] = m_new
    return jnp.exp(m_old - m_new), jnp.exp(sc - m_new).astype(BF16)


def _flash_init(m_ref, acc_ref):
    m_ref[...] = jnp.full(m_ref.shape, M_INIT, F32)
    acc_ref[...] = jnp.zeros(acc_ref.shape, F32)


def _split_halves(q_ref, n_pairs, out_ref, tq):
    q_t = q_ref[...].astype(F32).T
    feat = lax.broadcasted_iota(I32, (LANES, tq), 0)
    for j in range(n_pairs):
        qp = q_t[j * LANES:(j + 1) * LANES, :]
        out_ref[:LANES, 2 * j * tq:(2 * j + 1) * tq] = jnp.where(feat < 64, qp, 0.0).astype(BF16)
        out_ref[:LANES, (2 * j + 1) * tq:(2 * j + 2) * tq] = jnp.where(feat >= 64, qp, 0.0).astype(BF16)


def _dsa_kernel(qa_ref, qidx_ref, widx_ref, kk_ref, ckv_ref, ckvt_ref, wk_ref, wvt_ref, out_ref,
                keys_ref, khi_ref, klo_ref, qim_ref, qlat_ref, tj_ref, m_ref, acc_ref,
                *, tq, tk, ts, seq, topk):
    i = pl.program_id(1)
    n_off = i * (tq // tk)
    n_kt = n_off + tq // tk
    idx_bits = int(math.log2(seq))

    _split_halves(qidx_ref, IDX_HEADS // 2, qim_ref, tq)
    qlat_t = lax.dot_general(wk_ref[...], qa_ref[...], NT_DIMS,
                             preferred_element_type=F32)
    t_row = i * tq + lax.broadcasted_iota(I32, (LANES, tq), 1)
    for h in range(A_HEADS):
        qlat_ref[:KV_LORA, h * tq:(h + 1) * tq] = qlat_t[h * KV_LORA:(h + 1) * KV_LORA, :].astype(BF16)
        qlat_ref[KV_LORA:, h * tq:(h + 1) * tq] = _slope_features(
            t_row, 2.0 ** -(h + 1), (LANES, tq)).astype(BF16)
    w_t = widx_ref[...].T

    row = lax.broadcasted_iota(I32, (tk, tq), 0)
    tqi = i * tq + lax.broadcasted_iota(I32, (tk, tq), 1)
    qchunk = lax.shift_right_logical(tqi, CHUNK_SHIFT)

    def idx_tile(kt, overlaps_block):
        off = pl.multiple_of(kt * tk, tk)
        kkt = kk_ref[pl.ds(off, tk), :]
        s_all = jnp.dot(kkt, qim_ref[...], preferred_element_type=F32)
        acc = jnp.zeros((tk, tq), F32)
        for h in range(IDX_HEADS):
            acc = acc + w_t[h:h + 1, :] * jnp.maximum(s_all[:, h * tq:(h + 1) * tq], 0.0)
        bits = lax.bitcast_convert_type(acc + 0.0, I32)
        key = jnp.where(bits < 0, bits ^ jnp.int32(0x7FFFFFFF), bits)
        if overlaps_block:
            adm = lax.shift_right_logical(off + row, CHUNK_SHIFT) <= qchunk
            key = jnp.where(adm, key, jnp.int32(INT_MIN))
        keys_ref[pl.ds(off, tk), :] = key
        khi_ref[pl.ds(off, tk), :] = lax.shift_right_arithmetic(key, 16).astype(I16)
        klo_ref[pl.ds(off, tk), :] = ((key & 0xFFFF) + I16_MIN).astype(I16)

    def idx_off_body(kt, carry):
        idx_tile(kt, False)
        return carry

    def idx_diag_body(kt, carry):
        idx_tile(kt, True)
        return carry

    lax.fori_loop(0, n_off, idx_off_body, 0)
    lax.fori_loop(n_off, n_kt, idx_diag_body, 0)

    n_st = n_kt * (tk // ts)
    row_s = lax.broadcasted_iota(I32, (ts, tq), 0)

    def count(pred):
        def body(st, c):
            off = pl.multiple_of(st * ts, ts)
            ind = jnp.where(pred(keys_ref[pl.ds(off, ts), :], off + row_s), 1.0, 0.0)
            return c + jnp.sum(ind, axis=0, keepdims=True)
        return lax.fori_loop(0, n_st, body, jnp.zeros((1, tq), F32))

    def count16(ref, pred):
        pack = 2 * SUBLANES
        def body(st, c):
            off = pl.multiple_of(st * ts, ts)
            ind = jnp.where(pred(ref[pl.ds(off, ts), :]), jnp.int16(1), jnp.int16(0))
            parts = [ind[j * pack:(j + 1) * pack] for j in range(ts // pack)]
            while len(parts) > 1:
                parts = [parts[j] + parts[j + 1] for j in range(0, len(parts), 2)]
            return c + parts[0]
        c = lax.fori_loop(0, n_st, body, jnp.zeros((pack, tq), I16))
        return jnp.sum(c.astype(F32), axis=0, keepdims=True)

    def search16(ref, need):
        def body(it, t):
            cand = t + lax.shift_left(jnp.int32(1), 15 - it)
            cand16 = cand.astype(I16)
            cnt = count16(ref, lambda tile: tile >= cand16)
            return jnp.where(cnt >= need, cand, t)
        return lax.fori_loop(0, 16, body, jnp.full((1, tq), I16_MIN, I32))

    t_hi = search16(khi_ref, topk)
    t_hi16 = t_hi.astype(I16)
    n_above = count16(khi_ref, lambda tile: tile > t_hi16)

    def park_body(st, carry):
        off = pl.multiple_of(st * ts, ts)
        klo_ref[pl.ds(off, ts), :] = jnp.where(khi_ref[pl.ds(off, ts), :] == t_hi16,
                                               klo_ref[pl.ds(off, ts), :], jnp.int16(I16_MIN))
        return carry

    lax.fori_loop(0, n_st, park_body, 0)
    t_lo = search16(klo_ref, topk - n_above)
    thr = t_hi * 65536 + (t_lo - I16_MIN)

    live = thr > INT_MIN
    t_lo16 = t_lo.astype(I16)
    n_ge = n_above + count16(klo_ref, lambda tile: tile >= t_lo16)
    excess = jnp.logical_and(n_ge > topk, live)
    all_ties = jnp.where(live, seq, -1).astype(I32)
    tj_ref[0] = jnp.broadcast_to(thr, (SUBLANES, tq))
    tj_ref[1] = jnp.broadcast_to(all_ties, (SUBLANES, tq))

    @pl.when(jnp.max(jnp.where(excess, 1.0, 0.0)) > 0.0)
    def _():
        need = topk - count(lambda tile, _: tile > thr)

        def tie_body(it, hi):
            cand = hi + lax.shift_left(jnp.int32(1), idx_bits - 1 - it)
            cnt = count(lambda tile, pos: jnp.logical_and(tile == thr, pos < cand))
            return jnp.where(cnt < need, cand, hi)

        hi = lax.fori_loop(0, idx_bits, tie_body, jnp.zeros((1, tq), I32))
        tj_ref[1] = jnp.broadcast_to(jnp.where(excess, hi, all_ties), (SUBLANES, tq))

    thr_b = tj_ref[0][:1]
    tie_hi = tj_ref[1][:1]

    _flash_init(m_ref, acc_ref)

    def att_tile(kt, late_fix):
        off = pl.multiple_of(kt * tk, tk)
        tile = keys_ref[pl.ds(off, tk), :]
        pos = off + row
        sel = jnp.logical_or(tile > thr_b,
                             jnp.logical_and(tile == thr_b, pos <= tie_hi))
        s_all = jnp.dot(ckv_ref[pl.ds(off, tk), :], qlat_ref[...],
                        preferred_element_type=F32)
        if late_fix:
            late = jnp.maximum(pos - tqi, 0).astype(F32)
        alphas, probs = [], []
        for h in range(A_HEADS):
            sl = slice(h * tq, (h + 1) * tq)
            sc = s_all[:, sl]
            if late_fix:
                sc = sc - (2.0 * 2.0 ** -(h + 1)) * late
            alpha, p = _flash_probs(sl, jnp.where(sel, sc, NEG_BIG), m_ref)
            alphas.append(alpha)
            probs.append(p)
        pv = jnp.dot(ckvt_ref[:, pl.ds(off, tk)], jnp.concatenate(probs, axis=1),
                     preferred_element_type=F32)
        acc_ref[...] = jnp.concatenate(alphas, axis=1) * acc_ref[...] + pv

    def off_body(kt, carry):
        att_tile(kt, False)
        return carry

    def diag_body(kt, carry):
        att_tile(kt, True)
        return carry

    lax.fori_loop(0, n_off, off_body, 0)
    lax.fori_loop(n_off, n_kt, diag_body, 0)

    inv_l = 1.0 / acc_ref[KV_LORA:KV_LORA + 1, :]
    olat = jnp.concatenate([(acc_ref[:KV_LORA, h * tq:(h + 1) * tq]
                             * inv_l[:, h * tq:(h + 1) * tq]).astype(BF16)
                            for h in range(A_HEADS)], axis=0)
    out_t = jnp.dot(wvt_ref[...], olat, preferred_element_type=F32)
    out_ref[...] = out_t.T.astype(BF16)


def _dsa_call(qa, qidx, widx, kk, ckv, ckvt, wk_bd, wvt_bd, b, s, tq, tk, ts):
    topk = min(INDEX_TOPK_MAX, s // 4)
    assert s % tq == 0 and tq % tk == 0 and tk % ts == 0 and ts % LANES == 0 and s & (s - 1) == 0
    blk = lambda w: pl.BlockSpec((None, tq, w), lambda bi, i: (bi, i, 0))
    full = lambda w: pl.BlockSpec((None, s, w), lambda bi, i: (bi, 0, 0))
    const = lambda a: pl.BlockSpec(a.shape, lambda bi, i: (0, 0))
    r3 = lambda a: a.reshape(b, s, a.shape[-1])
    out = pl.pallas_call(
        functools.partial(_dsa_kernel, tq=tq, tk=tk, ts=ts, seq=s, topk=topk),
        grid=(b, s // tq),
        in_specs=[blk(A_WIDTH), blk(IDX_HEADS * IDX_DIM), blk(LANES), full(LANES),
                  full(2 * KV_LORA),
                  pl.BlockSpec((KV_LORA + V_PAD, s), lambda bi, i: (0, bi)),
                  const(wk_bd), const(wvt_bd)],
        out_specs=blk(A_WIDTH),
        out_shape=jax.ShapeDtypeStruct((b, s, A_WIDTH), BF16),
        scratch_shapes=[pltpu.VMEM((s, tq), I32),
                        pltpu.VMEM((s, tq), I16),
                        pltpu.VMEM((s, tq), I16),
                        pltpu.VMEM((LANES, IDX_HEADS * tq), BF16),
                        pltpu.VMEM((2 * KV_LORA, A_HEADS * tq), BF16),
                        pltpu.VMEM((2, SUBLANES, tq), I32),
                        pltpu.VMEM((1, A_HEADS * tq), F32),
                        pltpu.VMEM((KV_LORA + V_PAD, A_HEADS * tq), F32)],
        compiler_params=_cparams(2),
        name="dsa",
    )(r3(qa), r3(qidx), r3(widx), r3(kk), r3(ckv), ckvt, wk_bd, wvt_bd)
    return out.reshape(b * s, A_WIDTH)


def _block_diag_uk(w_uk):
    h, c, d = w_uk.shape
    eye = jnp.eye(h, dtype=w_uk.dtype)
    return jnp.einsum('hcd,hg->hcgd', w_uk, eye).reshape(h * c, h * d).astype(BF16)


def _block_diag_uv_t(w_uv):
    h, c, d = w_uv.shape
    eye = jnp.eye(h, dtype=w_uv.dtype)
    return jnp.einsum('hcd,hg->hdgc', w_uv, eye).reshape(h * d, h * c).astype(BF16)


def _diff_kernel(qb_ref, kb_ref, vbt_ref, lam_ref, subln_ref, out_ref,
                 qm_ref, m_ref, acc_ref, *, tq, tk, lambda_init):
    i = pl.program_id(1)
    n_off = i * (tq // tk)
    dvp = B_V_DIM + V_PAD
    _split_halves(qb_ref, B_HEADS, qm_ref, tq)
    t_row = i * tq + lax.broadcasted_iota(I32, (LANES, tq), 1)
    for h in range(B_HEADS):
        feat = _slope_features(t_row, 2.0 ** (-2 * (h + 1)), (LANES, tq)).astype(BF16)
        qm_ref[LANES:, 2 * h * tq:(2 * h + 1) * tq] = feat
        qm_ref[LANES:, (2 * h + 1) * tq:(2 * h + 2) * tq] = feat

    row = lax.broadcasted_iota(I32, (tk, tq), 0)
    tqi = i * tq + lax.broadcasted_iota(I32, (tk, tq), 1)
    qchunk = lax.shift_right_logical(tqi, CHUNK_SHIFT)
    _flash_init(m_ref, acc_ref)

    def tile_step(kt, diag):
        off = pl.multiple_of(kt * tk, tk)
        s_pairs = [jnp.dot(kb_ref[pl.ds(off, tk), 2 * h * LANES:(2 * h + 2) * LANES],
                           qm_ref[:, 2 * h * tq:(2 * h + 2) * tq],
                           preferred_element_type=F32) for h in range(B_HEADS)]
        if diag:
            pos = off + row
            adm = lax.shift_right_logical(pos, CHUNK_SHIFT) <= qchunk
            late = jnp.maximum(pos - tqi, 0).astype(F32)
        alphas, probs = [], []
        for h in range(B_HEADS):
            for j in range(2):
                sl = slice((2 * h + j) * tq, (2 * h + j + 1) * tq)
                sc = s_pairs[h][:, j * tq:(j + 1) * tq]
                if diag:
                    sc = jnp.where(adm, sc - (2.0 * 2.0 ** (-2 * (h + 1))) * late, NEG_BIG)
                alpha, p = _flash_probs(sl, sc, m_ref)
                alphas.append(alpha)
                probs.append(p)
        pv = jnp.concatenate(
            [jnp.dot(vbt_ref[h * dvp:(h + 1) * dvp, pl.ds(off, tk)],
                     jnp.concatenate(probs[2 * h:2 * h + 2], axis=1), preferred_element_type=F32)
             for h in range(B_HEADS)], axis=1)
        acc_ref[...] = jnp.concatenate(alphas, axis=1) * acc_ref[...] + pv

    def off_body(kt, carry):
        tile_step(kt, False)
        return carry

    def diag_body(kt, carry):
        tile_step(kt, True)
        return carry

    lax.fori_loop(0, n_off, off_body, 0)
    lax.fori_loop(n_off, n_off + tq // tk, diag_body, 0)

    lp = lam_ref[...]
    lam = (jnp.exp(jnp.sum(lp[0:1] * lp[1:2], axis=1, keepdims=True))
           - jnp.exp(jnp.sum(lp[2:3] * lp[3:4], axis=1, keepdims=True)) + lambda_init)
    o_all = acc_ref[:B_V_DIM, :] * (1.0 / acc_ref[B_V_DIM:B_V_DIM + 1, :])
    outs = []
    for h in range(B_HEADS):
        o = (o_all[:, 2 * h * tq:(2 * h + 1) * tq]
             - lam * o_all[:, (2 * h + 1) * tq:(2 * h + 2) * tq])
        o = o * lax.rsqrt(jnp.mean(o * o, axis=0, keepdims=True) + RMS_EPS) * subln_ref[...]
        outs.append(o * (1.0 - lambda_init))
    out_ref[...] = jnp.concatenate(outs, axis=0).T.astype(BF16)


def _diff_call(qb, kb, vbt, lam_rows, subln, b, s, tq, tk, lambda_init):
    blk = pl.BlockSpec((None, tq, B_WIDTH), lambda bi, i: (bi, i, 0))
    kw = 2 * B_HEADS * LANES
    dvp = B_V_DIM + V_PAD
    out = pl.pallas_call(
        functools.partial(_diff_kernel, tq=tq, tk=tk, lambda_init=lambda_init),
        grid=(b, s // tq),
        in_specs=[blk, pl.BlockSpec((None, s, kw), lambda bi, i: (bi, 0, 0)),
                  pl.BlockSpec((B_HEADS * dvp, s), lambda bi, i: (0, bi)),
                  pl.BlockSpec((SUBLANES, LANES), lambda bi, i: (0, 0)),
                  pl.BlockSpec((B_V_DIM, 1), lambda bi, i: (0, 0))],
        out_specs=blk,
        out_shape=jax.ShapeDtypeStruct((b, s, B_WIDTH), BF16),
        scratch_shapes=[pltpu.VMEM((2 * LANES, 2 * B_HEADS * tq), BF16),
                        pltpu.VMEM((1, 2 * B_HEADS * tq), F32),
                        pltpu.VMEM((dvp, 2 * B_HEADS * tq), F32)],
        compiler_params=_cparams(2),
        name="diff",
    )(qb.reshape(b, s, B_WIDTH), kb.reshape(b, s, kw), vbt, lam_rows, subln.reshape(B_V_DIM, 1))
    return out.reshape(b * s, B_WIDTH)


def _mix_out_kernel(h_ref, a_ref, b_ref, wa_ref, wb_ref, g_ref, beta_ref, o_ref):
    y = (jnp.dot(a_ref[...], wa_ref[...], preferred_element_type=F32)
         + jnp.dot(b_ref[...], wb_ref[...], preferred_element_type=F32))
    o_ref[...] = _layer_norm(ALPHA * h_ref[...] + y, g_ref[...], beta_ref[...])


def _mix_out_call(h, out_a, out_b, w_o, g, beta, tm):
    n, d = h.shape
    w = w_o.astype(BF16)
    row = lambda wd: pl.BlockSpec((tm, wd), lambda i: (i, 0))
    const = lambda shp: pl.BlockSpec(shp, lambda i: (0, 0))
    return pl.pallas_call(
        _mix_out_kernel,
        grid=(n // tm,),
        in_specs=[row(d), row(A_WIDTH), row(B_WIDTH), const((A_WIDTH, d)), const((B_WIDTH, d)),
                  const((1, d)), const((1, d))],
        out_specs=row(d),
        out_shape=jax.ShapeDtypeStruct((n, d), F32),
        compiler_params=_cparams(1),
        name="mix_out",
    )(h, out_a, out_b, w[:A_WIDTH], w[A_WIDTH:], g.reshape(1, d), beta.reshape(1, d))


def _mem_kv_kernel(m_ref, w_ref, k_ref, v_ref):
    x = m_ref[...].astype(BF16)
    d = k_ref.shape[-1]
    k_ref[...] = jnp.dot(x, w_ref[:, :d], preferred_element_type=F32).astype(BF16)
    v_ref[...] = jnp.dot(x, w_ref[:, d:], preferred_element_type=F32).astype(BF16)


def _mem_kv_call(mem2d, wkv, tm):
    n, d = mem2d.shape
    return pl.pallas_call(
        _mem_kv_kernel,
        grid=(n // tm,),
        in_specs=[pl.BlockSpec((tm, d), lambda i: (i, 0)),
                  pl.BlockSpec((d, 2 * d), lambda i: (0, 0))],
        out_specs=[pl.BlockSpec((tm, d), lambda i: (i, 0))] * 2,
        out_shape=[jax.ShapeDtypeStruct((n, d), BF16)] * 2,
        compiler_params=_cparams(1),
        name="mem_kv",
    )(mem2d, wkv.astype(BF16))


def _mem_attn_kernel(h_ref, wq_ref, k_ref, v_ref, wo_ref, g_ref, beta_ref, o_ref):
    h = h_ref[...]
    q = jnp.dot(h.astype(BF16), wq_ref[...], preferred_element_type=F32)
    q = (q * (MEM_HEAD_DIM ** -0.5)).astype(BF16)
    outs = []
    for hd in range(MEM_HEADS):
        sl = slice(hd * MEM_HEAD_DIM, (hd + 1) * MEM_HEAD_DIM)
        sc = lax.dot_general(q[:, sl], k_ref[:, sl], NT_DIMS, preferred_element_type=F32)
        p = jnp.exp(sc - jnp.max(sc, axis=1, keepdims=True))
        den = jnp.sum(p, axis=1, keepdims=True)
        o = jnp.dot(p.astype(BF16), v_ref[:, sl], preferred_element_type=F32) / den
        outs.append(o.astype(BF16))
    y = jnp.dot(jnp.concatenate(outs, axis=1), wo_ref[...], preferred_element_type=F32)
    o_ref[...] = _layer_norm(ALPHA * h + y, g_ref[...], beta_ref[...])


def _mem_attn_call(h, wq, k, v, wo, g, beta, b, s, tm):
    n, d = h.shape
    m = k.shape[0] // b
    const = lambda shp: pl.BlockSpec(shp, lambda bi, i: (0,) * len(shp))
    row = pl.BlockSpec((None, tm, d), lambda bi, i: (bi, i, 0))
    kv = pl.BlockSpec((None, m, d), lambda bi, i: (bi, 0, 0))
    out = pl.pallas_call(
        _mem_attn_kernel,
        grid=(b, s // tm),
        in_specs=[row, const((d, d)), kv, kv, const((d, d)), const((1, d)), const((1, d))],
        out_specs=row,
        out_shape=jax.ShapeDtypeStruct((b, s, d), F32),
        compiler_params=_cparams(2),
        name="mem_attn",
    )(h.reshape(b, s, d), wq.astype(BF16), k.reshape(b, m, d), v.reshape(b, m, d),
      wo.astype(BF16), g.reshape(1, d), beta.reshape(1, d))
    return out.reshape(n, d)


def _router_kernel(h_ref, rw_ref, bias_ref, tri_ref, e_ref, g_ref, rank_ref, cnt_ref, run_ref):
    tm = h_ref.shape[0]

    @pl.when(pl.program_id(0) == 0)
    def _():
        run_ref[...] = jnp.zeros(run_ref.shape, F32)

    logits = jnp.dot(h_ref[...].astype(BF16), rw_ref[...], preferred_element_type=F32)
    scores = 1.0 / (1.0 + jnp.exp(-logits))
    lane = lax.broadcasted_iota(I32, (tm, LANES), 1)
    lanef = lane.astype(F32)
    ninf = jnp.float32(-jnp.inf)
    cur = jnp.where(lane < N_EXPERTS, scores + bias_ref[...], ninf)
    top_e = jnp.zeros((tm, LANES), F32)
    top_s = jnp.zeros((tm, LANES), F32)
    picked = jnp.zeros((tm, LANES), F32)
    for k in range(TOP_K):
        mx = jnp.max(cur, axis=1, keepdims=True)
        first = jnp.min(jnp.where(cur == mx, lanef, float(LANES)), axis=1, keepdims=True)
        hit = lanef == first
        s_k = jnp.sum(jnp.where(hit, scores, 0.0), axis=1, keepdims=True)
        top_e = jnp.where(lane == k, first, top_e)
        top_s = jnp.where(lane == k, s_k, top_s)
        picked = jnp.where(hit, 1.0, picked)
        cur = jnp.where(hit, ninf, cur)
    e_ref[...] = top_e.astype(I32)
    g_ref[...] = top_s / jnp.sum(top_s, axis=1, keepdims=True) * ROUTED_SCALE

    slot = jnp.dot(tri_ref[...], picked.astype(BF16), preferred_element_type=F32) + run_ref[...]
    rank_ref[...] = slot.astype(I32)
    run = run_ref[...] + jnp.sum(picked, axis=0, keepdims=True)
    run_ref[...] = run
    cnt_ref[...] = run.astype(I32)


def _router_call(h, router_w, router_bias, tm):
    n, d = h.shape
    rw = jnp.pad(router_w, ((0, 0), (0, LANES - N_EXPERTS))).astype(BF16)
    bias = jnp.pad(router_bias, (0, LANES - N_EXPERTS)).reshape(1, LANES)
    tri = jnp.tri(tm, k=-1, dtype=BF16)
    row = pl.BlockSpec((tm, LANES), lambda i: (i, 0))
    const = lambda shp: pl.BlockSpec(shp, lambda i: (0, 0))
    return pl.pallas_call(
        _router_kernel,
        grid=(n // tm,),
        in_specs=[pl.BlockSpec((tm, d), lambda i: (i, 0)), const((d, LANES)), const((1, LANES)),
                  const((tm, tm))],
        out_specs=[row, row, row, const((1, LANES))],
        out_shape=[jax.ShapeDtypeStruct((n, LANES), I32),
                   jax.ShapeDtypeStruct((n, LANES), F32),
                   jax.ShapeDtypeStruct((n, LANES), I32),
                   jax.ShapeDtypeStruct((1, LANES), I32)],
        scratch_shapes=[pltpu.VMEM((1, LANES), F32)],
        compiler_params=_cparams(1),
        name="router",
    )(h, rw, bias, tri)


def _dest_kernel(e_ref, rank_ref, gs_ref, o_ref):
    tm = e_ref.shape[0]
    lane = lax.broadcasted_iota(I32, (tm, LANES), 1)
    e = e_ref[...]
    row_of = (gs_ref[...] + rank_ref[...]).astype(F32)
    dest = jnp.zeros((tm, LANES), F32)
    for k in range(TOP_K):
        d_k = jnp.sum(jnp.where(lane == e[:, k:k + 1], row_of, 0.0), axis=1, keepdims=True)
        dest = jnp.where(lane == k, d_k, dest)
    o_ref[...] = dest.astype(I32)


def _dest_call(top_e, rank, gstart, tm):
    n = top_e.shape[0]
    row = pl.BlockSpec((tm, LANES), lambda i: (i, 0))
    return pl.pallas_call(
        _dest_kernel,
        grid=(n // tm,),
        in_specs=[row, row, pl.BlockSpec((1, LANES), lambda i: (0, 0))],
        out_specs=row,
        out_shape=jax.ShapeDtypeStruct((n, LANES), I32),
        compiler_params=_cparams(1),
        name="dest",
    )(top_e, rank, gstart)


def _dispatch_kernel(dest_ref, x_ref, xs_ref, xp_ref, sem, *, td):
    half = x_ref.shape[1] // 2
    xp_ref[...] = pltpu.pack_elementwise([x_ref[:, :half], x_ref[:, half:]], packed_dtype=BF16)

    def row_copy(r, k):
        return pltpu.make_async_copy(xp_ref.at[pl.ds(r, 1), :],
                                     xs_ref.at[pl.ds(dest_ref[r, k], 1), :], sem)

    for r in range(td):
        for k in range(TOP_K):
            row_copy(r, k).start(priority=k % N_DMA_QUEUES)

    def drain(r, carry):
        for k in range(TOP_K):
            row_copy(r, k).wait()
        return carry

    lax.fori_loop(0, td, drain, 0)


def _dispatch_call(h, dest, td):
    n, d = h.shape
    m = n * TOP_K
    return pl.pallas_call(
        functools.partial(_dispatch_kernel, td=td),
        grid=(n // td,),
        in_specs=[pl.BlockSpec((td, LANES), lambda i: (i, 0), memory_space=pltpu.SMEM),
                  pl.BlockSpec((td, d), lambda i: (i, 0))],
        out_specs=pl.BlockSpec(memory_space=pl.ANY),
        out_shape=jax.ShapeDtypeStruct((m, d // 2), I32),
        scratch_shapes=[pltpu.VMEM((td, d // 2), I32), pltpu.SemaphoreType.DMA(())],
        compiler_params=_cparams(1),
        name="dispatch",
    )(dest, h)


def _moe_mm_kernel(ptile_ref, pexp_ref, plo_ref, phi_ref, xs_ref, wg_ref, wu_ref, wd_ref, y_ref,
                   *, tmm):
    p = pl.program_id(0)
    tile = ptile_ref[p]
    lo = plo_ref[p]
    hi = phi_ref[p]
    first = jnp.logical_or(p == 0, ptile_ref[jnp.maximum(p - 1, 0)] != tile)

    def expert_out():
        xp = xs_ref[...]
        x = jnp.concatenate(
            [pltpu.unpack_elementwise(xp, index=j, packed_dtype=BF16, unpacked_dtype=F32)
             for j in range(2)], axis=1).astype(BF16)
        g = jnp.dot(x, wg_ref[...].astype(BF16), preferred_element_type=F32)
        u = jnp.dot(x, wu_ref[...].astype(BF16), preferred_element_type=F32)
        hid = (_silu(g) * u).astype(BF16)
        return jnp.dot(hid, wd_ref[...].astype(BF16), preferred_element_type=F32)

    whole = jnp.logical_and(lo <= tile * tmm, hi >= (tile + 1) * tmm)

    @pl.when(whole)
    def _():
        y_ref[...] = expert_out()

    @pl.when(jnp.logical_and(first, jnp.logical_not(whole)))
    def _():
        y_ref[...] = jnp.zeros(y_ref.shape, F32)

    @pl.when(jnp.logical_and(lo < hi, jnp.logical_not(whole)))
    def _():
        y = expert_out()
        rowg = tile * tmm + lax.broadcasted_iota(I32, (tmm, 1), 0)
        y_ref[...] += jnp.where(jnp.logical_and(rowg >= lo, rowg < hi), y, 0.0)


def _moe_mm_call(xs, pairs, wg, wu, wd, layer, tmm):
    m = xs.shape[0]
    n_pairs = pairs[0].shape[0]
    rows = lambda w: pl.BlockSpec((tmm, w), lambda p, pt, pe, plo, phi: (pt[p], 0))
    wspec = lambda shp: pl.BlockSpec((None, None) + shp,
                                     lambda p, pt, pe, plo, phi: (layer, pe[p], 0, 0))
    return pl.pallas_call(
        functools.partial(_moe_mm_kernel, tmm=tmm),
        grid_spec=pltpu.PrefetchScalarGridSpec(
            num_scalar_prefetch=4,
            grid=(n_pairs,),
            in_specs=[rows(xs.shape[1]), wspec((D_MODEL, EXPERT_DIM)), wspec((D_MODEL, EXPERT_DIM)),
                      wspec((EXPERT_DIM, D_MODEL))],
            out_specs=rows(D_MODEL)),
        out_shape=jax.ShapeDtypeStruct((m, D_MODEL), F32),
        compiler_params=_cparams(1),
        name="moe_mm",
    )(*pairs, xs, wg, wu, wd)


def _group_pairs(counts, m, tmm):
    n_tiles = m // tmm
    n_pairs = n_tiles + N_EXPERTS
    gend = jnp.cumsum(counts).astype(I32)
    gstart = gend - counts.astype(I32)
    cuts = jnp.concatenate([jnp.arange(n_tiles, dtype=I32) * tmm, gstart])
    idx = jnp.arange(n_pairs, dtype=I32)
    before = jnp.logical_or(cuts[None, :] < cuts[:, None],
                            jnp.logical_and(cuts[None, :] == cuts[:, None], idx[None, :] < idx[:, None]))
    rank = jnp.sum(before, axis=1).astype(I32)
    lo = jnp.sum(jnp.where(rank[None, :] == idx[:, None], cuts[None, :], 0), axis=1).astype(I32)
    hi = jnp.concatenate([lo[1:], jnp.full((1,), m, I32)])
    tile = jnp.minimum(lo // tmm, n_tiles - 1)
    e = jnp.minimum(jnp.sum(gend[None, :] <= lo[:, None], axis=1), N_EXPERTS - 1).astype(I32)
    return tile, e, lo, hi


def _combine_kernel(dest_ref, gate_ref, h_ref, y_ref, sg_ref, su_ref, sd_ref, g_ref, beta_ref, o_ref,
                    buf_ref, sem, *, tc):
    def row_copy(r, k):
        return pltpu.make_async_copy(y_ref.at[pl.ds(dest_ref[r, k], 1), :],
                                     buf_ref.at[k, pl.ds(r, 1), :], sem)

    for r in range(tc):
        for k in range(TOP_K):
            row_copy(r, k).start(priority=k % N_DMA_QUEUES)

    h = h_ref[...]
    x = h.astype(BF16)
    hid = (_silu(jnp.dot(x, sg_ref[...], preferred_element_type=F32))
           * jnp.dot(x, su_ref[...], preferred_element_type=F32)).astype(BF16)
    shared = jnp.dot(hid, sd_ref[...], preferred_element_type=F32)

    def drain(r, carry):
        for k in range(TOP_K):
            row_copy(r, k).wait()
        return carry

    lax.fori_loop(0, tc, drain, 0)

    gates = gate_ref[...]
    routed = gates[:, 0:1] * buf_ref[0]
    for k in range(1, TOP_K):
        routed = routed + gates[:, k:k + 1] * buf_ref[k]
    o_ref[...] = _layer_norm(ALPHA * h + (routed + shared), g_ref[...], beta_ref[...])


def _combine_call(h, y, dest, gates, sg, su, sd, g, beta, tc):
    n, d = h.shape
    smem = pl.BlockSpec((tc, LANES), lambda i: (i, 0), memory_space=pltpu.SMEM)
    const = lambda shp: pl.BlockSpec(shp, lambda i: (0, 0))
    return pl.pallas_call(
        functools.partial(_combine_kernel, tc=tc),
        grid=(n // tc,),
        in_specs=[smem, pl.BlockSpec((tc, LANES), lambda i: (i, 0)),
                  pl.BlockSpec((tc, d), lambda i: (i, 0)),
                  pl.BlockSpec(memory_space=pl.ANY),
                  const((d, SHARED_DIM)), const((d, SHARED_DIM)), const((SHARED_DIM, d)),
                  const((1, d)), const((1, d))],
        out_specs=pl.BlockSpec((tc, d), lambda i: (i, 0)),
        out_shape=jax.ShapeDtypeStruct((n, d), F32),
        scratch_shapes=[pltpu.VMEM((TOP_K, tc, d), F32),
                        pltpu.SemaphoreType.DMA(())],
        compiler_params=_cparams(1),
        name="combine",
    )(dest, gates, h, y, sg.astype(BF16), su.astype(BF16), sd.astype(BF16),
      g.reshape(1, d), beta.reshape(1, d))


def _tiles(b, s):
    n = b * s
    pick = lambda pref, total: next(t for t in (pref, 512, 256, 128, 64, 32, 16, 8) if t <= pref and total % t == 0)
    return dict(
        rows=pick(512, n),
        attn_q=pick(512, s),
        attn_k=pick(512, s),
        select=pick(256, s),
        mem_rows=pick(512, s),
        mem_kv=pick(512, b * 256),
        router=pick(1024, n),
        dispatch=pick(128, n),
        moe=pick(512, n * TOP_K),
        combine=pick(128, n),
    )


def kernel(x, mem, ln_in_g, ln_in_b, w_in, a_kv_norm, a_w_uk, a_w_uv, b_lq1, b_lk1, b_lq2, b_lk2,
           b_subln, w_o, ln1_g, ln1_b, m_wq, m_wkv, m_wo, ln2_g, ln2_b, router_w, router_bias,
           e_w_gate, e_w_up, e_w_down, s_w_gate, s_w_up, s_w_down, ln3_g, ln3_b):
    b, s, d = x.shape
    n = b * s
    m = n * TOP_K
    t = _tiles(b, s)
    mem2d = mem.reshape(-1, d)

    h = _ln_call(x.reshape(n, d), ln_in_g, ln_in_b, t["rows"])
    for l in range(DEPTH):
        lambda_init = 0.8 - 0.6 * math.exp(-0.3 * l)

        qa, ckv, ckvt, qidx, kk, widx, qb, kb, vbt = _proj_call(h, _pad_w_in(w_in[l]), a_kv_norm[l],
                                                                  t["rows"], s)
        out_a = _dsa_call(qa, qidx, widx, kk, ckv, ckvt, _block_diag_uk(a_w_uk[l]),
                          _block_diag_uv_t(a_w_uv[l]), b, s, t["attn_q"], t["attn_k"], t["select"])
        lam_rows = jnp.pad(jnp.stack([b_lq1[l], b_lk1[l], b_lq2[l], b_lk2[l]]),
                           ((0, SUBLANES - 4), (0, LANES - B_QK_DIM)))
        out_b = _diff_call(qb, kb, vbt, lam_rows, b_subln[l], b, s, t["attn_q"], t["attn_k"], lambda_init)
        h = _mix_out_call(h, out_a, out_b, w_o[l], ln1_g[l], ln1_b[l], t["rows"])

        mk, mv = _mem_kv_call(mem2d, m_wkv[l], t["mem_kv"])
        h = _mem_attn_call(h, m_wq[l], mk, mv, m_wo[l], ln2_g[l], ln2_b[l], b, s, t["mem_rows"])

        top_e, gates, rank, counts = _router_call(h, router_w[l], router_bias[l], t["router"])
        gstart = jnp.cumsum(counts, axis=1) - counts
        dest = _dest_call(top_e, rank, gstart, t["router"])
        counts = counts[0, :N_EXPERTS]
        xs = _dispatch_call(h, dest, t["dispatch"])
        pairs = _group_pairs(counts, m, t["moe"])
        y = _moe_mm_call(xs, pairs, e_w_gate, e_w_up, e_w_down, l, t["moe"])
        h = _combine_call(h, y, dest, gates, s_w_gate[l], s_w_up[l], s_w_down[l],
                          ln3_g[l], ln3_b[l], t["combine"])
    return h.reshape(b, s, d)
```

```python
import functools
import math

import jax
import jax.numpy as jnp
import numpy as np
from jax import lax
from jax.experimental import pallas as pl
from jax.experimental.pallas import tpu as pltpu

F32 = jnp.float32
BF16 = jnp.bfloat16
I32 = jnp.int32
I16 = jnp.int16

D_MODEL = 1024
DEPTH = 2
CHUNK = 64
CHUNK_SHIFT = 6
A_HEADS = 8
A_HEAD_DIM = 64
A_WIDTH = A_HEADS * A_HEAD_DIM
KV_LORA = 128
IDX_HEADS = 8
IDX_DIM = 64
INDEX_TOPK_MAX = 256
B_HEADS = 4
B_QK_DIM = 64
B_V_DIM = 128
B_WIDTH = B_HEADS * B_V_DIM
MEM_HEADS = 4
MEM_HEAD_DIM = D_MODEL // MEM_HEADS
N_EXPERTS = 64
TOP_K = 8
EXPERT_DIM = 256
SHARED_DIM = 256
ROUTED_SCALE = 2.5
ALPHA = (2 * DEPTH) ** 0.25
LN_EPS = 1e-5
RMS_EPS = 1e-6

LANES = 128
SUBLANES = 8
VMEM_LIMIT = 56 * 1024 * 1024
INT_MIN = -2 ** 31
I16_MIN = -2 ** 15
V_PAD = 16
N_DMA_QUEUES = 2
NEG_BIG = -3.0e38
M_INIT = -1.0e30

NT_DIMS = (((1,), (1,)), ((), ()))


def _cparams(n_axes):
    return pltpu.CompilerParams(dimension_semantics=("arbitrary",) * n_axes,
                                vmem_limit_bytes=VMEM_LIMIT)


def _layer_norm(x, g, b):
    mu = jnp.mean(x, axis=-1, keepdims=True)
    xc = x - mu
    var = jnp.mean(xc * xc, axis=-1, keepdims=True)
    return xc * lax.rsqrt(var + LN_EPS) * g + b


def _silu(x):
    return x * (1.0 / (1.0 + jnp.exp(-x)))


def _ln_kernel(x_ref, g_ref, b_ref, o_ref):
    o_ref[...] = _layer_norm(x_ref[...], g_ref[...], b_ref[...])


def _ln_call(x, g, b, tm):
    n, d = x.shape
    return pl.pallas_call(
        _ln_kernel,
        grid=(n // tm,),
        in_specs=[pl.BlockSpec((tm, d), lambda i: (i, 0)),
                  pl.BlockSpec((1, d), lambda i: (0, 0)),
                  pl.BlockSpec((1, d), lambda i: (0, 0))],
        out_specs=pl.BlockSpec((tm, d), lambda i: (i, 0)),
        out_shape=jax.ShapeDtypeStruct((n, d), F32),
        compiler_params=_cparams(1),
        name="ln_in",
    )(x, g.reshape(1, d), b.reshape(1, d))


_SEG_QA = (0, 512)
_SEG_CKV = (512, 640)
_SEG_QIDX = (640, 1152)
_SEG_KK = (1152, 1280)
_SEG_WIDX = (1280, 1408)
_SEG_QB = (1408, 1920)
_SEG_KB = (1920, 2432)
_SEG_VB = (2432, 2944)
_PROJ_COLS = 2944


def _pos_features(pos, shape):
    lane = lax.broadcasted_iota(I32, shape, 1)
    hi = lax.shift_right_logical(pos, CHUNK_SHIFT).astype(F32)
    lo = (pos & (CHUNK - 1)).astype(F32)
    return jnp.where(lane < 2, 1.0, jnp.where(lane == 2, hi, jnp.where(lane == 3, lo, 0.0)))


def _slope_features(tqi, slope, shape):
    feat = lax.broadcasted_iota(I32, shape, 0)
    hi = lax.shift_right_logical(tqi, CHUNK_SHIFT).astype(F32)
    lo = (tqi & (CHUNK - 1)).astype(F32)
    return jnp.where(feat == 0, -slope * CHUNK * hi,
                     jnp.where(feat == 1, -slope * lo,
                               jnp.where(feat == 2, slope * CHUNK,
                                         jnp.where(feat == 3, slope, 0.0))))


def _with_ones_rows(vt):
    t = vt.shape[1]
    sub = lax.broadcasted_iota(I32, (V_PAD, t), 0)
    return jnp.concatenate([vt, jnp.where(sub == 0, 1.0, 0.0)], axis=0)


def _proj_kernel(h_ref, w_ref, kvn_ref, qa_ref, ckv_ref, ckvt_ref, qidx_ref, kk_ref, widx_ref,
                 qb_ref, kb_ref, vbt_ref, *, seq):
    tm = h_ref.shape[0]
    x = h_ref[...].astype(BF16)

    def seg(s):
        return jnp.dot(x, w_ref[:, s[0]:s[1]], preferred_element_type=F32)

    row = pl.program_id(0) * tm + lax.broadcasted_iota(I32, (tm, LANES), 0)
    feat = _pos_features(row & (seq - 1), (tm, LANES)).astype(BF16)

    qa_ref[...] = (seg(_SEG_QA) * (A_HEAD_DIM ** -0.5)).astype(BF16)
    c = seg(_SEG_CKV)
    c = c * lax.rsqrt(jnp.mean(c * c, axis=-1, keepdims=True) + RMS_EPS) * kvn_ref[...]
    ckv_ref[:, :KV_LORA] = c.astype(BF16)
    ckv_ref[:, KV_LORA:] = feat
    ckvt_ref[...] = _with_ones_rows(c.T).astype(BF16)
    qidx_ref[...] = (seg(_SEG_QIDX) * (IDX_DIM ** -0.5)).astype(BF16)
    kk_ref[...] = seg(_SEG_KK).astype(BF16)
    widx_ref[...] = seg(_SEG_WIDX) * (IDX_HEADS ** -0.5)
    qb_ref[...] = (seg(_SEG_QB) * (B_QK_DIM ** -0.5)).astype(BF16)
    kb = seg(_SEG_KB).astype(BF16)
    vt = seg(_SEG_VB).T
    for h in range(B_HEADS):
        kb_ref[:, 2 * h * LANES:(2 * h + 1) * LANES] = kb[:, h * LANES:(h + 1) * LANES]
        kb_ref[:, (2 * h + 1) * LANES:(2 * h + 2) * LANES] = feat
        vbt_ref[h * (B_V_DIM + V_PAD):(h + 1) * (B_V_DIM + V_PAD), :] = _with_ones_rows(
            vt[h * B_V_DIM:(h + 1) * B_V_DIM]).astype(BF16)


def _proj_call(h, w_pad, kvn, tm, seq):
    n, d = h.shape
    assert seq & (seq - 1) == 0 and seq <= CHUNK * 256 and seq % tm == 0
    outs = [(512, BF16, False), (2 * KV_LORA, BF16, False), (KV_LORA + V_PAD, BF16, True),
            (512, BF16, False), (128, BF16, False), (128, F32, False), (512, BF16, False),
            (2 * B_HEADS * LANES, BF16, False), (B_HEADS * (B_V_DIM + V_PAD), BF16, True)]
    spec = lambda w, t: (pl.BlockSpec((w, tm), lambda i: (0, i)) if t
                         else pl.BlockSpec((tm, w), lambda i: (i, 0)))
    shape = lambda w, dt, t: jax.ShapeDtypeStruct((w, n) if t else (n, w), dt)
    return pl.pallas_call(
        functools.partial(_proj_kernel, seq=seq),
        grid=(n // tm,),
        in_specs=[pl.BlockSpec((tm, d), lambda i: (i, 0)),
                  pl.BlockSpec((d, _PROJ_COLS), lambda i: (0, 0)),
                  pl.BlockSpec((1, KV_LORA), lambda i: (0, 0))],
        out_specs=[spec(w, t) for w, _, t in outs],
        out_shape=[shape(w, dt, t) for w, dt, t in outs],
        compiler_params=_cparams(1),
        name="proj_in",
    )(h, w_pad, kvn.reshape(1, KV_LORA))


def _pad_w_in(w_in):
    sizes = (A_WIDTH, KV_LORA, IDX_HEADS * IDX_DIM, IDX_DIM, IDX_HEADS,
             2 * B_HEADS * B_QK_DIM, 2 * B_HEADS * B_QK_DIM, B_WIDTH)
    splits = np.cumsum(sizes)[:-1].tolist()
    q_a, c_kv, q_idx, k_idx, w_idx, q_b, k_b, v_b = jnp.split(w_in, splits, axis=-1)
    w_idx = jnp.pad(w_idx, ((0, 0), (0, LANES - IDX_HEADS)))
    return jnp.concatenate([q_a, c_kv, q_idx, k_idx, k_idx, w_idx, q_b, k_b, v_b],
                           axis=-1).astype(BF16)


def _flash_probs(sl, sc, m_ref):
    m_old = m_ref[:, sl]
    m_new = jnp.maximum(m_old, jnp.max(sc, axis=0, keepdims=True))
    m_ref[:, sl] = m_new
    return jnp.exp(m_old - m_new), jnp.exp(sc - m_new).astype(BF16)


def _flash_init(m_ref, acc_ref):
    m_ref[...] = jnp.full(m_ref.shape, M_INIT, F32)
    acc_ref[...] = jnp.zeros(acc_ref.shape, F32)


def _split_halves(q_ref, n_pairs, out_ref, tq):
    q_t = q_ref[...].astype(F32).T
    feat = lax.broadcasted_iota(I32, (LANES, tq), 0)
    for j in range(n_pairs):
        qp = q_t[j * LANES:(j + 1) * LANES, :]
        out_ref[:LANES, 2 * j * tq:(2 * j + 1) * tq] = jnp.where(feat < 64, qp, 0.0).astype(BF16)
        out_ref[:LANES, (2 * j + 1) * tq:(2 * j + 2) * tq] = jnp.where(feat >= 64, qp, 0.0).astype(BF16)


def _dsa_kernel(qa_ref, qidx_ref, widx_ref, kk_ref, ckv_ref, ckvt_ref, wk_ref, wvt_ref, out_ref,
                keys_ref, khi_ref, klo_ref, qim_ref, qlat_ref, tj_ref, m_ref, acc_ref,
                *, tq, tk, ts, seq, topk):
    i = pl.program_id(1)
    n_off = i * (tq // tk)
    n_kt = n_off + tq // tk
    idx_bits = int(math.log2(seq))

    _split_halves(qidx_ref, IDX_HEADS // 2, qim_ref, tq)
    qlat_t = lax.dot_general(wk_ref[...], qa_ref[...], NT_DIMS,
                             preferred_element_type=F32)
    t_row = i * tq + lax.broadcasted_iota(I32, (LANES, tq), 1)
    for h in range(A_HEADS):
        qlat_ref[:KV_LORA, h * tq:(h + 1) * tq] = qlat_t[h * KV_LORA:(h + 1) * KV_LORA, :].astype(BF16)
        qlat_ref[KV_LORA:, h * tq:(h + 1) * tq] = _slope_features(
            t_row, 2.0 ** -(h + 1), (LANES, tq)).astype(BF16)
    w_t = widx_ref[...].T

    row = lax.broadcasted_iota(I32, (tk, tq), 0)
    tqi = i * tq + lax.broadcasted_iota(I32, (tk, tq), 1)
    qchunk = lax.shift_right_logical(tqi, CHUNK_SHIFT)

    def idx_tile(kt, overlaps_block):
        off = pl.multiple_of(kt * tk, tk)
        kkt = kk_ref[pl.ds(off, tk), :]
        s_all = jnp.dot(kkt, qim_ref[...], preferred_element_type=F32)
        acc = jnp.zeros((tk, tq), F32)
        for h in range(IDX_HEADS):
            acc = acc + w_t[h:h + 1, :] * jnp.maximum(s_all[:, h * tq:(h + 1) * tq], 0.0)
        bits = lax.bitcast_convert_type(acc + 0.0, I32)
        key = jnp.where(bits < 0, bits ^ jnp.int32(0x7FFFFFFF), bits)
        if overlaps_block:
            adm = lax.shift_right_logical(off + row, CHUNK_SHIFT) <= qchunk
            key = jnp.where(adm, key, jnp.int32(INT_MIN))
        keys_ref[pl.ds(off, tk), :] = key
        khi_ref[pl.ds(off, tk), :] = lax.shift_right_arithmetic(key, 16).astype(I16)
        klo_ref[pl.ds(off, tk), :] = ((key & 0xFFFF) + I16_MIN).astype(I16)

    def idx_off_body(kt, carry):
        idx_tile(kt, False)
        return carry

    def idx_diag_body(kt, carry):
        idx_tile(kt, True)
        return carry

    lax.fori_loop(0, n_off, idx_off_body, 0)
    lax.fori_loop(n_off, n_kt, idx_diag_body, 0)

    n_st = n_kt * (tk // ts)
    row_s = lax.broadcasted_iota(I32, (ts, tq), 0)

    def count(pred):
        def body(st, c):
            off = pl.multiple_of(st * ts, ts)
            ind = jnp.where(pred(keys_ref[pl.ds(off, ts), :], off + row_s), 1.0, 0.0)
            return c + jnp.sum(ind, axis=0, keepdims=True)
        return lax.fori_loop(0, n_st, body, jnp.zeros((1, tq), F32))

    def count16(ref, pred):
        pack = 2 * SUBLANES
        def body(st, c):
            off = pl.multiple_of(st * ts, ts)
            ind = jnp.where(pred(ref[pl.ds(off, ts), :]), jnp.int16(1), jnp.int16(0))
            parts = [ind[j * pack:(j + 1) * pack] for j in range(ts // pack)]
            while len(parts) > 1:
                parts = [parts[j] + parts[j + 1] for j in range(0, len(parts), 2)]
            return c + parts[0]
        c = lax.fori_loop(0, n_st, body, jnp.zeros((pack, tq), I16))
        return jnp.sum(c.astype(F32), axis=0, keepdims=True)

    def search16(ref, need):
        def body(it, t):
            cand = t + lax.shift_left(jnp.int32(1), 15 - it)
            cand16 = cand.astype(I16)
            cnt = count16(ref, lambda tile: tile >= cand16)
            return jnp.where(cnt >= need, cand, t)
        return lax.fori_loop(0, 16, body, jnp.full((1, tq), I16_MIN, I32))

    t_hi = search16(khi_ref, topk)
    t_hi16 = t_hi.astype(I16)
    n_above = count16(khi_ref, lambda tile: tile > t_hi16)

    def park_body(st, carry):
        off = pl.multiple_of(st * ts, ts)
        klo_ref[pl.ds(off, ts), :] = jnp.where(khi_ref[pl.ds(off, ts), :] == t_hi16,
                                               klo_ref[pl.ds(off, ts), :], jnp.int16(I16_MIN))
        return carry

    lax.fori_loop(0, n_st, park_body, 0)
    t_lo = search16(klo_ref, topk - n_above)
    thr = t_hi * 65536 + (t_lo - I16_MIN)

    live = thr > INT_MIN
    t_lo16 = t_lo.astype(I16)
    n_ge = n_above + count16(klo_ref, lambda tile: tile >= t_lo16)
    excess = jnp.logical_and(n_ge > topk, live)
    all_ties = jnp.where(live, seq, -1).astype(I32)
    tj_ref[0] = jnp.broadcast_to(thr, (SUBLANES, tq))
    tj_ref[1] = jnp.broadcast_to(all_ties, (SUBLANES, tq))

    @pl.when(jnp.max(jnp.where(excess, 1.0, 0.0)) > 0.0)
    def _():
        need = topk - count(lambda tile, _: tile > thr)

        def tie_body(it, hi):
            cand = hi + lax.shift_left(jnp.int32(1), idx_bits - 1 - it)
            cnt = count(lambda tile, pos: jnp.logical_and(tile == thr, pos < cand))
            return jnp.where(cnt < need, cand, hi)

        hi = lax.fori_loop(0, idx_bits, tie_body, jnp.zeros((1, tq), I32))
        tj_ref[1] = jnp.broadcast_to(jnp.where(excess, hi, all_ties), (SUBLANES, tq))

    thr_b = tj_ref[0][:1]
    tie_hi = tj_ref[1][:1]

    _flash_init(m_ref, acc_ref)

    def att_tile(kt, late_fix):
        off = pl.multiple_of(kt * tk, tk)
        tile = keys_ref[pl.ds(off, tk), :]
        pos = off + row
        sel = jnp.logical_or(tile > thr_b,
                             jnp.logical_and(tile == thr_b, pos <= tie_hi))
        s_all = jnp.dot(ckv_ref[pl.ds(off, tk), :], qlat_ref[...],
                        preferred_element_type=F32)
        if late_fix:
            late = jnp.maximum(pos - tqi, 0).astype(F32)
        alphas, probs = [], []
        for h in range(A_HEADS):
            sl = slice(h * tq, (h + 1) * tq)
            sc = s_all[:, sl]
            if late_fix:
                sc = sc - (2.0 * 2.0 ** -(h + 1)) * late
            alpha, p = _flash_probs(sl, jnp.where(sel, sc, NEG_BIG), m_ref)
            alphas.append(alpha)
            probs.append(p)
        pv = jnp.dot(ckvt_ref[:, pl.ds(off, tk)], jnp.concatenate(probs, axis=1),
                     preferred_element_type=F32)
        acc_ref[...] = jnp.concatenate(alphas, axis=1) * acc_ref[...] + pv

    def off_body(kt, carry):
        att_tile(kt, False)
        return carry

    def diag_body(kt, carry):
        att_tile(kt, True)
        return carry

    lax.fori_loop(0, n_off, off_body, 0)
    lax.fori_loop(n_off, n_kt, diag_body, 0)

    inv_l = 1.0 / acc_ref[KV_LORA:KV_LORA + 1, :]
    olat = jnp.concatenate([(acc_ref[:KV_LORA, h * tq:(h + 1) * tq]
                             * inv_l[:, h * tq:(h + 1) * tq]).astype(BF16)
                            for h in range(A_HEADS)], axis=0)
    out_t = jnp.dot(wvt_ref[...], olat, preferred_element_type=F32)
    out_ref[...] = out_t.T.astype(BF16)


def _dsa_call(qa, qidx, widx, kk, ckv, ckvt, wk_bd, wvt_bd, b, s, tq, tk, ts):
    topk = min(INDEX_TOPK_MAX, s // 4)
    assert s % tq == 0 and tq % tk == 0 and tk % ts == 0 and ts % LANES == 0 and s & (s - 1) == 0
    blk = lambda w: pl.BlockSpec((None, tq, w), lambda bi, i: (bi, i, 0))
    full = lambda w: pl.BlockSpec((None, s, w), lambda bi, i: (bi, 0, 0))
    const = lambda a: pl.BlockSpec(a.shape, lambda bi, i: (0, 0))
    r3 = lambda a: a.reshape(b, s, a.shape[-1])
    out = pl.pallas_call(
        functools.partial(_dsa_kernel, tq=tq, tk=tk, ts=ts, seq=s, topk=topk),
        grid=(b, s // tq),
        in_specs=[blk(A_WIDTH), blk(IDX_HEADS * IDX_DIM), blk(LANES), full(LANES),
                  full(2 * KV_LORA),
                  pl.BlockSpec((KV_LORA + V_PAD, s), lambda bi, i: (0, bi)),
                  const(wk_bd), const(wvt_bd)],
        out_specs=blk(A_WIDTH),
        out_shape=jax.ShapeDtypeStruct((b, s, A_WIDTH), BF16),
        scratch_shapes=[pltpu.VMEM((s, tq), I32),
                        pltpu.VMEM((s, tq), I16),
                        pltpu.VMEM((s, tq), I16),
                        pltpu.VMEM((LANES, IDX_HEADS * tq), BF16),
                        pltpu.VMEM((2 * KV_LORA, A_HEADS * tq), BF16),
                        pltpu.VMEM((2, SUBLANES, tq), I32),
                        pltpu.VMEM((1, A_HEADS * tq), F32),
                        pltpu.VMEM((KV_LORA + V_PAD, A_HEADS * tq), F32)],
        compiler_params=_cparams(2),
        name="dsa",
    )(r3(qa), r3(qidx), r3(widx), r3(kk), r3(ckv), ckvt, wk_bd, wvt_bd)
    return out.reshape(b * s, A_WIDTH)


def _block_diag_uk(w_uk):
    h, c, d = w_uk.shape
    eye = jnp.eye(h, dtype=w_uk.dtype)
    return jnp.einsum('hcd,hg->hcgd', w_uk, eye).reshape(h * c, h * d).astype(BF16)


def _block_diag_uv_t(w_uv):
    h, c, d = w_uv.shape
    eye = jnp.eye(h, dtype=w_uv.dtype)
    return jnp.einsum('hcd,hg->hdgc', w_uv, eye).reshape(h * d, h * c).astype(BF16)


def _diff_kernel(qb_ref, kb_ref, vbt_ref, lam_ref, subln_ref, out_ref,
                 qm_ref, m_ref, acc_ref, *, tq, tk, lambda_init):
    i = pl.program_id(1)
    n_off = i * (tq // tk)
    dvp = B_V_DIM + V_PAD
    _split_halves(qb_ref, B_HEADS, qm_ref, tq)
    t_row = i * tq + lax.broadcasted_iota(I32, (LANES, tq), 1)
    for h in range(B_HEADS):
        feat = _slope_features(t_row, 2.0 ** (-2 * (h + 1)), (LANES, tq)).astype(BF16)
        qm_ref[LANES:, 2 * h * tq:(2 * h + 1) * tq] = feat
        qm_ref[LANES:, (2 * h + 1) * tq:(2 * h + 2) * tq] = feat

    row = lax.broadcasted_iota(I32, (tk, tq), 0)
    tqi = i * tq + lax.broadcasted_iota(I32, (tk, tq), 1)
    qchunk = lax.shift_right_logical(tqi, CHUNK_SHIFT)
    _flash_init(m_ref, acc_ref)

    def tile_step(kt, diag):
        off = pl.multiple_of(kt * tk, tk)
        s_pairs = [jnp.dot(kb_ref[pl.ds(off, tk), 2 * h * LANES:(2 * h + 2) * LANES],
                           qm_ref[:, 2 * h * tq:(2 * h + 2) * tq],
                           preferred_element_type=F32) for h in range(B_HEADS)]
        if diag:
            pos = off + row
            adm = lax.shift_right_logical(pos, CHUNK_SHIFT) <= qchunk
            late = jnp.maximum(pos - tqi, 0).astype(F32)
        alphas, probs = [], []
        for h in range(B_HEADS):
            for j in range(2):
                sl = slice((2 * h + j) * tq, (2 * h + j + 1) * tq)
                sc = s_pairs[h][:, j * tq:(j + 1) * tq]
                if diag:
                    sc = jnp.where(adm, sc - (2.0 * 2.0 ** (-2 * (h + 1))) * late, NEG_BIG)
                alpha, p = _flash_probs(sl, sc, m_ref)
                alphas.append(alpha)
                probs.append(p)
        pv = jnp.concatenate(
            [jnp.dot(vbt_ref[h * dvp:(h + 1) * dvp, pl.ds(off, tk)],
                     jnp.concatenate(probs[2 * h:2 * h + 2], axis=1), preferred_element_type=F32)
             for h in range(B_HEADS)], axis=1)
        acc_ref[...] = jnp.concatenate(alphas, axis=1) * acc_ref[...] + pv

    def off_body(kt, carry):
        tile_step(kt, False)
        return carry

    def diag_body(kt, carry):
        tile_step(kt, True)
        return carry

    lax.fori_loop(0, n_off, off_body, 0)
    lax.fori_loop(n_off, n_off + tq // tk, diag_body, 0)

    lp = lam_ref[...]
    lam = (jnp.exp(jnp.sum(lp[0:1] * lp[1:2], axis=1, keepdims=True))
           - jnp.exp(jnp.sum(lp[2:3] * lp[3:4], axis=1, keepdims=True)) + lambda_init)
    o_all = acc_ref[:B_V_DIM, :] * (1.0 / acc_ref[B_V_DIM:B_V_DIM + 1, :])
    outs = []
    for h in range(B_HEADS):
        o = (o_all[:, 2 * h * tq:(2 * h + 1) * tq]
             - lam * o_all[:, (2 * h + 1) * tq:(2 * h + 2) * tq])
        o = o * lax.rsqrt(jnp.mean(o * o, axis=0, keepdims=True) + RMS_EPS) * subln_ref[...]
        outs.append(o * (1.0 - lambda_init))
    out_ref[...] = jnp.concatenate(outs, axis=0).T.astype(BF16)


def _diff_call(qb, kb, vbt, lam_rows, subln, b, s, tq, tk, lambda_init):
    blk = pl.BlockSpec((None, tq, B_WIDTH), lambda bi, i: (bi, i, 0))
    kw = 2 * B_HEADS * LANES
    dvp = B_V_DIM + V_PAD
    out = pl.pallas_call(
        functools.partial(_diff_kernel, tq=tq, tk=tk, lambda_init=lambda_init),
        grid=(b, s // tq),
        in_specs=[blk, pl.BlockSpec((None, s, kw), lambda bi, i: (bi, 0, 0)),
                  pl.BlockSpec((B_HEADS * dvp, s), lambda bi, i: (0, bi)),
                  pl.BlockSpec((SUBLANES, LANES), lambda bi, i: (0, 0)),
                  pl.BlockSpec((B_V_DIM, 1), lambda bi, i: (0, 0))],
        out_specs=blk,
        out_shape=jax.ShapeDtypeStruct((b, s, B_WIDTH), BF16),
        scratch_shapes=[pltpu.VMEM((2 * LANES, 2 * B_HEADS * tq), BF16),
                        pltpu.VMEM((1, 2 * B_HEADS * tq), F32),
                        pltpu.VMEM((dvp, 2 * B_HEADS * tq), F32)],
        compiler_params=_cparams(2),
        name="diff",
    )(qb.reshape(b, s, B_WIDTH), kb.reshape(b, s, kw), vbt, lam_rows, subln.reshape(B_V_DIM, 1))
    return out.reshape(b * s, B_WIDTH)


def _mix_out_kernel(h_ref, a_ref, b_ref, wa_ref, wb_ref, g_ref, beta_ref, o_ref):
    y = (jnp.dot(a_ref[...], wa_ref[...], preferred_element_type=F32)
         + jnp.dot(b_ref[...], wb_ref[...], preferred_element_type=F32))
    o_ref[...] = _layer_norm(ALPHA * h_ref[...] + y, g_ref[...], beta_ref[...])


def _mix_out_call(h, out_a, out_b, w_o, g, beta, tm):
    n, d = h.shape
    w = w_o.astype(BF16)
    row = lambda wd: pl.BlockSpec((tm, wd), lambda i: (i, 0))
    const = lambda shp: pl.BlockSpec(shp, lambda i: (0, 0))
    return pl.pallas_call(
        _mix_out_kernel,
        grid=(n // tm,),
        in_specs=[row(d), row(A_WIDTH), row(B_WIDTH), const((A_WIDTH, d)), const((B_WIDTH, d)),
                  const((1, d)), const((1, d))],
        out_specs=row(d),
        out_shape=jax.ShapeDtypeStruct((n, d), F32),
        compiler_params=_cparams(1),
        name="mix_out",
    )(h, out_a, out_b, w[:A_WIDTH], w[A_WIDTH:], g.reshape(1, d), beta.reshape(1, d))


def _mem_kv_kernel(m_ref, w_ref, k_ref, v_ref):
    x = m_ref[...].astype(BF16)
    d = k_ref.shape[-1]
    k_ref[...] = jnp.dot(x, w_ref[:, :d], preferred_element_type=F32).astype(BF16)
    v_ref[...] = jnp.dot(x, w_ref[:, d:], preferred_element_type=F32).astype(BF16)


def _mem_kv_call(mem2d, wkv, tm):
    n, d = mem2d.shape
    return pl.pallas_call(
        _mem_kv_kernel,
        grid=(n // tm,),
        in_specs=[pl.BlockSpec((tm, d), lambda i: (i, 0)),
                  pl.BlockSpec((d, 2 * d), lambda i: (0, 0))],
        out_specs=[pl.BlockSpec((tm, d), lambda i: (i, 0))] * 2,
        out_shape=[jax.ShapeDtypeStruct((n, d), BF16)] * 2,
        compiler_params=_cparams(1),
        name="mem_kv",
    )(mem2d, wkv.astype(BF16))


def _mem_attn_kernel(h_ref, wq_ref, k_ref, v_ref, wo_ref, g_ref, beta_ref, o_ref):
    h = h_ref[...]
    q = jnp.dot(h.astype(BF16), wq_ref[...], preferred_element_type=F32)
    q = (q * (MEM_HEAD_DIM ** -0.5)).astype(BF16)
    outs = []
    for hd in range(MEM_HEADS):
        sl = slice(hd * MEM_HEAD_DIM, (hd + 1) * MEM_HEAD_DIM)
        sc = lax.dot_general(q[:, sl], k_ref[:, sl], NT_DIMS, preferred_element_type=F32)
        p = jnp.exp(sc - jnp.max(sc, axis=1, keepdims=True))
        den = jnp.sum(p, axis=1, keepdims=True)
        o = jnp.dot(p.astype(BF16), v_ref[:, sl], preferred_element_type=F32) / den
        outs.append(o.astype(BF16))
    y = jnp.dot(jnp.concatenate(outs, axis=1), wo_ref[...], preferred_element_type=F32)
    o_ref[...] = _layer_norm(ALPHA * h + y, g_ref[...], beta_ref[...])


def _mem_attn_call(h, wq, k, v, wo, g, beta, b, s, tm):
    n, d = h.shape
    m = k.shape[0] // b
    const = lambda shp: pl.BlockSpec(shp, lambda bi, i: (0,) * len(shp))
    row = pl.BlockSpec((None, tm, d), lambda bi, i: (bi, i, 0))
    kv = pl.BlockSpec((None, m, d), lambda bi, i: (bi, 0, 0))
    out = pl.pallas_call(
        _mem_attn_kernel,
        grid=(b, s // tm),
        in_specs=[row, const((d, d)), kv, kv, const((d, d)), const((1, d)), const((1, d))],
        out_specs=row,
        out_shape=jax.ShapeDtypeStruct((b, s, d), F32),
        compiler_params=_cparams(2),
        name="mem_attn",
    )(h.reshape(b, s, d), wq.astype(BF16), k.reshape(b, m, d), v.reshape(b, m, d),
      wo.astype(BF16), g.reshape(1, d), beta.reshape(1, d))
    return out.reshape(n, d)


def _router_kernel(h_ref, rw_ref, bias_ref, tri_ref, e_ref, g_ref, rank_ref, cnt_ref, run_ref):
    tm = h_ref.shape[0]

    @pl.when(pl.program_id(0) == 0)
    def _():
        run_ref[...] = jnp.zeros(run_ref.shape, F32)

    logits = jnp.dot(h_ref[...].astype(BF16), rw_ref[...], preferred_element_type=F32)
    scores = 1.0 / (1.0 + jnp.exp(-logits))
    lane = lax.broadcasted_iota(I32, (tm, LANES), 1)
    lanef = lane.astype(F32)
    ninf = jnp.float32(-jnp.inf)
    cur = jnp.where(lane < N_EXPERTS, scores + bias_ref[...], ninf)
    top_e = jnp.zeros((tm, LANES), F32)
    top_s = jnp.zeros((tm, LANES), F32)
    picked = jnp.zeros((tm, LANES), F32)
    for k in range(TOP_K):
        mx = jnp.max(cur, axis=1, keepdims=True)
        first = jnp.min(jnp.where(cur == mx, lanef, float(LANES)), axis=1, keepdims=True)
        hit = lanef == first
        s_k = jnp.sum(jnp.where(hit, scores, 0.0), axis=1, keepdims=True)
        top_e = jnp.where(lane == k, first, top_e)
        top_s = jnp.where(lane == k, s_k, top_s)
        picked = jnp.where(hit, 1.0, picked)
        cur = jnp.where(hit, ninf, cur)
    e_ref[...] = top_e.astype(I32)
    g_ref[...] = top_s / jnp.sum(top_s, axis=1, keepdims=True) * ROUTED_SCALE

    slot = jnp.dot(tri_ref[...], picked.astype(BF16), preferred_element_type=F32) + run_ref[...]
    rank_ref[...] = slot.astype(I32)
    run = run_ref[...] + jnp.sum(picked, axis=0, keepdims=True)
    run_ref[...] = run
    cnt_ref[...] = run.astype(I32)


def _router_call(h, router_w, router_bias, tm):
    n, d = h.shape
    rw = jnp.pad(router_w, ((0, 0), (0, LANES - N_EXPERTS))).astype(BF16)
    bias = jnp.pad(router_bias, (0, LANES - N_EXPERTS)).reshape(1, LANES)
    tri = jnp.tri(tm, k=-1, dtype=BF16)
    row = pl.BlockSpec((tm, LANES), lambda i: (i, 0))
    const = lambda shp: pl.BlockSpec(shp, lambda i: (0, 0))
    return pl.pallas_call(
        _router_kernel,
        grid=(n // tm,),
        in_specs=[pl.BlockSpec((tm, d), lambda i: (i, 0)), const((d, LANES)), const((1, LANES)),
                  const((tm, tm))],
        out_specs=[row, row, row, const((1, LANES))],
        out_shape=[jax.ShapeDtypeStruct((n, LANES), I32),
                   jax.ShapeDtypeStruct((n, LANES), F32),
                   jax.ShapeDtypeStruct((n, LANES), I32),
                   jax.ShapeDtypeStruct((1, LANES), I32)],
        scratch_shapes=[pltpu.VMEM((1, LANES), F32)],
        compiler_params=_cparams(1),
        name="router",
    )(h, rw, bias, tri)


def _dest_kernel(e_ref, rank_ref, gs_ref, o_ref):
    tm = e_ref.shape[0]
    lane = lax.broadcasted_iota(I32, (tm, LANES), 1)
    e = e_ref[...]
    row_of = (gs_ref[...] + rank_ref[...]).astype(F32)
    dest = jnp.zeros((tm, LANES), F32)
    for k in range(TOP_K):
        d_k = jnp.sum(jnp.where(lane == e[:, k:k + 1], row_of, 0.0), axis=1, keepdims=True)
        dest = jnp.where(lane == k, d_k, dest)
    o_ref[...] = dest.astype(I32)


def _dest_call(top_e, rank, gstart, tm):
    n = top_e.shape[0]
    row = pl.BlockSpec((tm, LANES), lambda i: (i, 0))
    return pl.pallas_call(
        _dest_kernel,
        grid=(n // tm,),
        in_specs=[row, row, pl.BlockSpec((1, LANES), lambda i: (0, 0))],
        out_specs=row,
        out_shape=jax.ShapeDtypeStruct((n, LANES), I32),
        compiler_params=_cparams(1),
        name="dest",
    )(top_e, rank, gstart)


def _dispatch_kernel(dest_ref, x_ref, xs_ref, xp_ref, sem, *, td):
    half = x_ref.shape[1] // 2
    xp_ref[...] = pltpu.pack_elementwise([x_ref[:, :half], x_ref[:, half:]], packed_dtype=BF16)

    def row_copy(r, k):
        return pltpu.make_async_copy(xp_ref.at[pl.ds(r, 1), :],
                                     xs_ref.at[pl.ds(dest_ref[r, k], 1), :], sem)

    for r in range(td):
        for k in range(TOP_K):
            row_copy(r, k).start(priority=k % N_DMA_QUEUES)

    def drain(r, carry):
        for k in range(TOP_K):
            row_copy(r, k).wait()
        return carry

    lax.fori_loop(0, td, drain, 0)


def _dispatch_call(h, dest, td):
    n, d = h.shape
    m = n * TOP_K
    return pl.pallas_call(
        functools.partial(_dispatch_kernel, td=td),
        grid=(n // td,),
        in_specs=[pl.BlockSpec((td, LANES), lambda i: (i, 0), memory_space=pltpu.SMEM),
                  pl.BlockSpec((td, d), lambda i: (i, 0))],
        out_specs=pl.BlockSpec(memory_space=pl.ANY),
        out_shape=jax.ShapeDtypeStruct((m, d // 2), I32),
        scratch_shapes=[pltpu.VMEM((td, d // 2), I32), pltpu.SemaphoreType.DMA(())],
        compiler_params=_cparams(1),
        name="dispatch",
    )(dest, h)


def _moe_mm_kernel(ptile_ref, pexp_ref, plo_ref, phi_ref, xs_ref, wg_ref, wu_ref, wd_ref, y_ref,
                   *, tmm):
    p = pl.program_id(0)
    tile = ptile_ref[p]
    lo = plo_ref[p]
    hi = phi_ref[p]
    first = jnp.logical_or(p == 0, ptile_ref[jnp.maximum(p - 1, 0)] != tile)

    def expert_out():
        xp = xs_ref[...]
        x = jnp.concatenate(
            [pltpu.unpack_elementwise(xp, index=j, packed_dtype=BF16, unpacked_dtype=F32)
             for j in range(2)], axis=1).astype(BF16)
        g = jnp.dot(x, wg_ref[...].astype(BF16), preferred_element_type=F32)
        u = jnp.dot(x, wu_ref[...].astype(BF16), preferred_element_type=F32)
        hid = (_silu(g) * u).astype(BF16)
        return jnp.dot(hid, wd_ref[...].astype(BF16), preferred_element_type=F32)

    whole = jnp.logical_and(lo <= tile * tmm, hi >= (tile + 1) * tmm)

    @pl.when(whole)
    def _():
        y_ref[...] = expert_out()

    @pl.when(jnp.logical_and(first, jnp.logical_not(whole)))
    def _():
        y_ref[...] = jnp.zeros(y_ref.shape, F32)

    @pl.when(jnp.logical_and(lo < hi, jnp.logical_not(whole)))
    def _():
        y = expert_out()
        rowg = tile * tmm + lax.broadcasted_iota(I32, (tmm, 1), 0)
        y_ref[...] += jnp.where(jnp.logical_and(rowg >= lo, rowg < hi), y, 0.0)


def _moe_mm_call(xs, pairs, wg, wu, wd, layer, tmm):
    m = xs.shape[0]
    n_pairs = pairs[0].shape[0]
    rows = lambda w: pl.BlockSpec((tmm, w), lambda p, pt, pe, plo, phi: (pt[p], 0))
    wspec = lambda shp: pl.BlockSpec((None, None) + shp,
                                     lambda p, pt, pe, plo, phi: (layer, pe[p], 0, 0))
    return pl.pallas_call(
        functools.partial(_moe_mm_kernel, tmm=tmm),
        grid_spec=pltpu.PrefetchScalarGridSpec(
            num_scalar_prefetch=4,
            grid=(n_pairs,),
            in_specs=[rows(xs.shape[1]), wspec((D_MODEL, EXPERT_DIM)), wspec((D_MODEL, EXPERT_DIM)),
                      wspec((EXPERT_DIM, D_MODEL))],
            out_specs=rows(D_MODEL)),
        out_shape=jax.ShapeDtypeStruct((m, D_MODEL), F32),
        compiler_params=_cparams(1),
        name="moe_mm",
    )(*pairs, xs, wg, wu, wd)


def _group_pairs(counts, m, tmm):
    n_tiles = m // tmm
    n_pairs = n_tiles + N_EXPERTS
    gend = jnp.cumsum(counts).astype(I32)
    gstart = gend - counts.astype(I32)
    cuts = jnp.concatenate([jnp.arange(n_tiles, dtype=I32) * tmm, gstart])
    idx = jnp.arange(n_pairs, dtype=I32)
    before = jnp.logical_or(cuts[None, :] < cuts[:, None],
                            jnp.logical_and(cuts[None, :] == cuts[:, None], idx[None, :] < idx[:, None]))
    rank = jnp.sum(before, axis=1).astype(I32)
    lo = jnp.sum(jnp.where(rank[None, :] == idx[:, None], cuts[None, :], 0), axis=1).astype(I32)
    hi = jnp.concatenate([lo[1:], jnp.full((1,), m, I32)])
    tile = jnp.minimum(lo // tmm, n_tiles - 1)
    e = jnp.minimum(jnp.sum(gend[None, :] <= lo[:, None], axis=1), N_EXPERTS - 1).astype(I32)
    return tile, e, lo, hi


def _combine_kernel(dest_ref, gate_ref, h_ref, y_ref, sg_ref, su_ref, sd_ref, g_ref, beta_ref, o_ref,
                    buf_ref, sem, *, tc):
    def row_copy(r, k):
        return pltpu.make_async_copy(y_ref.at[pl.ds(dest_ref[r, k], 1), :],
                                     buf_ref.at[k, pl.ds(r, 1), :], sem)

    for r in range(tc):
        for k in range(TOP_K):
            row_copy(r, k).start(priority=k % N_DMA_QUEUES)

    h = h_ref[...]
    x = h.astype(BF16)
    hid = (_silu(jnp.dot(x, sg_ref[...], preferred_element_type=F32))
           * jnp.dot(x, su_ref[...], preferred_element_type=F32)).astype(BF16)
    shared = jnp.dot(hid, sd_ref[...], preferred_element_type=F32)

    def drain(r, carry):
        for k in range(TOP_K):
            row_copy(r, k).wait()
        return carry

    lax.fori_loop(0, tc, drain, 0)

    gates = gate_ref[...]
    routed = gates[:, 0:1] * buf_ref[0]
    for k in range(1, TOP_K):
        routed = routed + gates[:, k:k + 1] * buf_ref[k]
    o_ref[...] = _layer_norm(ALPHA * h + (routed + shared), g_ref[...], beta_ref[...])


def _combine_call(h, y, dest, gates, sg, su, sd, g, beta, tc):
    n, d = h.shape
    smem = pl.BlockSpec((tc, LANES), lambda i: (i, 0), memory_space=pltpu.SMEM)
    const = lambda shp: pl.BlockSpec(shp, lambda i: (0, 0))
    return pl.pallas_call(
        functools.partial(_combine_kernel, tc=tc),
        grid=(n // tc,),
        in_specs=[smem, pl.BlockSpec((tc, LANES), lambda i: (i, 0)),
                  pl.BlockSpec((tc, d), lambda i: (i, 0)),
                  pl.BlockSpec(memory_space=pl.ANY),
                  const((d, SHARED_DIM)), const((d, SHARED_DIM)), const((SHARED_DIM, d)),
                  const((1, d)), const((1, d))],
        out_specs=pl.BlockSpec((tc, d), lambda i: (i, 0)),
        out_shape=jax.ShapeDtypeStruct((n, d), F32),
        scratch_shapes=[pltpu.VMEM((TOP_K, tc, d), F32),
                        pltpu.SemaphoreType.DMA(())],
        compiler_params=_cparams(1),
        name="combine",
    )(dest, gates, h, y, sg.astype(BF16), su.astype(BF16), sd.astype(BF16),
      g.reshape(1, d), beta.reshape(1, d))


def _tiles(b, s):
    n = b * s
    pick = lambda pref, total: next(t for t in (pref, 512, 256, 128, 64, 32, 16, 8) if t <= pref and total % t == 0)
    return dict(
        rows=pick(1024, n),
        attn_q=pick(512, s),
        attn_k=pick(512, s),
        select=pick(256, s),
        mem_rows=pick(1024, s),
        mem_kv=pick(512, b * 256),
        router=pick(1024, n),
        dispatch=pick(128, n),
        moe=pick(512, n * TOP_K),
        combine=pick(128, n),
    )


def kernel(x, mem, ln_in_g, ln_in_b, w_in, a_kv_norm, a_w_uk, a_w_uv, b_lq1, b_lk1, b_lq2, b_lk2,
           b_subln, w_o, ln1_g, ln1_b, m_wq, m_wkv, m_wo, ln2_g, ln2_b, router_w, router_bias,
           e_w_gate, e_w_up, e_w_down, s_w_gate, s_w_up, s_w_down, ln3_g, ln3_b):
    b, s, d = x.shape
    n = b * s
    m = n * TOP_K
    t = _tiles(b, s)
    mem2d = mem.reshape(-1, d)

    h = _ln_call(x.reshape(n, d), ln_in_g, ln_in_b, t["rows"])
    for l in range(DEPTH):
        lambda_init = 0.8 - 0.6 * math.exp(-0.3 * l)

        qa, ckv, ckvt, qidx, kk, widx, qb, kb, vbt = _proj_call(h, _pad_w_in(w_in[l]), a_kv_norm[l],
                                                                  t["rows"], s)
        out_a = _dsa_call(qa, qidx, widx, kk, ckv, ckvt, _block_diag_uk(a_w_uk[l]),
                          _block_diag_uv_t(a_w_uv[l]), b, s, t["attn_q"], t["attn_k"], t["select"])
        lam_rows = jnp.pad(jnp.stack([b_lq1[l], b_lk1[l], b_lq2[l], b_lk2[l]]),
                           ((0, SUBLANES - 4), (0, LANES - B_QK_DIM)))
        out_b = _diff_call(qb, kb, vbt, lam_rows, b_subln[l], b, s, t["attn_q"], t["attn_k"], lambda_init)
        h = _mix_out_call(h, out_a, out_b, w_o[l], ln1_g[l], ln1_b[l], t["rows"])

        mk, mv = _mem_kv_call(mem2d, m_wkv[l], t["mem_kv"])
        h = _mem_attn_call(h, m_wq[l], mk, mv, m_wo[l], ln2_g[l], ln2_b[l], b, s, t["mem_rows"])

        top_e, gates, rank, counts = _router_call(h, router_w[l], router_bias[l], t["router"])
        gstart = jnp.cumsum(counts, axis=1) - counts
        dest = _dest_call(top_e, rank, gstart, t["router"])
        counts = counts[0, :N_EXPERTS]
        xs = _dispatch_call(h, dest, t["dispatch"])
        pairs = _group_pairs(counts, m, t["moe"])
        y = _moe_mm_call(xs, pairs, e_w_gate, e_w_up, e_w_down, l, t["moe"])
        h = _combine_call(h, y, dest, gates, s_w_gate[l], s_w_up[l], s_w_down[l],
                          ln3_g[l], ln3_b[l], t["combine"])
    return h.reshape(b, s, d)
```

```python
import functools
import math

import jax
import jax.numpy as jnp
import numpy as np
from jax import lax
from jax.experimental import pallas as pl
from jax.experimental.pallas import tpu as pltpu

F32 = jnp.float32
BF16 = jnp.bfloat16
I32 = jnp.int32
I16 = jnp.int16

D_MODEL = 1024
DEPTH = 2
CHUNK = 64
CHUNK_SHIFT = 6
A_HEADS = 8
A_HEAD_DIM = 64
A_WIDTH = A_HEADS * A_HEAD_DIM
KV_LORA = 128
IDX_HEADS = 8
IDX_DIM = 64
INDEX_TOPK_MAX = 256
B_HEADS = 4
B_QK_DIM = 64
B_V_DIM = 128
B_WIDTH = B_HEADS * B_V_DIM
MEM_HEADS = 4
MEM_HEAD_DIM = D_MODEL // MEM_HEADS
N_EXPERTS = 64
TOP_K = 8
EXPERT_DIM = 256
SHARED_DIM = 256
ROUTED_SCALE = 2.5
ALPHA = (2 * DEPTH) ** 0.25
LN_EPS = 1e-5
RMS_EPS = 1e-6

LANES = 128
SUBLANES = 8
VMEM_LIMIT = 56 * 1024 * 1024
INT_MIN = -2 ** 31
I16_MIN = -2 ** 15
V_PAD = 16
N_DMA_QUEUES = 2
NEG_BIG = -3.0e38
M_INIT = -1.0e30

NT_DIMS = (((1,), (1,)), ((), ()))


def _cparams(n_axes):
    return pltpu.CompilerParams(dimension_semantics=("arbitrary",) * n_axes,
                                vmem_limit_bytes=VMEM_LIMIT)


def _layer_norm(x, g, b):
    mu = jnp.mean(x, axis=-1, keepdims=True)
    xc = x - mu
    var = jnp.mean(xc * xc, axis=-1, keepdims=True)
    return xc * lax.rsqrt(var + LN_EPS) * g + b


def _silu(x):
    return x * (1.0 / (1.0 + jnp.exp(-x)))


def _ln_kernel(x_ref, g_ref, b_ref, o_ref):
    o_ref[...] = _layer_norm(x_ref[...], g_ref[...], b_ref[...])


def _ln_call(x, g, b, tm):
    n, d = x.shape
    return pl.pallas_call(
        _ln_kernel,
        grid=(n // tm,),
        in_specs=[pl.BlockSpec((tm, d), lambda i: (i, 0)),
                  pl.BlockSpec((1, d), lambda i: (0, 0)),
                  pl.BlockSpec((1, d), lambda i: (0, 0))],
        out_specs=pl.BlockSpec((tm, d), lambda i: (i, 0)),
        out_shape=jax.ShapeDtypeStruct((n, d), F32),
        compiler_params=_cparams(1),
        name="ln_in",
    )(x, g.reshape(1, d), b.reshape(1, d))


_SEG_QA = (0, 512)
_SEG_CKV = (512, 640)
_SEG_QIDX = (640, 1152)
_SEG_KK = (1152, 1280)
_SEG_WIDX = (1280, 1408)
_SEG_QB = (1408, 1920)
_SEG_KB = (1920, 2432)
_SEG_VB = (2432, 2944)
_PROJ_COLS = 2944


def _pos_features(pos, shape):
    lane = lax.broadcasted_iota(I32, shape, 1)
    hi = lax.shift_right_logical(pos, CHUNK_SHIFT).astype(F32)
    lo = (pos & (CHUNK - 1)).astype(F32)
    return jnp.where(lane < 2, 1.0, jnp.where(lane == 2, hi, jnp.where(lane == 3, lo, 0.0)))


def _slope_features(tqi, slope, shape):
    feat = lax.broadcasted_iota(I32, shape, 0)
    hi = lax.shift_right_logical(tqi, CHUNK_SHIFT).astype(F32)
    lo = (tqi & (CHUNK - 1)).astype(F32)
    return jnp.where(feat == 0, -slope * CHUNK * hi,
                     jnp.where(feat == 1, -slope * lo,
                               jnp.where(feat == 2, slope * CHUNK,
                                         jnp.where(feat == 3, slope, 0.0))))


def _with_ones_rows(vt):
    t = vt.shape[1]
    sub = lax.broadcasted_iota(I32, (V_PAD, t), 0)
    return jnp.concatenate([vt, jnp.where(sub == 0, 1.0, 0.0)], axis=0)


def _proj_kernel(h_ref, w_ref, kvn_ref, qa_ref, ckv_ref, ckvt_ref, qidx_ref, kk_ref, widx_ref,
                 qb_ref, kb_ref, vbt_ref, *, seq):
    tm = h_ref.shape[0]
    x = h_ref[...].astype(BF16)

    def seg(s):
        return jnp.dot(x, w_ref[:, s[0]:s[1]], preferred_element_type=F32)

    row = pl.program_id(0) * tm + lax.broadcasted_iota(I32, (tm, LANES), 0)
    feat = _pos_features(row & (seq - 1), (tm, LANES)).astype(BF16)

    qa_ref[...] = (seg(_SEG_QA) * (A_HEAD_DIM ** -0.5)).astype(BF16)
    c = seg(_SEG_CKV)
    c = c * lax.rsqrt(jnp.mean(c * c, axis=-1, keepdims=True) + RMS_EPS) * kvn_ref[...]
    ckv_ref[:, :KV_LORA] = c.astype(BF16)
    ckv_ref[:, KV_LORA:] = feat
    ckvt_ref[...] = _with_ones_rows(c.T).astype(BF16)
    qidx_ref[...] = (seg(_SEG_QIDX) * (IDX_DIM ** -0.5)).astype(BF16)
    kk_ref[...] = seg(_SEG_KK).astype(BF16)
    widx_ref[...] = seg(_SEG_WIDX) * (IDX_HEADS ** -0.5)
    qb_ref[...] = (seg(_SEG_QB) * (B_QK_DIM ** -0.5)).astype(BF16)
    kb = seg(_SEG_KB).astype(BF16)
    vt = seg(_SEG_VB).T
    for h in range(B_HEADS):
        kb_ref[:, 2 * h * LANES:(2 * h + 1) * LANES] = kb[:, h * LANES:(h + 1) * LANES]
        kb_ref[:, (2 * h + 1) * LANES:(2 * h + 2) * LANES] = feat
        vbt_ref[h * (B_V_DIM + V_PAD):(h + 1) * (B_V_DIM + V_PAD), :] = _with_ones_rows(
            vt[h * B_V_DIM:(h + 1) * B_V_DIM]).astype(BF16)


def _proj_call(h, w_pad, kvn, tm, seq):
    n, d = h.shape
    assert seq & (seq - 1) == 0 and seq <= CHUNK * 256 and seq % tm == 0
    outs = [(512, BF16, False), (2 * KV_LORA, BF16, False), (KV_LORA + V_PAD, BF16, True),
            (512, BF16, False), (128, BF16, False), (128, F32, False), (512, BF16, False),
            (2 * B_HEADS * LANES, BF16, False), (B_HEADS * (B_V_DIM + V_PAD), BF16, True)]
    spec = lambda w, t: (pl.BlockSpec((w, tm), lambda i: (0, i)) if t
                         else pl.BlockSpec((tm, w), lambda i: (i, 0)))
    shape = lambda w, dt, t: jax.ShapeDtypeStruct((w, n) if t else (n, w), dt)
    return pl.pallas_call(
        functools.partial(_proj_kernel, seq=seq),
        grid=(n // tm,),
        in_specs=[pl.BlockSpec((tm, d), lambda i: (i, 0)),
                  pl.BlockSpec((d, _PROJ_COLS), lambda i: (0, 0)),
                  pl.BlockSpec((1, KV_LORA), lambda i: (0, 0))],
        out_specs=[spec(w, t) for w, _, t in outs],
        out_shape=[shape(w, dt, t) for w, dt, t in outs],
        compiler_params=_cparams(1),
        name="proj_in",
    )(h, w_pad, kvn.reshape(1, KV_LORA))


def _pad_w_in(w_in):
    sizes = (A_WIDTH, KV_LORA, IDX_HEADS * IDX_DIM, IDX_DIM, IDX_HEADS,
             2 * B_HEADS * B_QK_DIM, 2 * B_HEADS * B_QK_DIM, B_WIDTH)
    splits = np.cumsum(sizes)[:-1].tolist()
    q_a, c_kv, q_idx, k_idx, w_idx, q_b, k_b, v_b = jnp.split(w_in, splits, axis=-1)
    w_idx = jnp.pad(w_idx, ((0, 0), (0, LANES - IDX_HEADS)))
    return jnp.concatenate([q_a, c_kv, q_idx, k_idx, k_idx, w_idx, q_b, k_b, v_b],
                           axis=-1).astype(BF16)


def _flash_probs(sl, sc, m_ref):
    m_old = m_ref[:, sl]
    m_new = jnp.maximum(m_old, jnp.max(sc, axis=0, keepdims=True))
    m_ref[:, sl] = m_new
    return jnp.exp(m_old - m_new), jnp.exp(sc - m_new).astype(BF16)


def _flash_init(m_ref, acc_ref):
    m_ref[...] = jnp.full(m_ref.shape, M_INIT, F32)
    acc_ref[...] = jnp.zeros(acc_ref.shape, F32)


def _split_halves(q_ref, n_pairs, out_ref, tq):
    q_t = q_ref[...].astype(F32).T
    feat = lax.broadcasted_iota(I32, (LANES, tq), 0)
    for j in range(n_pairs):
        qp = q_t[j * LANES:(j + 1) * LANES, :]
        out_ref[:LANES, 2 * j * tq:(2 * j + 1) * tq] = jnp.where(feat < 64, qp, 0.0).astype(BF16)
        out_ref[:LANES, (2 * j + 1) * tq:(2 * j + 2) * tq] = jnp.where(feat >= 64, qp, 0.0).astype(BF16)


def _dsa_kernel(qa_ref, qidx_ref, widx_ref, kk_ref, ckv_ref, ckvt_ref, wk_ref, wvt_ref, out_ref,
                keys_ref, khi_ref, klo_ref, qim_ref, qlat_ref, tj_ref, m_ref, acc_ref,
                *, tq, tk, ts, seq, topk):
    i = pl.program_id(1)
    n_off = i * (tq // tk)
    n_kt = n_off + tq // tk
    idx_bits = int(math.log2(seq))

    _split_halves(qidx_ref, IDX_HEADS // 2, qim_ref, tq)
    qlat_t = lax.dot_general(wk_ref[...], qa_ref[...], NT_DIMS,
                             preferred_element_type=F32)
    t_row = i * tq + lax.broadcasted_iota(I32, (LANES, tq), 1)
    for h in range(A_HEADS):
        qlat_ref[:KV_LORA, h * tq:(h + 1) * tq] = qlat_t[h * KV_LORA:(h + 1) * KV_LORA, :].astype(BF16)
        qlat_ref[KV_LORA:, h * tq:(h + 1) * tq] = _slope_features(
            t_row, 2.0 ** -(h + 1), (LANES, tq)).astype(BF16)
    w_t = widx_ref[...].T

    row = lax.broadcasted_iota(I32, (tk, tq), 0)
    tqi = i * tq + lax.broadcasted_iota(I32, (tk, tq), 1)
    qchunk = lax.shift_right_logical(tqi, CHUNK_SHIFT)

    def idx_tile(kt, overlaps_block):
        off = pl.multiple_of(kt * tk, tk)
        kkt = kk_ref[pl.ds(off, tk), :]
        s_all = jnp.dot(kkt, qim_ref[...], preferred_element_type=F32)
        acc = jnp.zeros((tk, tq), F32)
        for h in range(IDX_HEADS):
            acc = acc + w_t[h:h + 1, :] * jnp.maximum(s_all[:, h * tq:(h + 1) * tq], 0.0)
        bits = lax.bitcast_convert_type(acc + 0.0, I32)
        key = jnp.where(bits < 0, bits ^ jnp.int32(0x7FFFFFFF), bits)
        if overlaps_block:
            adm = lax.shift_right_logical(off + row, CHUNK_SHIFT) <= qchunk
            key = jnp.where(adm, key, jnp.int32(INT_MIN))
        keys_ref[pl.ds(off, tk), :] = key
        khi_ref[pl.ds(off, tk), :] = lax.shift_right_arithmetic(key, 16).astype(I16)
        klo_ref[pl.ds(off, tk), :] = ((key & 0xFFFF) + I16_MIN).astype(I16)

    def idx_off_body(kt, carry):
        idx_tile(kt, False)
        return carry

    def idx_diag_body(kt, carry):
        idx_tile(kt, True)
        return carry

    lax.fori_loop(0, n_off, idx_off_body, 0)
    lax.fori_loop(n_off, n_kt, idx_diag_body, 0)

    n_st = n_kt * (tk // ts)
    row_s = lax.broadcasted_iota(I32, (ts, tq), 0)

    def count(pred):
        def body(st, c):
            off = pl.multiple_of(st * ts, ts)
            ind = jnp.where(pred(keys_ref[pl.ds(off, ts), :], off + row_s), 1.0, 0.0)
            return c + jnp.sum(ind, axis=0, keepdims=True)
        return lax.fori_loop(0, n_st, body, jnp.zeros((1, tq), F32))

    def count16(ref, pred):
        pack = 2 * SUBLANES
        def body(st, c):
            off = pl.multiple_of(st * ts, ts)
            ind = jnp.where(pred(ref[pl.ds(off, ts), :]), jnp.int16(1), jnp.int16(0))
            parts = [ind[j * pack:(j + 1) * pack] for j in range(ts // pack)]
            while len(parts) > 1:
                parts = [parts[j] + parts[j + 1] for j in range(0, len(parts), 2)]
            return c + parts[0]
        c = lax.fori_loop(0, n_st, body, jnp.zeros((pack, tq), I16))
        return jnp.sum(c.astype(F32), axis=0, keepdims=True)

    def search16(ref, need):
        def body(it, t):
            cand = t + lax.shift_left(jnp.int32(1), 15 - it)
            cand16 = cand.astype(I16)
            cnt = count16(ref, lambda tile: tile >= cand16)
            return jnp.where(cnt >= need, cand, t)
        return lax.fori_loop(0, 16, body, jnp.full((1, tq), I16_MIN, I32))

    t_hi = search16(khi_ref, topk)
    t_hi16 = t_hi.astype(I16)
    n_above = count16(khi_ref, lambda tile: tile > t_hi16)

    def park_body(st, carry):
        off = pl.multiple_of(st * ts, ts)
        klo_ref[pl.ds(off, ts), :] = jnp.where(khi_ref[pl.ds(off, ts), :] == t_hi16,
                                               klo_ref[pl.ds(off, ts), :], jnp.int16(I16_MIN))
        return carry

    lax.fori_loop(0, n_st, park_body, 0)
    t_lo = search16(klo_ref, topk - n_above)
    thr = t_hi * 65536 + (t_lo - I16_MIN)

    live = thr > INT_MIN
    t_lo16 = t_lo.astype(I16)
    n_ge = n_above + count16(klo_ref, lambda tile: tile >= t_lo16)
    excess = jnp.logical_and(n_ge > topk, live)
    all_ties = jnp.where(live, seq, -1).astype(I32)
    tj_ref[0] = jnp.broadcast_to(thr, (SUBLANES, tq))
    tj_ref[1] = jnp.broadcast_to(all_ties, (SUBLANES, tq))

    @pl.when(jnp.max(jnp.where(excess, 1.0, 0.0)) > 0.0)
    def _():
        need = topk - count(lambda tile, _: tile > thr)

        def tie_body(it, hi):
            cand = hi + lax.shift_left(jnp.int32(1), idx_bits - 1 - it)
            cnt = count(lambda tile, pos: jnp.logical_and(tile == thr, pos < cand))
            return jnp.where(cnt < need, cand, hi)

        hi = lax.fori_loop(0, idx_bits, tie_body, jnp.zeros((1, tq), I32))
        tj_ref[1] = jnp.broadcast_to(jnp.where(excess, hi, all_ties), (SUBLANES, tq))

    thr_b = tj_ref[0][:1]
    tie_hi = tj_ref[1][:1]

    _flash_init(m_ref, acc_ref)

    def att_tile(kt, late_fix):
        off = pl.multiple_of(kt * tk, tk)
        tile = keys_ref[pl.ds(off, tk), :]
        pos = off + row
        sel = jnp.logical_or(tile > thr_b,
                             jnp.logical_and(tile == thr_b, pos <= tie_hi))
        s_all = jnp.dot(ckv_ref[pl.ds(off, tk), :], qlat_ref[...],
                        preferred_element_type=F32)
        if late_fix:
            late = jnp.maximum(pos - tqi, 0).astype(F32)
        alphas, probs = [], []
        for h in range(A_HEADS):
            sl = slice(h * tq, (h + 1) * tq)
            sc = s_all[:, sl]
            if late_fix:
                sc = sc - (2.0 * 2.0 ** -(h + 1)) * late
            alpha, p = _flash_probs(sl, jnp.where(sel, sc, NEG_BIG), m_ref)
            alphas.append(alpha)
            probs.append(p)
        pv = jnp.dot(ckvt_ref[:, pl.ds(off, tk)], jnp.concatenate(probs, axis=1),
                     preferred_element_type=F32)
        acc_ref[...] = jnp.concatenate(alphas, axis=1) * acc_ref[...] + pv

    def off_body(kt, carry):
        att_tile(kt, False)
        return carry

    def diag_body(kt, carry):
        att_tile(kt, True)
        return carry

    lax.fori_loop(0, n_off, off_body, 0)
    lax.fori_loop(n_off, n_kt, diag_body, 0)

    inv_l = 1.0 / acc_ref[KV_LORA:KV_LORA + 1, :]
    olat = jnp.concatenate([(acc_ref[:KV_LORA, h * tq:(h + 1) * tq]
                             * inv_l[:, h * tq:(h + 1) * tq]).astype(BF16)
                            for h in range(A_HEADS)], axis=0)
    out_t = jnp.dot(wvt_ref[...], olat, preferred_element_type=F32)
    out_ref[...] = out_t.T.astype(BF16)


def _dsa_call(qa, qidx, widx, kk, ckv, ckvt, wk_bd, wvt_bd, b, s, tq, tk, ts):
    topk = min(INDEX_TOPK_MAX, s // 4)
    assert s % tq == 0 and tq % tk == 0 and tk % ts == 0 and ts % LANES == 0 and s & (s - 1) == 0
    blk = lambda w: pl.BlockSpec((None, tq, w), lambda bi, i: (bi, i, 0))
    full = lambda w: pl.BlockSpec((None, s, w), lambda bi, i: (bi, 0, 0))
    const = lambda a: pl.BlockSpec(a.shape, lambda bi, i: (0, 0))
    r3 = lambda a: a.reshape(b, s, a.shape[-1])
    out = pl.pallas_call(
        functools.partial(_dsa_kernel, tq=tq, tk=tk, ts=ts, seq=s, topk=topk),
        grid=(b, s // tq),
        in_specs=[blk(A_WIDTH), blk(IDX_HEADS * IDX_DIM), blk(LANES), full(LANES),
                  full(2 * KV_LORA),
                  pl.BlockSpec((KV_LORA + V_PAD, s), lambda bi, i: (0, bi)),
                  const(wk_bd), const(wvt_bd)],
        out_specs=blk(A_WIDTH),
        out_shape=jax.ShapeDtypeStruct((b, s, A_WIDTH), BF16),
        scratch_shapes=[pltpu.VMEM((s, tq), I32),
                        pltpu.VMEM((s, tq), I16),
                        pltpu.VMEM((s, tq), I16),
                        pltpu.VMEM((LANES, IDX_HEADS * tq), BF16),
                        pltpu.VMEM((2 * KV_LORA, A_HEADS * tq), BF16),
                        pltpu.VMEM((2, SUBLANES, tq), I32),
                        pltpu.VMEM((1, A_HEADS * tq), F32),
                        pltpu.VMEM((KV_LORA + V_PAD, A_HEADS * tq), F32)],
        compiler_params=_cparams(2),
        name="dsa",
    )(r3(qa), r3(qidx), r3(widx), r3(kk), r3(ckv), ckvt, wk_bd, wvt_bd)
    return out.reshape(b * s, A_WIDTH)


def _block_diag_uk(w_uk):
    h, c, d = w_uk.shape
    eye = jnp.eye(h, dtype=w_uk.dtype)
    return jnp.einsum('hcd,hg->hcgd', w_uk, eye).reshape(h * c, h * d).astype(BF16)


def _block_diag_uv_t(w_uv):
    h, c, d = w_uv.shape
    eye = jnp.eye(h, dtype=w_uv.dtype)
    return jnp.einsum('hcd,hg->hdgc', w_uv, eye).reshape(h * d, h * c).astype(BF16)


def _diff_kernel(qb_ref, kb_ref, vbt_ref, lam_ref, subln_ref, out_ref,
                 qm_ref, m_ref, acc_ref, *, tq, tk, lambda_init):
    i = pl.program_id(1)
    n_off = i * (tq // tk)
    dvp = B_V_DIM + V_PAD
    _split_halves(qb_ref, B_HEADS, qm_ref, tq)
    t_row = i * tq + lax.broadcasted_iota(I32, (LANES, tq), 1)
    for h in range(B_HEADS):
        feat = _slope_features(t_row, 2.0 ** (-2 * (h + 1)), (LANES, tq)).astype(BF16)
        qm_ref[LANES:, 2 * h * tq:(2 * h + 1) * tq] = feat
        qm_ref[LANES:, (2 * h + 1) * tq:(2 * h + 2) * tq] = feat

    row = lax.broadcasted_iota(I32, (tk, tq), 0)
    tqi = i * tq + lax.broadcasted_iota(I32, (tk, tq), 1)
    qchunk = lax.shift_right_logical(tqi, CHUNK_SHIFT)
    _flash_init(m_ref, acc_ref)

    def tile_step(kt, diag):
        off = pl.multiple_of(kt * tk, tk)
        s_pairs = [jnp.dot(kb_ref[pl.ds(off, tk), 2 * h * LANES:(2 * h + 2) * LANES],
                           qm_ref[:, 2 * h * tq:(2 * h + 2) * tq],
                           preferred_element_type=F32) for h in range(B_HEADS)]
        if diag:
            pos = off + row
            adm = lax.shift_right_logical(pos, CHUNK_SHIFT) <= qchunk
            late = jnp.maximum(pos - tqi, 0).astype(F32)
        alphas, probs = [], []
        for h in range(B_HEADS):
            for j in range(2):
                sl = slice((2 * h + j) * tq, (2 * h + j + 1) * tq)
                sc = s_pairs[h][:, j * tq:(j + 1) * tq]
                if diag:
                    sc = jnp.where(adm, sc - (2.0 * 2.0 ** (-2 * (h + 1))) * late, NEG_BIG)
                alpha, p = _flash_probs(sl, sc, m_ref)
                alphas.append(alpha)
                probs.append(p)
        pv = jnp.concatenate(
            [jnp.dot(vbt_ref[h * dvp:(h + 1) * dvp, pl.ds(off, tk)],
                     jnp.concatenate(probs[2 * h:2 * h + 2], axis=1), preferred_element_type=F32)
             for h in range(B_HEADS)], axis=1)
        acc_ref[...] = jnp.concatenate(alphas, axis=1) * acc_ref[...] + pv

    def off_body(kt, carry):
        tile_step(kt, False)
        return carry

    def diag_body(kt, carry):
        tile_step(kt, True)
        return carry

    lax.fori_loop(0, n_off, off_body, 0)
    lax.fori_loop(n_off, n_off + tq // tk, diag_body, 0)

    lp = lam_ref[...]
    lam = (jnp.exp(jnp.sum(lp[0:1] * lp[1:2], axis=1, keepdims=True))
           - jnp.exp(jnp.sum(lp[2:3] * lp[3:4], axis=1, keepdims=True)) + lambda_init)
    o_all = acc_ref[:B_V_DIM, :] * (1.0 / acc_ref[B_V_DIM:B_V_DIM + 1, :])
    outs = []
    for h in range(B_HEADS):
        o = (o_all[:, 2 * h * tq:(2 * h + 1) * tq]
             - lam * o_all[:, (2 * h + 1) * tq:(2 * h + 2) * tq])
        o = o * lax.rsqrt(jnp.mean(o * o, axis=0, keepdims=True) + RMS_EPS) * subln_ref[...]
        outs.append(o * (1.0 - lambda_init))
    out_ref[...] = jnp.concatenate(outs, axis=0).T.astype(BF16)


def _diff_call(qb, kb, vbt, lam_rows, subln, b, s, tq, tk, lambda_init):
    blk = pl.BlockSpec((None, tq, B_WIDTH), lambda bi, i: (bi, i, 0))
    kw = 2 * B_HEADS * LANES
    dvp = B_V_DIM + V_PAD
    out = pl.pallas_call(
        functools.partial(_diff_kernel, tq=tq, tk=tk, lambda_init=lambda_init),
        grid=(b, s // tq),
        in_specs=[blk, pl.BlockSpec((None, s, kw), lambda bi, i: (bi, 0, 0)),
                  pl.BlockSpec((B_HEADS * dvp, s), lambda bi, i: (0, bi)),
                  pl.BlockSpec((SUBLANES, LANES), lambda bi, i: (0, 0)),
                  pl.BlockSpec((B_V_DIM, 1), lambda bi, i: (0, 0))],
        out_specs=blk,
        out_shape=jax.ShapeDtypeStruct((b, s, B_WIDTH), BF16),
        scratch_shapes=[pltpu.VMEM((2 * LANES, 2 * B_HEADS * tq), BF16),
                        pltpu.VMEM((1, 2 * B_HEADS * tq), F32),
                        pltpu.VMEM((dvp, 2 * B_HEADS * tq), F32)],
        compiler_params=_cparams(2),
        name="diff",
    )(qb.reshape(b, s, B_WIDTH), kb.reshape(b, s, kw), vbt, lam_rows, subln.reshape(B_V_DIM, 1))
    return out.reshape(b * s, B_WIDTH)


def _mix_out_kernel(h_ref, a_ref, b_ref, wa_ref, wb_ref, g_ref, beta_ref, o_ref):
    y = (jnp.dot(a_ref[...], wa_ref[...], preferred_element_type=F32)
         + jnp.dot(b_ref[...], wb_ref[...], preferred_element_type=F32))
    o_ref[...] = _layer_norm(ALPHA * h_ref[...] + y, g_ref[...], beta_ref[...])


def _mix_out_call(h, out_a, out_b, w_o, g, beta, tm):
    n, d = h.shape
    w = w_o.astype(BF16)
    row = lambda wd: pl.BlockSpec((tm, wd), lambda i: (i, 0))
    const = lambda shp: pl.BlockSpec(shp, lambda i: (0, 0))
    return pl.pallas_call(
        _mix_out_kernel,
        grid=(n // tm,),
        in_specs=[row(d), row(A_WIDTH), row(B_WIDTH), const((A_WIDTH, d)), const((B_WIDTH, d)),
                  const((1, d)), const((1, d))],
        out_specs=row(d),
        out_shape=jax.ShapeDtypeStruct((n, d), F32),
        compiler_params=_cparams(1),
        name="mix_out",
    )(h, out_a, out_b, w[:A_WIDTH], w[A_WIDTH:], g.reshape(1, d), beta.reshape(1, d))


def _mem_kv_kernel(m_ref, w_ref, k_ref, v_ref):
    x = m_ref[...].astype(BF16)
    d = k_ref.shape[-1]
    k_ref[...] = jnp.dot(x, w_ref[:, :d], preferred_element_type=F32).astype(BF16)
    v_ref[...] = jnp.dot(x, w_ref[:, d:], preferred_element_type=F32).astype(BF16)


def _mem_kv_call(mem2d, wkv, tm):
    n, d = mem2d.shape
    return pl.pallas_call(
        _mem_kv_kernel,
        grid=(n // tm,),
        in_specs=[pl.BlockSpec((tm, d), lambda i: (i, 0)),
                  pl.BlockSpec((d, 2 * d), lambda i: (0, 0))],
        out_specs=[pl.BlockSpec((tm, d), lambda i: (i, 0))] * 2,
        out_shape=[jax.ShapeDtypeStruct((n, d), BF16)] * 2,
        compiler_params=_cparams(1),
        name="mem_kv",
    )(mem2d, wkv.astype(BF16))


def _mem_attn_kernel(h_ref, wq_ref, k_ref, v_ref, wo_ref, g_ref, beta_ref, o_ref):
    h = h_ref[...]
    q = jnp.dot(h.astype(BF16), wq_ref[...], preferred_element_type=F32)
    q = (q * (MEM_HEAD_DIM ** -0.5)).astype(BF16)
    outs = []
    for hd in range(MEM_HEADS):
        sl = slice(hd * MEM_HEAD_DIM, (hd + 1) * MEM_HEAD_DIM)
        sc = lax.dot_general(q[:, sl], k_ref[:, sl], NT_DIMS, preferred_element_type=F32)
        p = jnp.exp(sc - jnp.max(sc, axis=1, keepdims=True))
        den = jnp.sum(p, axis=1, keepdims=True)
        o = jnp.dot(p.astype(BF16), v_ref[:, sl], preferred_element_type=F32) / den
        outs.append(o.astype(BF16))
    y = jnp.dot(jnp.concatenate(outs, axis=1), wo_ref[...], preferred_element_type=F32)
    o_ref[...] = _layer_norm(ALPHA * h + y, g_ref[...], beta_ref[...])


def _mem_attn_call(h, wq, k, v, wo, g, beta, b, s, tm):
    n, d = h.shape
    m = k.shape[0] // b
    const = lambda shp: pl.BlockSpec(shp, lambda bi, i: (0,) * len(shp))
    row = pl.BlockSpec((None, tm, d), lambda bi, i: (bi, i, 0))
    kv = pl.BlockSpec((None, m, d), lambda bi, i: (bi, 0, 0))
    out = pl.pallas_call(
        _mem_attn_kernel,
        grid=(b, s // tm),
        in_specs=[row, const((d, d)), kv, kv, const((d, d)), const((1, d)), const((1, d))],
        out_specs=row,
        out_shape=jax.ShapeDtypeStruct((b, s, d), F32),
        compiler_params=_cparams(2),
        name="mem_attn",
    )(h.reshape(b, s, d), wq.astype(BF16), k.reshape(b, m, d), v.reshape(b, m, d),
      wo.astype(BF16), g.reshape(1, d), beta.reshape(1, d))
    return out.reshape(n, d)


def _router_kernel(h_ref, rw_ref, bias_ref, tri_ref, e_ref, g_ref, rank_ref, cnt_ref, run_ref):
    tm = h_ref.shape[0]

    @pl.when(pl.program_id(0) == 0)
    def _():
        run_ref[...] = jnp.zeros(run_ref.shape, F32)

    logits = jnp.dot(h_ref[...].astype(BF16), rw_ref[...], preferred_element_type=F32)
    scores = 1.0 / (1.0 + jnp.exp(-logits))
    lane = lax.broadcasted_iota(I32, (tm, LANES), 1)
    lanef = lane.astype(F32)
    ninf = jnp.float32(-jnp.inf)
    cur = jnp.where(lane < N_EXPERTS, scores + bias_ref[...], ninf)
    top_e = jnp.zeros((tm, LANES), F32)
    top_s = jnp.zeros((tm, LANES), F32)
    picked = jnp.zeros((tm, LANES), F32)
    for k in range(TOP_K):
        mx = jnp.max(cur, axis=1, keepdims=True)
        first = jnp.min(jnp.where(cur == mx, lanef, float(LANES)), axis=1, keepdims=True)
        hit = lanef == first
        s_k = jnp.sum(jnp.where(hit, scores, 0.0), axis=1, keepdims=True)
        top_e = jnp.where(lane == k, first, top_e)
        top_s = jnp.where(lane == k, s_k, top_s)
        picked = jnp.where(hit, 1.0, picked)
        cur = jnp.where(hit, ninf, cur)
    e_ref[...] = top_e.astype(I32)
    g_ref[...] = top_s / jnp.sum(top_s, axis=1, keepdims=True) * ROUTED_SCALE

    slot = jnp.dot(tri_ref[...], picked.astype(BF16), preferred_element_type=F32) + run_ref[...]
    rank_ref[...] = slot.astype(I32)
    run = run_ref[...] + jnp.sum(picked, axis=0, keepdims=True)
    run_ref[...] = run
    cnt_ref[...] = run.astype(I32)


def _router_call(h, router_w, router_bias, tm):
    n, d = h.shape
    rw = jnp.pad(router_w, ((0, 0), (0, LANES - N_EXPERTS))).astype(BF16)
    bias = jnp.pad(router_bias, (0, LANES - N_EXPERTS)).reshape(1, LANES)
    tri = jnp.tri(tm, k=-1, dtype=BF16)
    row = pl.BlockSpec((tm, LANES), lambda i: (i, 0))
    const = lambda shp: pl.BlockSpec(shp, lambda i: (0, 0))
    return pl.pallas_call(
        _router_kernel,
        grid=(n // tm,),
        in_specs=[pl.BlockSpec((tm, d), lambda i: (i, 0)), const((d, LANES)), const((1, LANES)),
                  const((tm, tm))],
        out_specs=[row, row, row, const((1, LANES))],
        out_shape=[jax.ShapeDtypeStruct((n, LANES), I32),
                   jax.ShapeDtypeStruct((n, LANES), F32),
                   jax.ShapeDtypeStruct((n, LANES), I32),
                   jax.ShapeDtypeStruct((1, LANES), I32)],
        scratch_shapes=[pltpu.VMEM((1, LANES), F32)],
        compiler_params=_cparams(1),
        name="router",
    )(h, rw, bias, tri)


def _dest_kernel(e_ref, rank_ref, gs_ref, o_ref):
    tm = e_ref.shape[0]
    lane = lax.broadcasted_iota(I32, (tm, LANES), 1)
    e = e_ref[...]
    row_of = (gs_ref[...] + rank_ref[...]).astype(F32)
    dest = jnp.zeros((tm, LANES), F32)
    for k in range(TOP_K):
        d_k = jnp.sum(jnp.where(lane == e[:, k:k + 1], row_of, 0.0), axis=1, keepdims=True)
        dest = jnp.where(lane == k, d_k, dest)
    o_ref[...] = dest.astype(I32)


def _dest_call(top_e, rank, gstart, tm):
    n = top_e.shape[0]
    row = pl.BlockSpec((tm, LANES), lambda i: (i, 0))
    return pl.pallas_call(
        _dest_kernel,
        grid=(n // tm,),
        in_specs=[row, row, pl.BlockSpec((1, LANES), lambda i: (0, 0))],
        out_specs=row,
        out_shape=jax.ShapeDtypeStruct((n, LANES), I32),
        compiler_params=_cparams(1),
        name="dest",
    )(top_e, rank, gstart)


def _dispatch_kernel(dest_ref, x_ref, xs_ref, xp_ref, sem, *, td):
    i = pl.program_id(0)
    slot = lax.rem(i, 2)
    half = x_ref.shape[1] // 2
    xp_ref[slot] = pltpu.pack_elementwise([x_ref[:, :half], x_ref[:, half:]], packed_dtype=BF16)

    def row_copy(r, k, s):
        return pltpu.make_async_copy(xp_ref.at[s, pl.ds(r, 1), :],
                                     xs_ref.at[pl.ds(dest_ref[r, k], 1), :], sem.at[s])

    for r in range(td):
        for k in range(TOP_K):
            row_copy(r, k, slot).start(priority=k % N_DMA_QUEUES)

    def drain(s):
        def body(r, carry):
            for k in range(TOP_K):
                row_copy(r, k, s).wait()
            return carry
        lax.fori_loop(0, td, body, 0)

    @pl.when(i > 0)
    def _():
        drain(1 - slot)

    @pl.when(i == pl.num_programs(0) - 1)
    def _():
        drain(slot)


def _dispatch_call(h, dest, td):
    n, d = h.shape
    m = n * TOP_K
    return pl.pallas_call(
        functools.partial(_dispatch_kernel, td=td),
        grid=(n // td,),
        in_specs=[pl.BlockSpec((td, LANES), lambda i: (i, 0), memory_space=pltpu.SMEM),
                  pl.BlockSpec((td, d), lambda i: (i, 0))],
        out_specs=pl.BlockSpec(memory_space=pl.ANY),
        out_shape=jax.ShapeDtypeStruct((m, d // 2), I32),
        scratch_shapes=[pltpu.VMEM((2, td, d // 2), I32), pltpu.SemaphoreType.DMA((2,))],
        compiler_params=_cparams(1),
        name="dispatch",
    )(dest, h)


def _moe_mm_kernel(ptile_ref, pexp_ref, plo_ref, phi_ref, xs_ref, wg_ref, wu_ref, wd_ref, y_ref,
                   *, tmm):
    p = pl.program_id(0)
    tile = ptile_ref[p]
    lo = plo_ref[p]
    hi = phi_ref[p]
    first = jnp.logical_or(p == 0, ptile_ref[jnp.maximum(p - 1, 0)] != tile)

    def expert_out():
        xp = xs_ref[...]
        x = jnp.concatenate(
            [pltpu.unpack_elementwise(xp, index=j, packed_dtype=BF16, unpacked_dtype=F32)
             for j in range(2)], axis=1).astype(BF16)
        g = jnp.dot(x, wg_ref[...].astype(BF16), preferred_element_type=F32)
        u = jnp.dot(x, wu_ref[...].astype(BF16), preferred_element_type=F32)
        hid = (_silu(g) * u).astype(BF16)
        return jnp.dot(hid, wd_ref[...].astype(BF16), preferred_element_type=F32)

    whole = jnp.logical_and(lo <= tile * tmm, hi >= (tile + 1) * tmm)

    @pl.when(whole)
    def _():
        y_ref[...] = expert_out()

    @pl.when(jnp.logical_and(first, jnp.logical_not(whole)))
    def _():
        y_ref[...] = jnp.zeros(y_ref.shape, F32)

    @pl.when(jnp.logical_and(lo < hi, jnp.logical_not(whole)))
    def _():
        y = expert_out()
        rowg = tile * tmm + lax.broadcasted_iota(I32, (tmm, 1), 0)
        y_ref[...] += jnp.where(jnp.logical_and(rowg >= lo, rowg < hi), y, 0.0)


def _moe_mm_call(xs, pairs, wg, wu, wd, layer, tmm):
    m = xs.shape[0]
    n_pairs = pairs[0].shape[0]
    rows = lambda w: pl.BlockSpec((tmm, w), lambda p, pt, pe, plo, phi: (pt[p], 0))
    wspec = lambda shp: pl.BlockSpec((None, None) + shp,
                                     lambda p, pt, pe, plo, phi: (layer, pe[p], 0, 0))
    return pl.pallas_call(
        functools.partial(_moe_mm_kernel, tmm=tmm),
        grid_spec=pltpu.PrefetchScalarGridSpec(
            num_scalar_prefetch=4,
            grid=(n_pairs,),
            in_specs=[rows(xs.shape[1]), wspec((D_MODEL, EXPERT_DIM)), wspec((D_MODEL, EXPERT_DIM)),
                      wspec((EXPERT_DIM, D_MODEL))],
            out_specs=rows(D_MODEL)),
        out_shape=jax.ShapeDtypeStruct((m, D_MODEL), F32),
        compiler_params=_cparams(1),
        name="moe_mm",
    )(*pairs, xs, wg, wu, wd)


def _group_pairs(counts, m, tmm):
    n_tiles = m // tmm
    n_pairs = n_tiles + N_EXPERTS
    gend = jnp.cumsum(counts).astype(I32)
    gstart = gend - counts.astype(I32)
    cuts = jnp.concatenate([jnp.arange(n_tiles, dtype=I32) * tmm, gstart])
    idx = jnp.arange(n_pairs, dtype=I32)
    before = jnp.logical_or(cuts[None, :] < cuts[:, None],
                            jnp.logical_and(cuts[None, :] == cuts[:, None], idx[None, :] < idx[:, None]))
    rank = jnp.sum(before, axis=1).astype(I32)
    lo = jnp.sum(jnp.where(rank[None, :] == idx[:, None], cuts[None, :], 0), axis=1).astype(I32)
    hi = jnp.concatenate([lo[1:], jnp.full((1,), m, I32)])
    tile = jnp.minimum(lo // tmm, n_tiles - 1)
    e = jnp.minimum(jnp.sum(gend[None, :] <= lo[:, None], axis=1), N_EXPERTS - 1).astype(I32)
    return tile, e, lo, hi


def _combine_kernel(dest_ref, gate_ref, h_ref, y_ref, sg_ref, su_ref, sd_ref, g_ref, beta_ref, o_ref,
                    buf_ref, sem, *, tc):
    def row_copy(r, k):
        return pltpu.make_async_copy(y_ref.at[pl.ds(dest_ref[r, k], 1), :],
                                     buf_ref.at[k, pl.ds(r, 1), :], sem)

    for r in range(tc):
        for k in range(TOP_K):
            row_copy(r, k).start(priority=k % N_DMA_QUEUES)

    h = h_ref[...]
    x = h.astype(BF16)
    hid = (_silu(jnp.dot(x, sg_ref[...], preferred_element_type=F32))
           * jnp.dot(x, su_ref[...], preferred_element_type=F32)).astype(BF16)
    shared = jnp.dot(hid, sd_ref[...], preferred_element_type=F32)

    def drain(r, carry):
        for k in range(TOP_K):
            row_copy(r, k).wait()
        return carry

    lax.fori_loop(0, tc, drain, 0)

    gates = gate_ref[...]
    routed = gates[:, 0:1] * buf_ref[0]
    for k in range(1, TOP_K):
        routed = routed + gates[:, k:k + 1] * buf_ref[k]
    o_ref[...] = _layer_norm(ALPHA * h + (routed + shared), g_ref[...], beta_ref[...])


def _combine_call(h, y, dest, gates, sg, su, sd, g, beta, tc):
    n, d = h.shape
    smem = pl.BlockSpec((tc, LANES), lambda i: (i, 0), memory_space=pltpu.SMEM)
    const = lambda shp: pl.BlockSpec(shp, lambda i: (0, 0))
    return pl.pallas_call(
        functools.partial(_combine_kernel, tc=tc),
        grid=(n // tc,),
        in_specs=[smem, pl.BlockSpec((tc, LANES), lambda i: (i, 0)),
                  pl.BlockSpec((tc, d), lambda i: (i, 0)),
                  pl.BlockSpec(memory_space=pl.ANY),
                  const((d, SHARED_DIM)), const((d, SHARED_DIM)), const((SHARED_DIM, d)),
                  const((1, d)), const((1, d))],
        out_specs=pl.BlockSpec((tc, d), lambda i: (i, 0)),
        out_shape=jax.ShapeDtypeStruct((n, d), F32),
        scratch_shapes=[pltpu.VMEM((TOP_K, tc, d), F32),
                        pltpu.SemaphoreType.DMA(())],
        compiler_params=_cparams(1),
        name="combine",
    )(dest, gates, h, y, sg.astype(BF16), su.astype(BF16), sd.astype(BF16),
      g.reshape(1, d), beta.reshape(1, d))


def _tiles(b, s):
    n = b * s
    pick = lambda pref, total: next(t for t in (pref, 512, 256, 128, 64, 32, 16, 8) if t <= pref and total % t == 0)
    return dict(
        rows=pick(1024, n),
        attn_q=pick(512, s),
        attn_k=pick(512, s),
        select=pick(256, s),
        mem_rows=pick(1024, s),
        mem_kv=pick(512, b * 256),
        router=pick(1024, n),
        dispatch=pick(128, n),
        moe=pick(512, n * TOP_K),
        combine=pick(128, n),
    )


def kernel(x, mem, ln_in_g, ln_in_b, w_in, a_kv_norm, a_w_uk, a_w_uv, b_lq1, b_lk1, b_lq2, b_lk2,
           b_subln, w_o, ln1_g, ln1_b, m_wq, m_wkv, m_wo, ln2_g, ln2_b, router_w, router_bias,
           e_w_gate, e_w_up, e_w_down, s_w_gate, s_w_up, s_w_down, ln3_g, ln3_b):
    b, s, d = x.shape
    n = b * s
    m = n * TOP_K
    t = _tiles(b, s)
    mem2d = mem.reshape(-1, d)

    h = _ln_call(x.reshape(n, d), ln_in_g, ln_in_b, t["rows"])
    for l in range(DEPTH):
        lambda_init = 0.8 - 0.6 * math.exp(-0.3 * l)

        qa, ckv, ckvt, qidx, kk, widx, qb, kb, vbt = _proj_call(h, _pad_w_in(w_in[l]), a_kv_norm[l],
                                                                  t["rows"], s)
        out_a = _dsa_call(qa, qidx, widx, kk, ckv, ckvt, _block_diag_uk(a_w_uk[l]),
                          _block_diag_uv_t(a_w_uv[l]), b, s, t["attn_q"], t["attn_k"], t["select"])
        lam_rows = jnp.pad(jnp.stack([b_lq1[l], b_lk1[l], b_lq2[l], b_lk2[l]]),
                           ((0, SUBLANES - 4), (0, LANES - B_QK_DIM)))
        out_b = _diff_call(qb, kb, vbt, lam_rows, b_subln[l], b, s, t["attn_q"], t["attn_k"], lambda_init)
        h = _mix_out_call(h, out_a, out_b, w_o[l], ln1_g[l], ln1_b[l], t["rows"])

        mk, mv = _mem_kv_call(mem2d, m_wkv[l], t["mem_kv"])
        h = _mem_attn_call(h, m_wq[l], mk, mv, m_wo[l], ln2_g[l], ln2_b[l], b, s, t["mem_rows"])

        top_e, gates, rank, counts = _router_call(h, router_w[l], router_bias[l], t["router"])
        gstart = jnp.cumsum(counts, axis=1) - counts
        dest = _dest_call(top_e, rank, gstart, t["router"])
        counts = counts[0, :N_EXPERTS]
        xs = _dispatch_call(h, dest, t["dispatch"])
        pairs = _group_pairs(counts, m, t["moe"])
        y = _moe_mm_call(xs, pairs, e_w_gate, e_w_up, e_w_down, l, t["moe"])
        h = _combine_call(h, y, dest, gates, s_w_gate[l], s_w_up[l], s_w_down[l],
                          ln3_g[l], ln3_b[l], t["combine"])
    return h.reshape(b, s, d)
```
